```python
import math
import jax
import jax.numpy as jnp
from jax import lax
import numpy as np

D_MODEL = 1024
BATCH = 8
SEQ = 2048
DEPTH = 1

HEAD_DIM = 64
MIX_WIDTH = D_MODEL
A_WIDTH = MIX_WIDTH // 2
B_WIDTH = MIX_WIDTH - A_WIDTH
A_HEADS = A_WIDTH // HEAD_DIM
B_HEADS = B_WIDTH // (2 * HEAD_DIM)
DIL_PATTERNS = ((128, 1), (512, 4), (2048, 16))
BAND_BLOCK = 128
Q_BLOCK = 128
N_EXPERTS = 32
TOP_K = 4
D_FF = D_MODEL
SWIGLU_LIMIT = 7.0
SWIGLU_ALPHA = 1.702
MOE_BLOCK = 128
RMS_EPS = 1e-6
ATTN_SCALE = HEAD_DIM ** -0.5
IN_SIZES = (A_WIDTH, A_WIDTH, A_WIDTH, B_WIDTH, B_WIDTH, B_WIDTH)
IN_WIDTH = sum(IN_SIZES)
IN_SPLITS = tuple(int(v) for v in np.cumsum(IN_SIZES)[:-1])

kernel_name = 'hymba_style_dilated_diffattn_moe_block'


def rms_norm(x, g):
    xf = x.astype(jnp.float32)
    y = xf * lax.rsqrt(jnp.mean(xf * xf, axis=-1, keepdims=True) + RMS_EPS)
    return (y * g.astype(jnp.float32)).astype(x.dtype)


def alibi_slopes(n):
    return 2.0 ** (-8.0 * jnp.arange(1, n + 1, dtype=jnp.float32) / n)


def dilated_band_attention(q, k, v, slopes, window, dil):
    bsz, nh, n_tok, hd = q.shape
    L = n_tok // dil
    w_sub = window // dil
    nb = -(-L // BAND_BLOCK)
    Lp = nb * BAND_BLOCK

    def to_sub(t):
        t = jnp.swapaxes(t.reshape(bsz, nh, L, dil, hd), 2, 3)
        return jnp.pad(t, ((0, 0), (0, 0), (0, 0), (0, Lp - L), (0, 0)))

    def key_band(t):
        t = jnp.pad(to_sub(t), ((0, 0), (0, 0), (0, 0), (BAND_BLOCK, 0), (0, 0)))
        t = t.reshape(bsz, nh, dil, nb + 1, BAND_BLOCK, hd)
        return jnp.concatenate([t[:, :, :, :-1], t[:, :, :, 1:]], axis=4)

    qs = to_sub(q).reshape(bsz, nh, dil, nb, BAND_BLOCK, hd)
    ks, vs = key_band(k), key_band(v)
    iq = jnp.arange(BAND_BLOCK)[:, None]
    ik = jnp.arange(2 * BAND_BLOCK)[None, :]
    delta = iq - ik + BAND_BLOCK
    k_sub = (jnp.arange(nb)[:, None] * BAND_BLOCK - BAND_BLOCK
             + jnp.arange(2 * BAND_BLOCK)[None, :])
    valid = ((delta >= 0) & (delta <= w_sub))[None] & (k_sub >= 0)[:, None, :]
    bias = -slopes[:, None, None] * (delta * dil).astype(jnp.float32)[None]
    s = jnp.einsum('bhrnqd,bhrnkd->bhrnqk', qs, ks).astype(jnp.float32) * ATTN_SCALE
    s = s + bias[None, :, None, None]
    s = jnp.where(valid[None, None, None], s, -jnp.inf)
    m = jnp.max(s, axis=-1, keepdims=True)
    e = jnp.exp(s - m)
    den = jnp.sum(e, axis=-1, keepdims=True)
    o = jnp.einsum('bhrnqk,bhrnkd->bhrnqd', (e / den).astype(v.dtype), vs)
    lse = (m + jnp.log(den))[..., 0]

    def from_sub(t):
        tail = t.shape[5:]
        t = t.reshape((bsz, nh, dil, Lp) + tail)[:, :, :, :L]
        t = jnp.swapaxes(t, 2, 3)
        return t.reshape((bsz, nh, n_tok) + tail)

    return from_sub(o), from_sub(lse)


def dilated_mixture_attention(q, k, v, slopes):
    outs, lses = [], []
    for window, dil in DIL_PATTERNS:
        o, lse = dilated_band_attention(q, k, v, slopes, window, dil)
        outs.append(o)
        lses.append(lse)
    w = jax.nn.softmax(jnp.stack(lses, axis=0), axis=0)
    return jnp.einsum('pbhs,pbhsd->bhsd', w.astype(q.dtype), jnp.stack(outs, axis=0))


def differential_attention(q, k, v, slopes, lam):
    bsz, nh, n_tok, _, hd = q.shape
    nqb = n_tok // Q_BLOCK
    qb = jnp.moveaxis(q.reshape(bsz, nh, nqb, Q_BLOCK, 2, hd), 2, 0)
    k_pos = jnp.arange(n_tok)

    def block(args):
        q_blk, start = args
        dist = (start + jnp.arange(Q_BLOCK))[:, None] - k_pos[None, :]
        sc = jnp.einsum('bhqcd,bhkcd->bhcqk', q_blk, k).astype(jnp.float32) * ATTN_SCALE
        sc = sc - slopes[None, :, None, None, None] * dist.astype(jnp.float32)
        sc = jnp.where(dist >= 0, sc, -jnp.inf)
        p = jax.nn.softmax(sc, axis=-1)
        a = p[:, :, 0] - lam * p[:, :, 1]
        return jnp.einsum('bhqk,bhkd->bhqd', a.astype(v.dtype), v)

    o = lax.map(block, (qb, jnp.arange(nqb) * Q_BLOCK))
    return jnp.moveaxis(o, 0, 2).reshape(bsz, nh, n_tok, 2 * hd)


def moe_ffn(h, w_router, b_router, w_gate_up, b_gate_up, w_down, b_down):
    n, d = h.shape
    logits = (h @ w_router).astype(jnp.float32) + b_router.astype(jnp.float32)
    top_val, top_idx = lax.top_k(logits, TOP_K)
    gates = jax.nn.softmax(top_val, axis=-1)
    n_assign = n * TOP_K
    flat_e = top_idx.reshape(-1)
    order = jnp.argsort(flat_e)
    e_sorted = flat_e[order]
    tok_sorted = order // TOP_K
    gate_sorted = gates.reshape(-1)[order]
    counts = jnp.zeros((N_EXPERTS,), jnp.int32).at[flat_e].add(1)
    padded = (counts + MOE_BLOCK - 1) // MOE_BLOCK * MOE_BLOCK
    pad_end = jnp.cumsum(padded)
    pad_start = pad_end - padded
    grp_start = jnp.cumsum(counts) - counts
    dest = pad_start[e_sorted] + jnp.arange(n_assign) - grp_start[e_sorted]
    n_blocks = -(-(n_assign + N_EXPERTS * (MOE_BLOCK - 1)) // MOE_BLOCK)
    buf_tok = jnp.zeros((n_blocks * MOE_BLOCK,), jnp.int32).at[dest].set(tok_sorted)
    block_e = jnp.minimum(jnp.searchsorted(pad_end, jnp.arange(n_blocks) * MOE_BLOCK, side='right'),
                          N_EXPERTS - 1)
    xb = h[buf_tok].reshape(n_blocks, MOE_BLOCK, d)

    def expert_block(args):
        xe, e = args
        gu = xe @ w_gate_up[e] + b_gate_up[e]
        g = jnp.minimum(gu[:, :D_FF], SWIGLU_LIMIT)
        u = jnp.clip(gu[:, D_FF:], -SWIGLU_LIMIT, SWIGLU_LIMIT)
        act = (u + 1.0) * (g * jax.nn.sigmoid(SWIGLU_ALPHA * g))
        return act @ w_down[e] + b_down[e]

    yb = lax.map(expert_block, (xb, block_e)).reshape(-1, d)
    y = yb[dest] * gate_sorted[:, None].astype(yb.dtype)
    return jax.ops.segment_sum(y, tok_sorted, num_segments=n)


def hybrid_layer(x, c, w_ada, b_ada, norm1_g, w_in, a_q_norm_g, a_k_norm_g, b_q_norm_g,
                 b_k_norm_g, lambda_q1, lambda_k1, lambda_q2, lambda_k2, diff_norm_g, w_out,
                 norm2_g, w_router, b_router, w_gate_up, b_gate_up, w_down, b_down, lambda_init):
    bsz, n_tok, d = x.shape
    mod = jax.nn.silu(c) @ w_ada + b_ada
    shift1, scale1, gate1, shift2, scale2, gate2 = jnp.split(mod[:, None, :], 6, axis=-1)

    h = rms_norm(x, norm1_g) * (1.0 + scale1) + shift1
    proj = h @ w_in
    aq, ak, av, bq, bk, bv = jnp.split(proj, IN_SPLITS, axis=-1)

    def heads_a(t):
        return jnp.swapaxes(t.reshape(bsz, n_tok, A_HEADS, HEAD_DIM), 1, 2)

    aq = rms_norm(heads_a(aq), a_q_norm_g)
    ak = rms_norm(heads_a(ak), a_k_norm_g)
    o_a = dilated_mixture_attention(aq, ak, heads_a(av), alibi_slopes(A_HEADS))
    o_a = jnp.swapaxes(o_a, 1, 2).reshape(bsz, n_tok, A_WIDTH)

    def heads_b(t):
        return jnp.swapaxes(t.reshape(bsz, n_tok, B_HEADS, 2, HEAD_DIM), 1, 2)

    bq = rms_norm(heads_b(bq), b_q_norm_g)
    bk = rms_norm(heads_b(bk), b_k_norm_g)
    bv = jnp.swapaxes(bv.reshape(bsz, n_tok, B_HEADS, 2 * HEAD_DIM), 1, 2)
    lam = (jnp.exp(jnp.sum(lambda_q1.astype(jnp.float32) * lambda_k1.astype(jnp.float32)))
           - jnp.exp(jnp.sum(lambda_q2.astype(jnp.float32) * lambda_k2.astype(jnp.float32)))
           + lambda_init)
    o_b = differential_attention(bq, bk, bv, alibi_slopes(B_HEADS), lam)
    o_b = rms_norm(o_b, diff_norm_g) * (1.0 - lambda_init)
    o_b = jnp.swapaxes(o_b, 1, 2).reshape(bsz, n_tok, B_WIDTH)

    mixed = jnp.concatenate([o_a, o_b], axis=-1) @ w_out
    x = x + gate1 * mixed

    h2 = rms_norm(x, norm2_g) * (1.0 + scale2) + shift2
    y = moe_ffn(h2.reshape(bsz * n_tok, d), w_router, b_router, w_gate_up, b_gate_up,
                w_down, b_down).reshape(bsz, n_tok, d)
    return x + gate2 * y


def setup_inputs(seed: int = 0) -> dict:
    key = jax.random.key(seed)
    ks = jax.random.split(key, 24)

    def nrm(k, shape, scale):
        return jax.random.normal(k, shape, jnp.float32) * scale

    def gain(k, shape):
        return 1.0 + nrm(k, shape, 0.05)

    L = DEPTH
    return {
        'x': nrm(ks[0], (BATCH, SEQ, D_MODEL), 1.0),
        'c': nrm(ks[1], (BATCH, D_MODEL), 1.0),
        'w_ada': nrm(ks[2], (L, D_MODEL, 6 * D_MODEL), 0.5 * D_MODEL ** -0.5),
        'b_ada': nrm(ks[3], (L, 6 * D_MODEL), 0.02),
        'norm1_g': gain(ks[4], (L, D_MODEL)),
        'w_in': nrm(ks[5], (L, D_MODEL, IN_WIDTH), D_MODEL ** -0.5),
        'a_q_norm_g': gain(ks[6], (L, HEAD_DIM)),
        'a_k_norm_g': gain(ks[7], (L, HEAD_DIM)),
        'b_q_norm_g': gain(ks[8], (L, HEAD_DIM)),
        'b_k_norm_g': gain(ks[9], (L, HEAD_DIM)),
        'lambda_q1': nrm(ks[10], (L, HEAD_DIM), 0.1),
        'lambda_k1': nrm(ks[11], (L, HEAD_DIM), 0.1),
        'lambda_q2': nrm(ks[12], (L, HEAD_DIM), 0.1),
        'lambda_k2': nrm(ks[13], (L, HEAD_DIM), 0.1),
        'diff_norm_g': gain(ks[14], (L, 2 * HEAD_DIM)),
        'w_out': nrm(ks[15], (L, MIX_WIDTH, D_MODEL), MIX_WIDTH ** -0.5),
        'norm2_g': gain(ks[16], (L, D_MODEL)),
        'w_router': nrm(ks[17], (L, D_MODEL, N_EXPERTS), D_MODEL ** -0.5),
        'b_router': nrm(ks[18], (L, N_EXPERTS), 0.01),
        'w_gate_up': nrm(ks[19], (L, N_EXPERTS, D_MODEL, 2 * D_FF), D_MODEL ** -0.5),
        'b_gate_up': nrm(ks[20], (L, N_EXPERTS, 2 * D_FF), 0.01),
        'w_down': nrm(ks[21], (L, N_EXPERTS, D_FF, D_MODEL), D_FF ** -0.5),
        'b_down': nrm(ks[22], (L, N_EXPERTS, D_MODEL), 0.01),
    }


def reference(x, c, w_ada, b_ada, norm1_g, w_in, a_q_norm_g, a_k_norm_g, b_q_norm_g, b_k_norm_g,
              lambda_q1, lambda_k1, lambda_q2, lambda_k2, diff_norm_g, w_out, norm2_g, w_router,
              b_router, w_gate_up, b_gate_up, w_down, b_down):
    for l in range(DEPTH):
        lambda_init = 0.8 - 0.6 * math.exp(-0.3 * l)
        x = hybrid_layer(x, c, w_ada[l], b_ada[l], norm1_g[l], w_in[l], a_q_norm_g[l],
                         a_k_norm_g[l], b_q_norm_g[l], b_k_norm_g[l], lambda_q1[l], lambda_k1[l],
                         lambda_q2[l], lambda_k2[l], diff_norm_g[l], w_out[l], norm2_g[l],
                         w_router[l], b_router[l], w_gate_up[l], b_gate_up[l], w_down[l],
                         b_down[l], lambda_init)
    return x
```

```python
import functools

import numpy as np
import jax
import jax.numpy as jnp
from jax import lax
from jax.experimental import pallas as pl
from jax.experimental.pallas import tpu as pltpu

F32 = jnp.float32
BF16 = jnp.bfloat16
I32 = jnp.int32
U32 = jnp.uint32

D_MODEL = 1024
HEAD_DIM = 64
A_WIDTH = 512
B_WIDTH = 512
A_HEADS = 8
B_HEADS = 4
IN_WIDTH = 3072
DIL_PATTERNS = ((128, 1), (512, 4), (2048, 16))
BAND = 128
N_EXPERTS = 32
TOP_K = 4
D_FF = 1024
SWIGLU_LIMIT = 7.0
SWIGLU_ALPHA = 1.702
RMS_EPS = 1e-6
ATTN_SCALE = HEAD_DIM ** -0.5
LAMBDA_INIT = 0.8 - 0.6 * 1.0

ROW_TILE = 512
DIFF_BLOCK = 512
MOE_TILE = 256
COMBINE_TILE = 256
VMEM_LIMIT = 56 * 1024 * 1024


def _cparams(sem):
    return pltpu.CompilerParams(dimension_semantics=sem, vmem_limit_bytes=VMEM_LIMIT)


def _split_bf16(a):
    hi = a.astype(BF16)
    lo = (a - hi.astype(F32)).astype(BF16)
    return hi, lo


def _dot_nt(a, b):
    return lax.dot_general(a, b, (((1,), (1,)), ((), ())), preferred_element_type=F32)


def _dot(a, b):
    return jnp.dot(a, b, preferred_element_type=F32)


def _ada_kernel(c_ref, w_ref, b_ref, o_ref):
    c = c_ref[...]
    s = c / (1.0 + jnp.exp(-c))
    sh, sl = _split_bf16(s)
    wh, wl = _split_bf16(w_ref[...])
    o_ref[0] = _dot(sh, wh) + _dot(sh, wl) + _dot(sl, wh) + b_ref[0]


def _ada(c, w_ada, b_ada):
    bsz = c.shape[0]
    return pl.pallas_call(
        _ada_kernel,
        grid=(6,),
        in_specs=[
            pl.BlockSpec((bsz, D_MODEL), lambda j: (0, 0)),
            pl.BlockSpec((D_MODEL, D_MODEL), lambda j: (0, j)),
            pl.BlockSpec((1, 1, D_MODEL), lambda j: (j, 0, 0)),
        ],
        out_specs=pl.BlockSpec((1, bsz, D_MODEL), lambda j: (j, 0, 0)),
        out_shape=jax.ShapeDtypeStruct((6, bsz, D_MODEL), F32),
        compiler_params=_cparams(("arbitrary",)),
        name="ada",
    )(c, w_ada, b_ada.reshape(6, 1, D_MODEL))


def _inproj_kernel(x_ref, mod_ref, g1_ref, w_ref, gm_ref, qkg_ref, o_ref):
    x = x_ref[...]
    y = x * lax.rsqrt(jnp.mean(x * x, axis=-1, keepdims=True) + RMS_EPS) * g1_ref[...]
    h = y * (1.0 + mod_ref[1:2, :]) + mod_ref[0:1, :]
    p = _dot(h.astype(BF16), w_ref[...])
    gm = gm_ref[...]

    def head_norm(t, g):
        ss = _dot((t * t).astype(BF16), gm)
        return t * lax.rsqrt(ss * (1.0 / HEAD_DIM) + RMS_EPS) * g

    w = A_WIDTH
    o_ref[:, 0 * w:1 * w] = head_norm(p[:, 0 * w:1 * w], qkg_ref[0:1, :]).astype(BF16)
    o_ref[:, 1 * w:2 * w] = head_norm(p[:, 1 * w:2 * w], qkg_ref[1:2, :]).astype(BF16)
    o_ref[:, 2 * w:3 * w] = p[:, 2 * w:3 * w].astype(BF16)
    o_ref[:, 3 * w:4 * w] = head_norm(p[:, 3 * w:4 * w], qkg_ref[2:3, :]).astype(BF16)
    o_ref[:, 4 * w:5 * w] = head_norm(p[:, 4 * w:5 * w], qkg_ref[3:4, :]).astype(BF16)
    o_ref[:, 5 * w:6 * w] = p[:, 5 * w:6 * w].astype(BF16)


def _inproj(x2, mod_b, norm1_g, w_in, qk_gains, seq):
    n = x2.shape[0]
    t = ROW_TILE
    per_b = seq // t
    head_of_lane = np.arange(A_WIDTH) // HEAD_DIM
    gmat = jnp.asarray(head_of_lane[:, None] == head_of_lane[None, :], BF16)
    return pl.pallas_call(
        _inproj_kernel,
        grid=(n // t,),
        in_specs=[
            pl.BlockSpec((t, D_MODEL), lambda i: (i, 0)),
            pl.BlockSpec((None, 6, D_MODEL), lambda i: (i // per_b, 0, 0)),
            pl.BlockSpec((1, D_MODEL), lambda i: (0, 0)),
            pl.BlockSpec((D_MODEL, IN_WIDTH), lambda i: (0, 0)),
            pl.BlockSpec((A_WIDTH, A_WIDTH), lambda i: (0, 0)),
            pl.BlockSpec((4, A_WIDTH), lambda i: (0, 0)),
        ],
        out_specs=pl.BlockSpec((t, IN_WIDTH), lambda i: (i, 0)),
        out_shape=jax.ShapeDtypeStruct((n, IN_WIDTH), BF16),
        compiler_params=_cparams(("parallel",)),
        name="inproj",
    )(x2, mod_b, norm1_g.reshape(1, D_MODEL), w_in.astype(BF16), gmat, qk_gains)


def _dilated_bias_table():
    iq = np.arange(BAND)[:, None]
    ik = np.arange(2 * BAND)[None, :]
    delta = iq - ik + BAND
    in_band = (delta >= 0) & (delta <= BAND)
    slopes = 2.0 ** (-8.0 * np.arange(1, A_HEADS + 1) / A_HEADS)
    tbl = np.zeros((len(DIL_PATTERNS), 2, A_HEADS, BAND, 2 * BAND), np.float32)
    for p, (_, dil) in enumerate(DIL_PATTERNS):
        for first in range(2):
            valid = in_band & ((ik >= BAND) if first else True)
            for h in range(A_HEADS):
                tbl[p, first, h] = np.where(valid, -slopes[h] * (delta * dil), -np.inf)
    return jnp.asarray(tbl)


def _dilated_kernel(tbl_ref, *refs, n_blocks):
    ins, outs = refs[:15], refs[15:]
    n = pl.program_id(1)
    lane = lax.broadcasted_iota(I32, (BAND, 2 * HEAD_DIM), 1)
    low = lane < HEAD_DIM
    for p, (_, dil) in enumerate(DIL_PATTERNS):
        per_res = n_blocks // dil
        first = (n % per_res == 0).astype(I32)
        q_ref, kp_ref, kc_ref, vp_ref, vc_ref = ins[5 * p:5 * p + 5]
        o_ref, l_ref = outs[2 * p:2 * p + 2]
        for j in range(A_HEADS // 2):
            cs = slice(2 * HEAD_DIM * j, 2 * HEAD_DIM * (j + 1))
            q2 = q_ref[:, cs]
            k2 = jnp.concatenate([kp_ref[:, cs], kc_ref[:, cs]], axis=0)
            v2 = jnp.concatenate([vp_ref[:, cs], vc_ref[:, cs]], axis=0)
            res = []
            for hh in range(2):
                qh = jnp.where(low if hh == 0 else ~low, q2, jnp.zeros_like(q2))
                s = _dot_nt(qh, k2) + tbl_ref[p, first, 2 * j + hh]
                m = jnp.max(s, axis=-1, keepdims=True)
                e = jnp.exp(s - m)
                den = jnp.sum(e, axis=-1, keepdims=True)
                r = _dot(e.astype(BF16), v2)
                res.append((r / den, jnp.broadcast_to(m + jnp.log(den), r.shape)))
            o_ref[:, cs] = jnp.where(low, res[0][0], res[1][0])
            l_ref[:, cs] = jnp.where(low, res[0][1], res[1][1])


def _dilated(proj, bsz, seq):
    n_blocks = seq // BAND
    in_specs = [pl.BlockSpec((len(DIL_PATTERNS), 2, A_HEADS, BAND, 2 * BAND), lambda b, n: (0, 0, 0, 0, 0))]
    args = [_dilated_bias_table()]
    out_specs, out_shapes = [], []
    for _, dil in DIL_PATTERNS:
        per_res = n_blocks // dil
        view = proj.reshape(bsz, seq // dil, dil * IN_WIDTH)
        blk = (None, BAND, A_WIDTH)

        def cur(col, per_res=per_res):
            return lambda b, n: (b, n % per_res, (n // per_res) * 6 + col)

        def prev(col, per_res=per_res):
            return lambda b, n: (b, jnp.maximum(n % per_res - 1, 0), (n // per_res) * 6 + col)

        in_specs += [pl.BlockSpec(blk, cur(0)), pl.BlockSpec(blk, prev(1)), pl.BlockSpec(blk, cur(1)),
                     pl.BlockSpec(blk, prev(2)), pl.BlockSpec(blk, cur(2))]
        args += [view] * 5
        o_map = lambda b, n, per_res=per_res: (b, n % per_res, n // per_res)
        out_specs += [pl.BlockSpec(blk, o_map), pl.BlockSpec(blk, o_map)]
        out_shapes += [jax.ShapeDtypeStruct((bsz, seq // dil, dil * A_WIDTH), F32)] * 2
    outs = pl.pallas_call(
        functools.partial(_dilated_kernel, n_blocks=n_blocks),
        grid=(bsz, n_blocks),
        in_specs=in_specs,
        out_specs=out_specs,
        out_shape=out_shapes,
        compiler_params=_cparams(("parallel", "parallel")),
        name="dilated",
    )(*args)
    return [o.reshape(bsz * seq, A_WIDTH) for o in outs]


def _diff_kernel(sc_ref, q_ref, k_ref, v_ref, g_ref, o_ref, m_sc, l_sc, acc_sc):
    t = DIFF_BLOCK
    h = pl.program_id(1)
    qi = pl.program_id(2)
    slope = jnp.full((1, 1), sc_ref[1 + h], F32)
    lam = sc_ref[0]
    q = q_ref[...]
    lane = lax.broadcasted_iota(I32, q.shape, 1)
    zero = jnp.zeros_like(q)
    qs = jnp.concatenate([jnp.where(lane < HEAD_DIM, q, zero), jnp.where(lane >= HEAD_DIM, q, zero)], axis=0)
    rel = lax.broadcasted_iota(I32, (t, t), 0) - lax.broadcasted_iota(I32, (t, t), 1)
    rel2 = jnp.concatenate([rel, rel], axis=0)
    bias = -slope * rel2.astype(F32)
    m_sc[...] = jnp.full(m_sc.shape, -jnp.inf, F32)
    l_sc[...] = jnp.zeros(l_sc.shape, F32)
    acc_sc[...] = jnp.zeros(acc_sc.shape, F32)

    def block(n, diagonal):
        start = pl.multiple_of(n * t, t)
        kb = k_ref[pl.ds(start, t), :]
        vb = v_ref[pl.ds(start, t), :]
        s = _dot_nt(qs, kb) + bias
        if diagonal:
            s = jnp.where(rel2 >= 0, s, -jnp.inf)
        c = -slope * (jnp.full((1, 1), (qi - n) * t, I32)).astype(F32)
        m_prev = m_sc[...]
        m_new = jnp.maximum(m_prev, jnp.max(s, axis=-1, keepdims=True) + c)
        alpha = jnp.exp(m_prev - m_new)
        e = jnp.exp(s - (m_new - c))
        l_sc[...] = alpha * l_sc[...] + jnp.sum(e, axis=-1, keepdims=True)
        acc_sc[...] = alpha * acc_sc[...] + _dot(e.astype(BF16), vb)
        m_sc[...] = m_new

    def body(n, carry):
        block(n, False)
        return carry

    lax.fori_loop(0, qi, body, 0)
    block(qi, True)
    o = acc_sc[...] / l_sc[...]
    o = o[:t] - lam * o[t:]
    o = o * lax.rsqrt(jnp.mean(o * o, axis=-1, keepdims=True) + RMS_EPS) * g_ref[...]
    o_ref[...] = (o * (1.0 - LAMBDA_INIT)).astype(BF16)


def _diff(proj3, scalars, diff_norm_g, bsz, seq):
    t = DIFF_BLOCK
    lanes = 2 * HEAD_DIM
    q_col, k_col, v_col = 3 * A_WIDTH // lanes, 4 * A_WIDTH // lanes, 5 * A_WIDTH // lanes
    return pl.pallas_call(
        _diff_kernel,
        grid=(bsz, B_HEADS, seq // t),
        in_specs=[
            pl.BlockSpec(memory_space=pltpu.SMEM),
            pl.BlockSpec((None, t, lanes), lambda b, h, i: (b, i, q_col + h)),
            pl.BlockSpec((None, seq, lanes), lambda b, h, i: (b, 0, k_col + h)),
            pl.BlockSpec((None, seq, lanes), lambda b, h, i: (b, 0, v_col + h)),
            pl.BlockSpec((1, lanes), lambda b, h, i: (0, 0)),
        ],
        out_specs=pl.BlockSpec((None, t, lanes), lambda b, h, i: (b, i, h)),
        out_shape=jax.ShapeDtypeStruct((bsz, seq, B_WIDTH), BF16),
        scratch_shapes=[pltpu.VMEM((2 * t, 1), F32), pltpu.VMEM((2 * t, 1), F32), pltpu.VMEM((2 * t, lanes), F32)],
        compiler_params=_cparams(("parallel", "parallel", "parallel")),
        name="diff",
    )(scalars, proj3, proj3, proj3, diff_norm_g.reshape(1, lanes))


def _pack_rows(h):
    half = D_MODEL // 2
    lo = lax.bitcast_convert_type(h[:, :half].astype(BF16).astype(F32), U32)
    hi = lax.bitcast_convert_type(h[:, half:].astype(BF16).astype(F32), U32)
    return (lo >> 16) | (hi & jnp.uint32(0xFFFF0000))


def _unpack_rows(w):
    lo = lax.bitcast_convert_type(w << 16, F32).astype(BF16)
    hi = lax.bitcast_convert_type(w & jnp.uint32(0xFFFF0000), F32).astype(BF16)
    return lo, hi


def _router_kernel(o0, l0, o1, l1, o2, l2, ob_ref, x_ref, mod_ref, wout_ref, g2_ref, wrh_ref, wrl_ref, br_ref,
                   tri_ref, x1_ref, hp_ref, idx_ref, gate_ref, rank_ref, cnt_ref, cnt_sc):
    i = pl.program_id(0)

    @pl.when(i == 0)
    def _():
        cnt_sc[...] = jnp.zeros(cnt_sc.shape, F32)

    ls = [l0[...], l1[...], l2[...]]
    mx = jnp.maximum(jnp.maximum(ls[0], ls[1]), ls[2])
    ws = [jnp.exp(l - mx) for l in ls]
    den = ws[0] + ws[1] + ws[2]
    oa = (ws[0] * o0[...] + ws[1] * o1[...] + ws[2] * o2[...]) / den
    mixed = _dot(oa.astype(BF16), wout_ref[:A_WIDTH, :]) + _dot(ob_ref[...], wout_ref[A_WIDTH:, :])
    x1 = x_ref[...] + mod_ref[2:3, :] * mixed
    x1_ref[...] = x1
    y = x1 * lax.rsqrt(jnp.mean(x1 * x1, axis=-1, keepdims=True) + RMS_EPS) * g2_ref[...]
    h2 = y * (1.0 + mod_ref[4:5, :]) + mod_ref[3:4, :]
    hp_ref[...] = _pack_rows(h2)

    hh, hl = _split_bf16(h2)
    wrh = wrh_ref[...]
    logits = _dot_nt(wrh, hh) + _dot_nt(wrh, hl) + _dot_nt(wrl_ref[...], hh) + br_ref[...]
    t = logits.shape[1]
    eid = lax.broadcasted_iota(I32, (N_EXPERTS, t), 0)
    vals, idxs, hots = [], [], []
    cur = logits
    for _ in range(TOP_K):
        v = jnp.max(cur, axis=0, keepdims=True)
        ik = jnp.min(jnp.where(cur == v, eid, N_EXPERTS), axis=0, keepdims=True)
        hot = eid == ik
        vals.append(v)
        idxs.append(ik)
        hots.append(hot)
        cur = jnp.where(hot, -jnp.inf, cur)
    es = [jnp.exp(v - vals[0]) for v in vals]
    esum = es[0] + es[1] + es[2] + es[3]
    gate_ref[...] = jnp.concatenate([e / esum for e in es], axis=0)
    idx_ref[...] = jnp.concatenate(idxs, axis=0)

    sel = jnp.where(hots[0] | hots[1] | hots[2] | hots[3], 1.0, 0.0)
    before = _dot(sel.astype(BF16), tri_ref[...]) + cnt_sc[...]
    ranks = [jnp.sum(jnp.where(hot, before, 0.0), axis=0, keepdims=True) for hot in hots]
    rank_ref[...] = jnp.concatenate(ranks, axis=0).astype(I32)
    total = cnt_sc[...] + jnp.sum(sel, axis=1, keepdims=True)
    cnt_sc[...] = total
    cnt_ref[...] = jnp.broadcast_to(total, cnt_ref.shape)


def _router(dil_outs, o_b, x2, mod_b, w_out, norm2_g, w_router, b_router, seq):
    n = x2.shape[0]
    t = ROW_TILE
    per_b = seq // t
    wr_t = w_router.T
    wrh = wr_t.astype(BF16)
    wrl = (wr_t - wrh.astype(F32)).astype(BF16)
    tri = jnp.asarray(np.arange(t)[:, None] < np.arange(t)[None, :], BF16)
    row = lambda w: pl.BlockSpec((t, w), lambda i: (i, 0))
    full = lambda a, b: pl.BlockSpec((a, b), lambda i: (0, 0))
    tok = lambda: pl.BlockSpec((TOP_K, t), lambda i: (0, i))
    return pl.pallas_call(
        _router_kernel,
        grid=(n // t,),
        in_specs=[row(A_WIDTH)] * 6 + [
            row(B_WIDTH), row(D_MODEL),
            pl.BlockSpec((None, 6, D_MODEL), lambda i: (i // per_b, 0, 0)),
            full(D_MODEL, D_MODEL), full(1, D_MODEL), full(N_EXPERTS, D_MODEL), full(N_EXPERTS, D_MODEL),
            full(N_EXPERTS, 1), full(t, t),
        ],
        out_specs=[row(D_MODEL), row(D_MODEL // 2), tok(), tok(), tok(), full(N_EXPERTS, 128)],
        out_shape=[
            jax.ShapeDtypeStruct((n, D_MODEL), F32),
            jax.ShapeDtypeStruct((n, D_MODEL // 2), U32),
            jax.ShapeDtypeStruct((TOP_K, n), I32),
            jax.ShapeDtypeStruct((TOP_K, n), F32),
            jax.ShapeDtypeStruct((TOP_K, n), I32),
            jax.ShapeDtypeStruct((N_EXPERTS, 128), F32),
        ],
        scratch_shapes=[pltpu.VMEM((N_EXPERTS, 1), F32)],
        compiler_params=_cparams(("arbitrary",)),
        name="router",
    )(*dil_outs, o_b, x2, mod_b, w_out.astype(BF16), norm2_g.reshape(1, D_MODEL), wrh, wrl,
      b_router.reshape(N_EXPERTS, 1), tri)


def _index_tile(dest_hbm, idx_smem, isem, i, n_steps):
    def copy(step, slot):
        return pltpu.make_async_copy(dest_hbm.at[step], idx_smem.at[slot], isem.at[slot])

    @pl.when(i == 0)
    def _():
        copy(0, 0).start()

    @pl.when(i + 1 < n_steps)
    def _():
        copy(i + 1, (i + 1) % 2).start()

    slot = i % 2
    copy(i, slot).wait()
    return slot


def _scatter_kernel(dest_hbm, h_ref, xs_in, xs_hbm, idx_smem, isem, sem, *, tile):
    del xs_in
    i = pl.program_id(0)
    slot = _index_tile(dest_hbm, idx_smem, isem, i, pl.num_programs(0))

    def copy(t, k):
        d = idx_smem[slot, k * tile + t]
        return pltpu.make_async_copy(h_ref.at[pl.ds(t, 1)], xs_hbm.at[pl.ds(d, 1)], sem)

    def start(t, carry):
        for k in range(TOP_K):
            copy(t, k).start()
        return carry

    def wait(t, carry):
        for k in range(TOP_K):
            copy(t, k).wait()
        return carry

    lax.fori_loop(0, tile, start, 0)
    lax.fori_loop(0, tile, wait, 0)


def _scatter(dest, h_packed, n_rows):
    n = h_packed.shape[0]
    t = ROW_TILE
    steps = n // t
    dest_tiles = dest.reshape(TOP_K, steps, t).transpose(1, 0, 2).reshape(steps, TOP_K * t)
    return pl.pallas_call(
        functools.partial(_scatter_kernel, tile=t),
        grid=(steps,),
        in_specs=[
            pl.BlockSpec(memory_space=pl.ANY),
            pl.BlockSpec((t, D_MODEL // 2), lambda i: (i, 0)),
            pl.BlockSpec(memory_space=pl.ANY),
        ],
        out_specs=pl.BlockSpec(memory_space=pl.ANY),
        out_shape=jax.ShapeDtypeStruct((n_rows, D_MODEL // 2), U32),
        scratch_shapes=[pltpu.SMEM((2, TOP_K * t), I32), pltpu.SemaphoreType.DMA((2,)), pltpu.SemaphoreType.DMA(())],
        input_output_aliases={2: 0},
        compiler_params=_cparams(("arbitrary",)),
        name="scatter",
    )(dest_tiles, h_packed, jnp.zeros((n_rows, D_MODEL // 2), U32))


def _experts_kernel(be_ref, nu_ref, x_ref, wgu_ref, bgu_ref, wd_ref, bd_ref, y_ref, wgu_sc, wd_sc):
    b = pl.program_id(0)

    @pl.when(b < nu_ref[0])
    def _():
        prev = be_ref[jnp.maximum(b - 1, 0)]

        @pl.when((b == 0) | (be_ref[b] != prev))
        def _():
            rows = 128

            def cast(r, carry):
                s = pl.multiple_of(r * rows, rows)
                wgu_sc[pl.ds(s, rows), :] = wgu_ref[pl.ds(s, rows), :].astype(BF16)
                wd_sc[pl.ds(s, rows), :] = wd_ref[pl.ds(s, rows), :].astype(BF16)
                return carry

            lax.fori_loop(0, D_MODEL // rows, cast, 0)

        half = D_MODEL // 2
        lo, hi = _unpack_rows(x_ref[...])
        gu = _dot(lo, wgu_sc[:half, :]) + _dot(hi, wgu_sc[half:, :]) + bgu_ref[...]
        g = jnp.minimum(gu[:, :D_FF], SWIGLU_LIMIT)
        u = jnp.clip(gu[:, D_FF:], -SWIGLU_LIMIT, SWIGLU_LIMIT)
        act = (u + 1.0) * (g / (1.0 + jnp.exp(-SWIGLU_ALPHA * g)))
        y_ref[...] = _dot(act.astype(BF16), wd_sc[...]) + bd_ref[...]

    @pl.when(b >= nu_ref[0])
    def _():
        y_ref[...] = jnp.zeros(y_ref.shape, F32)


def _experts(block_e, n_used, xs, w_gate_up, b_gate_up, w_down, b_down):
    n_rows = xs.shape[0]
    tm = MOE_TILE
    n_blocks = n_rows // tm
    blk = lambda b, be, nu: (jnp.minimum(b, nu[0] - 1), 0)
    exp3 = lambda b, be, nu: (be[b], 0, 0)
    return pl.pallas_call(
        _experts_kernel,
        grid_spec=pltpu.PrefetchScalarGridSpec(
            num_scalar_prefetch=2,
            grid=(n_blocks,),
            in_specs=[
                pl.BlockSpec((tm, D_MODEL // 2), blk),
                pl.BlockSpec((None, D_MODEL, 2 * D_FF), exp3),
                pl.BlockSpec((None, 1, 2 * D_FF), exp3),
                pl.BlockSpec((None, D_FF, D_MODEL), exp3),
                pl.BlockSpec((None, 1, D_MODEL), exp3),
            ],
            out_specs=pl.BlockSpec((tm, D_MODEL), lambda b, be, nu: (b, 0)),
            scratch_shapes=[pltpu.VMEM((D_MODEL, 2 * D_FF), BF16), pltpu.VMEM((D_FF, D_MODEL), BF16)],
        ),
        out_shape=jax.ShapeDtypeStruct((n_rows, D_MODEL), F32),
        compiler_params=_cparams(("arbitrary",)),
        name="experts",
    )(block_e, n_used, xs, w_gate_up, b_gate_up.reshape(N_EXPERTS, 1, 2 * D_FF), w_down,
      b_down.reshape(N_EXPERTS, 1, D_MODEL))


def _combine_kernel(dest_hbm, yb_hbm, gate_ref, x1_ref, mod_ref, o_ref, buf, idx_smem, isem, sem, *, tile):
    i = pl.program_id(0)
    slot = _index_tile(dest_hbm, idx_smem, isem, i, pl.num_programs(0))

    def copy(t, k):
        d = idx_smem[slot, k * tile + t]
        return pltpu.make_async_copy(yb_hbm.at[pl.ds(d, 1)], buf.at[k, pl.ds(t, 1)], sem)

    def start(t, carry):
        for k in range(TOP_K):
            copy(t, k).start()
        return carry

    def wait(t, carry):
        for k in range(TOP_K):
            copy(t, k).wait()
        return carry

    lax.fori_loop(0, tile, start, 0)
    lax.fori_loop(0, tile, wait, 0)
    g = gate_ref[...]
    y = buf[0] * g[:, 0:1]
    for k in range(1, TOP_K):
        y = y + buf[k] * g[:, k:k + 1]
    o_ref[...] = x1_ref[...] + mod_ref[5:6, :] * y


def _combine(dest, yb, gates_t, x1, mod_b, seq):
    n = x1.shape[0]
    t = COMBINE_TILE
    steps = n // t
    per_b = seq // t
    dest_tiles = dest.reshape(TOP_K, steps, t).transpose(1, 0, 2).reshape(steps, TOP_K * t)
    return pl.pallas_call(
        functools.partial(_combine_kernel, tile=t),
        grid=(steps,),
        in_specs=[
            pl.BlockSpec(memory_space=pl.ANY),
            pl.BlockSpec(memory_space=pl.ANY),
            pl.BlockSpec((t, TOP_K), lambda i: (i, 0)),
            pl.BlockSpec((t, D_MODEL), lambda i: (i, 0)),
            pl.BlockSpec((None, 6, D_MODEL), lambda i: (i // per_b, 0, 0)),
        ],
        out_specs=pl.BlockSpec((t, D_MODEL), lambda i: (i, 0)),
        out_shape=jax.ShapeDtypeStruct((n, D_MODEL), F32),
        scratch_shapes=[pltpu.VMEM((TOP_K, t, D_MODEL), F32), pltpu.SMEM((2, TOP_K * t), I32),
                        pltpu.SemaphoreType.DMA((2,)), pltpu.SemaphoreType.DMA(())],
        compiler_params=_cparams(("arbitrary",)),
        name="combine",
    )(dest_tiles, yb, gates_t, x1, mod_b)


def _layer(x, c, w_ada, b_ada, norm1_g, w_in, a_q_norm_g, a_k_norm_g, b_q_norm_g, b_k_norm_g, lambda_q1,
           lambda_k1, lambda_q2, lambda_k2, diff_norm_g, w_out, norm2_g, w_router, b_router, w_gate_up,
           b_gate_up, w_down, b_down):
    bsz, seq, _ = x.shape
    n = bsz * seq
    x2 = x.reshape(n, D_MODEL)
    mod_b = _ada(c, w_ada, b_ada).transpose(1, 0, 2)

    qk_gains = jnp.stack([
        jnp.tile(a_q_norm_g, A_HEADS) * ATTN_SCALE, jnp.tile(a_k_norm_g, A_HEADS),
        jnp.tile(b_q_norm_g, 2 * B_HEADS) * ATTN_SCALE, jnp.tile(b_k_norm_g, 2 * B_HEADS)])
    proj = _inproj(x2, mod_b, norm1_g, w_in, qk_gains, seq)

    dil_outs = _dilated(proj, bsz, seq)
    lam = (jnp.exp(jnp.sum(lambda_q1 * lambda_k1)) - jnp.exp(jnp.sum(lambda_q2 * lambda_k2)) + LAMBDA_INIT)
    slopes_b = 2.0 ** (-8.0 * np.arange(1, B_HEADS + 1) / B_HEADS)
    scalars = jnp.concatenate([lam.reshape(1), jnp.asarray(slopes_b, F32)]).astype(F32)
    o_b = _diff(proj.reshape(bsz, seq, IN_WIDTH), scalars, diff_norm_g, bsz, seq).reshape(n, B_WIDTH)

    x1, h_packed, idx, gates, rank, cnt = _router(dil_outs, o_b, x2, mod_b, w_out, norm2_g, w_router, b_router, seq)

    counts = cnt[:, 0].astype(I32)
    padded = (counts + MOE_TILE - 1) // MOE_TILE * MOE_TILE
    pad_end = jnp.cumsum(padded)
    pad_start = pad_end - padded
    dest = pad_start[idx] + rank
    n_blocks = (n * TOP_K + N_EXPERTS * (MOE_TILE - 1)) // MOE_TILE
    n_used = (pad_end[-1] // MOE_TILE).astype(I32)
    blk_ids = jnp.minimum(jnp.arange(n_blocks, dtype=I32), n_used - 1)
    block_e = jnp.minimum(jnp.searchsorted(pad_end, blk_ids * MOE_TILE, side="right"), N_EXPERTS - 1).astype(I32)

    xs = _scatter(dest, h_packed, n_blocks * MOE_TILE)
    yb = _experts(block_e, n_used.reshape(1), xs, w_gate_up, b_gate_up, w_down, b_down)
    out = _combine(dest, yb, gates.T, x1, mod_b, seq)
    return out.reshape(bsz, seq, D_MODEL)


def kernel(x, c, w_ada, b_ada, norm1_g, w_in, a_q_norm_g, a_k_norm_g, b_q_norm_g, b_k_norm_g, lambda_q1, lambda_k1,
           lambda_q2, lambda_k2, diff_norm_g, w_out, norm2_g, w_router, b_router, w_gate_up, b_gate_up, w_down,
           b_down):
    args = (w_ada, b_ada, norm1_g, w_in, a_q_norm_g, a_k_norm_g, b_q_norm_g, b_k_norm_g, lambda_q1, lambda_k1,
            lambda_q2, lambda_k2, diff_norm_g, w_out, norm2_g, w_router, b_router, w_gate_up, b_gate_up, w_down,
            b_down)
    return _layer(x, c, *[a[0] for a in args])
```

```python
import functools

import numpy as np
import jax
import jax.numpy as jnp
from jax import lax
from jax.experimental import pallas as pl
from jax.experimental.pallas import tpu as pltpu

F32 = jnp.float32
BF16 = jnp.bfloat16
I32 = jnp.int32
U32 = jnp.uint32

D_MODEL = 1024
HEAD_DIM = 64
A_WIDTH = 512
B_WIDTH = 512
A_HEADS = 8
B_HEADS = 4
IN_WIDTH = 3072
DIL_PATTERNS = ((128, 1), (512, 4), (2048, 16))
BAND = 128
N_EXPERTS = 32
TOP_K = 4
D_FF = 1024
SWIGLU_LIMIT = 7.0
SWIGLU_ALPHA = 1.702
RMS_EPS = 1e-6
ATTN_SCALE = HEAD_DIM ** -0.5
LAMBDA_INIT = 0.8 - 0.6 * 1.0

ROW_TILE = 512
DIFF_BLOCK = 512
MOE_TILE = 256
EXTRA_LANES = 128
ROW_WIDTH = D_MODEL + EXTRA_LANES
SEG_ALIGN = 16
SEG_SIZES = (512, 256, 128, 64, 32, 16)
SLOT_CHUNK = 512
LOCAL_SLOTS = 2560
assert ROW_TILE == DIFF_BLOCK == SEG_SIZES[0]
assert LOCAL_SLOTS >= ROW_TILE * TOP_K + N_EXPERTS * (SEG_ALIGN - 1) and LOCAL_SLOTS % SLOT_CHUNK == 0
VMEM_LIMIT = 56 * 1024 * 1024


def _cparams(sem):
    return pltpu.CompilerParams(dimension_semantics=sem, vmem_limit_bytes=VMEM_LIMIT)


def _split_bf16(a):
    hi = a.astype(BF16)
    lo = (a - hi.astype(F32)).astype(BF16)
    return hi, lo


def _dot_nt(a, b):
    return lax.dot_general(a, b, (((1,), (1,)), ((), ())), preferred_element_type=F32)


def _dot(a, b):
    return jnp.dot(a, b, preferred_element_type=F32)


def _ada_kernel(c_ref, w_ref, b_ref, o_ref):
    c = c_ref[...]
    s = c / (1.0 + jnp.exp(-c))
    sh, sl = _split_bf16(s)
    wh, wl = _split_bf16(w_ref[...])
    o_ref[0] = _dot(sh, wh) + _dot(sh, wl) + _dot(sl, wh) + b_ref[0]


def _ada(c, w_ada, b_ada):
    bsz = c.shape[0]
    return pl.pallas_call(
        _ada_kernel,
        grid=(6,),
        in_specs=[
            pl.BlockSpec((bsz, D_MODEL), lambda j: (0, 0)),
            pl.BlockSpec((D_MODEL, D_MODEL), lambda j: (0, j)),
            pl.BlockSpec((1, 1, D_MODEL), lambda j: (j, 0, 0)),
        ],
        out_specs=pl.BlockSpec((1, bsz, D_MODEL), lambda j: (j, 0, 0)),
        out_shape=jax.ShapeDtypeStruct((6, bsz, D_MODEL), F32),
        compiler_params=_cparams(("arbitrary",)),
        name="ada",
    )(c, w_ada, b_ada.reshape(6, 1, D_MODEL))


def _inproj_kernel(x_ref, mod_ref, g1_ref, wn_ref, wt_ref, gm_ref, qkg_ref, bqg_ref, a_ref, bk_ref, bqt_ref, bvt_ref):
    x = x_ref[...]
    y = x * lax.rsqrt(jnp.mean(x * x, axis=-1, keepdims=True) + RMS_EPS) * g1_ref[...]
    h = (y * (1.0 + mod_ref[1:2, :]) + mod_ref[0:1, :]).astype(BF16)
    p = _dot(h, wn_ref[...])
    gm = gm_ref[...]

    def head_norm(t, g):
        ss = _dot((t * t).astype(BF16), gm)
        return t * lax.rsqrt(ss * (1.0 / HEAD_DIM) + RMS_EPS) * g

    w = A_WIDTH
    a_ref[:, 0 * w:1 * w] = head_norm(p[:, 0 * w:1 * w], qkg_ref[0:1, :]).astype(BF16)
    a_ref[:, 1 * w:2 * w] = head_norm(p[:, 1 * w:2 * w], qkg_ref[1:2, :]).astype(BF16)
    a_ref[:, 2 * w:3 * w] = p[:, 2 * w:3 * w].astype(BF16)
    bk_ref[...] = head_norm(p[:, 3 * w:4 * w], qkg_ref[2:3, :]).astype(BF16)

    pt = _dot_nt(wt_ref[...], h)
    t = pt.shape[1]
    bq = pt[:w].reshape(w // HEAD_DIM, HEAD_DIM, t)
    ss = jnp.sum(bq * bq, axis=1, keepdims=True)
    bq = (bq * lax.rsqrt(ss * (1.0 / HEAD_DIM) + RMS_EPS)).reshape(w, t) * bqg_ref[...]
    bqt_ref[...] = bq.astype(BF16)
    bvt_ref[...] = pt[w:].astype(BF16)


def _inproj(x2, mod_b, norm1_g, w_in, qk_gains, bq_gain, bsz, seq):
    n = x2.shape[0]
    t = ROW_TILE
    per_b = seq // t
    w = A_WIDTH
    head_of_lane = np.arange(w) // HEAD_DIM
    gmat = jnp.asarray(head_of_lane[:, None] == head_of_lane[None, :], BF16)
    w_bf = w_in.astype(BF16)
    w_nat = jnp.concatenate([w_bf[:, :3 * w], w_bf[:, 4 * w:5 * w]], axis=1)
    w_tr = jnp.concatenate([w_bf[:, 3 * w:4 * w], w_bf[:, 5 * w:]], axis=1).T
    tr_spec = pl.BlockSpec((None, None, w, t), lambda i: (i // per_b, i % per_b, 0, 0))
    tr_shape = jax.ShapeDtypeStruct((bsz, per_b, w, t), BF16)
    return pl.pallas_call(
        _inproj_kernel,
        grid=(n // t,),
        in_specs=[
            pl.BlockSpec((t, D_MODEL), lambda i: (i, 0)),
            pl.BlockSpec((None, 6, D_MODEL), lambda i: (i // per_b, 0, 0)),
            pl.BlockSpec((1, D_MODEL), lambda i: (0, 0)),
            pl.BlockSpec((D_MODEL, 4 * w), lambda i: (0, 0)),
            pl.BlockSpec((2 * w, D_MODEL), lambda i: (0, 0)),
            pl.BlockSpec((w, w), lambda i: (0, 0)),
            pl.BlockSpec((3, w), lambda i: (0, 0)),
            pl.BlockSpec((w, 1), lambda i: (0, 0)),
        ],
        out_specs=[pl.BlockSpec((t, 3 * w), lambda i: (i, 0)), pl.BlockSpec((t, w), lambda i: (i, 0)), tr_spec, tr_spec],
        out_shape=[jax.ShapeDtypeStruct((n, 3 * w), BF16), jax.ShapeDtypeStruct((n, w), BF16), tr_shape, tr_shape],
        compiler_params=_cparams(("parallel",)),
        name="inproj",
    )(x2, mod_b, norm1_g.reshape(1, D_MODEL), w_nat, w_tr, gmat, qk_gains, bq_gain)


def _dilated_bias_table():
    iq = np.arange(BAND)[:, None]
    ik = np.arange(2 * BAND)[None, :]
    delta = iq - ik + BAND
    in_band = (delta >= 0) & (delta <= BAND)
    slopes = 2.0 ** (-8.0 * np.arange(1, A_HEADS + 1) / A_HEADS)
    tbl = np.zeros((len(DIL_PATTERNS), 2, A_HEADS, BAND, 2 * BAND), np.float32)
    for p, (_, dil) in enumerate(DIL_PATTERNS):
        for first in range(2):
            valid = in_band & ((ik >= BAND) if first else True)
            for h in range(A_HEADS):
                tbl[p, first, h] = np.where(valid, -slopes[h] * (delta * dil), -np.inf)
    return jnp.asarray(tbl)


def _dilated_kernel(tbl_ref, *refs, n_blocks):
    ins, outs = refs[:15], refs[15:]
    n = pl.program_id(1)
    lane = lax.broadcasted_iota(I32, (BAND, 2 * HEAD_DIM), 1)
    low = lane < HEAD_DIM
    for p, (_, dil) in enumerate(DIL_PATTERNS):
        per_res = n_blocks // dil
        first = (n % per_res == 0).astype(I32)
        q_ref, kp_ref, kc_ref, vp_ref, vc_ref = ins[5 * p:5 * p + 5]
        o_ref, l_ref = outs[2 * p:2 * p + 2]
        for j in range(A_HEADS // 2):
            cs = slice(2 * HEAD_DIM * j, 2 * HEAD_DIM * (j + 1))
            q2 = q_ref[:, cs]
            k2 = jnp.concatenate([kp_ref[:, cs], kc_ref[:, cs]], axis=0)
            v2 = jnp.concatenate([vp_ref[:, cs], vc_ref[:, cs]], axis=0)
            res = []
            for hh in range(2):
                qh = jnp.where(low if hh == 0 else ~low, q2, jnp.zeros_like(q2))
                s = _dot_nt(qh, k2) + tbl_ref[p, first, 2 * j + hh]
                m = jnp.max(s, axis=-1, keepdims=True)
                e = jnp.exp(s - m)
                den = jnp.sum(e, axis=-1, keepdims=True)
                r = _dot(e.astype(BF16), v2)
                res.append((r / den, jnp.broadcast_to(m + jnp.log(den), r.shape)))
            o_ref[:, cs] = jnp.where(low, res[0][0], res[1][0])
            l_ref[:, cs] = jnp.where(low, res[0][1], res[1][1])


def _dilated(proj, bsz, seq):
    n_blocks = seq // BAND
    in_specs = [pl.BlockSpec((len(DIL_PATTERNS), 2, A_HEADS, BAND, 2 * BAND), lambda b, n: (0, 0, 0, 0, 0))]
    args = [_dilated_bias_table()]
    out_specs, out_shapes = [], []
    for _, dil in DIL_PATTERNS:
        per_res = n_blocks // dil
        view = proj.reshape(bsz, seq // dil, dil * 3 * A_WIDTH)
        blk = (None, BAND, A_WIDTH)

        def cur(col, per_res=per_res):
            return lambda b, n: (b, n % per_res, (n // per_res) * 3 + col)

        def prev(col, per_res=per_res):
            return lambda b, n: (b, jnp.maximum(n % per_res - 1, 0), (n // per_res) * 3 + col)

        in_specs += [pl.BlockSpec(blk, cur(0)), pl.BlockSpec(blk, prev(1)), pl.BlockSpec(blk, cur(1)),
                     pl.BlockSpec(blk, prev(2)), pl.BlockSpec(blk, cur(2))]
        args += [view] * 5
        o_map = lambda b, n, per_res=per_res: (b, n % per_res, n // per_res)
        out_specs += [pl.BlockSpec(blk, o_map), pl.BlockSpec(blk, o_map)]
        out_shapes += [jax.ShapeDtypeStruct((bsz, seq // dil, dil * A_WIDTH), F32)] * 2
    outs = pl.pallas_call(
        functools.partial(_dilated_kernel, n_blocks=n_blocks),
        grid=(bsz, n_blocks),
        in_specs=in_specs,
        out_specs=out_specs,
        out_shape=out_shapes,
        compiler_params=_cparams(("parallel", "parallel")),
        name="dilated",
    )(*args)
    return [o.reshape(bsz * seq, A_WIDTH) for o in outs]


def _diff_kernel(sc_ref, qt_ref, k_ref, vt_ref, g_ref, o_ref, m_sc, l_sc, acc_sc):
    t = DIFF_BLOCK
    h = pl.program_id(1)
    qi = pl.program_id(2)
    slope = jnp.full((1, 1), sc_ref[1 + h], F32)
    lam = sc_ref[0]
    qt = qt_ref[...]
    row = lax.broadcasted_iota(I32, qt.shape, 0)
    zero = jnp.zeros_like(qt)
    qst = jnp.concatenate([jnp.where(row < HEAD_DIM, qt, zero), jnp.where(row >= HEAD_DIM, qt, zero)], axis=1)
    rel = lax.broadcasted_iota(I32, (t, t), 1) - lax.broadcasted_iota(I32, (t, t), 0)
    rel2 = jnp.concatenate([rel, rel], axis=1)
    bias = -slope * rel2.astype(F32)
    m_sc[...] = jnp.full(m_sc.shape, -jnp.inf, F32)
    l_sc[...] = jnp.zeros(l_sc.shape, F32)
    acc_sc[...] = jnp.zeros(acc_sc.shape, F32)

    def block(n, diagonal):
        s = _dot(k_ref[n], qst) + bias
        if diagonal:
            s = jnp.where(rel2 >= 0, s, -jnp.inf)
        c = -slope * (jnp.full((1, 1), (qi - n) * t, I32)).astype(F32)
        m_prev = m_sc[...]
        m_new = jnp.maximum(m_prev, jnp.max(s, axis=0, keepdims=True) + c)
        alpha = jnp.exp(m_prev - m_new)
        e = jnp.exp(s - (m_new - c))
        l_sc[...] = alpha * l_sc[...] + jnp.sum(e, axis=0, keepdims=True)
        acc_sc[...] = alpha * acc_sc[...] + _dot(vt_ref[n], e.astype(BF16))
        m_sc[...] = m_new

    def body(n, carry):
        block(n, False)
        return carry

    lax.fori_loop(0, qi, body, 0)
    block(qi, True)
    o = acc_sc[...] / l_sc[...]
    o = o[:, :t] - lam * o[:, t:]
    o = o * lax.rsqrt(jnp.mean(o * o, axis=0, keepdims=True) + RMS_EPS) * g_ref[...]
    o_ref[...] = (o * (1.0 - LAMBDA_INIT)).T.astype(BF16)


def _diff(bqt, bk, bvt, scalars, diff_norm_g, bsz, seq):
    t = DIFF_BLOCK
    nb = seq // t
    lanes = 2 * HEAD_DIM
    return pl.pallas_call(
        _diff_kernel,
        grid=(bsz, B_HEADS, nb),
        in_specs=[
            pl.BlockSpec(memory_space=pltpu.SMEM),
            pl.BlockSpec((None, None, lanes, t), lambda b, h, i: (b, i, h, 0)),
            pl.BlockSpec((None, nb, t, lanes), lambda b, h, i: (b, 0, 0, h)),
            pl.BlockSpec((None, nb, lanes, t), lambda b, h, i: (b, 0, h, 0)),
            pl.BlockSpec((lanes, 1), lambda b, h, i: (0, 0)),
        ],
        out_specs=pl.BlockSpec((None, t, lanes), lambda b, h, i: (b, i, h)),
        out_shape=jax.ShapeDtypeStruct((bsz, seq, B_WIDTH), BF16),
        scratch_shapes=[pltpu.VMEM((1, 2 * t), F32), pltpu.VMEM((1, 2 * t), F32), pltpu.VMEM((lanes, 2 * t), F32)],
        compiler_params=_cparams(("parallel", "parallel", "parallel")),
        name="diff",
    )(scalars, bqt, bk, bvt, diff_norm_g.reshape(lanes, 1))


def _router_kernel(o0, l0, o1, l1, o2, l2, ob_ref, x_ref, mod_ref, wout_ref, g2_ref, wrh_ref, wrl_ref, br_ref,
                   tri_ref, x1_ref, he_ref, idx_ref, rank_ref, cnt_ref):
    ls = [l0[...], l1[...], l2[...]]
    mx = jnp.maximum(jnp.maximum(ls[0], ls[1]), ls[2])
    ws = [jnp.exp(l - mx) for l in ls]
    den = ws[0] + ws[1] + ws[2]
    oa = (ws[0] * o0[...] + ws[1] * o1[...] + ws[2] * o2[...]) / den
    mixed = _dot(oa.astype(BF16), wout_ref[:A_WIDTH, :]) + _dot(ob_ref[...], wout_ref[A_WIDTH:, :])
    x1 = x_ref[...] + mod_ref[2:3, :] * mixed
    x1_ref[...] = x1
    y = x1 * lax.rsqrt(jnp.mean(x1 * x1, axis=-1, keepdims=True) + RMS_EPS) * g2_ref[...]
    h2 = y * (1.0 + mod_ref[4:5, :]) + mod_ref[3:4, :]
    he_ref[:, :D_MODEL] = h2.astype(BF16)

    hh, hl = _split_bf16(h2)
    wrh = wrh_ref[...]
    logits = _dot_nt(wrh, hh) + _dot_nt(wrh, hl) + _dot_nt(wrl_ref[...], hh) + br_ref[...]
    t = logits.shape[1]
    eid = lax.broadcasted_iota(I32, (N_EXPERTS, t), 0)
    vals, idxs, hots = [], [], []
    cur = logits
    for _ in range(TOP_K):
        v = jnp.max(cur, axis=0, keepdims=True)
        ik = jnp.min(jnp.where(cur == v, eid, N_EXPERTS), axis=0, keepdims=True)
        hot = eid == ik
        vals.append(v)
        idxs.append(ik)
        hots.append(hot)
        cur = jnp.where(hot, -jnp.inf, cur)
    es = [jnp.exp(v - vals[0]) for v in vals]
    esum = es[0] + es[1] + es[2] + es[3]
    idx_ref[...] = jnp.concatenate(idxs, axis=0)

    rows = [ik.astype(F32) for ik in idxs]
    for e in es:
        g = e / esum
        hi = g.astype(BF16).astype(F32)
        mid = (g - hi).astype(BF16).astype(F32)
        rows += [hi, mid, g - hi - mid]
    rows.append(jnp.zeros((EXTRA_LANES - len(rows), t), F32))
    he_ref[:, D_MODEL:] = jnp.concatenate(rows, axis=0).T.astype(BF16)

    sel = jnp.where(hots[0] | hots[1] | hots[2] | hots[3], 1.0, 0.0)
    before = _dot(sel.astype(BF16), tri_ref[...])
    ranks = [jnp.sum(jnp.where(hot, before, 0.0), axis=0, keepdims=True) for hot in hots]
    rank_ref[...] = jnp.concatenate(ranks, axis=0).astype(I32)
    cnt_ref[...] = jnp.broadcast_to(jnp.sum(sel, axis=1, keepdims=True), cnt_ref.shape)


def _router(dil_outs, o_b, x2, mod_b, w_out, norm2_g, w_router, b_router, seq):
    n = x2.shape[0]
    t = ROW_TILE
    per_b = seq // t
    wr_t = w_router.T
    wrh = wr_t.astype(BF16)
    wrl = (wr_t - wrh.astype(F32)).astype(BF16)
    tri = jnp.asarray(np.arange(t)[:, None] < np.arange(t)[None, :], BF16)
    row = lambda w: pl.BlockSpec((t, w), lambda i: (i, 0))
    full = lambda a, b: pl.BlockSpec((a, b), lambda i: (0, 0))
    tok = lambda: pl.BlockSpec((TOP_K, t), lambda i: (0, i))
    return pl.pallas_call(
        _router_kernel,
        grid=(n // t,),
        in_specs=[row(A_WIDTH)] * 6 + [
            row(B_WIDTH), row(D_MODEL),
            pl.BlockSpec((None, 6, D_MODEL), lambda i: (i // per_b, 0, 0)),
            full(D_MODEL, D_MODEL), full(1, D_MODEL), full(N_EXPERTS, D_MODEL), full(N_EXPERTS, D_MODEL),
            full(N_EXPERTS, 1), full(t, t),
        ],
        out_specs=[row(D_MODEL), row(ROW_WIDTH), tok(), tok(),
                   pl.BlockSpec((None, N_EXPERTS, 128), lambda i: (i, 0, 0))],
        out_shape=[
            jax.ShapeDtypeStruct((n, D_MODEL), F32),
            jax.ShapeDtypeStruct((n, ROW_WIDTH), BF16),
            jax.ShapeDtypeStruct((TOP_K, n), I32),
            jax.ShapeDtypeStruct((TOP_K, n), I32),
            jax.ShapeDtypeStruct((n // t, N_EXPERTS, 128), F32),
        ],
        compiler_params=_cparams(("parallel",)),
        name="router",
    )(*dil_outs, o_b, x2, mod_b, w_out.astype(BF16), norm2_g.reshape(1, D_MODEL), wrh, wrl,
      b_router.reshape(N_EXPERTS, 1), tri)


def _segment_copies(loff_ref, goff_ref, len_ref, tile, local_ref, hbm_ref, sem, outbound, action):
    def per_expert(e, carry):
        base = tile * N_EXPERTS + e
        lo, go, n = loff_ref[base], goff_ref[base], len_ref[base]
        done = jnp.int32(0)
        for size in SEG_SIZES:
            take = (n & size) != 0
            loc = local_ref.at[pl.ds(pl.multiple_of(lo + done, SEG_ALIGN), size)]
            glob = hbm_ref.at[pl.ds(pl.multiple_of(go + done, SEG_ALIGN), size)]
            cp = pltpu.make_async_copy(loc, glob, sem) if outbound else pltpu.make_async_copy(glob, loc, sem)
            pl.when(take)(functools.partial(action, cp))
            done = done + jnp.where(take, size, 0)
        return carry

    lax.fori_loop(0, N_EXPERTS, per_expert, 0)


def _start(cp):
    cp.start()


def _wait(cp):
    cp.wait()


def _tile_rows(loff_ref, len_ref, tile):
    last = tile * N_EXPERTS + N_EXPERTS - 1
    return loff_ref[last] + len_ref[last]


def _wait_rows(rows, local_ref, hbm_ref, sem, outbound):
    size = 1 << (LOCAL_SLOTS.bit_length() - 1)
    while size >= SEG_ALIGN:
        loc, glob = local_ref.at[pl.ds(0, size)], hbm_ref.at[pl.ds(0, size)]
        cp = pltpu.make_async_copy(loc, glob, sem) if outbound else pltpu.make_async_copy(glob, loc, sem)
        pl.when((rows & size) != 0)(cp.wait)
        size //= 2


def _one_hot_any(j, targets):
    out = jnp.zeros(j.shape, F32)
    for tgt in targets:
        out = jnp.where(j == tgt, 1.0, out)
    return out


def _sort_kernel(loff_ref, goff_ref, len_ref, tail_ref, he_ref, idx_ref, rank_ref, lcol_ref, ls_ref, xs_hbm,
                 xl, zbuf, sems, zsem):
    i = pl.program_id(0)
    last = pl.num_programs(0) - 1
    slot = i % 2
    t = he_ref.shape[0]

    @pl.when(i == 0)
    def _():
        zbuf[...] = jnp.zeros(zbuf.shape, BF16)

        def tails(action):
            def per_expert(e, carry):
                off, n = tail_ref[e], tail_ref[N_EXPERTS + e]
                done = jnp.int32(0)
                for size in SEG_SIZES:
                    if size < MOE_TILE:
                        take = (n & size) != 0
                        dst = xs_hbm.at[pl.ds(pl.multiple_of(off + done, SEG_ALIGN), size)]
                        pl.when(take)(functools.partial(action, pltpu.make_async_copy(zbuf.at[pl.ds(0, size)], dst, zsem)))
                        done = done + jnp.where(take, size, 0)
                return carry

            lax.fori_loop(0, N_EXPERTS, per_expert, 0)

            def per_block(b, carry):
                dst = xs_hbm.at[pl.ds(pl.multiple_of(b * MOE_TILE, MOE_TILE), MOE_TILE)]
                action(pltpu.make_async_copy(zbuf, dst, zsem))
                return carry

            lax.fori_loop(tail_ref[2 * N_EXPERTS], xs_hbm.shape[0] // MOE_TILE, per_block, 0)

        tails(_start)
        tails(_wait)

    eid = lax.broadcasted_iota(I32, (N_EXPERTS, t), 0)
    lcol = lcol_ref[...]
    ls = []
    for k in range(TOP_K):
        off = jnp.sum(jnp.where(eid == idx_ref[k:k + 1, :], lcol, 0), axis=0, keepdims=True)
        ls.append(off + rank_ref[k:k + 1, :])
    ls_ref[...] = jnp.concatenate(ls, axis=0)

    he = he_ref[...]
    for jc in range(LOCAL_SLOTS // SLOT_CHUNK):
        j = lax.broadcasted_iota(I32, (SLOT_CHUNK, t), 0) + jc * SLOT_CHUNK
        perm = _one_hot_any(j, ls).astype(BF16)
        xl[slot, jc * SLOT_CHUNK:(jc + 1) * SLOT_CHUNK, :] = _dot(perm, he).astype(BF16)

    tables = (loff_ref, goff_ref, len_ref)
    _segment_copies(*tables, i, xl.at[slot], xs_hbm, sems.at[slot], True, _start)

    @pl.when(i > 0)
    def _():
        _wait_rows(_tile_rows(loff_ref, len_ref, i - 1), xl.at[1 - slot], xs_hbm, sems.at[1 - slot], True)

    @pl.when(i == last)
    def _():
        _wait_rows(_tile_rows(loff_ref, len_ref, i), xl.at[slot], xs_hbm, sems.at[slot], True)


def _sort(tables, tail, he, idx, rank, lcol, n_rows):
    n = he.shape[0]
    t = ROW_TILE
    tok = lambda: pl.BlockSpec((TOP_K, t), lambda i, *_: (0, i))
    return pl.pallas_call(
        _sort_kernel,
        grid_spec=pltpu.PrefetchScalarGridSpec(
            num_scalar_prefetch=4,
            grid=(n // t,),
            in_specs=[
                pl.BlockSpec((t, ROW_WIDTH), lambda i, *_: (i, 0)),
                tok(), tok(),
                pl.BlockSpec((None, N_EXPERTS, 1), lambda i, *_: (i, 0, 0)),
            ],
            out_specs=[tok(), pl.BlockSpec(memory_space=pl.ANY)],
            scratch_shapes=[pltpu.VMEM((2, LOCAL_SLOTS, ROW_WIDTH), BF16), pltpu.VMEM((MOE_TILE, ROW_WIDTH), BF16),
                            pltpu.SemaphoreType.DMA((2,)), pltpu.SemaphoreType.DMA(())],
        ),
        out_shape=[jax.ShapeDtypeStruct((TOP_K, n), I32), jax.ShapeDtypeStruct((n_rows, ROW_WIDTH), BF16)],
        compiler_params=_cparams(("arbitrary",)),
        name="sort",
    )(*tables, tail, he, idx, rank, lcol)


def _experts_kernel(be_ref, nu_ref, x_ref, wgu_ref, bgu_ref, wd_ref, bd_ref, y_ref, wgu_sc, wd_sc):
    b = pl.program_id(0)

    @pl.when(b < nu_ref[0])
    def _():
        prev = be_ref[jnp.maximum(b - 1, 0)]

        @pl.when((b == 0) | (be_ref[b] != prev))
        def _():
            rows = 128

            def cast(r, carry):
                s = pl.multiple_of(r * rows, rows)
                wgu_sc[pl.ds(s, rows), :] = wgu_ref[pl.ds(s, rows), :].astype(BF16)
                wd_sc[pl.ds(s, rows), :] = wd_ref[pl.ds(s, rows), :].astype(BF16)
                return carry

            lax.fori_loop(0, D_MODEL // rows, cast, 0)

        ext = x_ref[:, D_MODEL:].astype(F32)
        me = jnp.full((1, 1), be_ref[b], I32).astype(F32)
        gate = jnp.zeros((x_ref.shape[0], 1), F32)
        for k in range(TOP_K):
            c = TOP_K + 3 * k
            gk = ext[:, c:c + 1] + ext[:, c + 1:c + 2] + ext[:, c + 2:c + 3]
            gate = gate + jnp.where(ext[:, k:k + 1] == me, gk, 0.0)

        gu = _dot(x_ref[:, :D_MODEL], wgu_sc[...]) + bgu_ref[...]
        g = jnp.minimum(gu[:, :D_FF], SWIGLU_LIMIT)
        u = jnp.clip(gu[:, D_FF:], -SWIGLU_LIMIT, SWIGLU_LIMIT)
        act = (u + 1.0) * (g / (1.0 + jnp.exp(-SWIGLU_ALPHA * g)))
        y_ref[...] = (gate * (_dot(act.astype(BF16), wd_sc[...]) + bd_ref[...])).astype(BF16)

    @pl.when(b >= nu_ref[0])
    def _():
        y_ref[...] = jnp.zeros(y_ref.shape, BF16)


def _experts(block_e, n_used, xs, w_gate_up, b_gate_up, w_down, b_down):
    n_rows = xs.shape[0]
    tm = MOE_TILE
    n_blocks = n_rows // tm
    blk = lambda b, be, nu: (jnp.minimum(b, nu[0] - 1), 0)
    exp3 = lambda b, be, nu: (be[b], 0, 0)
    return pl.pallas_call(
        _experts_kernel,
        grid_spec=pltpu.PrefetchScalarGridSpec(
            num_scalar_prefetch=2,
            grid=(n_blocks,),
            in_specs=[
                pl.BlockSpec((tm, ROW_WIDTH), blk),
                pl.BlockSpec((None, D_MODEL, 2 * D_FF), exp3),
                pl.BlockSpec((None, 1, 2 * D_FF), exp3),
                pl.BlockSpec((None, D_FF, D_MODEL), exp3),
                pl.BlockSpec((None, 1, D_MODEL), exp3),
            ],
            out_specs=pl.BlockSpec((tm, D_MODEL), lambda b, be, nu: (b, 0)),
            scratch_shapes=[pltpu.VMEM((D_MODEL, 2 * D_FF), BF16), pltpu.VMEM((D_FF, D_MODEL), BF16)],
        ),
        out_shape=jax.ShapeDtypeStruct((n_rows, D_MODEL), BF16),
        compiler_params=_cparams(("arbitrary",)),
        name="experts",
    )(block_e, n_used, xs, w_gate_up, b_gate_up.reshape(N_EXPERTS, 1, 2 * D_FF), w_down,
      b_down.reshape(N_EXPERTS, 1, D_MODEL))


def _combine_kernel(loff_ref, goff_ref, len_ref, yb_hbm, lst_ref, x1_ref, mod_ref, o_ref, ybuf, sems):
    i = pl.program_id(0)
    slot = i % 2
    tables = (loff_ref, goff_ref, len_ref)

    @pl.when(i == 0)
    def _():
        ybuf[...] = jnp.zeros(ybuf.shape, BF16)
        _segment_copies(*tables, 0, ybuf.at[0], yb_hbm, sems.at[0], False, _start)

    @pl.when(i + 1 < pl.num_programs(0))
    def _():
        _segment_copies(*tables, i + 1, ybuf.at[1 - slot], yb_hbm, sems.at[1 - slot], False, _start)

    _wait_rows(_tile_rows(loff_ref, len_ref, i), ybuf.at[slot], yb_hbm, sems.at[slot], False)

    lst = lst_ref[...]
    t = lst.shape[0]
    targets = [lst[:, k:k + 1] for k in range(TOP_K)]
    y = jnp.zeros((t, D_MODEL), F32)
    for jc in range(LOCAL_SLOTS // SLOT_CHUNK):
        j = lax.broadcasted_iota(I32, (t, SLOT_CHUNK), 1) + jc * SLOT_CHUNK
        pick = _one_hot_any(j, targets).astype(BF16)
        y = y + _dot(pick, ybuf[slot, jc * SLOT_CHUNK:(jc + 1) * SLOT_CHUNK, :])
    o_ref[...] = x1_ref[...] + mod_ref[5:6, :] * y


def _combine(tables, yb, ls_t, x1, mod_b, seq):
    n = x1.shape[0]
    t = ROW_TILE
    per_b = seq // t
    return pl.pallas_call(
        _combine_kernel,
        grid_spec=pltpu.PrefetchScalarGridSpec(
            num_scalar_prefetch=3,
            grid=(n // t,),
            in_specs=[
                pl.BlockSpec(memory_space=pl.ANY),
                pl.BlockSpec((t, TOP_K), lambda i, *_: (i, 0)),
                pl.BlockSpec((t, D_MODEL), lambda i, *_: (i, 0)),
                pl.BlockSpec((None, 6, D_MODEL), lambda i, *_: (i // per_b, 0, 0)),
            ],
            out_specs=pl.BlockSpec((t, D_MODEL), lambda i, *_: (i, 0)),
            scratch_shapes=[pltpu.VMEM((2, LOCAL_SLOTS, D_MODEL), BF16), pltpu.SemaphoreType.DMA((2,))],
        ),
        out_shape=jax.ShapeDtypeStruct((n, D_MODEL), F32),
        compiler_params=_cparams(("arbitrary",)),
        name="combine",
    )(*tables, yb, ls_t, x1, mod_b)


def _layer(x, c, w_ada, b_ada, norm1_g, w_in, a_q_norm_g, a_k_norm_g, b_q_norm_g, b_k_norm_g, lambda_q1,
           lambda_k1, lambda_q2, lambda_k2, diff_norm_g, w_out, norm2_g, w_router, b_router, w_gate_up,
           b_gate_up, w_down, b_down):
    bsz, seq, _ = x.shape
    n = bsz * seq
    x2 = x.reshape(n, D_MODEL)
    mod_b = _ada(c, w_ada, b_ada).transpose(1, 0, 2)

    qk_gains = jnp.stack([
        jnp.tile(a_q_norm_g, A_HEADS) * ATTN_SCALE, jnp.tile(a_k_norm_g, A_HEADS), jnp.tile(b_k_norm_g, 2 * B_HEADS)])
    bq_gain = (jnp.tile(b_q_norm_g, 2 * B_HEADS) * ATTN_SCALE).reshape(B_WIDTH, 1)
    proj_a, bk, bqt, bvt = _inproj(x2, mod_b, norm1_g, w_in, qk_gains, bq_gain, bsz, seq)

    dil_outs = _dilated(proj_a, bsz, seq)
    lam = (jnp.exp(jnp.sum(lambda_q1 * lambda_k1)) - jnp.exp(jnp.sum(lambda_q2 * lambda_k2)) + LAMBDA_INIT)
    slopes_b = 2.0 ** (-8.0 * np.arange(1, B_HEADS + 1) / B_HEADS)
    scalars = jnp.concatenate([lam.reshape(1), jnp.asarray(slopes_b, F32)]).astype(F32)
    bk4 = bk.reshape(bsz, seq // DIFF_BLOCK, DIFF_BLOCK, B_WIDTH)
    o_b = _diff(bqt, bk4, bvt, scalars, diff_norm_g, bsz, seq).reshape(n, B_WIDTH)

    x1, he, idx, rank, cnt = _router(dil_outs, o_b, x2, mod_b, w_out, norm2_g, w_router, b_router, seq)

    n_tiles = n // ROW_TILE
    counts = cnt[:, :, 0].astype(I32)
    seg = (counts + SEG_ALIGN - 1) // SEG_ALIGN * SEG_ALIGN
    loff = jnp.cumsum(seg, axis=1) - seg
    region = jnp.sum(seg, axis=0)
    padded = (region + MOE_TILE - 1) // MOE_TILE * MOE_TILE
    pad_end = jnp.cumsum(padded)
    pad_start = pad_end - padded
    goff = pad_start[None, :] + jnp.cumsum(seg, axis=0) - seg
    tables = (loff.reshape(-1), goff.reshape(-1), seg.reshape(-1))
    n_blocks = (n * TOP_K + n_tiles * N_EXPERTS * (SEG_ALIGN - 1) + N_EXPERTS * (MOE_TILE - 1)) // MOE_TILE
    n_used = (pad_end[-1] // MOE_TILE).astype(I32)
    tail = jnp.concatenate([pad_start + region, padded - region, n_used.reshape(1)])
    blk_ids = jnp.minimum(jnp.arange(n_blocks, dtype=I32), n_used - 1)
    block_e = jnp.sum((pad_end[None, :] <= (blk_ids * MOE_TILE)[:, None]).astype(I32), axis=1)
    block_e = jnp.minimum(block_e, N_EXPERTS - 1)

    ls, xs = _sort(tables, tail, he, idx, rank, loff.reshape(n_tiles, N_EXPERTS, 1), n_blocks * MOE_TILE)
    yb = _experts(block_e, n_used.reshape(1), xs, w_gate_up, b_gate_up, w_down, b_down)
    out = _combine(tables, yb, ls.T, x1, mod_b, seq)
    return out.reshape(bsz, seq, D_MODEL)


def kernel(x, c, w_ada, b_ada, norm1_g, w_in, a_q_norm_g, a_k_norm_g, b_q_norm_g, b_k_norm_g, lambda_q1, lambda_k1,
           lambda_q2, lambda_k2, diff_norm_g, w_out, norm2_g, w_router, b_router, w_gate_up, b_gate_up, w_down,
           b_down):
    args = (w_ada, b_ada, norm1_g, w_in, a_q_norm_g, a_k_norm_g, b_q_norm_g, b_k_norm_g, lambda_q1, lambda_k1,
            lambda_q2, lambda_k2, diff_norm_g, w_out, norm2_g, w_router, b_router, w_gate_up, b_gate_up, w_down,
            b_down)
    return _layer(x, c, *[a[0] for a in args])
```

```python
import functools

import numpy as np
import jax
import jax.numpy as jnp
from jax import lax
from jax.experimental import pallas as pl
from jax.experimental.pallas import tpu as pltpu

F32 = jnp.float32
BF16 = jnp.bfloat16
I32 = jnp.int32
U32 = jnp.uint32

D_MODEL = 1024
HEAD_DIM = 64
A_WIDTH = 512
B_WIDTH = 512
A_HEADS = 8
B_HEADS = 4
IN_WIDTH = 3072
DIL_PATTERNS = ((128, 1), (512, 4), (2048, 16))
BAND = 128
N_EXPERTS = 32
TOP_K = 4
D_FF = 1024
SWIGLU_LIMIT = 7.0
SWIGLU_ALPHA = 1.702
RMS_EPS = 1e-6
ATTN_SCALE = HEAD_DIM ** -0.5
LAMBDA_INIT = 0.8 - 0.6 * 1.0

LANES = 128
ROW_TILE = 512
DIFF_BLOCK = 512
MOE_TILE = 256
EXTRA_LANES = 128
ROW_WIDTH = D_MODEL + EXTRA_LANES
SEG_ALIGN = 16
SEG_SIZES = (512, 256, 128, 64, 32, 16)
SLOT_CHUNK = 512
LOCAL_SLOTS = 2560
assert ROW_TILE == DIFF_BLOCK == SEG_SIZES[0]
assert LOCAL_SLOTS >= ROW_TILE * TOP_K + N_EXPERTS * (SEG_ALIGN - 1) and LOCAL_SLOTS % SLOT_CHUNK == 0
VMEM_LIMIT = 56 * 1024 * 1024


def _cparams(sem, **flags):
    return pltpu.CompilerParams(dimension_semantics=sem, vmem_limit_bytes=VMEM_LIMIT, flags=flags or None)


def _split_bf16(a):
    hi = a.astype(BF16)
    lo = (a - hi.astype(F32)).astype(BF16)
    return hi, lo


def _dot_nt(a, b):
    return lax.dot_general(a, b, (((1,), (1,)), ((), ())), preferred_element_type=F32)


def _dot(a, b):
    return jnp.dot(a, b, preferred_element_type=F32)


def _ada_kernel(c_ref, w_ref, b_ref, o_ref):
    c = c_ref[...]
    s = c / (1.0 + jnp.exp(-c))
    sh, sl = _split_bf16(s)
    wh, wl = _split_bf16(w_ref[...])
    o_ref[0] = _dot(sh, wh) + _dot(sh, wl) + _dot(sl, wh) + b_ref[0]


def _ada(c, w_ada, b_ada):
    bsz = c.shape[0]
    return pl.pallas_call(
        _ada_kernel,
        grid=(6,),
        in_specs=[
            pl.BlockSpec((bsz, D_MODEL), lambda j: (0, 0)),
            pl.BlockSpec((D_MODEL, D_MODEL), lambda j: (0, j)),
            pl.BlockSpec((1, 1, D_MODEL), lambda j: (j, 0, 0)),
        ],
        out_specs=pl.BlockSpec((1, bsz, D_MODEL), lambda j: (j, 0, 0)),
        out_shape=jax.ShapeDtypeStruct((6, bsz, D_MODEL), F32),
        compiler_params=_cparams(("arbitrary",)),
        name="ada",
    )(c, w_ada, b_ada.reshape(6, 1, D_MODEL))


def _inproj_kernel(x_ref, mod_ref, g1_ref, wn_ref, wt_ref, gm_ref, qkg_ref, bqg_ref, a1_ref, a4_ref, a16_ref, bk_ref,
                   bqt_ref, bvt_ref, a_sc):
    a_refs = (a1_ref, a4_ref, a16_ref)
    x = x_ref[...]
    y = x * lax.rsqrt(jnp.mean(x * x, axis=-1, keepdims=True) + RMS_EPS) * g1_ref[...]
    h = (y * (1.0 + mod_ref[1:2, :]) + mod_ref[0:1, :]).astype(BF16)
    p = _dot(h, wn_ref[...])
    gm = gm_ref[...]

    def head_norm(t, g):
        ss = _dot((t * t).astype(BF16), gm)
        return t * lax.rsqrt(ss * (1.0 / HEAD_DIM) + RMS_EPS) * g

    w = A_WIDTH
    a_part = jnp.concatenate([head_norm(p[:, 0 * w:1 * w], qkg_ref[0:1, :]),
                              head_norm(p[:, 1 * w:2 * w], qkg_ref[1:2, :]), p[:, 2 * w:3 * w]], axis=1)
    n_col = a_sc.shape[0]
    for c in range(n_col):
        a_sc[c] = a_part[:, c * LANES:(c + 1) * LANES]
    for a_ref, (_, dil) in zip(a_refs, DIL_PATTERNS):
        if dil == 1:
            a_ref[0] = a_part.astype(BF16)
            continue
        rows = a_part.shape[0] // dil
        for r in range(dil):
            for c in range(n_col):
                a_ref[r, :, c * LANES:(c + 1) * LANES] = a_sc[c, pl.ds(r, rows, stride=dil), :].astype(BF16)
    bk_ref[...] = head_norm(p[:, 3 * w:4 * w], qkg_ref[2:3, :]).astype(BF16)

    pt = _dot_nt(wt_ref[...], h)
    t = pt.shape[1]
    bq = pt[:w].reshape(w // HEAD_DIM, HEAD_DIM, t)
    ss = jnp.sum(bq * bq, axis=1, keepdims=True)
    bq = (bq * lax.rsqrt(ss * (1.0 / HEAD_DIM) + RMS_EPS)).reshape(w, t) * bqg_ref[...]
    bqt_ref[...] = bq.astype(BF16)
    bvt_ref[...] = pt[w:].astype(BF16)


def _inproj(x2, mod_b, norm1_g, w_in, qk_gains, bq_gain, bsz, seq):
    n = x2.shape[0]
    t = ROW_TILE
    per_b = seq // t
    w = A_WIDTH
    head_of_lane = np.arange(w) // HEAD_DIM
    gmat = jnp.asarray(head_of_lane[:, None] == head_of_lane[None, :], BF16)
    w_bf = w_in.astype(BF16)
    w_nat = jnp.concatenate([w_bf[:, :3 * w], w_bf[:, 4 * w:5 * w]], axis=1)
    w_tr = jnp.concatenate([w_bf[:, 3 * w:4 * w], w_bf[:, 5 * w:]], axis=1).T
    tr_spec = pl.BlockSpec((None, None, w, t), lambda i: (i // per_b, i % per_b, 0, 0))
    tr_shape = jax.ShapeDtypeStruct((bsz, per_b, w, t), BF16)
    a_specs = [pl.BlockSpec((None, dil, t // dil, 3 * w), lambda i: (i // per_b, 0, i % per_b, 0))
               for _, dil in DIL_PATTERNS]
    a_shapes = [jax.ShapeDtypeStruct((bsz, dil, seq // dil, 3 * w), BF16) for _, dil in DIL_PATTERNS]
    return pl.pallas_call(
        _inproj_kernel,
        grid=(n // t,),
        in_specs=[
            pl.BlockSpec((t, D_MODEL), lambda i: (i, 0)),
            pl.BlockSpec((None, 6, D_MODEL), lambda i: (i // per_b, 0, 0)),
            pl.BlockSpec((1, D_MODEL), lambda i: (0, 0)),
            pl.BlockSpec((D_MODEL, 4 * w), lambda i: (0, 0)),
            pl.BlockSpec((2 * w, D_MODEL), lambda i: (0, 0)),
            pl.BlockSpec((w, w), lambda i: (0, 0)),
            pl.BlockSpec((3, w), lambda i: (0, 0)),
            pl.BlockSpec((w, 1), lambda i: (0, 0)),
        ],
        out_specs=a_specs + [pl.BlockSpec((t, w), lambda i: (i, 0)), tr_spec, tr_spec],
        out_shape=a_shapes + [jax.ShapeDtypeStruct((n, w), BF16), tr_shape, tr_shape],
        scratch_shapes=[pltpu.VMEM((3 * w // LANES, t, LANES), F32)],
        compiler_params=_cparams(("parallel",)),
        name="inproj",
    )(x2, mod_b, norm1_g.reshape(1, D_MODEL), w_nat, w_tr, gmat, qk_gains, bq_gain)


def _dilated_bias_table():
    ik = np.arange(2 * BAND)[:, None]
    iq = np.arange(BAND)[None, :]
    delta = iq - ik + BAND
    in_band = (delta >= 0) & (delta <= BAND)
    slopes = 2.0 ** (-8.0 * np.arange(1, A_HEADS + 1) / A_HEADS)
    tbl = np.zeros((len(DIL_PATTERNS), 2, A_HEADS // 2, 2 * BAND, 2 * BAND), np.float32)
    for p, (_, dil) in enumerate(DIL_PATTERNS):
        for first in range(2):
            valid = in_band & ((ik >= BAND) if first else True)
            for h in range(A_HEADS):
                cols = slice(BAND * (h % 2), BAND * (h % 2 + 1))
                tbl[p, first, h // 2, :, cols] = np.where(valid, -slopes[h] * (delta * dil), -np.inf)
    return jnp.asarray(tbl)


def _dilated_kernel(tbl_ref, *refs, n_blocks):
    lanes = 2 * HEAD_DIM
    n = pl.program_id(1)
    n_pat = len(DIL_PATTERNS)
    s_sc, p_sc = refs[len(refs) - 2:]
    outs = refs[len(refs) - 2 - 2 * n_pat:len(refs) - 2]
    lane = lax.broadcasted_iota(I32, (BAND, lanes), 1)
    low = lane < HEAD_DIM
    row = lax.broadcasted_iota(I32, (lanes, BAND), 0)
    top = row < HEAD_DIM
    work = []
    pos = 0
    for p, (_, dil) in enumerate(DIL_PATTERNS):
        per_res = n_blocks // dil
        if per_res > 1:
            q_ref, kp_ref, kc_ref, vp_ref, vc_ref = refs[pos:pos + 5]
            pos += 5
            first = (n % per_res == 0).astype(I32)
        else:
            q_ref, kc_ref, vc_ref = refs[pos:pos + 3]
            kp_ref = vp_ref = first = None
            pos += 3
        for j in range(A_HEADS // 2):
            work.append((p, j, (q_ref, kp_ref, kc_ref, vp_ref, vc_ref), first))

    def keys_of(prev_ref, cur_ref, cs):
        return cur_ref[:, cs] if prev_ref is None else jnp.concatenate([prev_ref[:, cs], cur_ref[:, cs]], axis=0)

    for w, (p, j, (q_ref, kp_ref, kc_ref, _, _), first) in enumerate(work):
        cs = slice(lanes * j, lanes * (j + 1))
        q2 = q_ref[:, cs]
        zero = jnp.zeros_like(q2)
        qcat = jnp.concatenate([jnp.where(low, q2, zero), jnp.where(low, zero, q2)], axis=0)
        k2 = keys_of(kp_ref, kc_ref, cs)
        bias = tbl_ref[p, 1, j, BAND:, :] if kp_ref is None else tbl_ref[p, first, j]
        s_sc[w, :k2.shape[0], :] = _dot_nt(k2, qcat) + bias
    lses = []
    for w, (p, j, (_, kp_ref, _, _, _), _) in enumerate(work):
        nk = BAND if kp_ref is None else 2 * BAND
        s = s_sc[w, :nk, :]
        m = jnp.max(s, axis=0, keepdims=True)
        e = jnp.exp(s - m)
        den = jnp.sum(e, axis=0, keepdims=True)
        p_sc[w, :nk, :] = (e * (1.0 / den)).astype(BF16)
        lses.append(m + jnp.log(den))
    for w, (p, j, (_, kp_ref, _, vp_ref, vc_ref), _) in enumerate(work):
        nk = BAND if kp_ref is None else 2 * BAND
        cs = slice(lanes * j, lanes * (j + 1))
        v2 = keys_of(vp_ref, vc_ref, cs)
        r = lax.dot_general(p_sc[w, :nk, :], v2, (((0,), (0,)), ((), ())), preferred_element_type=F32)
        o_ref, l_ref = outs[2 * p:2 * p + 2]
        o_ref[:, cs] = jnp.where(low, r[:BAND], r[BAND:])
        lse = lses[w]
        lse_t = jnp.where(top, jnp.broadcast_to(lse[:, :BAND], (lanes, BAND)),
                          jnp.broadcast_to(lse[:, BAND:], (lanes, BAND)))
        l_ref[:, cs] = lse_t.T


def _dilated(a_parts, bsz, seq):
    n_blocks = seq // BAND
    in_specs = [pl.BlockSpec((len(DIL_PATTERNS), 2, A_HEADS // 2, 2 * BAND, 2 * BAND), lambda b, n: (0, 0, 0, 0, 0))]
    n_work = len(DIL_PATTERNS) * A_HEADS // 2
    args = [_dilated_bias_table()]
    out_specs, out_shapes = [], []
    blk = (None, None, BAND, A_WIDTH)
    for a_part, (_, dil) in zip(a_parts, DIL_PATTERNS):
        per_res = n_blocks // dil

        def cur(col, per_res=per_res):
            return lambda b, n: (b, n // per_res, n % per_res, col)

        def prev(col, per_res=per_res):
            return lambda b, n: (b, n // per_res, jnp.maximum(n % per_res - 1, 0), col)

        if per_res > 1:
            in_specs += [pl.BlockSpec(blk, cur(0)), pl.BlockSpec(blk, prev(1)), pl.BlockSpec(blk, cur(1)),
                         pl.BlockSpec(blk, prev(2)), pl.BlockSpec(blk, cur(2))]
            args += [a_part] * 5
        else:
            in_specs += [pl.BlockSpec(blk, cur(0)), pl.BlockSpec(blk, cur(1)), pl.BlockSpec(blk, cur(2))]
            args += [a_part] * 3
        out_specs += [pl.BlockSpec(blk, cur(0))] * 2
        out_shapes += [jax.ShapeDtypeStruct((bsz, dil, seq // dil, A_WIDTH), F32)] * 2
    return pl.pallas_call(
        functools.partial(_dilated_kernel, n_blocks=n_blocks),
        grid=(bsz, n_blocks),
        in_specs=in_specs,
        out_specs=out_specs,
        out_shape=out_shapes,
        scratch_shapes=[pltpu.VMEM((n_work, 2 * BAND, 2 * BAND), F32), pltpu.VMEM((n_work, 2 * BAND, 2 * BAND), BF16)],
        compiler_params=_cparams(("parallel", "parallel")),
        name="dilated",
    )(*args)


def _diff_kernel(sc_ref, qt_ref, k_ref, vt_ref, g_ref, o_ref, m_sc, l_sc, acc_sc):
    t = DIFF_BLOCK
    h = pl.program_id(1)
    qi = pl.program_id(2)
    slope = jnp.full((1, 1), sc_ref[1 + h], F32)
    lam = sc_ref[0]
    qt = qt_ref[...]
    row = lax.broadcasted_iota(I32, qt.shape, 0)
    zero = jnp.zeros_like(qt)
    qst = jnp.concatenate([jnp.where(row < HEAD_DIM, qt, zero), jnp.where(row >= HEAD_DIM, qt, zero)], axis=1)
    rel = lax.broadcasted_iota(I32, (t, t), 1) - lax.broadcasted_iota(I32, (t, t), 0)
    rel2 = jnp.concatenate([rel, rel], axis=1)
    bias = -slope * rel2.astype(F32)
    m_sc[...] = jnp.full(m_sc.shape, -jnp.inf, F32)
    l_sc[...] = jnp.zeros(l_sc.shape, F32)
    acc_sc[...] = jnp.zeros(acc_sc.shape, F32)

    def block(n, diagonal):
        s = _dot(k_ref[n], qst) + bias
        if diagonal:
            s = jnp.where(rel2 >= 0, s, -jnp.inf)
        c = -slope * (jnp.full((1, 1), (qi - n) * t, I32)).astype(F32)
        m_prev = m_sc[...]
        m_new = jnp.maximum(m_prev, jnp.max(s, axis=0, keepdims=True) + c)
        alpha = jnp.exp(m_prev - m_new)
        e = jnp.exp(s - (m_new - c))
        l_sc[...] = alpha * l_sc[...] + jnp.sum(e, axis=0, keepdims=True)
        acc_sc[...] = alpha * acc_sc[...] + _dot(vt_ref[n], e.astype(BF16))
        m_sc[...] = m_new

    def body(n, carry):
        block(n, False)
        return carry

    lax.fori_loop(0, qi, body, 0)
    block(qi, True)
    o = acc_sc[...] / l_sc[...]
    o = o[:, :t] - lam * o[:, t:]
    o = o * lax.rsqrt(jnp.mean(o * o, axis=0, keepdims=True) + RMS_EPS) * g_ref[...]
    o_ref[...] = (o * (1.0 - LAMBDA_INIT)).T.astype(BF16)


def _diff(bqt, bk, bvt, scalars, diff_norm_g, bsz, seq):
    t = DIFF_BLOCK
    nb = seq // t
    lanes = 2 * HEAD_DIM
    return pl.pallas_call(
        _diff_kernel,
        grid=(bsz, B_HEADS, nb),
        in_specs=[
            pl.BlockSpec(memory_space=pltpu.SMEM),
            pl.BlockSpec((None, None, lanes, t), lambda b, h, i: (b, i, h, 0)),
            pl.BlockSpec((None, nb, t, lanes), lambda b, h, i: (b, 0, 0, h)),
            pl.BlockSpec((None, nb, lanes, t), lambda b, h, i: (b, 0, h, 0)),
            pl.BlockSpec((lanes, 1), lambda b, h, i: (0, 0)),
        ],
        out_specs=pl.BlockSpec((None, t, lanes), lambda b, h, i: (b, i, h)),
        out_shape=jax.ShapeDtypeStruct((bsz, seq, B_WIDTH), BF16),
        scratch_shapes=[pltpu.VMEM((1, 2 * t), F32), pltpu.VMEM((1, 2 * t), F32), pltpu.VMEM((lanes, 2 * t), F32)],
        compiler_params=_cparams(("parallel", "parallel", "parallel")),
        name="diff",
    )(scalars, bqt, bk, bvt, diff_norm_g.reshape(lanes, 1))


def _router_kernel(o0, l0, o1, l1, o2, l2, ob_ref, x_ref, mod_ref, wout_ref, g2_ref, wrh_ref, wrl_ref, br_ref,
                   tri_ref, x1_ref, he_ref, idx_ref, rank_ref, cnt_ref, *order_sc):
    def token_order(ref, scratch):
        dil, rows, width = ref.shape
        if dil == 1:
            return ref[0]
        n_col = width // LANES
        for r in range(dil):
            for c in range(n_col):
                scratch[c, pl.ds(r, rows, stride=dil), :] = ref[r, :, c * LANES:(c + 1) * LANES]
        return jnp.concatenate([scratch[c] for c in range(n_col)], axis=1)

    os_ = [token_order(o0, None), token_order(o1, order_sc[0]), token_order(o2, order_sc[2])]
    ls = [token_order(l0, None), token_order(l1, order_sc[1]), token_order(l2, order_sc[3])]
    mx = jnp.maximum(jnp.maximum(ls[0], ls[1]), ls[2])
    ws = [jnp.exp(l - mx) for l in ls]
    den = ws[0] + ws[1] + ws[2]
    oa = (ws[0] * os_[0] + ws[1] * os_[1] + ws[2] * os_[2]) / den
    mixed = _dot(oa.astype(BF16), wout_ref[:A_WIDTH, :]) + _dot(ob_ref[...], wout_ref[A_WIDTH:, :])
    x1 = x_ref[...] + mod_ref[2:3, :] * mixed
    x1_ref[...] = x1
    y = x1 * lax.rsqrt(jnp.mean(x1 * x1, axis=-1, keepdims=True) + RMS_EPS) * g2_ref[...]
    h2 = y * (1.0 + mod_ref[4:5, :]) + mod_ref[3:4, :]
    he_ref[:, :D_MODEL] = h2.astype(BF16)

    hh, hl = _split_bf16(h2)
    wrh = wrh_ref[...]
    logits = _dot_nt(wrh, hh) + _dot_nt(wrh, hl) + _dot_nt(wrl_ref[...], hh) + br_ref[...]
    t = logits.shape[1]
    eid = lax.broadcasted_iota(I32, (N_EXPERTS, t), 0)
    vals, idxs, hots = [], [], []
    cur = logits
    for _ in range(TOP_K):
        v = jnp.max(cur, axis=0, keepdims=True)
        ik = jnp.min(jnp.where(cur == v, eid, N_EXPERTS), axis=0, keepdims=True)
        hot = eid == ik
        vals.append(v)
        idxs.append(ik)
        hots.append(hot)
        cur = jnp.where(hot, -jnp.inf, cur)
    es = [jnp.exp(v - vals[0]) for v in vals]
    esum = es[0] + es[1] + es[2] + es[3]
    idx_ref[...] = jnp.concatenate(idxs, axis=0)

    rows = [ik.astype(F32) for ik in idxs]
    for e in es:
        g = e / esum
        hi = g.astype(BF16).astype(F32)
        mid = (g - hi).astype(BF16).astype(F32)
        rows += [hi, mid, g - hi - mid]
    rows.append(jnp.zeros((EXTRA_LANES - len(rows), t), F32))
    he_ref[:, D_MODEL:] = jnp.concatenate(rows, axis=0).T.astype(BF16)

    sel = jnp.where(hots[0] | hots[1] | hots[2] | hots[3], 1.0, 0.0)
    before = _dot(sel.astype(BF16), tri_ref[...])
    ranks = [jnp.sum(jnp.where(hot, before, 0.0), axis=0, keepdims=True) for hot in hots]
    rank_ref[...] = jnp.concatenate(ranks, axis=0).astype(I32)
    cnt_ref[...] = jnp.broadcast_to(jnp.sum(sel, axis=1, keepdims=True), cnt_ref.shape)


def _router(dil_outs, o_b, x2, mod_b, w_out, norm2_g, w_router, b_router, seq):
    n = x2.shape[0]
    t = ROW_TILE
    per_b = seq // t
    wr_t = w_router.T
    wrh = wr_t.astype(BF16)
    wrl = (wr_t - wrh.astype(F32)).astype(BF16)
    tri = jnp.asarray(np.arange(t)[:, None] < np.arange(t)[None, :], BF16)
    row = lambda w: pl.BlockSpec((t, w), lambda i: (i, 0))
    full = lambda a, b: pl.BlockSpec((a, b), lambda i: (0, 0))
    tok = lambda: pl.BlockSpec((TOP_K, t), lambda i: (0, i))
    grouped = lambda dil: pl.BlockSpec((None, dil, t // dil, A_WIDTH), lambda i: (i // per_b, 0, i % per_b, 0))
    n_regrouped = 2 * sum(dil > 1 for _, dil in DIL_PATTERNS)
    return pl.pallas_call(
        _router_kernel,
        grid=(n // t,),
        in_specs=[grouped(dil) for _, dil in DIL_PATTERNS for _ in range(2)] + [
            row(B_WIDTH), row(D_MODEL),
            pl.BlockSpec((None, 6, D_MODEL), lambda i: (i // per_b, 0, 0)),
            full(D_MODEL, D_MODEL), full(1, D_MODEL), full(N_EXPERTS, D_MODEL), full(N_EXPERTS, D_MODEL),
            full(N_EXPERTS, 1), full(t, t),
        ],
        out_specs=[row(D_MODEL), row(ROW_WIDTH), tok(), tok(),
                   pl.BlockSpec((None, N_EXPERTS, 128), lambda i: (i, 0, 0))],
        out_shape=[
            jax.ShapeDtypeStruct((n, D_MODEL), F32),
            jax.ShapeDtypeStruct((n, ROW_WIDTH), BF16),
            jax.ShapeDtypeStruct((TOP_K, n), I32),
            jax.ShapeDtypeStruct((TOP_K, n), I32),
            jax.ShapeDtypeStruct((n // t, N_EXPERTS, 128), F32),
        ],
        scratch_shapes=[pltpu.VMEM((A_WIDTH // LANES, t, LANES), F32)] * n_regrouped,
        compiler_params=_cparams(("parallel",)),
        name="router",
    )(*dil_outs, o_b, x2, mod_b, w_out.astype(BF16), norm2_g.reshape(1, D_MODEL), wrh, wrl,
      b_router.reshape(N_EXPERTS, 1), tri)


def _segment_copies(loff_ref, goff_ref, len_ref, tile, local_ref, hbm_ref, sem, outbound, action):
    def per_expert(e, carry):
        base = tile * N_EXPERTS + e
        lo, go, n = loff_ref[base], goff_ref[base], len_ref[base]
        done = jnp.int32(0)
        for size in SEG_SIZES:
            take = (n & size) != 0
            loc = local_ref.at[pl.ds(pl.multiple_of(lo + done, SEG_ALIGN), size)]
            glob = hbm_ref.at[pl.ds(pl.multiple_of(go + done, SEG_ALIGN), size)]
            cp = pltpu.make_async_copy(loc, glob, sem) if outbound else pltpu.make_async_copy(glob, loc, sem)
            pl.when(take)(functools.partial(action, cp))
            done = done + jnp.where(take, size, 0)
        return carry

    lax.fori_loop(0, N_EXPERTS, per_expert, 0)


def _start(cp):
    cp.start()


def _wait(cp):
    cp.wait()


def _tile_rows(loff_ref, len_ref, tile):
    last = tile * N_EXPERTS + N_EXPERTS - 1
    return loff_ref[last] + len_ref[last]


def _wait_rows(rows, local_ref, hbm_ref, sem, outbound):
    size = 1 << (LOCAL_SLOTS.bit_length() - 1)
    while size >= SEG_ALIGN:
        loc, glob = local_ref.at[pl.ds(0, size)], hbm_ref.at[pl.ds(0, size)]
        cp = pltpu.make_async_copy(loc, glob, sem) if outbound else pltpu.make_async_copy(glob, loc, sem)
        pl.when((rows & size) != 0)(cp.wait)
        size //= 2


def _one_hot_any(j, targets):
    out = jnp.zeros(j.shape, F32)
    for tgt in targets:
        out = jnp.where(j == tgt, 1.0, out)
    return out


def _sort_kernel(loff_ref, goff_ref, len_ref, tail_ref, he_ref, idx_ref, rank_ref, lcol_ref, ls_ref, xs_hbm,
                 xl, zbuf, sems, zsem):
    i = pl.program_id(0)
    last = pl.num_programs(0) - 1
    slot = i % 2
    t = he_ref.shape[0]

    @pl.when(i == 0)
    def _():
        zbuf[...] = jnp.zeros(zbuf.shape, BF16)

        def tails(action):
            def per_expert(e, carry):
                off, n = tail_ref[e], tail_ref[N_EXPERTS + e]
                done = jnp.int32(0)
                for size in SEG_SIZES:
                    if size < MOE_TILE:
                        take = (n & size) != 0
                        dst = xs_hbm.at[pl.ds(pl.multiple_of(off + done, SEG_ALIGN), size)]
                        pl.when(take)(functools.partial(action, pltpu.make_async_copy(zbuf.at[pl.ds(0, size)], dst, zsem)))
                        done = done + jnp.where(take, size, 0)
                return carry

            lax.fori_loop(0, N_EXPERTS, per_expert, 0)

            def per_block(b, carry):
                dst = xs_hbm.at[pl.ds(pl.multiple_of(b * MOE_TILE, MOE_TILE), MOE_TILE)]
                action(pltpu.make_async_copy(zbuf, dst, zsem))
                return carry

            lax.fori_loop(tail_ref[2 * N_EXPERTS], xs_hbm.shape[0] // MOE_TILE, per_block, 0)

        tails(_start)
        tails(_wait)

    eid = lax.broadcasted_iota(I32, (N_EXPERTS, t), 0)
    lcol = lcol_ref[...]
    ls = []
    for k in range(TOP_K):
        off = jnp.sum(jnp.where(eid == idx_ref[k:k + 1, :], lcol, 0), axis=0, keepdims=True)
        ls.append(off + rank_ref[k:k + 1, :])
    ls_ref[...] = jnp.concatenate(ls, axis=0)

    he = he_ref[...]
    for jc in range(LOCAL_SLOTS // SLOT_CHUNK):
        j = lax.broadcasted_iota(I32, (SLOT_CHUNK, t), 0) + jc * SLOT_CHUNK
        perm = _one_hot_any(j, ls).astype(BF16)
        xl[slot, jc * SLOT_CHUNK:(jc + 1) * SLOT_CHUNK, :] = _dot(perm, he).astype(BF16)

    tables = (loff_ref, goff_ref, len_ref)
    _segment_copies(*tables, i, xl.at[slot], xs_hbm, sems.at[slot], True, _start)

    @pl.when(i > 0)
    def _():
        _wait_rows(_tile_rows(loff_ref, len_ref, i - 1), xl.at[1 - slot], xs_hbm, sems.at[1 - slot], True)

    @pl.when(i == last)
    def _():
        _wait_rows(_tile_rows(loff_ref, len_ref, i), xl.at[slot], xs_hbm, sems.at[slot], True)


def _sort(tables, tail, he, idx, rank, lcol, n_rows):
    n = he.shape[0]
    t = ROW_TILE
    tok = lambda: pl.BlockSpec((TOP_K, t), lambda i, *_: (0, i))
    return pl.pallas_call(
        _sort_kernel,
        grid_spec=pltpu.PrefetchScalarGridSpec(
            num_scalar_prefetch=4,
            grid=(n // t,),
            in_specs=[
                pl.BlockSpec((t, ROW_WIDTH), lambda i, *_: (i, 0)),
                tok(), tok(),
                pl.BlockSpec((None, N_EXPERTS, 1), lambda i, *_: (i, 0, 0)),
            ],
            out_specs=[tok(), pl.BlockSpec(memory_space=pl.ANY)],
            scratch_shapes=[pltpu.VMEM((2, LOCAL_SLOTS, ROW_WIDTH), BF16), pltpu.VMEM((MOE_TILE, ROW_WIDTH), BF16),
                            pltpu.SemaphoreType.DMA((2,)), pltpu.SemaphoreType.DMA(())],
        ),
        out_shape=[jax.ShapeDtypeStruct((TOP_K, n), I32), jax.ShapeDtypeStruct((n_rows, ROW_WIDTH), BF16)],
        compiler_params=_cparams(("arbitrary",)),
        name="sort",
    )(*tables, tail, he, idx, rank, lcol)


def _experts_kernel(be_ref, nu_ref, x_ref, wgu_ref, bgu_ref, wd_ref, bd_ref, y_ref, wgu_sc, wd_sc):
    b = pl.program_id(0)

    @pl.when(b < nu_ref[0])
    def _():
        prev = be_ref[jnp.maximum(b - 1, 0)]

        @pl.when((b == 0) | (be_ref[b] != prev))
        def _():
            rows = 128

            def cast(r, carry):
                s = pl.multiple_of(r * rows, rows)
                wgu_sc[pl.ds(s, rows), :] = wgu_ref[pl.ds(s, rows), :].astype(BF16)
                wd_sc[pl.ds(s, rows), :] = wd_ref[pl.ds(s, rows), :].astype(BF16)
                return carry

            lax.fori_loop(0, D_MODEL // rows, cast, 0)

        ext = x_ref[:, D_MODEL:].astype(F32)
        me = jnp.full((1, 1), be_ref[b], I32).astype(F32)
        gate = jnp.zeros((x_ref.shape[0], 1), F32)
        for k in range(TOP_K):
            c = TOP_K + 3 * k
            gk = ext[:, c:c + 1] + ext[:, c + 1:c + 2] + ext[:, c + 2:c + 3]
            gate = gate + jnp.where(ext[:, k:k + 1] == me, gk, 0.0)

        gu = _dot(x_ref[:, :D_MODEL], wgu_sc[...]) + bgu_ref[...]
        g = jnp.minimum(gu[:, :D_FF], SWIGLU_LIMIT)
        u = jnp.clip(gu[:, D_FF:], -SWIGLU_LIMIT, SWIGLU_LIMIT)
        act = (u + 1.0) * (g / (1.0 + jnp.exp(-SWIGLU_ALPHA * g)))
        y_ref[...] = (gate * (_dot(act.astype(BF16), wd_sc[...]) + bd_ref[...])).astype(BF16)

    @pl.when(b >= nu_ref[0])
    def _():
        y_ref[...] = jnp.zeros(y_ref.shape, BF16)


def _experts(block_e, n_used, xs, w_gate_up, b_gate_up, w_down, b_down):
    n_rows = xs.shape[0]
    tm = MOE_TILE
    n_blocks = n_rows // tm
    blk = lambda b, be, nu: (jnp.minimum(b, nu[0] - 1), 0)
    exp3 = lambda b, be, nu: (be[b], 0, 0)
    return pl.pallas_call(
        _experts_kernel,
        grid_spec=pltpu.PrefetchScalarGridSpec(
            num_scalar_prefetch=2,
            grid=(n_blocks,),
            in_specs=[
                pl.BlockSpec((tm, ROW_WIDTH), blk),
                pl.BlockSpec((None, D_MODEL, 2 * D_FF), exp3),
                pl.BlockSpec((None, 1, 2 * D_FF), exp3),
                pl.BlockSpec((None, D_FF, D_MODEL), exp3),
                pl.BlockSpec((None, 1, D_MODEL), exp3),
            ],
            out_specs=pl.BlockSpec((tm, D_MODEL), lambda b, be, nu: (b, 0)),
            scratch_shapes=[pltpu.VMEM((D_MODEL, 2 * D_FF), BF16), pltpu.VMEM((D_FF, D_MODEL), BF16)],
        ),
        out_shape=jax.ShapeDtypeStruct((n_rows, D_MODEL), BF16),
        compiler_params=_cparams(("arbitrary",)),
        name="experts",
    )(block_e, n_used, xs, w_gate_up, b_gate_up.reshape(N_EXPERTS, 1, 2 * D_FF), w_down,
      b_down.reshape(N_EXPERTS, 1, D_MODEL))


def _combine_kernel(loff_ref, goff_ref, len_ref, yb_hbm, lst_ref, x1_ref, mod_ref, o_ref, ybuf, sems):
    i = pl.program_id(0)
    slot = i % 2
    tables = (loff_ref, goff_ref, len_ref)

    @pl.when(i == 0)
    def _():
        ybuf[...] = jnp.zeros(ybuf.shape, BF16)
        _segment_copies(*tables, 0, ybuf.at[0], yb_hbm, sems.at[0], False, _start)

    @pl.when(i + 1 < pl.num_programs(0))
    def _():
        _segment_copies(*tables, i + 1, ybuf.at[1 - slot], yb_hbm, sems.at[1 - slot], False, _start)

    _wait_rows(_tile_rows(loff_ref, len_ref, i), ybuf.at[slot], yb_hbm, sems.at[slot], False)

    lst = lst_ref[...]
    t = lst.shape[0]
    targets = [lst[:, k:k + 1] for k in range(TOP_K)]
    y = jnp.zeros((t, D_MODEL), F32)
    for jc in range(LOCAL_SLOTS // SLOT_CHUNK):
        j = lax.broadcasted_iota(I32, (t, SLOT_CHUNK), 1) + jc * SLOT_CHUNK
        pick = _one_hot_any(j, targets).astype(BF16)
        y = y + _dot(pick, ybuf[slot, jc * SLOT_CHUNK:(jc + 1) * SLOT_CHUNK, :])
    o_ref[...] = x1_ref[...] + mod_ref[5:6, :] * y


def _combine(tables, yb, ls_t, x1, mod_b, seq):
    n = x1.shape[0]
    t = ROW_TILE
    per_b = seq // t
    return pl.pallas_call(
        _combine_kernel,
        grid_spec=pltpu.PrefetchScalarGridSpec(
            num_scalar_prefetch=3,
            grid=(n // t,),
            in_specs=[
                pl.BlockSpec(memory_space=pl.ANY),
                pl.BlockSpec((t, TOP_K), lambda i, *_: (i, 0)),
                pl.BlockSpec((t, D_MODEL), lambda i, *_: (i, 0)),
                pl.BlockSpec((None, 6, D_MODEL), lambda i, *_: (i // per_b, 0, 0)),
            ],
            out_specs=pl.BlockSpec((t, D_MODEL), lambda i, *_: (i, 0)),
            scratch_shapes=[pltpu.VMEM((2, LOCAL_SLOTS, D_MODEL), BF16), pltpu.SemaphoreType.DMA((2,))],
        ),
        out_shape=jax.ShapeDtypeStruct((n, D_MODEL), F32),
        compiler_params=_cparams(("arbitrary",)),
        name="combine",
    )(*tables, yb, ls_t, x1, mod_b)


def _layer(x, c, w_ada, b_ada, norm1_g, w_in, a_q_norm_g, a_k_norm_g, b_q_norm_g, b_k_norm_g, lambda_q1,
           lambda_k1, lambda_q2, lambda_k2, diff_norm_g, w_out, norm2_g, w_router, b_router, w_gate_up,
           b_gate_up, w_down, b_down):
    bsz, seq, _ = x.shape
    n = bsz * seq
    x2 = x.reshape(n, D_MODEL)
    mod_b = _ada(c, w_ada, b_ada).transpose(1, 0, 2)

    qk_gains = jnp.stack([
        jnp.tile(a_q_norm_g, A_HEADS) * ATTN_SCALE, jnp.tile(a_k_norm_g, A_HEADS), jnp.tile(b_k_norm_g, 2 * B_HEADS)])
    bq_gain = (jnp.tile(b_q_norm_g, 2 * B_HEADS) * ATTN_SCALE).reshape(B_WIDTH, 1)
    *a_parts, bk, bqt, bvt = _inproj(x2, mod_b, norm1_g, w_in, qk_gains, bq_gain, bsz, seq)

    dil_outs = _dilated(a_parts, bsz, seq)
    lam = (jnp.exp(jnp.sum(lambda_q1 * lambda_k1)) - jnp.exp(jnp.sum(lambda_q2 * lambda_k2)) + LAMBDA_INIT)
    slopes_b = 2.0 ** (-8.0 * np.arange(1, B_HEADS + 1) / B_HEADS)
    scalars = jnp.concatenate([lam.reshape(1), jnp.asarray(slopes_b, F32)]).astype(F32)
    bk4 = bk.reshape(bsz, seq // DIFF_BLOCK, DIFF_BLOCK, B_WIDTH)
    o_b = _diff(bqt, bk4, bvt, scalars, diff_norm_g, bsz, seq).reshape(n, B_WIDTH)

    x1, he, idx, rank, cnt = _router(dil_outs, o_b, x2, mod_b, w_out, norm2_g, w_router, b_router, seq)

    n_tiles = n // ROW_TILE
    counts = cnt[:, :, 0].astype(I32)
    seg = (counts + SEG_ALIGN - 1) // SEG_ALIGN * SEG_ALIGN
    loff = jnp.cumsum(seg, axis=1) - seg
    region = jnp.sum(seg, axis=0)
    padded = (region + MOE_TILE - 1) // MOE_TILE * MOE_TILE
    pad_end = jnp.cumsum(padded)
    pad_start = pad_end - padded
    goff = pad_start[None, :] + jnp.cumsum(seg, axis=0) - seg
    tables = (loff.reshape(-1), goff.reshape(-1), seg.reshape(-1))
    n_blocks = (n * TOP_K + n_tiles * N_EXPERTS * (SEG_ALIGN - 1) + N_EXPERTS * (MOE_TILE - 1)) // MOE_TILE
    n_used = (pad_end[-1] // MOE_TILE).astype(I32)
    tail = jnp.concatenate([pad_start + region, padded - region, n_used.reshape(1)])
    blk_ids = jnp.minimum(jnp.arange(n_blocks, dtype=I32), n_used - 1)
    block_e = jnp.sum((pad_end[None, :] <= (blk_ids * MOE_TILE)[:, None]).astype(I32), axis=1)
    block_e = jnp.minimum(block_e, N_EXPERTS - 1)

    ls, xs = _sort(tables, tail, he, idx, rank, loff.reshape(n_tiles, N_EXPERTS, 1), n_blocks * MOE_TILE)
    yb = _experts(block_e, n_used.reshape(1), xs, w_gate_up, b_gate_up, w_down, b_down)
    out = _combine(tables, yb, ls.T, x1, mod_b, seq)
    return out.reshape(bsz, seq, D_MODEL)


def kernel(x, c, w_ada, b_ada, norm1_g, w_in, a_q_norm_g, a_k_norm_g, b_q_norm_g, b_k_norm_g, lambda_q1, lambda_k1,
           lambda_q2, lambda_k2, diff_norm_g, w_out, norm2_g, w_router, b_router, w_gate_up, b_gate_up, w_down,
           b_down):
    args = (w_ada, b_ada, norm1_g, w_in, a_q_norm_g, a_k_norm_g, b_q_norm_g, b_k_norm_g, lambda_q1, lambda_k1,
            lambda_q2, lambda_k2, diff_norm_g, w_out, norm2_g, w_router, b_router, w_gate_up, b_gate_up, w_down,
            b_down)
    return _layer(x, c, *[a[0] for a in args])
```

```python
import functools

import numpy as np
import jax
import jax.numpy as jnp
from jax import lax
from jax.experimental import pallas as pl
from jax.experimental.pallas import tpu as pltpu

F32 = jnp.float32
BF16 = jnp.bfloat16
I32 = jnp.int32
U32 = jnp.uint32

D_MODEL = 1024
HEAD_DIM = 64
A_WIDTH = 512
B_WIDTH = 512
A_HEADS = 8
B_HEADS = 4
IN_WIDTH = 3072
DIL_PATTERNS = ((128, 1), (512, 4), (2048, 16))
BAND = 128
N_EXPERTS = 32
TOP_K = 4
D_FF = 1024
SWIGLU_LIMIT = 7.0
SWIGLU_ALPHA = 1.702
RMS_EPS = 1e-6
ATTN_SCALE = HEAD_DIM ** -0.5
LAMBDA_INIT = 0.8 - 0.6 * 1.0

LANES = 128
ROW_TILE = 512
DIFF_BLOCK = 512
DIFF_LANES = 256
DIFF_ROWS = 64
MOE_TILE = 256
EXTRA_LANES = 128
ROW_WIDTH = D_MODEL + EXTRA_LANES
SEG_ALIGN = 16
SEG_SIZES = (512, 256, 128, 64, 32, 16)
SLOT_CHUNK = 512
LOCAL_SLOTS = 2560
assert ROW_TILE == DIFF_BLOCK == SEG_SIZES[0]
assert LOCAL_SLOTS >= ROW_TILE * TOP_K + N_EXPERTS * (SEG_ALIGN - 1) and LOCAL_SLOTS % SLOT_CHUNK == 0
VMEM_LIMIT = 56 * 1024 * 1024


def _cparams(sem, **flags):
    return pltpu.CompilerParams(dimension_semantics=sem, vmem_limit_bytes=VMEM_LIMIT, flags=flags or None)


def _split_bf16(a):
    hi = a.astype(BF16)
    lo = (a - hi.astype(F32)).astype(BF16)
    return hi, lo


def _dot_nt(a, b):
    return lax.dot_general(a, b, (((1,), (1,)), ((), ())), preferred_element_type=F32)


def _dot(a, b):
    return jnp.dot(a, b, preferred_element_type=F32)


def _ada_kernel(c_ref, w_ref, b_ref, o_ref):
    c = c_ref[...]
    s = c / (1.0 + jnp.exp(-c))
    sh, sl = _split_bf16(s)
    wh, wl = _split_bf16(w_ref[...])
    o_ref[0] = _dot(sh, wh) + _dot(sh, wl) + _dot(sl, wh) + b_ref[0]


def _ada(c, w_ada, b_ada):
    bsz = c.shape[0]
    return pl.pallas_call(
        _ada_kernel,
        grid=(6,),
        in_specs=[
            pl.BlockSpec((bsz, D_MODEL), lambda j: (0, 0)),
            pl.BlockSpec((D_MODEL, D_MODEL), lambda j: (0, j)),
            pl.BlockSpec((1, 1, D_MODEL), lambda j: (j, 0, 0)),
        ],
        out_specs=pl.BlockSpec((1, bsz, D_MODEL), lambda j: (j, 0, 0)),
        out_shape=jax.ShapeDtypeStruct((6, bsz, D_MODEL), F32),
        compiler_params=_cparams(("arbitrary",)),
        name="ada",
    )(c, w_ada, b_ada.reshape(6, 1, D_MODEL))


def _inproj_kernel(x_ref, mod_ref, g1_ref, wn_ref, wt_ref, gm_ref, qkg_ref, bqg_ref, a1_ref, a4_ref, a16_ref, bk_ref,
                   bqt_ref, bvt_ref, a_sc):
    a_refs = (a1_ref, a4_ref, a16_ref)
    x = x_ref[...]
    y = x * lax.rsqrt(jnp.mean(x * x, axis=-1, keepdims=True) + RMS_EPS) * g1_ref[...]
    h = (y * (1.0 + mod_ref[1:2, :]) + mod_ref[0:1, :]).astype(BF16)
    p = _dot(h, wn_ref[...])
    gm = gm_ref[...]

    def head_norm(t, g):
        ss = _dot((t * t).astype(BF16), gm)
        return t * lax.rsqrt(ss * (1.0 / HEAD_DIM) + RMS_EPS) * g

    w = A_WIDTH
    a_part = jnp.concatenate([head_norm(p[:, 0 * w:1 * w], qkg_ref[0:1, :]),
                              head_norm(p[:, 1 * w:2 * w], qkg_ref[1:2, :]), p[:, 2 * w:3 * w]], axis=1)
    n_col = a_sc.shape[0]
    for c in range(n_col):
        a_sc[c] = a_part[:, c * LANES:(c + 1) * LANES]
    for a_ref, (_, dil) in zip(a_refs, DIL_PATTERNS):
        if dil == 1:
            a_ref[0] = a_part.astype(BF16)
            continue
        rows = a_part.shape[0] // dil
        for r in range(dil):
            for c in range(n_col):
                a_ref[r, :, c * LANES:(c + 1) * LANES] = a_sc[c, pl.ds(r, rows, stride=dil), :].astype(BF16)
    bk_ref[...] = head_norm(p[:, 3 * w:4 * w], qkg_ref[2:3, :]).astype(BF16)

    pt = _dot_nt(wt_ref[...], h)
    t = pt.shape[1]
    bq = pt[:w].reshape(w // HEAD_DIM, HEAD_DIM, t)
    ss = jnp.sum(bq * bq, axis=1, keepdims=True)
    bq = (bq * lax.rsqrt(ss * (1.0 / HEAD_DIM) + RMS_EPS)).reshape(w, t) * bqg_ref[...]
    bqt_ref[...] = bq.astype(BF16)
    bvt_ref[...] = pt[w:].astype(BF16)


def _inproj(x2, mod_b, norm1_g, w_in, qk_gains, bq_gain, bsz, seq):
    n = x2.shape[0]
    t = ROW_TILE
    per_b = seq // t
    w = A_WIDTH
    head_of_lane = np.arange(w) // HEAD_DIM
    gmat = jnp.asarray(head_of_lane[:, None] == head_of_lane[None, :], BF16)
    w_bf = w_in.astype(BF16)
    w_nat = jnp.concatenate([w_bf[:, :3 * w], w_bf[:, 4 * w:5 * w]], axis=1)
    w_tr = jnp.concatenate([w_bf[:, 3 * w:4 * w], w_bf[:, 5 * w:]], axis=1).T
    tr_spec = pl.BlockSpec((None, None, w, t), lambda i: (i // per_b, i % per_b, 0, 0))
    tr_shape = jax.ShapeDtypeStruct((bsz, per_b, w, t), BF16)
    a_specs = [pl.BlockSpec((None, dil, t // dil, 3 * w), lambda i: (i // per_b, 0, i % per_b, 0))
               for _, dil in DIL_PATTERNS]
    a_shapes = [jax.ShapeDtypeStruct((bsz, dil, seq // dil, 3 * w), BF16) for _, dil in DIL_PATTERNS]
    return pl.pallas_call(
        _inproj_kernel,
        grid=(n // t,),
        in_specs=[
            pl.BlockSpec((t, D_MODEL), lambda i: (i, 0)),
            pl.BlockSpec((None, 6, D_MODEL), lambda i: (i // per_b, 0, 0)),
            pl.BlockSpec((1, D_MODEL), lambda i: (0, 0)),
            pl.BlockSpec((D_MODEL, 4 * w), lambda i: (0, 0)),
            pl.BlockSpec((2 * w, D_MODEL), lambda i: (0, 0)),
            pl.BlockSpec((w, w), lambda i: (0, 0)),
            pl.BlockSpec((3, w), lambda i: (0, 0)),
            pl.BlockSpec((w, 1), lambda i: (0, 0)),
        ],
        out_specs=a_specs + [pl.BlockSpec((t, w), lambda i: (i, 0)), tr_spec, tr_spec],
        out_shape=a_shapes + [jax.ShapeDtypeStruct((n, w), BF16), tr_shape, tr_shape],
        scratch_shapes=[pltpu.VMEM((3 * w // LANES, t, LANES), F32)],
        compiler_params=_cparams(("parallel",)),
        name="inproj",
    )(x2, mod_b, norm1_g.reshape(1, D_MODEL), w_nat, w_tr, gmat, qk_gains, bq_gain)


def _dilated_bias_table():
    ik = np.arange(2 * BAND)[:, None]
    iq = np.arange(BAND)[None, :]
    delta = iq - ik + BAND
    in_band = (delta >= 0) & (delta <= BAND)
    slopes = 2.0 ** (-8.0 * np.arange(1, A_HEADS + 1) / A_HEADS)
    tbl = np.zeros((len(DIL_PATTERNS), 2, A_HEADS // 2, 2 * BAND, 2 * BAND), np.float32)
    for p, (_, dil) in enumerate(DIL_PATTERNS):
        for first in range(2):
            valid = in_band & ((ik >= BAND) if first else True)
            for h in range(A_HEADS):
                cols = slice(BAND * (h % 2), BAND * (h % 2 + 1))
                tbl[p, first, h // 2, :, cols] = np.where(valid, -slopes[h] * (delta * dil), -np.inf)
    return jnp.asarray(tbl)


def _dilated_kernel(tbl_ref, *refs, n_blocks):
    lanes = 2 * HEAD_DIM
    n = pl.program_id(1)
    n_pat = len(DIL_PATTERNS)
    s_sc, p_sc = refs[len(refs) - 2:]
    outs = refs[len(refs) - 2 - 2 * n_pat:len(refs) - 2]
    lane = lax.broadcasted_iota(I32, (BAND, lanes), 1)
    low = lane < HEAD_DIM
    row = lax.broadcasted_iota(I32, (lanes, BAND), 0)
    top = row < HEAD_DIM
    work = []
    pos = 0
    for p, (_, dil) in enumerate(DIL_PATTERNS):
        per_res = n_blocks // dil
        if per_res > 1:
            q_ref, kp_ref, kc_ref, vp_ref, vc_ref = refs[pos:pos + 5]
            pos += 5
            first = (n % per_res == 0).astype(I32)
        else:
            q_ref, kc_ref, vc_ref = refs[pos:pos + 3]
            kp_ref = vp_ref = first = None
            pos += 3
        for j in range(A_HEADS // 2):
            work.append((p, j, (q_ref, kp_ref, kc_ref, vp_ref, vc_ref), first))

    def keys_of(prev_ref, cur_ref, cs):
        return cur_ref[:, cs] if prev_ref is None else jnp.concatenate([prev_ref[:, cs], cur_ref[:, cs]], axis=0)

    for w, (p, j, (q_ref, kp_ref, kc_ref, _, _), first) in enumerate(work):
        cs = slice(lanes * j, lanes * (j + 1))
        q2 = q_ref[:, cs]
        zero = jnp.zeros_like(q2)
        qcat = jnp.concatenate([jnp.where(low, q2, zero), jnp.where(low, zero, q2)], axis=0)
        k2 = keys_of(kp_ref, kc_ref, cs)
        bias = tbl_ref[p, 1, j, BAND:, :] if kp_ref is None else tbl_ref[p, first, j]
        s_sc[w, :k2.shape[0], :] = _dot_nt(k2, qcat) + bias
    lses = []
    for w, (p, j, (_, kp_ref, _, _, _), _) in enumerate(work):
        nk = BAND if kp_ref is None else 2 * BAND
        s = s_sc[w, :nk, :]
        m = jnp.max(s, axis=0, keepdims=True)
        e = jnp.exp(s - m)
        den = jnp.sum(e, axis=0, keepdims=True)
        p_sc[w, :nk, :] = (e * (1.0 / den)).astype(BF16)
        lses.append(m + jnp.log(den))
    for w, (p, j, (_, kp_ref, _, vp_ref, vc_ref), _) in enumerate(work):
        nk = BAND if kp_ref is None else 2 * BAND
        cs = slice(lanes * j, lanes * (j + 1))
        v2 = keys_of(vp_ref, vc_ref, cs)
        r = lax.dot_general(p_sc[w, :nk, :], v2, (((0,), (0,)), ((), ())), preferred_element_type=F32)
        o_ref, l_ref = outs[2 * p:2 * p + 2]
        o_ref[:, cs] = jnp.where(low, r[:BAND], r[BAND:])
        lse = lses[w]
        lse_t = jnp.where(top, jnp.broadcast_to(lse[:, :BAND], (lanes, BAND)),
                          jnp.broadcast_to(lse[:, BAND:], (lanes, BAND)))
        l_ref[:, cs] = lse_t.T


def _dilated(a_parts, bsz, seq):
    n_blocks = seq // BAND
    in_specs = [pl.BlockSpec((len(DIL_PATTERNS), 2, A_HEADS // 2, 2 * BAND, 2 * BAND), lambda b, n: (0, 0, 0, 0, 0))]
    n_work = len(DIL_PATTERNS) * A_HEADS // 2
    args = [_dilated_bias_table()]
    out_specs, out_shapes = [], []
    blk = (None, None, BAND, A_WIDTH)
    for a_part, (_, dil) in zip(a_parts, DIL_PATTERNS):
        per_res = n_blocks // dil

        def cur(col, per_res=per_res):
            return lambda b, n: (b, n // per_res, n % per_res, col)

        def prev(col, per_res=per_res):
            return lambda b, n: (b, n // per_res, jnp.maximum(n % per_res - 1, 0), col)

        if per_res > 1:
            in_specs += [pl.BlockSpec(blk, cur(0)), pl.BlockSpec(blk, prev(1)), pl.BlockSpec(blk, cur(1)),
                         pl.BlockSpec(blk, prev(2)), pl.BlockSpec(blk, cur(2))]
            args += [a_part] * 5
        else:
            in_specs += [pl.BlockSpec(blk, cur(0)), pl.BlockSpec(blk, cur(1)), pl.BlockSpec(blk, cur(2))]
            args += [a_part] * 3
        out_specs += [pl.BlockSpec(blk, cur(0))] * 2
        out_shapes += [jax.ShapeDtypeStruct((bsz, dil, seq // dil, A_WIDTH), F32)] * 2
    return pl.pallas_call(
        functools.partial(_dilated_kernel, n_blocks=n_blocks),
        grid=(bsz, n_blocks),
        in_specs=in_specs,
        out_specs=out_specs,
        out_shape=out_shapes,
        scratch_shapes=[pltpu.VMEM((n_work, 2 * BAND, 2 * BAND), F32), pltpu.VMEM((n_work, 2 * BAND, 2 * BAND), BF16)],
        compiler_params=_cparams(("parallel", "parallel")),
        name="dilated",
    )(*args)


def _diff_kernel(sc_ref, qt_ref, k_ref, vt_ref, g_ref, o_ref, m_sc, l_sc, a_sc, acc_sc, bias_sc, s_sc, p_sc):
    t = DIFF_BLOCK
    h = pl.program_id(1)
    qi = pl.program_id(2)
    slope = jnp.full((1, 1), sc_ref[1 + h], F32)
    lam = sc_ref[0]
    qt = qt_ref[...]
    row = lax.broadcasted_iota(I32, qt.shape, 0)
    zero = jnp.zeros_like(qt)
    qst = jnp.concatenate([jnp.where(row < HEAD_DIM, qt, zero), jnp.where(row >= HEAD_DIM, qt, zero)], axis=1)
    rel = lax.broadcasted_iota(I32, (t, t), 1) - lax.broadcasted_iota(I32, (t, t), 0)
    rel2 = jnp.concatenate([rel, rel], axis=1)
    bias = -slope * rel2.astype(F32)
    bias_sc[0] = bias
    bias_sc[1] = jnp.where(rel2 >= 0, bias, -jnp.inf)
    m_sc[...] = jnp.full(m_sc.shape, -jnp.inf, F32)
    l_sc[...] = jnp.zeros(l_sc.shape, F32)
    acc_sc[...] = jnp.zeros(acc_sc.shape, F32)

    groups = [slice(g * DIFF_LANES, (g + 1) * DIFF_LANES) for g in range(2 * t // DIFF_LANES)]
    chunks = [slice(r * DIFF_ROWS, (r + 1) * DIFF_ROWS) for r in range(t // DIFF_ROWS)]

    def block(n, diag):
        c = -slope * (jnp.full((1, 1), (qi - n) * t, I32)).astype(F32)
        kb, vtb = k_ref[n], vt_ref[n]
        for g, cols in enumerate(groups):
            s_sc[g] = _dot(kb, qst[:, cols]) + bias_sc[diag, :, cols]
        for g, cols in enumerate(groups):
            top = s_sc[g, chunks[0], :]
            for rows in chunks[1:]:
                top = jnp.maximum(top, s_sc[g, rows, :])
            m_prev = m_sc[:, cols]
            m_new = jnp.maximum(m_prev, jnp.max(top, axis=0, keepdims=True) + c)
            alpha = jnp.exp(m_prev - m_new)
            shift = m_new - c
            part = jnp.zeros(top.shape, F32)
            for rows in chunks:
                e = jnp.exp(s_sc[g, rows, :] - shift)
                p_sc[g, rows, :] = e.astype(BF16)
                part = part + e
            l_sc[:, cols] = alpha * l_sc[:, cols] + jnp.sum(part, axis=0, keepdims=True)
            m_sc[:, cols] = m_new
            a_sc[:, cols] = alpha
        for g, cols in enumerate(groups):
            acc_sc[:, cols] = a_sc[:, cols] * acc_sc[:, cols] + _dot(vtb, p_sc[g])

    def body(n, carry):
        block(n, 0)
        return carry

    lax.fori_loop(0, qi, body, 0)
    block(qi, 1)
    o = acc_sc[...] / l_sc[...]
    o = o[:, :t] - lam * o[:, t:]
    o = o * lax.rsqrt(jnp.mean(o * o, axis=0, keepdims=True) + RMS_EPS) * g_ref[...]
    o_ref[...] = (o * (1.0 - LAMBDA_INIT)).T.astype(BF16)


def _diff(bqt, bk, bvt, scalars, diff_norm_g, bsz, seq):
    t = DIFF_BLOCK
    nb = seq // t
    lanes = 2 * HEAD_DIM
    return pl.pallas_call(
        _diff_kernel,
        grid=(bsz, B_HEADS, nb),
        in_specs=[
            pl.BlockSpec(memory_space=pltpu.SMEM),
            pl.BlockSpec((None, None, lanes, t), lambda b, h, i: (b, i, h, 0)),
            pl.BlockSpec((None, nb, t, lanes), lambda b, h, i: (b, 0, 0, h)),
            pl.BlockSpec((None, nb, lanes, t), lambda b, h, i: (b, 0, h, 0)),
            pl.BlockSpec((lanes, 1), lambda b, h, i: (0, 0)),
        ],
        out_specs=pl.BlockSpec((None, t, lanes), lambda b, h, i: (b, i, h)),
        out_shape=jax.ShapeDtypeStruct((bsz, seq, B_WIDTH), BF16),
        scratch_shapes=[pltpu.VMEM((1, 2 * t), F32)] * 3 + [
            pltpu.VMEM((lanes, 2 * t), F32), pltpu.VMEM((2, t, 2 * t), F32),
            pltpu.VMEM((2 * t // DIFF_LANES, t, DIFF_LANES), F32), pltpu.VMEM((2 * t // DIFF_LANES, t, DIFF_LANES), BF16)],
        compiler_params=_cparams(("parallel", "parallel", "parallel")),
        name="diff",
    )(scalars, bqt, bk, bvt, diff_norm_g.reshape(lanes, 1))


def _router_kernel(o0, l0, o1, l1, o2, l2, ob_ref, x_ref, mod_ref, wout_ref, g2_ref, wrh_ref, wrl_ref, br_ref,
                   tri_ref, x1_ref, he_ref, idx_ref, rank_ref, cnt_ref, *order_sc):
    def token_order(ref, scratch):
        dil, rows, width = ref.shape
        if dil == 1:
            return ref[0]
        n_col = width // LANES
        for r in range(dil):
            for c in range(n_col):
                scratch[c, pl.ds(r, rows, stride=dil), :] = ref[r, :, c * LANES:(c + 1) * LANES]
        return jnp.concatenate([scratch[c] for c in range(n_col)], axis=1)

    os_ = [token_order(o0, None), token_order(o1, order_sc[0]), token_order(o2, order_sc[2])]
    ls = [token_order(l0, None), token_order(l1, order_sc[1]), token_order(l2, order_sc[3])]
    mx = jnp.maximum(jnp.maximum(ls[0], ls[1]), ls[2])
    ws = [jnp.exp(l - mx) for l in ls]
    den = ws[0] + ws[1] + ws[2]
    oa = (ws[0] * os_[0] + ws[1] * os_[1] + ws[2] * os_[2]) / den
    mixed = _dot(oa.astype(BF16), wout_ref[:A_WIDTH, :]) + _dot(ob_ref[...], wout_ref[A_WIDTH:, :])
    x1 = x_ref[...] + mod_ref[2:3, :] * mixed
    x1_ref[...] = x1
    y = x1 * lax.rsqrt(jnp.mean(x1 * x1, axis=-1, keepdims=True) + RMS_EPS) * g2_ref[...]
    h2 = y * (1.0 + mod_ref[4:5, :]) + mod_ref[3:4, :]
    he_ref[:, :D_MODEL] = h2.astype(BF16)

    hh, hl = _split_bf16(h2)
    wrh = wrh_ref[...]
    logits = _dot_nt(wrh, hh) + _dot_nt(wrh, hl) + _dot_nt(wrl_ref[...], hh) + br_ref[...]
    t = logits.shape[1]
    eid = lax.broadcasted_iota(I32, (N_EXPERTS, t), 0)
    vals, idxs, hots = [], [], []
    cur = logits
    for _ in range(TOP_K):
        v = jnp.max(cur, axis=0, keepdims=True)
        ik = jnp.min(jnp.where(cur == v, eid, N_EXPERTS), axis=0, keepdims=True)
        hot = eid == ik
        vals.append(v)
        idxs.append(ik)
        hots.append(hot)
        cur = jnp.where(hot, -jnp.inf, cur)
    es = [jnp.exp(v - vals[0]) for v in vals]
    esum = es[0] + es[1] + es[2] + es[3]
    idx_ref[...] = jnp.concatenate(idxs, axis=0)

    rows = [ik.astype(F32) for ik in idxs]
    for e in es:
        g = e / esum
        hi = g.astype(BF16).astype(F32)
        mid = (g - hi).astype(BF16).astype(F32)
        rows += [hi, mid, g - hi - mid]
    rows.append(jnp.zeros((EXTRA_LANES - len(rows), t), F32))
    he_ref[:, D_MODEL:] = jnp.concatenate(rows, axis=0).T.astype(BF16)

    sel = jnp.where(hots[0] | hots[1] | hots[2] | hots[3], 1.0, 0.0)
    before = _dot(sel.astype(BF16), tri_ref[...])
    ranks = [jnp.sum(jnp.where(hot, before, 0.0), axis=0, keepdims=True) for hot in hots]
    rank_ref[...] = jnp.concatenate(ranks, axis=0).astype(I32)
    cnt_ref[...] = jnp.broadcast_to(jnp.sum(sel, axis=1, keepdims=True), cnt_ref.shape)


def _router(dil_outs, o_b, x2, mod_b, w_out, norm2_g, w_router, b_router, seq):
    n = x2.shape[0]
    t = ROW_TILE
    per_b = seq // t
    wr_t = w_router.T
    wrh = wr_t.astype(BF16)
    wrl = (wr_t - wrh.astype(F32)).astype(BF16)
    tri = jnp.asarray(np.arange(t)[:, None] < np.arange(t)[None, :], BF16)
    row = lambda w: pl.BlockSpec((t, w), lambda i: (i, 0))
    full = lambda a, b: pl.BlockSpec((a, b), lambda i: (0, 0))
    tok = lambda: pl.BlockSpec((TOP_K, t), lambda i: (0, i))
    grouped = lambda dil: pl.BlockSpec((None, dil, t // dil, A_WIDTH), lambda i: (i // per_b, 0, i % per_b, 0))
    n_regrouped = 2 * sum(dil > 1 for _, dil in DIL_PATTERNS)
    return pl.pallas_call(
        _router_kernel,
        grid=(n // t,),
        in_specs=[grouped(dil) for _, dil in DIL_PATTERNS for _ in range(2)] + [
            row(B_WIDTH), row(D_MODEL),
            pl.BlockSpec((None, 6, D_MODEL), lambda i: (i // per_b, 0, 0)),
            full(D_MODEL, D_MODEL), full(1, D_MODEL), full(N_EXPERTS, D_MODEL), full(N_EXPERTS, D_MODEL),
            full(N_EXPERTS, 1), full(t, t),
        ],
        out_specs=[row(D_MODEL), row(ROW_WIDTH), tok(), tok(),
                   pl.BlockSpec((None, N_EXPERTS, 128), lambda i: (i, 0, 0))],
        out_shape=[
            jax.ShapeDtypeStruct((n, D_MODEL), F32),
            jax.ShapeDtypeStruct((n, ROW_WIDTH), BF16),
            jax.ShapeDtypeStruct((TOP_K, n), I32),
            jax.ShapeDtypeStruct((TOP_K, n), I32),
            jax.ShapeDtypeStruct((n // t, N_EXPERTS, 128), F32),
        ],
        scratch_shapes=[pltpu.VMEM((A_WIDTH // LANES, t, LANES), F32)] * n_regrouped,
        compiler_params=_cparams(("parallel",)),
        name="router",
    )(*dil_outs, o_b, x2, mod_b, w_out.astype(BF16), norm2_g.reshape(1, D_MODEL), wrh, wrl,
      b_router.reshape(N_EXPERTS, 1), tri)


def _segment_copies(loff_ref, goff_ref, len_ref, tile, local_ref, hbm_ref, sem, outbound, action):
    def per_expert(e, carry):
        base = tile * N_EXPERTS + e
        lo, go, n = loff_ref[base], goff_ref[base], len_ref[base]
        done = jnp.int32(0)
        for size in SEG_SIZES:
            take = (n & size) != 0
            loc = local_ref.at[pl.ds(pl.multiple_of(lo + done, SEG_ALIGN), size)]
            glob = hbm_ref.at[pl.ds(pl.multiple_of(go + done, SEG_ALIGN), size)]
            cp = pltpu.make_async_copy(loc, glob, sem) if outbound else pltpu.make_async_copy(glob, loc, sem)
            pl.when(take)(functools.partial(action, cp))
            done = done + jnp.where(take, size, 0)
        return carry

    lax.fori_loop(0, N_EXPERTS, per_expert, 0)


def _start(cp):
    cp.start()


def _wait(cp):
    cp.wait()


def _tile_rows(loff_ref, len_ref, tile):
    last = tile * N_EXPERTS + N_EXPERTS - 1
    return loff_ref[last] + len_ref[last]


def _wait_rows(rows, local_ref, hbm_ref, sem, outbound):
    size = 1 << (LOCAL_SLOTS.bit_length() - 1)
    while size >= SEG_ALIGN:
        loc, glob = local_ref.at[pl.ds(0, size)], hbm_ref.at[pl.ds(0, size)]
        cp = pltpu.make_async_copy(loc, glob, sem) if outbound else pltpu.make_async_copy(glob, loc, sem)
        pl.when((rows & size) != 0)(cp.wait)
        size //= 2


def _one_hot_any(j, targets):
    out = jnp.zeros(j.shape, F32)
    for tgt in targets:
        out = jnp.where(j == tgt, 1.0, out)
    return out


def _sort_kernel(loff_ref, goff_ref, len_ref, tail_ref, he_ref, idx_ref, rank_ref, lcol_ref, ls_ref, xs_hbm,
                 xl, zbuf, sems, zsem):
    i = pl.program_id(0)
    last = pl.num_programs(0) - 1
    slot = i % 2
    t = he_ref.shape[0]

    @pl.when(i == 0)
    def _():
        zbuf[...] = jnp.zeros(zbuf.shape, BF16)

        def tails(action):
            def per_expert(e, carry):
                off, n = tail_ref[e], tail_ref[N_EXPERTS + e]
                done = jnp.int32(0)
                for size in SEG_SIZES:
                    if size < MOE_TILE:
                        take = (n & size) != 0
                        dst = xs_hbm.at[pl.ds(pl.multiple_of(off + done, SEG_ALIGN), size)]
                        pl.when(take)(functools.partial(action, pltpu.make_async_copy(zbuf.at[pl.ds(0, size)], dst, zsem)))
                        done = done + jnp.where(take, size, 0)
                return carry

            lax.fori_loop(0, N_EXPERTS, per_expert, 0)

            def per_block(b, carry):
                dst = xs_hbm.at[pl.ds(pl.multiple_of(b * MOE_TILE, MOE_TILE), MOE_TILE)]
                action(pltpu.make_async_copy(zbuf, dst, zsem))
                return carry

            lax.fori_loop(tail_ref[2 * N_EXPERTS], xs_hbm.shape[0] // MOE_TILE, per_block, 0)

        tails(_start)
        tails(_wait)

    eid = lax.broadcasted_iota(I32, (N_EXPERTS, t), 0)
    lcol = lcol_ref[...]
    ls = []
    for k in range(TOP_K):
        off = jnp.sum(jnp.where(eid == idx_ref[k:k + 1, :], lcol, 0), axis=0, keepdims=True)
        ls.append(off + rank_ref[k:k + 1, :])
    ls_ref[...] = jnp.concatenate(ls, axis=0)

    he = he_ref[...]
    for jc in range(LOCAL_SLOTS // SLOT_CHUNK):
        j = lax.broadcasted_iota(I32, (SLOT_CHUNK, t), 0) + jc * SLOT_CHUNK
        perm = _one_hot_any(j, ls).astype(BF16)
        xl[slot, jc * SLOT_CHUNK:(jc + 1) * SLOT_CHUNK, :] = _dot(perm, he).astype(BF16)

    tables = (loff_ref, goff_ref, len_ref)
    _segment_copies(*tables, i, xl.at[slot], xs_hbm, sems.at[slot], True, _start)

    @pl.when(i > 0)
    def _():
        _wait_rows(_tile_rows(loff_ref, len_ref, i - 1), xl.at[1 - slot], xs_hbm, sems.at[1 - slot], True)

    @pl.when(i == last)
    def _():
        _wait_rows(_tile_rows(loff_ref, len_ref, i), xl.at[slot], xs_hbm, sems.at[slot], True)


def _sort(tables, tail, he, idx, rank, lcol, n_rows):
    n = he.shape[0]
    t = ROW_TILE
    tok = lambda: pl.BlockSpec((TOP_K, t), lambda i, *_: (0, i))
    return pl.pallas_call(
        _sort_kernel,
        grid_spec=pltpu.PrefetchScalarGridSpec(
            num_scalar_prefetch=4,
            grid=(n // t,),
            in_specs=[
                pl.BlockSpec((t, ROW_WIDTH), lambda i, *_: (i, 0)),
                tok(), tok(),
                pl.BlockSpec((None, N_EXPERTS, 1), lambda i, *_: (i, 0, 0)),
            ],
            out_specs=[tok(), pl.BlockSpec(memory_space=pl.ANY)],
            scratch_shapes=[pltpu.VMEM((2, LOCAL_SLOTS, ROW_WIDTH), BF16), pltpu.VMEM((MOE_TILE, ROW_WIDTH), BF16),
                            pltpu.SemaphoreType.DMA((2,)), pltpu.SemaphoreType.DMA(())],
        ),
        out_shape=[jax.ShapeDtypeStruct((TOP_K, n), I32), jax.ShapeDtypeStruct((n_rows, ROW_WIDTH), BF16)],
        compiler_params=_cparams(("arbitrary",)),
        name="sort",
    )(*tables, tail, he, idx, rank, lcol)


def _experts_kernel(blk0_ref, nblk_ref, xs_hbm, wgu_ref, bgu_ref, wd_ref, bd_ref, yb_hbm, wgu_sc, wd_sc, xbuf, ybuf,
                    xsem, ysem):
    e = pl.program_id(0)
    tm = MOE_TILE
    first, nb = blk0_ref[e], nblk_ref[e]

    def rows(i):
        return pl.ds(pl.multiple_of((first + i) * tm, tm), tm)

    def x_copy(i, slot):
        return pltpu.make_async_copy(xs_hbm.at[rows(i)], xbuf.at[slot], xsem.at[slot])

    def y_copy(i, slot):
        return pltpu.make_async_copy(ybuf.at[slot], yb_hbm.at[rows(i)], ysem.at[slot])

    @pl.when(nb > 0)
    def _():
        x_copy(0, 0).start()

    def cast(r, carry):
        s = pl.multiple_of(r * LANES, LANES)
        wgu_sc[pl.ds(s, LANES), :] = wgu_ref[pl.ds(s, LANES), :].astype(BF16)
        wd_sc[pl.ds(s, LANES), :] = wd_ref[pl.ds(s, LANES), :].astype(BF16)
        return carry

    lax.fori_loop(0, D_MODEL // LANES, cast, 0)
    me = jnp.full((1, 1), e, I32).astype(F32)

    def block(i, carry):
        slot = i % 2

        @pl.when(i + 1 < nb)
        def _():
            x_copy(i + 1, 1 - slot).start()

        x_copy(i, slot).wait()

        @pl.when(i >= 2)
        def _():
            y_copy(i - 2, slot).wait()

        ext = xbuf[slot, :, D_MODEL:].astype(F32)
        gate = jnp.zeros((tm, 1), F32)
        for k in range(TOP_K):
            c = TOP_K + 3 * k
            gk = ext[:, c:c + 1] + ext[:, c + 1:c + 2] + ext[:, c + 2:c + 3]
            gate = gate + jnp.where(ext[:, k:k + 1] == me, gk, 0.0)

        gu = _dot(xbuf[slot, :, :D_MODEL], wgu_sc[...]) + bgu_ref[...]
        g = jnp.minimum(gu[:, :D_FF], SWIGLU_LIMIT)
        u = jnp.clip(gu[:, D_FF:], -SWIGLU_LIMIT, SWIGLU_LIMIT)
        act = (u + 1.0) * (g / (1.0 + jnp.exp(-SWIGLU_ALPHA * g)))
        ybuf[slot] = (gate * (_dot(act.astype(BF16), wd_sc[...]) + bd_ref[...])).astype(BF16)
        y_copy(i, slot).start()
        return carry

    lax.fori_loop(0, nb, block, 0)

    @pl.when(nb >= 2)
    def _():
        y_copy(nb - 2, nb % 2).wait()

    @pl.when(nb >= 1)
    def _():
        y_copy(nb - 1, (nb - 1) % 2).wait()

    @pl.when(e == pl.num_programs(0) - 1)
    def _():
        ybuf[0] = jnp.zeros((tm, D_MODEL), BF16)
        used = first + nb

        def fill(action):
            def per_block(b, carry):
                dst = yb_hbm.at[pl.ds(pl.multiple_of(b * tm, tm), tm)]
                action(pltpu.make_async_copy(ybuf.at[0], dst, ysem.at[0]))
                return carry

            lax.fori_loop(used, yb_hbm.shape[0] // tm, per_block, 0)

        fill(_start)
        fill(_wait)


def _experts(first_block, n_block, xs, w_gate_up, b_gate_up, w_down, b_down):
    n_rows = xs.shape[0]
    exp3 = lambda e, *_: (e, 0, 0)
    return pl.pallas_call(
        _experts_kernel,
        grid_spec=pltpu.PrefetchScalarGridSpec(
            num_scalar_prefetch=2,
            grid=(N_EXPERTS,),
            in_specs=[
                pl.BlockSpec(memory_space=pl.ANY),
                pl.BlockSpec((None, D_MODEL, 2 * D_FF), exp3),
                pl.BlockSpec((None, 1, 2 * D_FF), exp3),
                pl.BlockSpec((None, D_FF, D_MODEL), exp3),
                pl.BlockSpec((None, 1, D_MODEL), exp3),
            ],
            out_specs=pl.BlockSpec(memory_space=pl.ANY),
            scratch_shapes=[pltpu.VMEM((D_MODEL, 2 * D_FF), BF16), pltpu.VMEM((D_FF, D_MODEL), BF16),
                            pltpu.VMEM((2, MOE_TILE, ROW_WIDTH), BF16), pltpu.VMEM((2, MOE_TILE, D_MODEL), BF16),
                            pltpu.SemaphoreType.DMA((2,)), pltpu.SemaphoreType.DMA((2,))],
        ),
        out_shape=jax.ShapeDtypeStruct((n_rows, D_MODEL), BF16),
        compiler_params=_cparams(("arbitrary",)),
        name="experts",
    )(first_block, n_block, xs, w_gate_up, b_gate_up.reshape(N_EXPERTS, 1, 2 * D_FF), w_down,
      b_down.reshape(N_EXPERTS, 1, D_MODEL))


def _combine_kernel(loff_ref, goff_ref, len_ref, yb_hbm, lst_ref, x1_ref, mod_ref, o_ref, ybuf, sems):
    i = pl.program_id(0)
    slot = i % 2
    tables = (loff_ref, goff_ref, len_ref)

    @pl.when(i == 0)
    def _():
        ybuf[...] = jnp.zeros(ybuf.shape, BF16)
        _segment_copies(*tables, 0, ybuf.at[0], yb_hbm, sems.at[0], False, _start)

    @pl.when(i + 1 < pl.num_programs(0))
    def _():
        _segment_copies(*tables, i + 1, ybuf.at[1 - slot], yb_hbm, sems.at[1 - slot], False, _start)

    _wait_rows(_tile_rows(loff_ref, len_ref, i), ybuf.at[slot], yb_hbm, sems.at[slot], False)

    lst = lst_ref[...]
    t = lst.shape[0]
    targets = [lst[:, k:k + 1] for k in range(TOP_K)]
    y = jnp.zeros((t, D_MODEL), F32)
    for jc in range(LOCAL_SLOTS // SLOT_CHUNK):
        j = lax.broadcasted_iota(I32, (t, SLOT_CHUNK), 1) + jc * SLOT_CHUNK
        pick = _one_hot_any(j, targets).astype(BF16)
        y = y + _dot(pick, ybuf[slot, jc * SLOT_CHUNK:(jc + 1) * SLOT_CHUNK, :])
    o_ref[...] = x1_ref[...] + mod_ref[5:6, :] * y


def _combine(tables, yb, ls_t, x1, mod_b, seq):
    n = x1.shape[0]
    t = ROW_TILE
    per_b = seq // t
    return pl.pallas_call(
        _combine_kernel,
        grid_spec=pltpu.PrefetchScalarGridSpec(
            num_scalar_prefetch=3,
            grid=(n // t,),
            in_specs=[
                pl.BlockSpec(memory_space=pl.ANY),
                pl.BlockSpec((t, TOP_K), lambda i, *_: (i, 0)),
                pl.BlockSpec((t, D_MODEL), lambda i, *_: (i, 0)),
                pl.BlockSpec((None, 6, D_MODEL), lambda i, *_: (i // per_b, 0, 0)),
            ],
            out_specs=pl.BlockSpec((t, D_MODEL), lambda i, *_: (i, 0)),
            scratch_shapes=[pltpu.VMEM((2, LOCAL_SLOTS, D_MODEL), BF16), pltpu.SemaphoreType.DMA((2,))],
        ),
        out_shape=jax.ShapeDtypeStruct((n, D_MODEL), F32),
        compiler_params=_cparams(("arbitrary",)),
        name="combine",
    )(*tables, yb, ls_t, x1, mod_b)


def _layer(x, c, w_ada, b_ada, norm1_g, w_in, a_q_norm_g, a_k_norm_g, b_q_norm_g, b_k_norm_g, lambda_q1,
           lambda_k1, lambda_q2, lambda_k2, diff_norm_g, w_out, norm2_g, w_router, b_router, w_gate_up,
           b_gate_up, w_down, b_down):
    bsz, seq, _ = x.shape
    n = bsz * seq
    x2 = x.reshape(n, D_MODEL)
    mod_b = _ada(c, w_ada, b_ada).transpose(1, 0, 2)

    qk_gains = jnp.stack([
        jnp.tile(a_q_norm_g, A_HEADS) * ATTN_SCALE, jnp.tile(a_k_norm_g, A_HEADS), jnp.tile(b_k_norm_g, 2 * B_HEADS)])
    bq_gain = (jnp.tile(b_q_norm_g, 2 * B_HEADS) * ATTN_SCALE).reshape(B_WIDTH, 1)
    *a_parts, bk, bqt, bvt = _inproj(x2, mod_b, norm1_g, w_in, qk_gains, bq_gain, bsz, seq)

    dil_outs = _dilated(a_parts, bsz, seq)
    lam = (jnp.exp(jnp.sum(lambda_q1 * lambda_k1)) - jnp.exp(jnp.sum(lambda_q2 * lambda_k2)) + LAMBDA_INIT)
    slopes_b = 2.0 ** (-8.0 * np.arange(1, B_HEADS + 1) / B_HEADS)
    scalars = jnp.concatenate([lam.reshape(1), jnp.asarray(slopes_b, F32)]).astype(F32)
    bk4 = bk.reshape(bsz, seq // DIFF_BLOCK, DIFF_BLOCK, B_WIDTH)
    o_b = _diff(bqt, bk4, bvt, scalars, diff_norm_g, bsz, seq).reshape(n, B_WIDTH)

    x1, he, idx, rank, cnt = _router(dil_outs, o_b, x2, mod_b, w_out, norm2_g, w_router, b_router, seq)

    n_tiles = n // ROW_TILE
    counts = cnt[:, :, 0].astype(I32)
    seg = (counts + SEG_ALIGN - 1) // SEG_ALIGN * SEG_ALIGN
    loff = jnp.cumsum(seg, axis=1) - seg
    region = jnp.sum(seg, axis=0)
    padded = (region + MOE_TILE - 1) // MOE_TILE * MOE_TILE
    pad_end = jnp.cumsum(padded)
    pad_start = pad_end - padded
    goff = pad_start[None, :] + jnp.cumsum(seg, axis=0) - seg
    tables = (loff.reshape(-1), goff.reshape(-1), seg.reshape(-1))
    n_blocks = (n * TOP_K + n_tiles * N_EXPERTS * (SEG_ALIGN - 1) + N_EXPERTS * (MOE_TILE - 1)) // MOE_TILE
    n_used = (pad_end[-1] // MOE_TILE).astype(I32)
    tail = jnp.concatenate([pad_start + region, padded - region, n_used.reshape(1)])

    ls, xs = _sort(tables, tail, he, idx, rank, loff.reshape(n_tiles, N_EXPERTS, 1), n_blocks * MOE_TILE)
    yb = _experts(pad_start // MOE_TILE, padded // MOE_TILE, xs, w_gate_up, b_gate_up, w_down, b_down)
    out = _combine(tables, yb, ls.T, x1, mod_b, seq)
    return out.reshape(bsz, seq, D_MODEL)


def kernel(x, c, w_ada, b_ada, norm1_g, w_in, a_q_norm_g, a_k_norm_g, b_q_norm_g, b_k_norm_g, lambda_q1, lambda_k1,
           lambda_q2, lambda_k2, diff_norm_g, w_out, norm2_g, w_router, b_router, w_gate_up, b_gate_up, w_down,
           b_down):
    args = (w_ada, b_ada, norm1_g, w_in, a_q_norm_g, a_k_norm_g, b_q_norm_g, b_k_norm_g, lambda_q1, lambda_k1,
            lambda_q2, lambda_k2, diff_norm_g, w_out, norm2_g, w_router, b_router, w_gate_up, b_gate_up, w_down,
            b_down)
    return _layer(x, c, *[a[0] for a in args])
```

```python
import functools

import numpy as np
import jax
import jax.numpy as jnp
from jax import lax
from jax.experimental import pallas as pl
from jax.experimental.pallas import tpu as pltpu

F32 = jnp.float32
BF16 = jnp.bfloat16
I32 = jnp.int32
U32 = jnp.uint32

D_MODEL = 1024
HEAD_DIM = 64
A_WIDTH = 512
B_WIDTH = 512
A_HEADS = 8
B_HEADS = 4
IN_WIDTH = 3072
DIL_PATTERNS = ((128, 1), (512, 4), (2048, 16))
BAND = 128
N_EXPERTS = 32
TOP_K = 4
D_FF = 1024
SWIGLU_LIMIT = 7.0
SWIGLU_ALPHA = 1.702
RMS_EPS = 1e-6
ATTN_SCALE = HEAD_DIM ** -0.5
LAMBDA_INIT = 0.8 - 0.6 * 1.0

LANES = 128
ROW_TILE = 512
DIFF_BLOCK = 512
DIFF_LANES = 256
DIFF_ROWS = 64
MOE_TILE = 512
EXTRA_LANES = 128
ROW_WIDTH = D_MODEL + EXTRA_LANES
SEG_ALIGN = 16
SEG_SIZES = (512, 256, 128, 64, 32, 16)
SLOT_CHUNK = 512
LOCAL_SLOTS = 2560
assert ROW_TILE == DIFF_BLOCK == SEG_SIZES[0]
assert LOCAL_SLOTS >= ROW_TILE * TOP_K + N_EXPERTS * (SEG_ALIGN - 1) and LOCAL_SLOTS % SLOT_CHUNK == 0
VMEM_LIMIT = 56 * 1024 * 1024


def _cparams(sem, **flags):
    return pltpu.CompilerParams(dimension_semantics=sem, vmem_limit_bytes=VMEM_LIMIT, flags=flags or None)


def _split_bf16(a):
    hi = a.astype(BF16)
    lo = (a - hi.astype(F32)).astype(BF16)
    return hi, lo


def _dot_nt(a, b):
    return lax.dot_general(a, b, (((1,), (1,)), ((), ())), preferred_element_type=F32)


def _dot(a, b):
    return jnp.dot(a, b, preferred_element_type=F32)


def _ada_kernel(c_ref, w_ref, b_ref, o_ref):
    c = c_ref[...]
    s = c / (1.0 + jnp.exp(-c))
    sh, sl = _split_bf16(s)
    wh, wl = _split_bf16(w_ref[...])
    o_ref[0] = _dot(sh, wh) + _dot(sh, wl) + _dot(sl, wh) + b_ref[0]


def _ada(c, w_ada, b_ada):
    bsz = c.shape[0]
    return pl.pallas_call(
        _ada_kernel,
        grid=(6,),
        in_specs=[
            pl.BlockSpec((bsz, D_MODEL), lambda j: (0, 0)),
            pl.BlockSpec((D_MODEL, D_MODEL), lambda j: (0, j)),
            pl.BlockSpec((1, 1, D_MODEL), lambda j: (j, 0, 0)),
        ],
        out_specs=pl.BlockSpec((1, bsz, D_MODEL), lambda j: (j, 0, 0)),
        out_shape=jax.ShapeDtypeStruct((6, bsz, D_MODEL), F32),
        compiler_params=_cparams(("arbitrary",)),
        name="ada",
    )(c, w_ada, b_ada.reshape(6, 1, D_MODEL))


def _inproj_kernel(x_ref, mod_ref, g1_ref, wn_ref, wt_ref, gm_ref, qkg_ref, bqg_ref, a1_ref, a4_ref, a16_ref, bk_ref,
                   bqt_ref, bvt_ref, a_sc):
    a_refs = (a1_ref, a4_ref, a16_ref)
    x = x_ref[...]
    y = x * lax.rsqrt(jnp.mean(x * x, axis=-1, keepdims=True) + RMS_EPS) * g1_ref[...]
    h = (y * (1.0 + mod_ref[1:2, :]) + mod_ref[0:1, :]).astype(BF16)
    p = _dot(h, wn_ref[...])
    gm = gm_ref[...]

    def head_norm(t, g):
        ss = _dot((t * t).astype(BF16), gm)
        return t * lax.rsqrt(ss * (1.0 / HEAD_DIM) + RMS_EPS) * g

    w = A_WIDTH
    a_part = jnp.concatenate([head_norm(p[:, 0 * w:1 * w], qkg_ref[0:1, :]),
                              head_norm(p[:, 1 * w:2 * w], qkg_ref[1:2, :]), p[:, 2 * w:3 * w]], axis=1)
    n_col = a_sc.shape[0]
    for c in range(n_col):
        a_sc[c] = a_part[:, c * LANES:(c + 1) * LANES]
    for a_ref, (_, dil) in zip(a_refs, DIL_PATTERNS):
        if dil == 1:
            a_ref[0] = a_part.astype(BF16)
            continue
        rows = a_part.shape[0] // dil
        for r in range(dil):
            for c in range(n_col):
                a_ref[r, :, c * LANES:(c + 1) * LANES] = a_sc[c, pl.ds(r, rows, stride=dil), :].astype(BF16)
    bk_ref[...] = head_norm(p[:, 3 * w:4 * w], qkg_ref[2:3, :]).astype(BF16)

    pt = _dot_nt(wt_ref[...], h)
    t = pt.shape[1]
    bq = pt[:w].reshape(w // HEAD_DIM, HEAD_DIM, t)
    ss = jnp.sum(bq * bq, axis=1, keepdims=True)
    bq = (bq * lax.rsqrt(ss * (1.0 / HEAD_DIM) + RMS_EPS)).reshape(w, t) * bqg_ref[...]
    bqt_ref[...] = bq.astype(BF16)
    bvt_ref[...] = pt[w:].astype(BF16)


def _inproj(x2, mod_b, norm1_g, w_in, qk_gains, bq_gain, bsz, seq):
    n = x2.shape[0]
    t = ROW_TILE
    per_b = seq // t
    w = A_WIDTH
    head_of_lane = np.arange(w) // HEAD_DIM
    gmat = jnp.asarray(head_of_lane[:, None] == head_of_lane[None, :], BF16)
    w_bf = w_in.astype(BF16)
    w_nat = jnp.concatenate([w_bf[:, :3 * w], w_bf[:, 4 * w:5 * w]], axis=1)
    w_tr = jnp.concatenate([w_bf[:, 3 * w:4 * w], w_bf[:, 5 * w:]], axis=1).T
    tr_spec = pl.BlockSpec((None, None, w, t), lambda i: (i // per_b, i % per_b, 0, 0))
    tr_shape = jax.ShapeDtypeStruct((bsz, per_b, w, t), BF16)
    a_specs = [pl.BlockSpec((None, dil, t // dil, 3 * w), lambda i: (i // per_b, 0, i % per_b, 0))
               for _, dil in DIL_PATTERNS]
    a_shapes = [jax.ShapeDtypeStruct((bsz, dil, seq // dil, 3 * w), BF16) for _, dil in DIL_PATTERNS]
    return pl.pallas_call(
        _inproj_kernel,
        grid=(n // t,),
        in_specs=[
            pl.BlockSpec((t, D_MODEL), lambda i: (i, 0)),
            pl.BlockSpec((None, 6, D_MODEL), lambda i: (i // per_b, 0, 0)),
            pl.BlockSpec((1, D_MODEL), lambda i: (0, 0)),
            pl.BlockSpec((D_MODEL, 4 * w), lambda i: (0, 0)),
            pl.BlockSpec((2 * w, D_MODEL), lambda i: (0, 0)),
            pl.BlockSpec((w, w), lambda i: (0, 0)),
            pl.BlockSpec((3, w), lambda i: (0, 0)),
            pl.BlockSpec((w, 1), lambda i: (0, 0)),
        ],
        out_specs=a_specs + [pl.BlockSpec((t, w), lambda i: (i, 0)), tr_spec, tr_spec],
        out_shape=a_shapes + [jax.ShapeDtypeStruct((n, w), BF16), tr_shape, tr_shape],
        scratch_shapes=[pltpu.VMEM((3 * w // LANES, t, LANES), F32)],
        compiler_params=_cparams(("parallel",)),
        name="inproj",
    )(x2, mod_b, norm1_g.reshape(1, D_MODEL), w_nat, w_tr, gmat, qk_gains, bq_gain)


def _dilated_bias_table():
    ik = np.arange(2 * BAND)[:, None]
    iq = np.arange(BAND)[None, :]
    delta = iq - ik + BAND
    in_band = (delta >= 0) & (delta <= BAND)
    slopes = 2.0 ** (-8.0 * np.arange(1, A_HEADS + 1) / A_HEADS)
    tbl = np.zeros((len(DIL_PATTERNS), 2, A_HEADS // 2, 2 * BAND, 2 * BAND), np.float32)
    for p, (_, dil) in enumerate(DIL_PATTERNS):
        for first in range(2):
            valid = in_band & ((ik >= BAND) if first else True)
            for h in range(A_HEADS):
                cols = slice(BAND * (h % 2), BAND * (h % 2 + 1))
                tbl[p, first, h // 2, :, cols] = np.where(valid, -slopes[h] * (delta * dil), -np.inf)
    return jnp.asarray(tbl)


def _dilated_kernel(tbl_ref, *refs, n_blocks):
    lanes = 2 * HEAD_DIM
    n = pl.program_id(1)
    n_pat = len(DIL_PATTERNS)
    s_sc, p_sc = refs[len(refs) - 2:]
    outs = refs[len(refs) - 2 - 2 * n_pat:len(refs) - 2]
    lane = lax.broadcasted_iota(I32, (BAND, lanes), 1)
    low = lane < HEAD_DIM
    row = lax.broadcasted_iota(I32, (lanes, BAND), 0)
    top = row < HEAD_DIM
    work = []
    pos = 0
    for p, (_, dil) in enumerate(DIL_PATTERNS):
        per_res = n_blocks // dil
        if per_res > 1:
            q_ref, kp_ref, kc_ref, vp_ref, vc_ref = refs[pos:pos + 5]
            pos += 5
            first = (n % per_res == 0).astype(I32)
        else:
            q_ref, kc_ref, vc_ref = refs[pos:pos + 3]
            kp_ref = vp_ref = first = None
            pos += 3
        for j in range(A_HEADS // 2):
            work.append((p, j, (q_ref, kp_ref, kc_ref, vp_ref, vc_ref), first))

    def keys_of(prev_ref, cur_ref, cs):
        return cur_ref[:, cs] if prev_ref is None else jnp.concatenate([prev_ref[:, cs], cur_ref[:, cs]], axis=0)

    for w, (p, j, (q_ref, kp_ref, kc_ref, _, _), first) in enumerate(work):
        cs = slice(lanes * j, lanes * (j + 1))
        q2 = q_ref[:, cs]
        zero = jnp.zeros_like(q2)
        qcat = jnp.concatenate([jnp.where(low, q2, zero), jnp.where(low, zero, q2)], axis=0)
        k2 = keys_of(kp_ref, kc_ref, cs)
        bias = tbl_ref[p, 1, j, BAND:, :] if kp_ref is None else tbl_ref[p, first, j]
        s_sc[w, :k2.shape[0], :] = _dot_nt(k2, qcat) + bias
    lses = []
    for w, (p, j, (_, kp_ref, _, _, _), _) in enumerate(work):
        nk = BAND if kp_ref is None else 2 * BAND
        s = s_sc[w, :nk, :]
        m = jnp.max(s, axis=0, keepdims=True)
        e = jnp.exp(s - m)
        den = jnp.sum(e, axis=0, keepdims=True)
        p_sc[w, :nk, :] = (e * (1.0 / den)).astype(BF16)
        lses.append(m + jnp.log(den))
    for w, (p, j, (_, kp_ref, _, vp_ref, vc_ref), _) in enumerate(work):
        nk = BAND if kp_ref is None else 2 * BAND
        cs = slice(lanes * j, lanes * (j + 1))
        v2 = keys_of(vp_ref, vc_ref, cs)
        r = lax.dot_general(p_sc[w, :nk, :], v2, (((0,), (0,)), ((), ())), preferred_element_type=F32)
        o_ref, l_ref = outs[2 * p:2 * p + 2]
        o_ref[:, cs] = jnp.where(low, r[:BAND], r[BAND:])
        lse = lses[w]
        lse_t = jnp.where(top, jnp.broadcast_to(lse[:, :BAND], (lanes, BAND)),
                          jnp.broadcast_to(lse[:, BAND:], (lanes, BAND)))
        l_ref[:, cs] = lse_t.T


def _dilated(a_parts, bsz, seq):
    n_blocks = seq // BAND
    in_specs = [pl.BlockSpec((len(DIL_PATTERNS), 2, A_HEADS // 2, 2 * BAND, 2 * BAND), lambda b, n: (0, 0, 0, 0, 0))]
    n_work = len(DIL_PATTERNS) * A_HEADS // 2
    args = [_dilated_bias_table()]
    out_specs, out_shapes = [], []
    blk = (None, None, BAND, A_WIDTH)
    for a_part, (_, dil) in zip(a_parts, DIL_PATTERNS):
        per_res = n_blocks // dil

        def cur(col, per_res=per_res):
            return lambda b, n: (b, n // per_res, n % per_res, col)

        def prev(col, per_res=per_res):
            return lambda b, n: (b, n // per_res, jnp.maximum(n % per_res - 1, 0), col)

        if per_res > 1:
            in_specs += [pl.BlockSpec(blk, cur(0)), pl.BlockSpec(blk, prev(1)), pl.BlockSpec(blk, cur(1)),
                         pl.BlockSpec(blk, prev(2)), pl.BlockSpec(blk, cur(2))]
            args += [a_part] * 5
        else:
            in_specs += [pl.BlockSpec(blk, cur(0)), pl.BlockSpec(blk, cur(1)), pl.BlockSpec(blk, cur(2))]
            args += [a_part] * 3
        out_specs += [pl.BlockSpec(blk, cur(0))] * 2
        out_shapes += [jax.ShapeDtypeStruct((bsz, dil, seq // dil, A_WIDTH), F32)] * 2
    return pl.pallas_call(
        functools.partial(_dilated_kernel, n_blocks=n_blocks),
        grid=(bsz, n_blocks),
        in_specs=in_specs,
        out_specs=out_specs,
        out_shape=out_shapes,
        scratch_shapes=[pltpu.VMEM((n_work, 2 * BAND, 2 * BAND), F32), pltpu.VMEM((n_work, 2 * BAND, 2 * BAND), BF16)],
        compiler_params=_cparams(("parallel", "parallel")),
        name="dilated",
    )(*args)


def _diff_bias_table():
    t = DIFF_BLOCK
    rel = np.arange(t)[None, :] - np.arange(t)[:, None]
    slopes = 2.0 ** (-8.0 * np.arange(1, B_HEADS + 1) / B_HEADS)
    tbl = np.zeros((B_HEADS, 2, t, t), np.float32)
    for h in range(B_HEADS):
        tbl[h, 0] = -slopes[h] * rel
        tbl[h, 1] = np.where(rel >= 0, -slopes[h] * rel, -np.inf)
    return jnp.asarray(tbl)


def _diff_kernel(sc_ref, qt_ref, k_ref, vt_ref, g_ref, bias_sc, o_ref, qst_sc, m_sc, l_sc, a_sc, acc_sc, s_sc, p_sc):
    t = DIFF_BLOCK
    lanes = 2 * HEAD_DIM
    qi = pl.program_id(1)
    lam = sc_ref[0]
    slopes = [jnp.full((1, 1), sc_ref[1 + h], F32) for h in range(B_HEADS)]
    row = lax.broadcasted_iota(I32, (lanes, t), 0)
    for h in range(B_HEADS):
        qt = qt_ref[h * lanes:(h + 1) * lanes, :]
        zero = jnp.zeros_like(qt)
        qst_sc[h, :, :t] = jnp.where(row < HEAD_DIM, qt, zero)
        qst_sc[h, :, t:] = jnp.where(row < HEAD_DIM, zero, qt)
    m_sc[...] = jnp.full(m_sc.shape, -jnp.inf, F32)
    l_sc[...] = jnp.zeros(l_sc.shape, F32)
    acc_sc[...] = jnp.zeros(acc_sc.shape, F32)

    n_grp = 2 * t // DIFF_LANES
    items = [(h, g) for h in range(B_HEADS) for g in range(n_grp)]
    chunks = [slice(r * DIFF_ROWS, (r + 1) * DIFF_ROWS) for r in range(t // DIFF_ROWS)]

    def block(n, diag):
        cs = [-slopes[h] * (jnp.full((1, 1), (qi - n) * t, I32)).astype(F32) for h in range(B_HEADS)]
        for w, (h, g) in enumerate(items):
            cols = slice(g * DIFF_LANES, (g + 1) * DIFF_LANES)
            bcols = slice((g * DIFF_LANES) % t, (g * DIFF_LANES) % t + DIFF_LANES)
            s_sc[w] = _dot(k_ref[n, :, h * lanes:(h + 1) * lanes], qst_sc[h, :, cols]) + bias_sc[h, diag, :, bcols]
        for w, (h, g) in enumerate(items):
            cols = slice(g * DIFF_LANES, (g + 1) * DIFF_LANES)
            top = s_sc[w, chunks[0], :]
            for rows in chunks[1:]:
                top = jnp.maximum(top, s_sc[w, rows, :])
            m_prev = m_sc[h, :, cols]
            m_new = jnp.maximum(m_prev, jnp.max(top, axis=0, keepdims=True) + cs[h])
            alpha = jnp.exp(m_prev - m_new)
            shift = m_new - cs[h]
            part = jnp.zeros(top.shape, F32)
            for rows in chunks:
                e = jnp.exp(s_sc[w, rows, :] - shift)
                p_sc[w, rows, :] = e.astype(BF16)
                part = part + e
            l_sc[h, :, cols] = alpha * l_sc[h, :, cols] + jnp.sum(part, axis=0, keepdims=True)
            m_sc[h, :, cols] = m_new
            a_sc[h, :, cols] = alpha
        for w, (h, g) in enumerate(items):
            cols = slice(g * DIFF_LANES, (g + 1) * DIFF_LANES)
            pv = _dot(vt_ref[n, h * lanes:(h + 1) * lanes, :], p_sc[w])
            acc_sc[h, :, cols] = a_sc[h, :, cols] * acc_sc[h, :, cols] + pv

    def body(n, carry):
        block(n, 0)
        return carry

    lax.fori_loop(0, qi, body, 0)
    block(qi, 1)
    for h in range(B_HEADS):
        o = acc_sc[h] / l_sc[h]
        o = o[:, :t] - lam * o[:, t:]
        o = o * lax.rsqrt(jnp.mean(o * o, axis=0, keepdims=True) + RMS_EPS) * g_ref[...]
        o_ref[:, h * lanes:(h + 1) * lanes] = (o * (1.0 - LAMBDA_INIT)).T.astype(BF16)


def _diff(bqt, bk, bvt, scalars, diff_norm_g, bsz, seq):
    t = DIFF_BLOCK
    nb = seq // t
    lanes = 2 * HEAD_DIM
    n_items = B_HEADS * 2 * t // DIFF_LANES
    return pl.pallas_call(
        _diff_kernel,
        grid=(bsz, nb),
        in_specs=[
            pl.BlockSpec(memory_space=pltpu.SMEM),
            pl.BlockSpec((None, None, B_WIDTH, t), lambda b, i: (b, i, 0, 0)),
            pl.BlockSpec((None, nb, t, B_WIDTH), lambda b, i: (b, 0, 0, 0)),
            pl.BlockSpec((None, nb, B_WIDTH, t), lambda b, i: (b, 0, 0, 0)),
            pl.BlockSpec((lanes, 1), lambda b, i: (0, 0)),
            pl.BlockSpec((B_HEADS, 2, t, t), lambda b, i: (0, 0, 0, 0)),
        ],
        out_specs=pl.BlockSpec((None, t, B_WIDTH), lambda b, i: (b, i, 0)),
        out_shape=jax.ShapeDtypeStruct((bsz, seq, B_WIDTH), BF16),
        scratch_shapes=[pltpu.VMEM((B_HEADS, lanes, 2 * t), BF16)] + [pltpu.VMEM((B_HEADS, 1, 2 * t), F32)] * 3 + [
            pltpu.VMEM((B_HEADS, lanes, 2 * t), F32),
            pltpu.VMEM((n_items, t, DIFF_LANES), F32), pltpu.VMEM((n_items, t, DIFF_LANES), BF16)],
        compiler_params=_cparams(("parallel", "parallel")),
        name="diff",
    )(scalars, bqt, bk, bvt, diff_norm_g.reshape(lanes, 1), _diff_bias_table())


def _router_kernel(o0, l0, o1, l1, o2, l2, ob_ref, x_ref, mod_ref, wout_ref, g2_ref, wrh_ref, wrl_ref, br_ref,
                   tri_ref, x1_ref, he_ref, idx_ref, rank_ref, cnt_ref, *order_sc):
    def token_order(ref, scratch):
        dil, rows, width = ref.shape
        if dil == 1:
            return ref[0]
        n_col = width // LANES
        for r in range(dil):
            for c in range(n_col):
                scratch[c, pl.ds(r, rows, stride=dil), :] = ref[r, :, c * LANES:(c + 1) * LANES]
        return jnp.concatenate([scratch[c] for c in range(n_col)], axis=1)

    os_ = [token_order(o0, None), token_order(o1, order_sc[0]), token_order(o2, order_sc[2])]
    ls = [token_order(l0, None), token_order(l1, order_sc[1]), token_order(l2, order_sc[3])]
    mx = jnp.maximum(jnp.maximum(ls[0], ls[1]), ls[2])
    ws = [jnp.exp(l - mx) for l in ls]
    den = ws[0] + ws[1] + ws[2]
    oa = (ws[0] * os_[0] + ws[1] * os_[1] + ws[2] * os_[2]) / den
    mixed = _dot(oa.astype(BF16), wout_ref[:A_WIDTH, :]) + _dot(ob_ref[...], wout_ref[A_WIDTH:, :])
    x1 = x_ref[...] + mod_ref[2:3, :] * mixed
    x1_ref[...] = x1
    y = x1 * lax.rsqrt(jnp.mean(x1 * x1, axis=-1, keepdims=True) + RMS_EPS) * g2_ref[...]
    h2 = y * (1.0 + mod_ref[4:5, :]) + mod_ref[3:4, :]
    he_ref[:, :D_MODEL] = h2.astype(BF16)

    hh, hl = _split_bf16(h2)
    wrh = wrh_ref[...]
    logits = _dot_nt(wrh, hh) + _dot_nt(wrh, hl) + _dot_nt(wrl_ref[...], hh) + br_ref[...]
    t = logits.shape[1]
    eid = lax.broadcasted_iota(I32, (N_EXPERTS, t), 0)
    vals, idxs, hots = [], [], []
    cur = logits
    for _ in range(TOP_K):
        v = jnp.max(cur, axis=0, keepdims=True)
        ik = jnp.min(jnp.where(cur == v, eid, N_EXPERTS), axis=0, keepdims=True)
        hot = eid == ik
        vals.append(v)
        idxs.append(ik)
        hots.append(hot)
        cur = jnp.where(hot, -jnp.inf, cur)
    es = [jnp.exp(v - vals[0]) for v in vals]
    esum = es[0] + es[1] + es[2] + es[3]
    idx_ref[...] = jnp.concatenate(idxs, axis=0)

    rows = [ik.astype(F32) for ik in idxs]
    for e in es:
        g = e / esum
        hi = g.astype(BF16).astype(F32)
        mid = (g - hi).astype(BF16).astype(F32)
        rows += [hi, mid, g - hi - mid]
    rows.append(jnp.zeros((EXTRA_LANES - len(rows), t), F32))
    he_ref[:, D_MODEL:] = jnp.concatenate(rows, axis=0).T.astype(BF16)

    sel = jnp.where(hots[0] | hots[1] | hots[2] | hots[3], 1.0, 0.0)
    before = _dot(sel.astype(BF16), tri_ref[...])
    ranks = [jnp.sum(jnp.where(hot, before, 0.0), axis=0, keepdims=True) for hot in hots]
    rank_ref[...] = jnp.concatenate(ranks, axis=0).astype(I32)
    cnt_ref[...] = jnp.broadcast_to(jnp.sum(sel, axis=1, keepdims=True), cnt_ref.shape)


def _router(dil_outs, o_b, x2, mod_b, w_out, norm2_g, w_router, b_router, seq):
    n = x2.shape[0]
    t = ROW_TILE
    per_b = seq // t
    wr_t = w_router.T
    wrh = wr_t.astype(BF16)
    wrl = (wr_t - wrh.astype(F32)).astype(BF16)
    tri = jnp.asarray(np.arange(t)[:, None] < np.arange(t)[None, :], BF16)
    row = lambda w: pl.BlockSpec((t, w), lambda i: (i, 0))
    full = lambda a, b: pl.BlockSpec((a, b), lambda i: (0, 0))
    tok = lambda: pl.BlockSpec((TOP_K, t), lambda i: (0, i))
    grouped = lambda dil: pl.BlockSpec((None, dil, t // dil, A_WIDTH), lambda i: (i // per_b, 0, i % per_b, 0))
    n_regrouped = 2 * sum(dil > 1 for _, dil in DIL_PATTERNS)
    return pl.pallas_call(
        _router_kernel,
        grid=(n // t,),
        in_specs=[grouped(dil) for _, dil in DIL_PATTERNS for _ in range(2)] + [
            row(B_WIDTH), row(D_MODEL),
            pl.BlockSpec((None, 6, D_MODEL), lambda i: (i // per_b, 0, 0)),
            full(D_MODEL, D_MODEL), full(1, D_MODEL), full(N_EXPERTS, D_MODEL), full(N_EXPERTS, D_MODEL),
            full(N_EXPERTS, 1), full(t, t),
        ],
        out_specs=[row(D_MODEL), row(ROW_WIDTH), tok(), tok(),
                   pl.BlockSpec((None, N_EXPERTS, 128), lambda i: (i, 0, 0))],
        out_shape=[
            jax.ShapeDtypeStruct((n, D_MODEL), F32),
            jax.ShapeDtypeStruct((n, ROW_WIDTH), BF16),
            jax.ShapeDtypeStruct((TOP_K, n), I32),
            jax.ShapeDtypeStruct((TOP_K, n), I32),
            jax.ShapeDtypeStruct((n // t, N_EXPERTS, 128), F32),
        ],
        scratch_shapes=[pltpu.VMEM((A_WIDTH // LANES, t, LANES), F32)] * n_regrouped,
        compiler_params=_cparams(("parallel",)),
        name="router",
    )(*dil_outs, o_b, x2, mod_b, w_out.astype(BF16), norm2_g.reshape(1, D_MODEL), wrh, wrl,
      b_router.reshape(N_EXPERTS, 1), tri)


def _segment_copies(loff_ref, goff_ref, len_ref, tile, local_ref, hbm_ref, sem, outbound, action):
    def per_expert(e, carry):
        base = tile * N_EXPERTS + e
        lo, go, n = loff_ref[base], goff_ref[base], len_ref[base]
        done = jnp.int32(0)
        for size in SEG_SIZES:
            take = (n & size) != 0
            loc = local_ref.at[pl.ds(pl.multiple_of(lo + done, SEG_ALIGN), size)]
            glob = hbm_ref.at[pl.ds(pl.multiple_of(go + done, SEG_ALIGN), size)]
            cp = pltpu.make_async_copy(loc, glob, sem) if outbound else pltpu.make_async_copy(glob, loc, sem)
            pl.when(take)(functools.partial(action, cp))
            done = done + jnp.where(take, size, 0)
        return carry

    lax.fori_loop(0, N_EXPERTS, per_expert, 0)


def _start(cp):
    cp.start()


def _wait(cp):
    cp.wait()


def _tile_rows(loff_ref, len_ref, tile):
    last = tile * N_EXPERTS + N_EXPERTS - 1
    return loff_ref[last] + len_ref[last]


def _wait_rows(rows, local_ref, hbm_ref, sem, outbound):
    size = 1 << (LOCAL_SLOTS.bit_length() - 1)
    while size >= SEG_ALIGN:
        loc, glob = local_ref.at[pl.ds(0, size)], hbm_ref.at[pl.ds(0, size)]
        cp = pltpu.make_async_copy(loc, glob, sem) if outbound else pltpu.make_async_copy(glob, loc, sem)
        pl.when((rows & size) != 0)(cp.wait)
        size //= 2


def _one_hot_any(j, targets):
    out = jnp.zeros(j.shape, F32)
    for tgt in targets:
        out = jnp.where(j == tgt, 1.0, out)
    return out


def _sort_kernel(loff_ref, goff_ref, len_ref, tail_ref, he_ref, idx_ref, rank_ref, lcol_ref, ls_ref, xs_hbm,
                 xl, zbuf, sems, zsem):
    i = pl.program_id(0)
    last = pl.num_programs(0) - 1
    slot = i % 2
    t = he_ref.shape[0]

    @pl.when(i == 0)
    def _():
        zbuf[...] = jnp.zeros(zbuf.shape, BF16)

        def tails(action):
            def per_expert(e, carry):
                off, n = tail_ref[e], tail_ref[N_EXPERTS + e]
                done = jnp.int32(0)
                for size in SEG_SIZES:
                    if size < MOE_TILE:
                        take = (n & size) != 0
                        dst = xs_hbm.at[pl.ds(pl.multiple_of(off + done, SEG_ALIGN), size)]
                        pl.when(take)(functools.partial(action, pltpu.make_async_copy(zbuf.at[pl.ds(0, size)], dst, zsem)))
                        done = done + jnp.where(take, size, 0)
                return carry

            lax.fori_loop(0, N_EXPERTS, per_expert, 0)

            def per_block(b, carry):
                dst = xs_hbm.at[pl.ds(pl.multiple_of(b * MOE_TILE, MOE_TILE), MOE_TILE)]
                action(pltpu.make_async_copy(zbuf, dst, zsem))
                return carry

            lax.fori_loop(tail_ref[2 * N_EXPERTS], xs_hbm.shape[0] // MOE_TILE, per_block, 0)

        tails(_start)
        tails(_wait)

    eid = lax.broadcasted_iota(I32, (N_EXPERTS, t), 0)
    lcol = lcol_ref[...]
    ls = []
    for k in range(TOP_K):
        off = jnp.sum(jnp.where(eid == idx_ref[k:k + 1, :], lcol, 0), axis=0, keepdims=True)
        ls.append(off + rank_ref[k:k + 1, :])
    ls_ref[...] = jnp.concatenate(ls, axis=0)

    he = he_ref[...]
    for jc in range(LOCAL_SLOTS // SLOT_CHUNK):
        j = lax.broadcasted_iota(I32, (SLOT_CHUNK, t), 0) + jc * SLOT_CHUNK
        perm = _one_hot_any(j, ls).astype(BF16)
        xl[slot, jc * SLOT_CHUNK:(jc + 1) * SLOT_CHUNK, :] = _dot(perm, he).astype(BF16)

    tables = (loff_ref, goff_ref, len_ref)
    _segment_copies(*tables, i, xl.at[slot], xs_hbm, sems.at[slot], True, _start)

    @pl.when(i > 0)
    def _():
        _wait_rows(_tile_rows(loff_ref, len_ref, i - 1), xl.at[1 - slot], xs_hbm, sems.at[1 - slot], True)

    @pl.when(i == last)
    def _():
        _wait_rows(_tile_rows(loff_ref, len_ref, i), xl.at[slot], xs_hbm, sems.at[slot], True)


def _sort(tables, tail, he, idx, rank, lcol, n_rows):
    n = he.shape[0]
    t = ROW_TILE
    tok = lambda: pl.BlockSpec((TOP_K, t), lambda i, *_: (0, i))
    return pl.pallas_call(
        _sort_kernel,
        grid_spec=pltpu.PrefetchScalarGridSpec(
            num_scalar_prefetch=4,
            grid=(n // t,),
            in_specs=[
                pl.BlockSpec((t, ROW_WIDTH), lambda i, *_: (i, 0)),
                tok(), tok(),
                pl.BlockSpec((None, N_EXPERTS, 1), lambda i, *_: (i, 0, 0)),
            ],
            out_specs=[tok(), pl.BlockSpec(memory_space=pl.ANY)],
            scratch_shapes=[pltpu.VMEM((2, LOCAL_SLOTS, ROW_WIDTH), BF16), pltpu.VMEM((MOE_TILE, ROW_WIDTH), BF16),
                            pltpu.SemaphoreType.DMA((2,)), pltpu.SemaphoreType.DMA(())],
        ),
        out_shape=[jax.ShapeDtypeStruct((TOP_K, n), I32), jax.ShapeDtypeStruct((n_rows, ROW_WIDTH), BF16)],
        compiler_params=_cparams(("arbitrary",)),
        name="sort",
    )(*tables, tail, he, idx, rank, lcol)


def _experts_kernel(blk0_ref, nblk_ref, xs_hbm, wgu_ref, bgu_ref, wd_ref, bd_ref, yb_hbm, wgu_sc, wd_sc, xbuf, ybuf,
                    xsem, ysem):
    e = pl.program_id(0)
    tm = MOE_TILE
    first, nb = blk0_ref[e], nblk_ref[e]

    def rows(i):
        return pl.ds(pl.multiple_of((first + i) * tm, tm), tm)

    def x_copy(i, slot):
        return pltpu.make_async_copy(xs_hbm.at[rows(i)], xbuf.at[slot], xsem.at[slot])

    def y_copy(i, slot):
        return pltpu.make_async_copy(ybuf.at[slot], yb_hbm.at[rows(i)], ysem.at[slot])

    @pl.when(nb > 0)
    def _():
        x_copy(0, 0).start()

    def cast(r, carry):
        s = pl.multiple_of(r * LANES, LANES)
        wgu_sc[pl.ds(s, LANES), :] = wgu_ref[pl.ds(s, LANES), :].astype(BF16)
        wd_sc[pl.ds(s, LANES), :] = wd_ref[pl.ds(s, LANES), :].astype(BF16)
        return carry

    lax.fori_loop(0, D_MODEL // LANES, cast, 0)
    me = jnp.full((1, 1), e, I32).astype(F32)

    def block(i, carry):
        slot = i % 2

        @pl.when(i + 1 < nb)
        def _():
            x_copy(i + 1, 1 - slot).start()

        x_copy(i, slot).wait()

        @pl.when(i >= 2)
        def _():
            y_copy(i - 2, slot).wait()

        ext = xbuf[slot, :, D_MODEL:].astype(F32)
        gate = jnp.zeros((tm, 1), F32)
        for k in range(TOP_K):
            c = TOP_K + 3 * k
            gk = ext[:, c:c + 1] + ext[:, c + 1:c + 2] + ext[:, c + 2:c + 3]
            gate = gate + jnp.where(ext[:, k:k + 1] == me, gk, 0.0)

        gu = _dot(xbuf[slot, :, :D_MODEL], wgu_sc[...]) + bgu_ref[...]
        g = jnp.minimum(gu[:, :D_FF], SWIGLU_LIMIT)
        u = jnp.clip(gu[:, D_FF:], -SWIGLU_LIMIT, SWIGLU_LIMIT)
        act = (u + 1.0) * (g / (1.0 + jnp.exp(-SWIGLU_ALPHA * g)))
        ybuf[slot] = (gate * (_dot(act.astype(BF16), wd_sc[...]) + bd_ref[...])).astype(BF16)
        y_copy(i, slot).start()
        return carry

    lax.fori_loop(0, nb, block, 0)

    @pl.when(nb >= 2)
    def _():
        y_copy(nb - 2, nb % 2).wait()

    @pl.when(nb >= 1)
    def _():
        y_copy(nb - 1, (nb - 1) % 2).wait()

    @pl.when(e == pl.num_programs(0) - 1)
    def _():
        ybuf[0] = jnp.zeros((tm, D_MODEL), BF16)
        used = first + nb

        def fill(action):
            def per_block(b, carry):
                dst = yb_hbm.at[pl.ds(pl.multiple_of(b * tm, tm), tm)]
                action(pltpu.make_async_copy(ybuf.at[0], dst, ysem.at[0]))
                return carry

            lax.fori_loop(used, yb_hbm.shape[0] // tm, per_block, 0)

        fill(_start)
        fill(_wait)


def _experts(first_block, n_block, xs, w_gate_up, b_gate_up, w_down, b_down):
    n_rows = xs.shape[0]
    exp3 = lambda e, *_: (e, 0, 0)
    return pl.pallas_call(
        _experts_kernel,
        grid_spec=pltpu.PrefetchScalarGridSpec(
            num_scalar_prefetch=2,
            grid=(N_EXPERTS,),
            in_specs=[
                pl.BlockSpec(memory_space=pl.ANY),
                pl.BlockSpec((None, D_MODEL, 2 * D_FF), exp3),
                pl.BlockSpec((None, 1, 2 * D_FF), exp3),
                pl.BlockSpec((None, D_FF, D_MODEL), exp3),
                pl.BlockSpec((None, 1, D_MODEL), exp3),
            ],
            out_specs=pl.BlockSpec(memory_space=pl.ANY),
            scratch_shapes=[pltpu.VMEM((D_MODEL, 2 * D_FF), BF16), pltpu.VMEM((D_FF, D_MODEL), BF16),
                            pltpu.VMEM((2, MOE_TILE, ROW_WIDTH), BF16), pltpu.VMEM((2, MOE_TILE, D_MODEL), BF16),
                            pltpu.SemaphoreType.DMA((2,)), pltpu.SemaphoreType.DMA((2,))],
        ),
        out_shape=jax.ShapeDtypeStruct((n_rows, D_MODEL), BF16),
        compiler_params=_cparams(("arbitrary",)),
        name="experts",
    )(first_block, n_block, xs, w_gate_up, b_gate_up.reshape(N_EXPERTS, 1, 2 * D_FF), w_down,
      b_down.reshape(N_EXPERTS, 1, D_MODEL))


def _combine_kernel(loff_ref, goff_ref, len_ref, yb_hbm, lst_ref, x1_ref, mod_ref, o_ref, ybuf, sems):
    i = pl.program_id(0)
    slot = i % 2
    tables = (loff_ref, goff_ref, len_ref)

    @pl.when(i == 0)
    def _():
        ybuf[...] = jnp.zeros(ybuf.shape, BF16)
        _segment_copies(*tables, 0, ybuf.at[0], yb_hbm, sems.at[0], False, _start)

    @pl.when(i + 1 < pl.num_programs(0))
    def _():
        _segment_copies(*tables, i + 1, ybuf.at[1 - slot], yb_hbm, sems.at[1 - slot], False, _start)

    _wait_rows(_tile_rows(loff_ref, len_ref, i), ybuf.at[slot], yb_hbm, sems.at[slot], False)

    lst = lst_ref[...]
    t = lst.shape[0]
    targets = [lst[:, k:k + 1] for k in range(TOP_K)]
    y = jnp.zeros((t, D_MODEL), F32)
    for jc in range(LOCAL_SLOTS // SLOT_CHUNK):
        j = lax.broadcasted_iota(I32, (t, SLOT_CHUNK), 1) + jc * SLOT_CHUNK
        pick = _one_hot_any(j, targets).astype(BF16)
        y = y + _dot(pick, ybuf[slot, jc * SLOT_CHUNK:(jc + 1) * SLOT_CHUNK, :])
    o_ref[...] = x1_ref[...] + mod_ref[5:6, :] * y


def _combine(tables, yb, ls_t, x1, mod_b, seq):
    n = x1.shape[0]
    t = ROW_TILE
    per_b = seq // t
    return pl.pallas_call(
        _combine_kernel,
        grid_spec=pltpu.PrefetchScalarGridSpec(
            num_scalar_prefetch=3,
            grid=(n // t,),
            in_specs=[
                pl.BlockSpec(memory_space=pl.ANY),
                pl.BlockSpec((t, TOP_K), lambda i, *_: (i, 0)),
                pl.BlockSpec((t, D_MODEL), lambda i, *_: (i, 0)),
                pl.BlockSpec((None, 6, D_MODEL), lambda i, *_: (i // per_b, 0, 0)),
            ],
            out_specs=pl.BlockSpec((t, D_MODEL), lambda i, *_: (i, 0)),
            scratch_shapes=[pltpu.VMEM((2, LOCAL_SLOTS, D_MODEL), BF16), pltpu.SemaphoreType.DMA((2,))],
        ),
        out_shape=jax.ShapeDtypeStruct((n, D_MODEL), F32),
        compiler_params=_cparams(("arbitrary",)),
        name="combine",
    )(*tables, yb, ls_t, x1, mod_b)


def _layer(x, c, w_ada, b_ada, norm1_g, w_in, a_q_norm_g, a_k_norm_g, b_q_norm_g, b_k_norm_g, lambda_q1,
           lambda_k1, lambda_q2, lambda_k2, diff_norm_g, w_out, norm2_g, w_router, b_router, w_gate_up,
           b_gate_up, w_down, b_down):
    bsz, seq, _ = x.shape
    n = bsz * seq
    x2 = x.reshape(n, D_MODEL)
    mod_b = _ada(c, w_ada, b_ada).transpose(1, 0, 2)

    qk_gains = jnp.stack([
        jnp.tile(a_q_norm_g, A_HEADS) * ATTN_SCALE, jnp.tile(a_k_norm_g, A_HEADS), jnp.tile(b_k_norm_g, 2 * B_HEADS)])
    bq_gain = (jnp.tile(b_q_norm_g, 2 * B_HEADS) * ATTN_SCALE).reshape(B_WIDTH, 1)
    *a_parts, bk, bqt, bvt = _inproj(x2, mod_b, norm1_g, w_in, qk_gains, bq_gain, bsz, seq)

    dil_outs = _dilated(a_parts, bsz, seq)
    lam = (jnp.exp(jnp.sum(lambda_q1 * lambda_k1)) - jnp.exp(jnp.sum(lambda_q2 * lambda_k2)) + LAMBDA_INIT)
    slopes_b = 2.0 ** (-8.0 * np.arange(1, B_HEADS + 1) / B_HEADS)
    scalars = jnp.concatenate([lam.reshape(1), jnp.asarray(slopes_b, F32)]).astype(F32)
    bk4 = bk.reshape(bsz, seq // DIFF_BLOCK, DIFF_BLOCK, B_WIDTH)
    o_b = _diff(bqt, bk4, bvt, scalars, diff_norm_g, bsz, seq).reshape(n, B_WIDTH)

    x1, he, idx, rank, cnt = _router(dil_outs, o_b, x2, mod_b, w_out, norm2_g, w_router, b_router, seq)

    n_tiles = n // ROW_TILE
    counts = cnt[:, :, 0].astype(I32)
    seg = (counts + SEG_ALIGN - 1) // SEG_ALIGN * SEG_ALIGN
    loff = jnp.cumsum(seg, axis=1) - seg
    region = jnp.sum(seg, axis=0)
    padded = (region + MOE_TILE - 1) // MOE_TILE * MOE_TILE
    pad_end = jnp.cumsum(padded)
    pad_start = pad_end - padded
    goff = pad_start[None, :] + jnp.cumsum(seg, axis=0) - seg
    tables = (loff.reshape(-1), goff.reshape(-1), seg.reshape(-1))
    n_blocks = (n * TOP_K + n_tiles * N_EXPERTS * (SEG_ALIGN - 1) + N_EXPERTS * (MOE_TILE - 1)) // MOE_TILE
    n_used = (pad_end[-1] // MOE_TILE).astype(I32)
    tail = jnp.concatenate([pad_start + region, padded - region, n_used.reshape(1)])

    ls, xs = _sort(tables, tail, he, idx, rank, loff.reshape(n_tiles, N_EXPERTS, 1), n_blocks * MOE_TILE)
    yb = _experts(pad_start // MOE_TILE, padded // MOE_TILE, xs, w_gate_up, b_gate_up, w_down, b_down)
    out = _combine(tables, yb, ls.T, x1, mod_b, seq)
    return out.reshape(bsz, seq, D_MODEL)


def kernel(x, c, w_ada, b_ada, norm1_g, w_in, a_q_norm_g, a_k_norm_g, b_q_norm_g, b_k_norm_g, lambda_q1, lambda_k1,
           lambda_q2, lambda_k2, diff_norm_g, w_out, norm2_g, w_router, b_router, w_gate_up, b_gate_up, w_down,
           b_down):
    args = (w_ada, b_ada, norm1_g, w_in, a_q_norm_g, a_k_norm_g, b_q_norm_g, b_k_norm_g, lambda_q1, lambda_k1,
            lambda_q2, lambda_k2, diff_norm_g, w_out, norm2_g, w_router, b_router, w_gate_up, b_gate_up, w_down,
            b_down)
    return _layer(x, c, *[a[0] for a in args])
```

```python
import functools

import numpy as np
import jax
import jax.numpy as jnp
from jax import lax
from jax.experimental import pallas as pl
from jax.experimental.pallas import tpu as pltpu

F32 = jnp.float32
BF16 = jnp.bfloat16
I32 = jnp.int32
U32 = jnp.uint32

D_MODEL = 1024
HEAD_DIM = 64
A_WIDTH = 512
B_WIDTH = 512
A_HEADS = 8
B_HEADS = 4
IN_WIDTH = 3072
DIL_PATTERNS = ((128, 1), (512, 4), (2048, 16))
BAND = 128
N_EXPERTS = 32
TOP_K = 4
D_FF = 1024
SWIGLU_LIMIT = 7.0
SWIGLU_ALPHA = 1.702
RMS_EPS = 1e-6
ATTN_SCALE = HEAD_DIM ** -0.5
LAMBDA_INIT = 0.8 - 0.6 * 1.0

LANES = 128
ROW_TILE = 512
DIFF_BLOCK = 512
DIFF_LANES = 256
DIFF_ROWS = 64
MOE_TILE = 256
X_AHEAD = 4
X_SLOTS = X_AHEAD + 1
EXTRA_LANES = 128
ROW_WIDTH = D_MODEL + EXTRA_LANES
SEG_ALIGN = 16
SEG_SIZES = (512, 256, 128, 64, 32, 16)
SLOT_CHUNK = 512
LOCAL_SLOTS = 2560
assert ROW_TILE == DIFF_BLOCK == SEG_SIZES[0]
assert LOCAL_SLOTS >= ROW_TILE * TOP_K + N_EXPERTS * (SEG_ALIGN - 1) and LOCAL_SLOTS % SLOT_CHUNK == 0
VMEM_LIMIT = 56 * 1024 * 1024


def _cparams(sem, **flags):
    return pltpu.CompilerParams(dimension_semantics=sem, vmem_limit_bytes=VMEM_LIMIT, flags=flags or None)


def _split_bf16(a):
    hi = a.astype(BF16)
    lo = (a - hi.astype(F32)).astype(BF16)
    return hi, lo


def _dot_nt(a, b):
    return lax.dot_general(a, b, (((1,), (1,)), ((), ())), preferred_element_type=F32)


def _dot(a, b):
    return jnp.dot(a, b, preferred_element_type=F32)


def _ada_kernel(c_ref, w_ref, b_ref, o_ref):
    c = c_ref[...]
    s = c / (1.0 + jnp.exp(-c))
    sh, sl = _split_bf16(s)
    wh, wl = _split_bf16(w_ref[...])
    o_ref[0] = _dot(sh, wh) + _dot(sh, wl) + _dot(sl, wh) + b_ref[0]


def _ada(c, w_ada, b_ada):
    bsz = c.shape[0]
    return pl.pallas_call(
        _ada_kernel,
        grid=(6,),
        in_specs=[
            pl.BlockSpec((bsz, D_MODEL), lambda j: (0, 0)),
            pl.BlockSpec((D_MODEL, D_MODEL), lambda j: (0, j)),
            pl.BlockSpec((1, 1, D_MODEL), lambda j: (j, 0, 0)),
        ],
        out_specs=pl.BlockSpec((1, bsz, D_MODEL), lambda j: (j, 0, 0)),
        out_shape=jax.ShapeDtypeStruct((6, bsz, D_MODEL), F32),
        compiler_params=_cparams(("arbitrary",)),
        name="ada",
    )(c, w_ada, b_ada.reshape(6, 1, D_MODEL))


def _inproj_kernel(x_ref, mod_ref, g1_ref, wn_ref, wt_ref, gm_ref, qkg_ref, bqg_ref, a1_ref, a4_ref, a16_ref, bk_ref,
                   bqt_ref, bvt_ref, a_sc):
    a_refs = (a1_ref, a4_ref, a16_ref)
    x = x_ref[...]
    y = x * lax.rsqrt(jnp.mean(x * x, axis=-1, keepdims=True) + RMS_EPS) * g1_ref[...]
    h = (y * (1.0 + mod_ref[1:2, :]) + mod_ref[0:1, :]).astype(BF16)
    p = _dot(h, wn_ref[...])
    gm = gm_ref[...]

    def head_norm(t, g):
        ss = _dot((t * t).astype(BF16), gm)
        return t * lax.rsqrt(ss * (1.0 / HEAD_DIM) + RMS_EPS) * g

    w = A_WIDTH
    a_part = jnp.concatenate([head_norm(p[:, 0 * w:1 * w], qkg_ref[0:1, :]),
                              head_norm(p[:, 1 * w:2 * w], qkg_ref[1:2, :]), p[:, 2 * w:3 * w]], axis=1)
    n_col = a_sc.shape[0]
    for c in range(n_col):
        a_sc[c] = a_part[:, c * LANES:(c + 1) * LANES]
    for a_ref, (_, dil) in zip(a_refs, DIL_PATTERNS):
        if dil == 1:
            a_ref[0] = a_part.astype(BF16)
            continue
        rows = a_part.shape[0] // dil
        for r in range(dil):
            for c in range(n_col):
                a_ref[r, :, c * LANES:(c + 1) * LANES] = a_sc[c, pl.ds(r, rows, stride=dil), :].astype(BF16)
    bk_ref[...] = head_norm(p[:, 3 * w:4 * w], qkg_ref[2:3, :]).astype(BF16)

    pt = _dot_nt(wt_ref[...], h)
    t = pt.shape[1]
    bq = pt[:w].reshape(w // HEAD_DIM, HEAD_DIM, t)
    ss = jnp.sum(bq * bq, axis=1, keepdims=True)
    bq = (bq * lax.rsqrt(ss * (1.0 / HEAD_DIM) + RMS_EPS)).reshape(w, t) * bqg_ref[...]
    bqt_ref[...] = bq.astype(BF16)
    bvt_ref[...] = pt[w:].astype(BF16)


def _inproj(x2, mod_b, norm1_g, w_in, qk_gains, bq_gain, bsz, seq):
    n = x2.shape[0]
    t = ROW_TILE
    per_b = seq // t
    w = A_WIDTH
    head_of_lane = np.arange(w) // HEAD_DIM
    gmat = jnp.asarray(head_of_lane[:, None] == head_of_lane[None, :], BF16)
    w_bf = w_in.astype(BF16)
    w_nat = jnp.concatenate([w_bf[:, :3 * w], w_bf[:, 4 * w:5 * w]], axis=1)
    w_tr = jnp.concatenate([w_bf[:, 3 * w:4 * w], w_bf[:, 5 * w:]], axis=1).T
    tr_spec = pl.BlockSpec((None, None, w, t), lambda i: (i // per_b, i % per_b, 0, 0))
    tr_shape = jax.ShapeDtypeStruct((bsz, per_b, w, t), BF16)
    a_specs = [pl.BlockSpec((None, dil, t // dil, 3 * w), lambda i: (i // per_b, 0, i % per_b, 0))
               for _, dil in DIL_PATTERNS]
    a_shapes = [jax.ShapeDtypeStruct((bsz, dil, seq // dil, 3 * w), BF16) for _, dil in DIL_PATTERNS]
    return pl.pallas_call(
        _inproj_kernel,
        grid=(n // t,),
        in_specs=[
            pl.BlockSpec((t, D_MODEL), lambda i: (i, 0)),
            pl.BlockSpec((None, 6, D_MODEL), lambda i: (i // per_b, 0, 0)),
            pl.BlockSpec((1, D_MODEL), lambda i: (0, 0)),
            pl.BlockSpec((D_MODEL, 4 * w), lambda i: (0, 0)),
            pl.BlockSpec((2 * w, D_MODEL), lambda i: (0, 0)),
            pl.BlockSpec((w, w), lambda i: (0, 0)),
            pl.BlockSpec((3, w), lambda i: (0, 0)),
            pl.BlockSpec((w, 1), lambda i: (0, 0)),
        ],
        out_specs=a_specs + [pl.BlockSpec((t, w), lambda i: (i, 0)), tr_spec, tr_spec],
        out_shape=a_shapes + [jax.ShapeDtypeStruct((n, w), BF16), tr_shape, tr_shape],
        scratch_shapes=[pltpu.VMEM((3 * w // LANES, t, LANES), F32)],
        compiler_params=_cparams(("parallel",)),
        name="inproj",
    )(x2, mod_b, norm1_g.reshape(1, D_MODEL), w_nat, w_tr, gmat, qk_gains, bq_gain)


def _dilated_bias_table():
    ik = np.arange(2 * BAND)[:, None]
    iq = np.arange(BAND)[None, :]
    delta = iq - ik + BAND
    in_band = (delta >= 0) & (delta <= BAND)
    slopes = 2.0 ** (-8.0 * np.arange(1, A_HEADS + 1) / A_HEADS)
    tbl = np.zeros((len(DIL_PATTERNS), 2, A_HEADS // 2, 2 * BAND, 2 * BAND), np.float32)
    for p, (_, dil) in enumerate(DIL_PATTERNS):
        for first in range(2):
            valid = in_band & ((ik >= BAND) if first else True)
            for h in range(A_HEADS):
                cols = slice(BAND * (h % 2), BAND * (h % 2 + 1))
                tbl[p, first, h // 2, :, cols] = np.where(valid, -slopes[h] * (delta * dil), -np.inf)
    return jnp.asarray(tbl)


def _dilated_kernel(tbl_ref, *refs, n_blocks):
    lanes = 2 * HEAD_DIM
    n = pl.program_id(1)
    n_pat = len(DIL_PATTERNS)
    s_sc, p_sc = refs[len(refs) - 2:]
    outs = refs[len(refs) - 2 - 2 * n_pat:len(refs) - 2]
    lane = lax.broadcasted_iota(I32, (BAND, lanes), 1)
    low = lane < HEAD_DIM
    row = lax.broadcasted_iota(I32, (lanes, BAND), 0)
    top = row < HEAD_DIM
    work = []
    pos = 0
    for p, (_, dil) in enumerate(DIL_PATTERNS):
        per_res = n_blocks // dil
        if per_res > 1:
            q_ref, kp_ref, kc_ref, vp_ref, vc_ref = refs[pos:pos + 5]
            pos += 5
            first = (n % per_res == 0).astype(I32)
        else:
            q_ref, kc_ref, vc_ref = refs[pos:pos + 3]
            kp_ref = vp_ref = first = None
            pos += 3
        for j in range(A_HEADS // 2):
            work.append((p, j, (q_ref, kp_ref, kc_ref, vp_ref, vc_ref), first))

    def keys_of(prev_ref, cur_ref, cs):
        return cur_ref[:, cs] if prev_ref is None else jnp.concatenate([prev_ref[:, cs], cur_ref[:, cs]], axis=0)

    for w, (p, j, (q_ref, kp_ref, kc_ref, _, _), first) in enumerate(work):
        cs = slice(lanes * j, lanes * (j + 1))
        q2 = q_ref[:, cs]
        zero = jnp.zeros_like(q2)
        qcat = jnp.concatenate([jnp.where(low, q2, zero), jnp.where(low, zero, q2)], axis=0)
        k2 = keys_of(kp_ref, kc_ref, cs)
        bias = tbl_ref[p, 1, j, BAND:, :] if kp_ref is None else tbl_ref[p, first, j]
        s_sc[w, :k2.shape[0], :] = _dot_nt(k2, qcat) + bias
    lses = []
    for w, (p, j, (_, kp_ref, _, _, _), _) in enumerate(work):
        nk = BAND if kp_ref is None else 2 * BAND
        s = s_sc[w, :nk, :]
        m = jnp.max(s, axis=0, keepdims=True)
        e = jnp.exp(s - m)
        den = jnp.sum(e, axis=0, keepdims=True)
        p_sc[w, :nk, :] = (e * (1.0 / den)).astype(BF16)
        lses.append(m + jnp.log(den))
    for w, (p, j, (_, kp_ref, _, vp_ref, vc_ref), _) in enumerate(work):
        nk = BAND if kp_ref is None else 2 * BAND
        cs = slice(lanes * j, lanes * (j + 1))
        v2 = keys_of(vp_ref, vc_ref, cs)
        r = lax.dot_general(p_sc[w, :nk, :], v2, (((0,), (0,)), ((), ())), preferred_element_type=F32)
        o_ref, l_ref = outs[2 * p:2 * p + 2]
        o_ref[:, cs] = jnp.where(low, r[:BAND], r[BAND:])
        lse = lses[w]
        lse_t = jnp.where(top, jnp.broadcast_to(lse[:, :BAND], (lanes, BAND)),
                          jnp.broadcast_to(lse[:, BAND:], (lanes, BAND)))
        l_ref[:, cs] = lse_t.T


def _dilated(a_parts, bsz, seq):
    n_blocks = seq // BAND
    in_specs = [pl.BlockSpec((len(DIL_PATTERNS), 2, A_HEADS // 2, 2 * BAND, 2 * BAND), lambda b, n: (0, 0, 0, 0, 0))]
    n_work = len(DIL_PATTERNS) * A_HEADS // 2
    args = [_dilated_bias_table()]
    out_specs, out_shapes = [], []
    blk = (None, None, BAND, A_WIDTH)
    for a_part, (_, dil) in zip(a_parts, DIL_PATTERNS):
        per_res = n_blocks // dil

        def cur(col, per_res=per_res):
            return lambda b, n: (b, n // per_res, n % per_res, col)

        def prev(col, per_res=per_res):
            return lambda b, n: (b, n // per_res, jnp.maximum(n % per_res - 1, 0), col)

        if per_res > 1:
            in_specs += [pl.BlockSpec(blk, cur(0)), pl.BlockSpec(blk, prev(1)), pl.BlockSpec(blk, cur(1)),
                         pl.BlockSpec(blk, prev(2)), pl.BlockSpec(blk, cur(2))]
            args += [a_part] * 5
        else:
            in_specs += [pl.BlockSpec(blk, cur(0)), pl.BlockSpec(blk, cur(1)), pl.BlockSpec(blk, cur(2))]
            args += [a_part] * 3
        out_specs += [pl.BlockSpec(blk, cur(0))] * 2
        out_shapes += [jax.ShapeDtypeStruct((bsz, dil, seq // dil, A_WIDTH), F32)] * 2
    return pl.pallas_call(
        functools.partial(_dilated_kernel, n_blocks=n_blocks),
        grid=(bsz, n_blocks),
        in_specs=in_specs,
        out_specs=out_specs,
        out_shape=out_shapes,
        scratch_shapes=[pltpu.VMEM((n_work, 2 * BAND, 2 * BAND), F32), pltpu.VMEM((n_work, 2 * BAND, 2 * BAND), BF16)],
        compiler_params=_cparams(("parallel", "parallel")),
        name="dilated",
    )(*args)


def _diff_bias_table():
    t = DIFF_BLOCK
    rel = np.arange(t)[None, :] - np.arange(t)[:, None]
    slopes = 2.0 ** (-8.0 * np.arange(1, B_HEADS + 1) / B_HEADS)
    tbl = np.zeros((B_HEADS, 2, t, t), np.float32)
    for h in range(B_HEADS):
        tbl[h, 0] = -slopes[h] * rel
        tbl[h, 1] = np.where(rel >= 0, -slopes[h] * rel, -np.inf)
    return jnp.asarray(tbl)


def _diff_kernel(sc_ref, qt_ref, k_ref, vt_ref, g_ref, bias_sc, o_ref, qst_sc, m_sc, l_sc, a_sc, acc_sc, s_sc, p_sc):
    t = DIFF_BLOCK
    lanes = 2 * HEAD_DIM
    qi = pl.program_id(1)
    lam = sc_ref[0]
    slopes = [jnp.full((1, 1), sc_ref[1 + h], F32) for h in range(B_HEADS)]
    row = lax.broadcasted_iota(I32, (lanes, t), 0)
    for h in range(B_HEADS):
        qt = qt_ref[h * lanes:(h + 1) * lanes, :]
        zero = jnp.zeros_like(qt)
        qst_sc[h, :, :t] = jnp.where(row < HEAD_DIM, qt, zero)
        qst_sc[h, :, t:] = jnp.where(row < HEAD_DIM, zero, qt)
    m_sc[...] = jnp.full(m_sc.shape, -jnp.inf, F32)
    l_sc[...] = jnp.zeros(l_sc.shape, F32)
    acc_sc[...] = jnp.zeros(acc_sc.shape, F32)

    n_grp = 2 * t // DIFF_LANES
    items = [(h, g) for h in range(B_HEADS) for g in range(n_grp)]
    chunks = [slice(r * DIFF_ROWS, (r + 1) * DIFF_ROWS) for r in range(t // DIFF_ROWS)]

    def block(n, diag):
        cs = [-slopes[h] * (jnp.full((1, 1), (qi - n) * t, I32)).astype(F32) for h in range(B_HEADS)]
        for w, (h, g) in enumerate(items):
            cols = slice(g * DIFF_LANES, (g + 1) * DIFF_LANES)
            bcols = slice((g * DIFF_LANES) % t, (g * DIFF_LANES) % t + DIFF_LANES)
            s_sc[w] = _dot(k_ref[n, :, h * lanes:(h + 1) * lanes], qst_sc[h, :, cols]) + bias_sc[h, diag, :, bcols]
        for w, (h, g) in enumerate(items):
            cols = slice(g * DIFF_LANES, (g + 1) * DIFF_LANES)
            top = s_sc[w, chunks[0], :]
            for rows in chunks[1:]:
                top = jnp.maximum(top, s_sc[w, rows, :])
            m_prev = m_sc[h, :, cols]
            m_new = jnp.maximum(m_prev, jnp.max(top, axis=0, keepdims=True) + cs[h])
            alpha = jnp.exp(m_prev - m_new)
            shift = m_new - cs[h]
            part = jnp.zeros(top.shape, F32)
            for rows in chunks:
                e = jnp.exp(s_sc[w, rows, :] - shift)
                p_sc[w, rows, :] = e.astype(BF16)
                part = part + e
            l_sc[h, :, cols] = alpha * l_sc[h, :, cols] + jnp.sum(part, axis=0, keepdims=True)
            m_sc[h, :, cols] = m_new
            a_sc[h, :, cols] = alpha
        for w, (h, g) in enumerate(items):
            cols = slice(g * DIFF_LANES, (g + 1) * DIFF_LANES)
            pv = _dot(vt_ref[n, h * lanes:(h + 1) * lanes, :], p_sc[w])
            acc_sc[h, :, cols] = a_sc[h, :, cols] * acc_sc[h, :, cols] + pv

    def body(n, carry):
        block(n, 0)
        return carry

    lax.fori_loop(0, qi, body, 0)
    block(qi, 1)
    for h in range(B_HEADS):
        o = acc_sc[h] / l_sc[h]
        o = o[:, :t] - lam * o[:, t:]
        o = o * lax.rsqrt(jnp.mean(o * o, axis=0, keepdims=True) + RMS_EPS) * g_ref[...]
        o_ref[:, h * lanes:(h + 1) * lanes] = (o * (1.0 - LAMBDA_INIT)).T.astype(BF16)


def _diff(bqt, bk, bvt, scalars, diff_norm_g, bsz, seq):
    t = DIFF_BLOCK
    nb = seq // t
    lanes = 2 * HEAD_DIM
    n_items = B_HEADS * 2 * t // DIFF_LANES
    return pl.pallas_call(
        _diff_kernel,
        grid=(bsz, nb),
        in_specs=[
            pl.BlockSpec(memory_space=pltpu.SMEM),
            pl.BlockSpec((None, None, B_WIDTH, t), lambda b, i: (b, i, 0, 0)),
            pl.BlockSpec((None, nb, t, B_WIDTH), lambda b, i: (b, 0, 0, 0)),
            pl.BlockSpec((None, nb, B_WIDTH, t), lambda b, i: (b, 0, 0, 0)),
            pl.BlockSpec((lanes, 1), lambda b, i: (0, 0)),
            pl.BlockSpec((B_HEADS, 2, t, t), lambda b, i: (0, 0, 0, 0)),
        ],
        out_specs=pl.BlockSpec((None, t, B_WIDTH), lambda b, i: (b, i, 0)),
        out_shape=jax.ShapeDtypeStruct((bsz, seq, B_WIDTH), BF16),
        scratch_shapes=[pltpu.VMEM((B_HEADS, lanes, 2 * t), BF16)] + [pltpu.VMEM((B_HEADS, 1, 2 * t), F32)] * 3 + [
            pltpu.VMEM((B_HEADS, lanes, 2 * t), F32),
            pltpu.VMEM((n_items, t, DIFF_LANES), F32), pltpu.VMEM((n_items, t, DIFF_LANES), BF16)],
        compiler_params=_cparams(("parallel", "parallel")),
        name="diff",
    )(scalars, bqt, bk, bvt, diff_norm_g.reshape(lanes, 1), _diff_bias_table())


def _router_kernel(o0, l0, o1, l1, o2, l2, ob_ref, x_ref, mod_ref, wout_ref, g2_ref, wrh_ref, wrl_ref, br_ref,
                   tri_ref, x1_ref, he_ref, idx_ref, rank_ref, cnt_ref, *order_sc):
    def token_order(ref, scratch):
        dil, rows, width = ref.shape
        if dil == 1:
            return ref[0]
        n_col = width // LANES
        for r in range(dil):
            for c in range(n_col):
                scratch[c, pl.ds(r, rows, stride=dil), :] = ref[r, :, c * LANES:(c + 1) * LANES]
        return jnp.concatenate([scratch[c] for c in range(n_col)], axis=1)

    os_ = [token_order(o0, None), token_order(o1, order_sc[0]), token_order(o2, order_sc[2])]
    ls = [token_order(l0, None), token_order(l1, order_sc[1]), token_order(l2, order_sc[3])]
    mx = jnp.maximum(jnp.maximum(ls[0], ls[1]), ls[2])
    ws = [jnp.exp(l - mx) for l in ls]
    den = ws[0] + ws[1] + ws[2]
    oa = (ws[0] * os_[0] + ws[1] * os_[1] + ws[2] * os_[2]) / den
    mixed = _dot(oa.astype(BF16), wout_ref[:A_WIDTH, :]) + _dot(ob_ref[...], wout_ref[A_WIDTH:, :])
    x1 = x_ref[...] + mod_ref[2:3, :] * mixed
    x1_ref[...] = x1
    y = x1 * lax.rsqrt(jnp.mean(x1 * x1, axis=-1, keepdims=True) + RMS_EPS) * g2_ref[...]
    h2 = y * (1.0 + mod_ref[4:5, :]) + mod_ref[3:4, :]
    he_ref[:, :D_MODEL] = h2.astype(BF16)

    hh, hl = _split_bf16(h2)
    wrh = wrh_ref[...]
    logits = _dot_nt(wrh, hh) + _dot_nt(wrh, hl) + _dot_nt(wrl_ref[...], hh) + br_ref[...]
    t = logits.shape[1]
    eid = lax.broadcasted_iota(I32, (N_EXPERTS, t), 0)
    vals, idxs, hots = [], [], []
    cur = logits
    for _ in range(TOP_K):
        v = jnp.max(cur, axis=0, keepdims=True)
        ik = jnp.min(jnp.where(cur == v, eid, N_EXPERTS), axis=0, keepdims=True)
        hot = eid == ik
        vals.append(v)
        idxs.append(ik)
        hots.append(hot)
        cur = jnp.where(hot, -jnp.inf, cur)
    es = [jnp.exp(v - vals[0]) for v in vals]
    esum = es[0] + es[1] + es[2] + es[3]
    idx_ref[...] = jnp.concatenate(idxs, axis=0)

    rows = [ik.astype(F32) for ik in idxs]
    for e in es:
        g = e / esum
        hi = g.astype(BF16).astype(F32)
        mid = (g - hi).astype(BF16).astype(F32)
        rows += [hi, mid, g - hi - mid]
    rows.append(jnp.zeros((EXTRA_LANES - len(rows), t), F32))
    he_ref[:, D_MODEL:] = jnp.concatenate(rows, axis=0).T.astype(BF16)

    sel = jnp.where(hots[0] | hots[1] | hots[2] | hots[3], 1.0, 0.0)
    before = _dot(sel.astype(BF16), tri_ref[...])
    ranks = [jnp.sum(jnp.where(hot, before, 0.0), axis=0, keepdims=True) for hot in hots]
    rank_ref[...] = jnp.concatenate(ranks, axis=0).astype(I32)
    cnt_ref[...] = jnp.broadcast_to(jnp.sum(sel, axis=1, keepdims=True), cnt_ref.shape)


def _router(dil_outs, o_b, x2, mod_b, w_out, norm2_g, w_router, b_router, seq):
    n = x2.shape[0]
    t = ROW_TILE
    per_b = seq // t
    wr_t = w_router.T
    wrh = wr_t.astype(BF16)
    wrl = (wr_t - wrh.astype(F32)).astype(BF16)
    tri = jnp.asarray(np.arange(t)[:, None] < np.arange(t)[None, :], BF16)
    row = lambda w: pl.BlockSpec((t, w), lambda i: (i, 0))
    full = lambda a, b: pl.BlockSpec((a, b), lambda i: (0, 0))
    tok = lambda: pl.BlockSpec((TOP_K, t), lambda i: (0, i))
    grouped = lambda dil: pl.BlockSpec((None, dil, t // dil, A_WIDTH), lambda i: (i // per_b, 0, i % per_b, 0))
    n_regrouped = 2 * sum(dil > 1 for _, dil in DIL_PATTERNS)
    return pl.pallas_call(
        _router_kernel,
        grid=(n // t,),
        in_specs=[grouped(dil) for _, dil in DIL_PATTERNS for _ in range(2)] + [
            row(B_WIDTH), row(D_MODEL),
            pl.BlockSpec((None, 6, D_MODEL), lambda i: (i // per_b, 0, 0)),
            full(D_MODEL, D_MODEL), full(1, D_MODEL), full(N_EXPERTS, D_MODEL), full(N_EXPERTS, D_MODEL),
            full(N_EXPERTS, 1), full(t, t),
        ],
        out_specs=[row(D_MODEL), row(ROW_WIDTH), tok(), tok(),
                   pl.BlockSpec((None, N_EXPERTS, 128), lambda i: (i, 0, 0))],
        out_shape=[
            jax.ShapeDtypeStruct((n, D_MODEL), F32),
            jax.ShapeDtypeStruct((n, ROW_WIDTH), BF16),
            jax.ShapeDtypeStruct((TOP_K, n), I32),
            jax.ShapeDtypeStruct((TOP_K, n), I32),
            jax.ShapeDtypeStruct((n // t, N_EXPERTS, 128), F32),
        ],
        scratch_shapes=[pltpu.VMEM((A_WIDTH // LANES, t, LANES), F32)] * n_regrouped,
        compiler_params=_cparams(("parallel",)),
        name="router",
    )(*dil_outs, o_b, x2, mod_b, w_out.astype(BF16), norm2_g.reshape(1, D_MODEL), wrh, wrl,
      b_router.reshape(N_EXPERTS, 1), tri)


def _segment_copies(loff_ref, goff_ref, len_ref, tile, local_ref, hbm_ref, sem, outbound, action):
    def per_expert(e, carry):
        base = tile * N_EXPERTS + e
        lo, go, n = loff_ref[base], goff_ref[base], len_ref[base]
        done = jnp.int32(0)
        for size in SEG_SIZES:
            take = (n & size) != 0
            loc = local_ref.at[pl.ds(pl.multiple_of(lo + done, SEG_ALIGN), size)]
            glob = hbm_ref.at[pl.ds(pl.multiple_of(go + done, SEG_ALIGN), size)]
            cp = pltpu.make_async_copy(loc, glob, sem) if outbound else pltpu.make_async_copy(glob, loc, sem)
            pl.when(take)(functools.partial(action, cp))
            done = done + jnp.where(take, size, 0)
        return carry

    lax.fori_loop(0, N_EXPERTS, per_expert, 0)


def _start(cp):
    cp.start()


def _wait(cp):
    cp.wait()


def _tile_rows(loff_ref, len_ref, tile):
    last = tile * N_EXPERTS + N_EXPERTS - 1
    return loff_ref[last] + len_ref[last]


def _wait_rows(rows, local_ref, hbm_ref, sem, outbound):
    size = 1 << (LOCAL_SLOTS.bit_length() - 1)
    while size >= SEG_ALIGN:
        loc, glob = local_ref.at[pl.ds(0, size)], hbm_ref.at[pl.ds(0, size)]
        cp = pltpu.make_async_copy(loc, glob, sem) if outbound else pltpu.make_async_copy(glob, loc, sem)
        pl.when((rows & size) != 0)(cp.wait)
        size //= 2


def _one_hot_any(j, targets):
    out = jnp.zeros(j.shape, F32)
    for tgt in targets:
        out = jnp.where(j == tgt, 1.0, out)
    return out


def _sort_kernel(loff_ref, goff_ref, len_ref, tail_ref, he_ref, idx_ref, rank_ref, lcol_ref, ls_ref, xs_hbm,
                 xl, zbuf, sems, zsem):
    i = pl.program_id(0)
    last = pl.num_programs(0) - 1
    slot = i % 2
    t = he_ref.shape[0]

    @pl.when(i == 0)
    def _():
        zbuf[...] = jnp.zeros(zbuf.shape, BF16)

        def tails(action):
            def per_expert(e, carry):
                off, n = tail_ref[e], tail_ref[N_EXPERTS + e]
                done = jnp.int32(0)
                for size in SEG_SIZES:
                    if size < MOE_TILE:
                        take = (n & size) != 0
                        dst = xs_hbm.at[pl.ds(pl.multiple_of(off + done, SEG_ALIGN), size)]
                        pl.when(take)(functools.partial(action, pltpu.make_async_copy(zbuf.at[pl.ds(0, size)], dst, zsem)))
                        done = done + jnp.where(take, size, 0)
                return carry

            lax.fori_loop(0, N_EXPERTS, per_expert, 0)

            def per_block(b, carry):
                dst = xs_hbm.at[pl.ds(pl.multiple_of(b * MOE_TILE, MOE_TILE), MOE_TILE)]
                action(pltpu.make_async_copy(zbuf, dst, zsem))
                return carry

            lax.fori_loop(tail_ref[2 * N_EXPERTS], xs_hbm.shape[0] // MOE_TILE, per_block, 0)

        tails(_start)
        tails(_wait)

    eid = lax.broadcasted_iota(I32, (N_EXPERTS, t), 0)
    lcol = lcol_ref[...]
    ls = []
    for k in range(TOP_K):
        off = jnp.sum(jnp.where(eid == idx_ref[k:k + 1, :], lcol, 0), axis=0, keepdims=True)
        ls.append(off + rank_ref[k:k + 1, :])
    ls_ref[...] = jnp.concatenate(ls, axis=0)

    he = he_ref[...]
    for jc in range(LOCAL_SLOTS // SLOT_CHUNK):
        j = lax.broadcasted_iota(I32, (SLOT_CHUNK, t), 0) + jc * SLOT_CHUNK
        perm = _one_hot_any(j, ls).astype(BF16)
        xl[slot, jc * SLOT_CHUNK:(jc + 1) * SLOT_CHUNK, :] = _dot(perm, he).astype(BF16)

    tables = (loff_ref, goff_ref, len_ref)
    _segment_copies(*tables, i, xl.at[slot], xs_hbm, sems.at[slot], True, _start)

    @pl.when(i > 0)
    def _():
        _wait_rows(_tile_rows(loff_ref, len_ref, i - 1), xl.at[1 - slot], xs_hbm, sems.at[1 - slot], True)

    @pl.when(i == last)
    def _():
        _wait_rows(_tile_rows(loff_ref, len_ref, i), xl.at[slot], xs_hbm, sems.at[slot], True)


def _sort(tables, tail, he, idx, rank, lcol, n_rows):
    n = he.shape[0]
    t = ROW_TILE
    tok = lambda: pl.BlockSpec((TOP_K, t), lambda i, *_: (0, i))
    return pl.pallas_call(
        _sort_kernel,
        grid_spec=pltpu.PrefetchScalarGridSpec(
            num_scalar_prefetch=4,
            grid=(n // t,),
            in_specs=[
                pl.BlockSpec((t, ROW_WIDTH), lambda i, *_: (i, 0)),
                tok(), tok(),
                pl.BlockSpec((None, N_EXPERTS, 1), lambda i, *_: (i, 0, 0)),
            ],
            out_specs=[tok(), pl.BlockSpec(memory_space=pl.ANY)],
            scratch_shapes=[pltpu.VMEM((2, LOCAL_SLOTS, ROW_WIDTH), BF16), pltpu.VMEM((MOE_TILE, ROW_WIDTH), BF16),
                            pltpu.SemaphoreType.DMA((2,)), pltpu.SemaphoreType.DMA(())],
        ),
        out_shape=[jax.ShapeDtypeStruct((TOP_K, n), I32), jax.ShapeDtypeStruct((n_rows, ROW_WIDTH), BF16)],
        compiler_params=_cparams(("arbitrary",)),
        name="sort",
    )(*tables, tail, he, idx, rank, lcol)


def _experts_kernel(blk0_ref, nblk_ref, xs_hbm, wgu_ref, bgu_ref, wd_ref, bd_ref, yb_hbm, wgu_sc, wd_sc, xbuf, ybuf,
                    xsem, ysem):
    e = pl.program_id(0)
    tm = MOE_TILE
    first, nb = blk0_ref[e], nblk_ref[e]
    last_e = pl.num_programs(0) - 1
    total = blk0_ref[last_e] + nblk_ref[last_e]

    def rows(b):
        return pl.ds(pl.multiple_of(b * tm, tm), tm)

    def x_copy(b):
        slot = b % X_SLOTS
        return pltpu.make_async_copy(xs_hbm.at[rows(b)], xbuf.at[slot], xsem.at[slot])

    def y_copy(b, slot):
        return pltpu.make_async_copy(ybuf.at[slot], yb_hbm.at[rows(b)], ysem.at[slot])

    @pl.when(e == 0)
    def _():
        for b in range(X_AHEAD):
            pl.when(b < total)(x_copy(b).start)

    def cast(r, carry):
        s = pl.multiple_of(r * LANES, LANES)
        wgu_sc[pl.ds(s, LANES), :] = wgu_ref[pl.ds(s, LANES), :].astype(BF16)
        wd_sc[pl.ds(s, LANES), :] = wd_ref[pl.ds(s, LANES), :].astype(BF16)
        return carry

    lax.fori_loop(0, D_MODEL // LANES, cast, 0)
    me = jnp.full((1, 1), e, I32).astype(F32)

    def block(i, carry):
        b = first + i
        slot = i % 2
        xslot = b % X_SLOTS

        @pl.when(b + X_AHEAD < total)
        def _():
            x_copy(b + X_AHEAD).start()

        x_copy(b).wait()

        @pl.when(i >= 2)
        def _():
            y_copy(b - 2, slot).wait()

        ext = xbuf[xslot, :, D_MODEL:].astype(F32)
        gate = jnp.zeros((tm, 1), F32)
        for k in range(TOP_K):
            c = TOP_K + 3 * k
            gk = ext[:, c:c + 1] + ext[:, c + 1:c + 2] + ext[:, c + 2:c + 3]
            gate = gate + jnp.where(ext[:, k:k + 1] == me, gk, 0.0)

        gu = _dot(xbuf[xslot, :, :D_MODEL], wgu_sc[...]) + bgu_ref[...]
        g = jnp.minimum(gu[:, :D_FF], SWIGLU_LIMIT)
        u = jnp.clip(gu[:, D_FF:], -SWIGLU_LIMIT, SWIGLU_LIMIT)
        act = (u + 1.0) * (g / (1.0 + jnp.exp(-SWIGLU_ALPHA * g)))
        ybuf[slot] = (gate * (_dot(act.astype(BF16), wd_sc[...]) + bd_ref[...])).astype(BF16)
        y_copy(b, slot).start()
        return carry

    lax.fori_loop(0, nb, block, 0)

    @pl.when(nb >= 2)
    def _():
        y_copy(first + nb - 2, nb % 2).wait()

    @pl.when(nb >= 1)
    def _():
        y_copy(first + nb - 1, (nb - 1) % 2).wait()

    @pl.when(e == last_e)
    def _():
        ybuf[0] = jnp.zeros((tm, D_MODEL), BF16)

        def fill(action):
            def per_block(b, carry):
                action(pltpu.make_async_copy(ybuf.at[0], yb_hbm.at[rows(b)], ysem.at[0]))
                return carry

            lax.fori_loop(total, yb_hbm.shape[0] // tm, per_block, 0)

        fill(_start)
        fill(_wait)


def _experts(first_block, n_block, xs, w_gate_up, b_gate_up, w_down, b_down):
    n_rows = xs.shape[0]
    exp3 = lambda e, *_: (e, 0, 0)
    return pl.pallas_call(
        _experts_kernel,
        grid_spec=pltpu.PrefetchScalarGridSpec(
            num_scalar_prefetch=2,
            grid=(N_EXPERTS,),
            in_specs=[
                pl.BlockSpec(memory_space=pl.ANY),
                pl.BlockSpec((None, D_MODEL, 2 * D_FF), exp3),
                pl.BlockSpec((None, 1, 2 * D_FF), exp3),
                pl.BlockSpec((None, D_FF, D_MODEL), exp3),
                pl.BlockSpec((None, 1, D_MODEL), exp3),
            ],
            out_specs=pl.BlockSpec(memory_space=pl.ANY),
            scratch_shapes=[pltpu.VMEM((D_MODEL, 2 * D_FF), BF16), pltpu.VMEM((D_FF, D_MODEL), BF16),
                            pltpu.VMEM((X_SLOTS, MOE_TILE, ROW_WIDTH), BF16), pltpu.VMEM((2, MOE_TILE, D_MODEL), BF16),
                            pltpu.SemaphoreType.DMA((X_SLOTS,)), pltpu.SemaphoreType.DMA((2,))],
        ),
        out_shape=jax.ShapeDtypeStruct((n_rows, D_MODEL), BF16),
        compiler_params=_cparams(("arbitrary",)),
        name="experts",
    )(first_block, n_block, xs, w_gate_up, b_gate_up.reshape(N_EXPERTS, 1, 2 * D_FF), w_down,
      b_down.reshape(N_EXPERTS, 1, D_MODEL))


def _combine_kernel(loff_ref, goff_ref, len_ref, yb_hbm, lst_ref, x1_ref, mod_ref, o_ref, ybuf, sems):
    i = pl.program_id(0)
    slot = i % 2
    tables = (loff_ref, goff_ref, len_ref)

    @pl.when(i == 0)
    def _():
        ybuf[...] = jnp.zeros(ybuf.shape, BF16)
        _segment_copies(*tables, 0, ybuf.at[0], yb_hbm, sems.at[0], False, _start)

    @pl.when(i + 1 < pl.num_programs(0))
    def _():
        _segment_copies(*tables, i + 1, ybuf.at[1 - slot], yb_hbm, sems.at[1 - slot], False, _start)

    _wait_rows(_tile_rows(loff_ref, len_ref, i), ybuf.at[slot], yb_hbm, sems.at[slot], False)

    lst = lst_ref[...]
    t = lst.shape[0]
    targets = [lst[:, k:k + 1] for k in range(TOP_K)]
    y = jnp.zeros((t, D_MODEL), F32)
    for jc in range(LOCAL_SLOTS // SLOT_CHUNK):
        j = lax.broadcasted_iota(I32, (t, SLOT_CHUNK), 1) + jc * SLOT_CHUNK
        pick = _one_hot_any(j, targets).astype(BF16)
        y = y + _dot(pick, ybuf[slot, jc * SLOT_CHUNK:(jc + 1) * SLOT_CHUNK, :])
    o_ref[...] = x1_ref[...] + mod_ref[5:6, :] * y


def _combine(tables, yb, ls_t, x1, mod_b, seq):
    n = x1.shape[0]
    t = ROW_TILE
    per_b = seq // t
    return pl.pallas_call(
        _combine_kernel,
        grid_spec=pltpu.PrefetchScalarGridSpec(
            num_scalar_prefetch=3,
            grid=(n // t,),
            in_specs=[
                pl.BlockSpec(memory_space=pl.ANY),
                pl.BlockSpec((t, TOP_K), lambda i, *_: (i, 0)),
                pl.BlockSpec((t, D_MODEL), lambda i, *_: (i, 0)),
                pl.BlockSpec((None, 6, D_MODEL), lambda i, *_: (i // per_b, 0, 0)),
            ],
            out_specs=pl.BlockSpec((t, D_MODEL), lambda i, *_: (i, 0)),
            scratch_shapes=[pltpu.VMEM((2, LOCAL_SLOTS, D_MODEL), BF16), pltpu.SemaphoreType.DMA((2,))],
        ),
        out_shape=jax.ShapeDtypeStruct((n, D_MODEL), F32),
        compiler_params=_cparams(("arbitrary",)),
        name="combine",
    )(*tables, yb, ls_t, x1, mod_b)


def _layer(x, c, w_ada, b_ada, norm1_g, w_in, a_q_norm_g, a_k_norm_g, b_q_norm_g, b_k_norm_g, lambda_q1,
           lambda_k1, lambda_q2, lambda_k2, diff_norm_g, w_out, norm2_g, w_router, b_router, w_gate_up,
           b_gate_up, w_down, b_down):
    bsz, seq, _ = x.shape
    n = bsz * seq
    x2 = x.reshape(n, D_MODEL)
    mod_b = _ada(c, w_ada, b_ada).transpose(1, 0, 2)

    qk_gains = jnp.stack([
        jnp.tile(a_q_norm_g, A_HEADS) * ATTN_SCALE, jnp.tile(a_k_norm_g, A_HEADS), jnp.tile(b_k_norm_g, 2 * B_HEADS)])
    bq_gain = (jnp.tile(b_q_norm_g, 2 * B_HEADS) * ATTN_SCALE).reshape(B_WIDTH, 1)
    *a_parts, bk, bqt, bvt = _inproj(x2, mod_b, norm1_g, w_in, qk_gains, bq_gain, bsz, seq)

    dil_outs = _dilated(a_parts, bsz, seq)
    lam = (jnp.exp(jnp.sum(lambda_q1 * lambda_k1)) - jnp.exp(jnp.sum(lambda_q2 * lambda_k2)) + LAMBDA_INIT)
    slopes_b = 2.0 ** (-8.0 * np.arange(1, B_HEADS + 1) / B_HEADS)
    scalars = jnp.concatenate([lam.reshape(1), jnp.asarray(slopes_b, F32)]).astype(F32)
    bk4 = bk.reshape(bsz, seq // DIFF_BLOCK, DIFF_BLOCK, B_WIDTH)
    o_b = _diff(bqt, bk4, bvt, scalars, diff_norm_g, bsz, seq).reshape(n, B_WIDTH)

    x1, he, idx, rank, cnt = _router(dil_outs, o_b, x2, mod_b, w_out, norm2_g, w_router, b_router, seq)

    n_tiles = n // ROW_TILE
    counts = cnt[:, :, 0].astype(I32)
    seg = (counts + SEG_ALIGN - 1) // SEG_ALIGN * SEG_ALIGN
    loff = jnp.cumsum(seg, axis=1) - seg
    region = jnp.sum(seg, axis=0)
    padded = (region + MOE_TILE - 1) // MOE_TILE * MOE_TILE
    pad_end = jnp.cumsum(padded)
    pad_start = pad_end - padded
    goff = pad_start[None, :] + jnp.cumsum(seg, axis=0) - seg
    tables = (loff.reshape(-1), goff.reshape(-1), seg.reshape(-1))
    n_blocks = (n * TOP_K + n_tiles * N_EXPERTS * (SEG_ALIGN - 1) + N_EXPERTS * (MOE_TILE - 1)) // MOE_TILE
    n_used = (pad_end[-1] // MOE_TILE).astype(I32)
    tail = jnp.concatenate([pad_start + region, padded - region, n_used.reshape(1)])

    ls, xs = _sort(tables, tail, he, idx, rank, loff.reshape(n_tiles, N_EXPERTS, 1), n_blocks * MOE_TILE)
    yb = _experts(pad_start // MOE_TILE, padded // MOE_TILE, xs, w_gate_up, b_gate_up, w_down, b_down)
    out = _combine(tables, yb, ls.T, x1, mod_b, seq)
    return out.reshape(bsz, seq, D_MODEL)


def kernel(x, c, w_ada, b_ada, norm1_g, w_in, a_q_norm_g, a_k_norm_g, b_q_norm_g, b_k_norm_g, lambda_q1, lambda_k1,
           lambda_q2, lambda_k2, diff_norm_g, w_out, norm2_g, w_router, b_router, w_gate_up, b_gate_up, w_down,
           b_down):
    args = (w_ada, b_ada, norm1_g, w_in, a_q_norm_g, a_k_norm_g, b_q_norm_g, b_k_norm_g, lambda_q1, lambda_k1,
            lambda_q2, lambda_k2, diff_norm_g, w_out, norm2_g, w_router, b_router, w_gate_up, b_gate_up, w_down,
            b_down)
    return _layer(x, c, *[a[0] for a in args])
```

```python
import functools

import numpy as np
import jax
import jax.numpy as jnp
from jax import lax
from jax.experimental import pallas as pl
from jax.experimental.pallas import tpu as pltpu

F32 = jnp.float32
BF16 = jnp.bfloat16
I32 = jnp.int32
U32 = jnp.uint32

D_MODEL = 1024
HEAD_DIM = 64
A_WIDTH = 512
B_WIDTH = 512
A_HEADS = 8
B_HEADS = 4
IN_WIDTH = 3072
DIL_PATTERNS = ((128, 1), (512, 4), (2048, 16))
BAND = 128
N_EXPERTS = 32
TOP_K = 4
D_FF = 1024
SWIGLU_LIMIT = 7.0
SWIGLU_ALPHA = 1.702
RMS_EPS = 1e-6
ATTN_SCALE = HEAD_DIM ** -0.5
LOG2E = 1.4426950408889634
LN2 = 0.6931471805599453
LAMBDA_INIT = 0.8 - 0.6 * 1.0

LANES = 128
ROW_TILE = 512
DIFF_BLOCK = 512
DIFF_LANES = 256
DIFF_ROWS = 64
MOE_TILE = 256
X_AHEAD = 4
X_SLOTS = X_AHEAD + 1
EXTRA_LANES = 128
ROW_WIDTH = D_MODEL + EXTRA_LANES
SEG_ALIGN = 16
SEG_SIZES = (512, 256, 128, 64, 32, 16)
SLOT_CHUNK = 512
LOCAL_SLOTS = 2560
LOCAL_PIECES = LOCAL_SLOTS // SEG_ALIGN
assert ROW_TILE == DIFF_BLOCK == SEG_SIZES[0]
assert LOCAL_SLOTS >= ROW_TILE * TOP_K + N_EXPERTS * (SEG_ALIGN - 1) and LOCAL_SLOTS % SLOT_CHUNK == 0
VMEM_LIMIT = 56 * 1024 * 1024


def _cparams(sem, **flags):
    return pltpu.CompilerParams(dimension_semantics=sem, vmem_limit_bytes=VMEM_LIMIT, flags=flags or None)


def _split_bf16(a):
    hi = a.astype(BF16)
    lo = (a - hi.astype(F32)).astype(BF16)
    return hi, lo


def _dot_nt(a, b):
    return lax.dot_general(a, b, (((1,), (1,)), ((), ())), preferred_element_type=F32)


def _dot(a, b):
    return jnp.dot(a, b, preferred_element_type=F32)


def _ada_kernel(c_ref, w_ref, b_ref, o_ref):
    c = c_ref[...]
    s = c / (1.0 + jnp.exp(-c))
    sh, sl = _split_bf16(s)
    wh, wl = _split_bf16(w_ref[...])
    o_ref[0] = _dot(sh, wh) + _dot(sh, wl) + _dot(sl, wh) + b_ref[0]


def _ada(c, w_ada, b_ada):
    bsz = c.shape[0]
    return pl.pallas_call(
        _ada_kernel,
        grid=(6,),
        in_specs=[
            pl.BlockSpec((bsz, D_MODEL), lambda j: (0, 0)),
            pl.BlockSpec((D_MODEL, D_MODEL), lambda j: (0, j)),
            pl.BlockSpec((1, 1, D_MODEL), lambda j: (j, 0, 0)),
        ],
        out_specs=pl.BlockSpec((1, bsz, D_MODEL), lambda j: (j, 0, 0)),
        out_shape=jax.ShapeDtypeStruct((6, bsz, D_MODEL), F32),
        compiler_params=_cparams(("arbitrary",)),
        name="ada",
    )(c, w_ada, b_ada.reshape(6, 1, D_MODEL))


def _inproj_kernel(x_ref, mod_ref, g1_ref, wn_ref, wt_ref, gm_ref, qkg_ref, bqg_ref, a1_ref, a4_ref, a16_ref, bk_ref,
                   bqt_ref, bvt_ref, a_sc):
    a_refs = (a1_ref, a4_ref, a16_ref)
    x = x_ref[...]
    y = x * lax.rsqrt(jnp.mean(x * x, axis=-1, keepdims=True) + RMS_EPS) * g1_ref[...]
    h = (y * (1.0 + mod_ref[1:2, :]) + mod_ref[0:1, :]).astype(BF16)
    p = _dot(h, wn_ref[...])
    gm = gm_ref[...]

    def head_norm(t, g):
        ss = _dot((t * t).astype(BF16), gm)
        return t * lax.rsqrt(ss * (1.0 / HEAD_DIM) + RMS_EPS) * g

    w = A_WIDTH
    a_part = jnp.concatenate([head_norm(p[:, 0 * w:1 * w], qkg_ref[0:1, :]),
                              head_norm(p[:, 1 * w:2 * w], qkg_ref[1:2, :]), p[:, 2 * w:3 * w]], axis=1)
    n_col = a_sc.shape[0]
    for c in range(n_col):
        a_sc[c] = a_part[:, c * LANES:(c + 1) * LANES]
    for a_ref, (_, dil) in zip(a_refs, DIL_PATTERNS):
        if dil == 1:
            a_ref[0] = a_part.astype(BF16)
            continue
        rows = a_part.shape[0] // dil
        for r in range(dil):
            for c in range(n_col):
                a_ref[r, :, c * LANES:(c + 1) * LANES] = a_sc[c, pl.ds(r, rows, stride=dil), :].astype(BF16)
    bk_ref[...] = head_norm(p[:, 3 * w:4 * w], qkg_ref[2:3, :]).astype(BF16)

    pt = _dot_nt(wt_ref[...], h)
    t = pt.shape[1]
    bq = pt[:w].reshape(w // HEAD_DIM, HEAD_DIM, t)
    ss = jnp.sum(bq * bq, axis=1, keepdims=True)
    bq = (bq * lax.rsqrt(ss * (1.0 / HEAD_DIM) + RMS_EPS)).reshape(w, t) * bqg_ref[...]
    bqt_ref[...] = bq.astype(BF16)
    bvt_ref[...] = pt[w:].astype(BF16)


def _inproj(x2, mod_b, norm1_g, w_in, qk_gains, bq_gain, bsz, seq):
    n = x2.shape[0]
    t = ROW_TILE
    per_b = seq // t
    w = A_WIDTH
    head_of_lane = np.arange(w) // HEAD_DIM
    gmat = jnp.asarray(head_of_lane[:, None] == head_of_lane[None, :], BF16)
    w_bf = w_in.astype(BF16)
    w_nat = jnp.concatenate([w_bf[:, :3 * w], w_bf[:, 4 * w:5 * w]], axis=1)
    w_tr = jnp.concatenate([w_bf[:, 3 * w:4 * w], w_bf[:, 5 * w:]], axis=1).T
    tr_spec = pl.BlockSpec((None, None, w, t), lambda i: (i // per_b, i % per_b, 0, 0))
    tr_shape = jax.ShapeDtypeStruct((bsz, per_b, w, t), BF16)
    a_specs = [pl.BlockSpec((None, dil, t // dil, 3 * w), lambda i: (i // per_b, 0, i % per_b, 0))
               for _, dil in DIL_PATTERNS]
    a_shapes = [jax.ShapeDtypeStruct((bsz, dil, seq // dil, 3 * w), BF16) for _, dil in DIL_PATTERNS]
    return pl.pallas_call(
        _inproj_kernel,
        grid=(n // t,),
        in_specs=[
            pl.BlockSpec((t, D_MODEL), lambda i: (i, 0)),
            pl.BlockSpec((None, 6, D_MODEL), lambda i: (i // per_b, 0, 0)),
            pl.BlockSpec((1, D_MODEL), lambda i: (0, 0)),
            pl.BlockSpec((D_MODEL, 4 * w), lambda i: (0, 0)),
            pl.BlockSpec((2 * w, D_MODEL), lambda i: (0, 0)),
            pl.BlockSpec((w, w), lambda i: (0, 0)),
            pl.BlockSpec((3, w), lambda i: (0, 0)),
            pl.BlockSpec((w, 1), lambda i: (0, 0)),
        ],
        out_specs=a_specs + [pl.BlockSpec((t, w), lambda i: (i, 0)), tr_spec, tr_spec],
        out_shape=a_shapes + [jax.ShapeDtypeStruct((n, w), BF16), tr_shape, tr_shape],
        scratch_shapes=[pltpu.VMEM((3 * w // LANES, t, LANES), F32)],
        compiler_params=_cparams(("parallel",)),
        name="inproj",
    )(x2, mod_b, norm1_g.reshape(1, D_MODEL), w_nat, w_tr, gmat, qk_gains, bq_gain)


def _dilated_bias_table():
    ik = np.arange(2 * BAND)[:, None]
    iq = np.arange(BAND)[None, :]
    delta = iq - ik + BAND
    in_band = (delta >= 0) & (delta <= BAND)
    slopes = 2.0 ** (-8.0 * np.arange(1, A_HEADS + 1) / A_HEADS)
    tbl = np.zeros((len(DIL_PATTERNS), 2, A_HEADS // 2, 2 * BAND, 2 * BAND), np.float32)
    for p, (_, dil) in enumerate(DIL_PATTERNS):
        for first in range(2):
            valid = in_band & ((ik >= BAND) if first else True)
            for h in range(A_HEADS):
                cols = slice(BAND * (h % 2), BAND * (h % 2 + 1))
                tbl[p, first, h // 2, :, cols] = np.where(valid, -slopes[h] * LOG2E * (delta * dil), -np.inf)
    return jnp.asarray(tbl)


def _dilated_kernel(tbl_ref, *refs, n_blocks):
    lanes = 2 * HEAD_DIM
    n = pl.program_id(1)
    n_pat = len(DIL_PATTERNS)
    s_sc, p_sc = refs[len(refs) - 2:]
    outs = refs[len(refs) - 2 - 2 * n_pat:len(refs) - 2]
    lane = lax.broadcasted_iota(I32, (BAND, lanes), 1)
    low = lane < HEAD_DIM
    row = lax.broadcasted_iota(I32, (lanes, BAND), 0)
    top = row < HEAD_DIM
    work = []
    pos = 0
    for p, (_, dil) in enumerate(DIL_PATTERNS):
        per_res = n_blocks // dil
        if per_res > 1:
            q_ref, kp_ref, kc_ref, vp_ref, vc_ref = refs[pos:pos + 5]
            pos += 5
            first = (n % per_res == 0).astype(I32)
        else:
            q_ref, kc_ref, vc_ref = refs[pos:pos + 3]
            kp_ref = vp_ref = first = None
            pos += 3
        for j in range(A_HEADS // 2):
            work.append((p, j, (q_ref, kp_ref, kc_ref, vp_ref, vc_ref), first))

    def keys_of(prev_ref, cur_ref, cs):
        return cur_ref[:, cs] if prev_ref is None else jnp.concatenate([prev_ref[:, cs], cur_ref[:, cs]], axis=0)

    for w, (p, j, (q_ref, kp_ref, kc_ref, _, _), first) in enumerate(work):
        cs = slice(lanes * j, lanes * (j + 1))
        q2 = q_ref[:, cs]
        zero = jnp.zeros_like(q2)
        qcat = jnp.concatenate([jnp.where(low, q2, zero), jnp.where(low, zero, q2)], axis=0)
        k2 = keys_of(kp_ref, kc_ref, cs)
        bias = tbl_ref[p, 1, j, BAND:, :] if kp_ref is None else tbl_ref[p, first, j]
        s_sc[w, :k2.shape[0], :] = _dot_nt(k2, qcat) + bias
    lses = []
    for w, (p, j, (_, kp_ref, _, _, _), _) in enumerate(work):
        nk = BAND if kp_ref is None else 2 * BAND
        s = s_sc[w, :nk, :]
        m = jnp.max(s, axis=0, keepdims=True)
        e = jnp.exp2(s - m)
        den = jnp.sum(e, axis=0, keepdims=True)
        p_sc[w, :nk, :] = (e * (1.0 / den)).astype(BF16)
        lses.append((m + jnp.log2(den)) * LN2)
    for w, (p, j, (_, kp_ref, _, vp_ref, vc_ref), _) in enumerate(work):
        nk = BAND if kp_ref is None else 2 * BAND
        cs = slice(lanes * j, lanes * (j + 1))
        v2 = keys_of(vp_ref, vc_ref, cs)
        r = lax.dot_general(p_sc[w, :nk, :], v2, (((0,), (0,)), ((), ())), preferred_element_type=F32)
        o_ref, l_ref = outs[2 * p:2 * p + 2]
        o_ref[:, cs] = jnp.where(low, r[:BAND], r[BAND:])
        lse = lses[w]
        lse_t = jnp.where(top, jnp.broadcast_to(lse[:, :BAND], (lanes, BAND)),
                          jnp.broadcast_to(lse[:, BAND:], (lanes, BAND)))
        l_ref[:, cs] = lse_t.T


def _dilated(a_parts, bsz, seq):
    n_blocks = seq // BAND
    in_specs = [pl.BlockSpec((len(DIL_PATTERNS), 2, A_HEADS // 2, 2 * BAND, 2 * BAND), lambda b, n: (0, 0, 0, 0, 0))]
    n_work = len(DIL_PATTERNS) * A_HEADS // 2
    args = [_dilated_bias_table()]
    out_specs, out_shapes = [], []
    blk = (None, None, BAND, A_WIDTH)
    for a_part, (_, dil) in zip(a_parts, DIL_PATTERNS):
        per_res = n_blocks // dil

        def cur(col, per_res=per_res):
            return lambda b, n: (b, n // per_res, n % per_res, col)

        def prev(col, per_res=per_res):
            return lambda b, n: (b, n // per_res, jnp.maximum(n % per_res - 1, 0), col)

        if per_res > 1:
            in_specs += [pl.BlockSpec(blk, cur(0)), pl.BlockSpec(blk, prev(1)), pl.BlockSpec(blk, cur(1)),
                         pl.BlockSpec(blk, prev(2)), pl.BlockSpec(blk, cur(2))]
            args += [a_part] * 5
        else:
            in_specs += [pl.BlockSpec(blk, cur(0)), pl.BlockSpec(blk, cur(1)), pl.BlockSpec(blk, cur(2))]
            args += [a_part] * 3
        out_specs += [pl.BlockSpec(blk, cur(0))] * 2
        out_shapes += [jax.ShapeDtypeStruct((bsz, dil, seq // dil, A_WIDTH), F32)] * 2
    return pl.pallas_call(
        functools.partial(_dilated_kernel, n_blocks=n_blocks),
        grid=(bsz, n_blocks),
        in_specs=in_specs,
        out_specs=out_specs,
        out_shape=out_shapes,
        scratch_shapes=[pltpu.VMEM((n_work, 2 * BAND, 2 * BAND), F32), pltpu.VMEM((n_work, 2 * BAND, 2 * BAND), BF16)],
        compiler_params=_cparams(("parallel", "parallel")),
        name="dilated",
    )(*args)


def _diff_bias_table():
    t = DIFF_BLOCK
    rel = np.arange(t)[None, :] - np.arange(t)[:, None]
    slopes = 2.0 ** (-8.0 * np.arange(1, B_HEADS + 1) / B_HEADS)
    tbl = np.zeros((B_HEADS, 2, t, t), np.float32)
    for h in range(B_HEADS):
        tbl[h, 0] = -slopes[h] * LOG2E * rel
        tbl[h, 1] = np.where(rel >= 0, -slopes[h] * LOG2E * rel, -np.inf)
    return jnp.asarray(tbl)


def _diff_kernel(sc_ref, qt_ref, k_ref, vt_ref, g_ref, bias_sc, o_ref, qst_sc, m_sc, l_sc, a_sc, acc_sc, s_sc, p_sc):
    t = DIFF_BLOCK
    lanes = 2 * HEAD_DIM
    qi = pl.program_id(1)
    lam = sc_ref[0]
    slopes = [jnp.full((1, 1), sc_ref[1 + h], F32) for h in range(B_HEADS)]
    row = lax.broadcasted_iota(I32, (lanes, t), 0)
    for h in range(B_HEADS):
        qt = qt_ref[h * lanes:(h + 1) * lanes, :]
        zero = jnp.zeros_like(qt)
        qst_sc[h, :, :t] = jnp.where(row < HEAD_DIM, qt, zero)
        qst_sc[h, :, t:] = jnp.where(row < HEAD_DIM, zero, qt)
    m_sc[...] = jnp.full(m_sc.shape, -jnp.inf, F32)
    l_sc[...] = jnp.zeros(l_sc.shape, F32)
    acc_sc[...] = jnp.zeros(acc_sc.shape, F32)

    n_grp = 2 * t // DIFF_LANES
    items = [(h, g) for h in range(B_HEADS) for g in range(n_grp)]
    chunks = [slice(r * DIFF_ROWS, (r + 1) * DIFF_ROWS) for r in range(t // DIFF_ROWS)]

    def block(n, diag):
        cs = [-slopes[h] * LOG2E * (jnp.full((1, 1), (qi - n) * t, I32)).astype(F32) for h in range(B_HEADS)]
        for w, (h, g) in enumerate(items):
            cols = slice(g * DIFF_LANES, (g + 1) * DIFF_LANES)
            bcols = slice((g * DIFF_LANES) % t, (g * DIFF_LANES) % t + DIFF_LANES)
            s_sc[w] = _dot(k_ref[n, :, h * lanes:(h + 1) * lanes], qst_sc[h, :, cols]) + bias_sc[h, diag, :, bcols]
        for w, (h, g) in enumerate(items):
            cols = slice(g * DIFF_LANES, (g + 1) * DIFF_LANES)
            top = s_sc[w, chunks[0], :]
            for rows in chunks[1:]:
                top = jnp.maximum(top, s_sc[w, rows, :])
            m_prev = m_sc[h, :, cols]
            m_new = jnp.maximum(m_prev, jnp.max(top, axis=0, keepdims=True) + cs[h])
            alpha = jnp.exp2(m_prev - m_new)
            shift = m_new - cs[h]
            part = jnp.zeros(top.shape, F32)
            for rows in chunks:
                e = jnp.exp2(s_sc[w, rows, :] - shift)
                p_sc[w, rows, :] = e.astype(BF16)
                part = part + e
            l_sc[h, :, cols] = alpha * l_sc[h, :, cols] + jnp.sum(part, axis=0, keepdims=True)
            m_sc[h, :, cols] = m_new
            a_sc[h, :, cols] = alpha
        for w, (h, g) in enumerate(items):
            cols = slice(g * DIFF_LANES, (g + 1) * DIFF_LANES)
            pv = _dot(vt_ref[n, h * lanes:(h + 1) * lanes, :], p_sc[w])
            acc_sc[h, :, cols] = a_sc[h, :, cols] * acc_sc[h, :, cols] + pv

    def body(n, carry):
        block(n, 0)
        return carry

    lax.fori_loop(0, qi, body, 0)
    block(qi, 1)
    for h in range(B_HEADS):
        o = acc_sc[h] / l_sc[h]
        o = o[:, :t] - lam * o[:, t:]
        o = o * lax.rsqrt(jnp.mean(o * o, axis=0, keepdims=True) + RMS_EPS) * g_ref[...]
        o_ref[:, h * lanes:(h + 1) * lanes] = (o * (1.0 - LAMBDA_INIT)).T.astype(BF16)


def _diff(bqt, bk, bvt, scalars, diff_norm_g, bsz, seq):
    t = DIFF_BLOCK
    nb = seq // t
    lanes = 2 * HEAD_DIM
    n_items = B_HEADS * 2 * t // DIFF_LANES
    return pl.pallas_call(
        _diff_kernel,
        grid=(bsz, nb),
        in_specs=[
            pl.BlockSpec(memory_space=pltpu.SMEM),
            pl.BlockSpec((None, None, B_WIDTH, t), lambda b, i: (b, i, 0, 0)),
            pl.BlockSpec((None, nb, t, B_WIDTH), lambda b, i: (b, 0, 0, 0)),
            pl.BlockSpec((None, nb, B_WIDTH, t), lambda b, i: (b, 0, 0, 0)),
            pl.BlockSpec((lanes, 1), lambda b, i: (0, 0)),
            pl.BlockSpec((B_HEADS, 2, t, t), lambda b, i: (0, 0, 0, 0)),
        ],
        out_specs=pl.BlockSpec((None, t, B_WIDTH), lambda b, i: (b, i, 0)),
        out_shape=jax.ShapeDtypeStruct((bsz, seq, B_WIDTH), BF16),
        scratch_shapes=[pltpu.VMEM((B_HEADS, lanes, 2 * t), BF16)] + [pltpu.VMEM((B_HEADS, 1, 2 * t), F32)] * 3 + [
            pltpu.VMEM((B_HEADS, lanes, 2 * t), F32),
            pltpu.VMEM((n_items, t, DIFF_LANES), F32), pltpu.VMEM((n_items, t, DIFF_LANES), BF16)],
        compiler_params=_cparams(("parallel", "parallel")),
        name="diff",
    )(scalars, bqt, bk, bvt, diff_norm_g.reshape(lanes, 1), _diff_bias_table())


def _router_kernel(o0, l0, o1, l1, o2, l2, ob_ref, x_ref, mod_ref, wout_ref, g2_ref, wrh_ref, wrl_ref, br_ref,
                   tri_ref, x1_ref, he_ref, idx_ref, rank_ref, cnt_ref, *order_sc):
    def token_order(ref, scratch):
        dil, rows, width = ref.shape
        if dil == 1:
            return ref[0]
        n_col = width // LANES
        for r in range(dil):
            for c in range(n_col):
                scratch[c, pl.ds(r, rows, stride=dil), :] = ref[r, :, c * LANES:(c + 1) * LANES]
        return jnp.concatenate([scratch[c] for c in range(n_col)], axis=1)

    os_ = [token_order(o0, None), token_order(o1, order_sc[0]), token_order(o2, order_sc[2])]
    ls = [token_order(l0, None), token_order(l1, order_sc[1]), token_order(l2, order_sc[3])]
    mx = jnp.maximum(jnp.maximum(ls[0], ls[1]), ls[2])
    ws = [jnp.exp(l - mx) for l in ls]
    den = ws[0] + ws[1] + ws[2]
    oa = (ws[0] * os_[0] + ws[1] * os_[1] + ws[2] * os_[2]) / den
    mixed = _dot(oa.astype(BF16), wout_ref[:A_WIDTH, :]) + _dot(ob_ref[...], wout_ref[A_WIDTH:, :])
    x1 = x_ref[...] + mod_ref[2:3, :] * mixed
    x1_ref[...] = x1
    y = x1 * lax.rsqrt(jnp.mean(x1 * x1, axis=-1, keepdims=True) + RMS_EPS) * g2_ref[...]
    h2 = y * (1.0 + mod_ref[4:5, :]) + mod_ref[3:4, :]
    he_ref[:, :D_MODEL] = h2.astype(BF16)

    hh, hl = _split_bf16(h2)
    wrh = wrh_ref[...]
    logits = _dot_nt(wrh, hh) + _dot_nt(wrh, hl) + _dot_nt(wrl_ref[...], hh) + br_ref[...]
    t = logits.shape[1]
    eid = lax.broadcasted_iota(I32, (N_EXPERTS, t), 0)
    vals, idxs, hots = [], [], []
    cur = logits
    for _ in range(TOP_K):
        v = jnp.max(cur, axis=0, keepdims=True)
        ik = jnp.min(jnp.where(cur == v, eid, N_EXPERTS), axis=0, keepdims=True)
        hot = eid == ik
        vals.append(v)
        idxs.append(ik)
        hots.append(hot)
        cur = jnp.where(hot, -jnp.inf, cur)
    es = [jnp.exp(v - vals[0]) for v in vals]
    esum = es[0] + es[1] + es[2] + es[3]
    idx_ref[...] = jnp.concatenate(idxs, axis=0)

    rows = [ik.astype(F32) for ik in idxs]
    for e in es:
        g = e / esum
        hi = g.astype(BF16).astype(F32)
        mid = (g - hi).astype(BF16).astype(F32)
        rows += [hi, mid, g - hi - mid]
    rows.append(jnp.zeros((EXTRA_LANES - len(rows), t), F32))
    he_ref[:, D_MODEL:] = jnp.concatenate(rows, axis=0).T.astype(BF16)

    sel = jnp.where(hots[0] | hots[1] | hots[2] | hots[3], 1.0, 0.0)
    before = _dot(sel.astype(BF16), tri_ref[...])
    ranks = [jnp.sum(jnp.where(hot, before, 0.0), axis=0, keepdims=True) for hot in hots]
    rank_ref[...] = jnp.concatenate(ranks, axis=0).astype(I32)
    cnt_ref[...] = jnp.broadcast_to(jnp.sum(sel, axis=1, keepdims=True), cnt_ref.shape)


def _router(dil_outs, o_b, x2, mod_b, w_out, norm2_g, w_router, b_router, seq):
    n = x2.shape[0]
    t = ROW_TILE
    per_b = seq // t
    wr_t = w_router.T
    wrh = wr_t.astype(BF16)
    wrl = (wr_t - wrh.astype(F32)).astype(BF16)
    tri = jnp.asarray(np.arange(t)[:, None] < np.arange(t)[None, :], BF16)
    row = lambda w: pl.BlockSpec((t, w), lambda i: (i, 0))
    full = lambda a, b: pl.BlockSpec((a, b), lambda i: (0, 0))
    tok = lambda: pl.BlockSpec((TOP_K, t), lambda i: (0, i))
    grouped = lambda dil: pl.BlockSpec((None, dil, t // dil, A_WIDTH), lambda i: (i // per_b, 0, i % per_b, 0))
    n_regrouped = 2 * sum(dil > 1 for _, dil in DIL_PATTERNS)
    return pl.pallas_call(
        _router_kernel,
        grid=(n // t,),
        in_specs=[grouped(dil) for _, dil in DIL_PATTERNS for _ in range(2)] + [
            row(B_WIDTH), row(D_MODEL),
            pl.BlockSpec((None, 6, D_MODEL), lambda i: (i // per_b, 0, 0)),
            full(D_MODEL, D_MODEL), full(1, D_MODEL), full(N_EXPERTS, D_MODEL), full(N_EXPERTS, D_MODEL),
            full(N_EXPERTS, 1), full(t, t),
        ],
        out_specs=[row(D_MODEL), row(ROW_WIDTH), tok(), tok(),
                   pl.BlockSpec((None, N_EXPERTS, 128), lambda i: (i, 0, 0))],
        out_shape=[
            jax.ShapeDtypeStruct((n, D_MODEL), F32),
            jax.ShapeDtypeStruct((n, ROW_WIDTH), BF16),
            jax.ShapeDtypeStruct((TOP_K, n), I32),
            jax.ShapeDtypeStruct((TOP_K, n), I32),
            jax.ShapeDtypeStruct((n // t, N_EXPERTS, 128), F32),
        ],
        scratch_shapes=[pltpu.VMEM((A_WIDTH // LANES, t, LANES), F32)] * n_regrouped,
        compiler_params=_cparams(("parallel",)),
        name="router",
    )(*dil_outs, o_b, x2, mod_b, w_out.astype(BF16), norm2_g.reshape(1, D_MODEL), wrh, wrl,
      b_router.reshape(N_EXPERTS, 1), tri)


def _start_pieces(tables, tile, local_ref, hbm_ref, sem, outbound):
    lo_ref, go_ref, cnt_ref, _ = tables
    for c, size in enumerate(SEG_SIZES):
        cls = tile * len(SEG_SIZES) + c

        def per_piece(k, carry, size=size, cls=cls):
            lo, go = lo_ref[cls * N_EXPERTS + k], go_ref[cls * N_EXPERTS + k]
            loc = local_ref.at[pl.ds(pl.multiple_of(lo, SEG_ALIGN), size)]
            glob = hbm_ref.at[pl.ds(pl.multiple_of(go, SEG_ALIGN), size)]
            cp = pltpu.make_async_copy(loc, glob, sem) if outbound else pltpu.make_async_copy(glob, loc, sem)
            cp.start()
            return carry

        lax.fori_loop(0, cnt_ref[cls], per_piece, 0)


def _start(cp):
    cp.start()


def _wait(cp):
    cp.wait()


def _tile_rows(tables, tile):
    return tables[3][tile]


def _wait_rows(rows, local_ref, hbm_ref, sem, outbound):
    size = 1 << (LOCAL_SLOTS.bit_length() - 1)
    while size >= SEG_ALIGN:
        loc, glob = local_ref.at[pl.ds(0, size)], hbm_ref.at[pl.ds(0, size)]
        cp = pltpu.make_async_copy(loc, glob, sem) if outbound else pltpu.make_async_copy(glob, loc, sem)
        pl.when((rows & size) != 0)(cp.wait)
        size //= 2


def _one_hot_any(j, targets):
    out = jnp.zeros(j.shape, F32)
    for tgt in targets:
        out = jnp.where(j == tgt, 1.0, out)
    return out


def _sort_kernel(lo_ref, go_ref, cnt_ref, rows_ref, tail_ref, he_ref, idx_ref, rank_ref, lcol_ref, ls_ref, xs_hbm,
                 xl, zbuf, sems, zsem):
    tables = (lo_ref, go_ref, cnt_ref, rows_ref)
    i = pl.program_id(0)
    last = pl.num_programs(0) - 1
    slot = i % 2
    t = he_ref.shape[0]

    @pl.when(i == 0)
    def _():
        zbuf[...] = jnp.zeros(zbuf.shape, BF16)

        def tails(action):
            def per_expert(e, carry):
                off, n = tail_ref[e], tail_ref[N_EXPERTS + e]
                done = jnp.int32(0)
                for size in SEG_SIZES:
                    if size < MOE_TILE:
                        take = (n & size) != 0
                        dst = xs_hbm.at[pl.ds(pl.multiple_of(off + done, SEG_ALIGN), size)]
                        pl.when(take)(functools.partial(action, pltpu.make_async_copy(zbuf.at[pl.ds(0, size)], dst, zsem)))
                        done = done + jnp.where(take, size, 0)
                return carry

            lax.fori_loop(0, N_EXPERTS, per_expert, 0)

            def per_block(b, carry):
                dst = xs_hbm.at[pl.ds(pl.multiple_of(b * MOE_TILE, MOE_TILE), MOE_TILE)]
                action(pltpu.make_async_copy(zbuf, dst, zsem))
                return carry

            lax.fori_loop(tail_ref[2 * N_EXPERTS], xs_hbm.shape[0] // MOE_TILE, per_block, 0)

        tails(_start)
        tails(_wait)

    eid = lax.broadcasted_iota(I32, (N_EXPERTS, t), 0)
    lcol = lcol_ref[...]
    ls = []
    for k in range(TOP_K):
        off = jnp.sum(jnp.where(eid == idx_ref[k:k + 1, :], lcol, 0), axis=0, keepdims=True)
        ls.append(off + rank_ref[k:k + 1, :])
    ls_ref[...] = jnp.concatenate(ls, axis=0)

    he = he_ref[...]
    for jc in range(LOCAL_SLOTS // SLOT_CHUNK):
        j = lax.broadcasted_iota(I32, (SLOT_CHUNK, t), 0) + jc * SLOT_CHUNK
        perm = _one_hot_any(j, ls).astype(BF16)
        xl[slot, jc * SLOT_CHUNK:(jc + 1) * SLOT_CHUNK, :] = _dot(perm, he).astype(BF16)

    _start_pieces(tables, i, xl.at[slot], xs_hbm, sems.at[slot], True)

    @pl.when(i > 0)
    def _():
        _wait_rows(_tile_rows(tables, i - 1), xl.at[1 - slot], xs_hbm, sems.at[1 - slot], True)

    @pl.when(i == last)
    def _():
        _wait_rows(_tile_rows(tables, i), xl.at[slot], xs_hbm, sems.at[slot], True)


def _sort(tables, tail, he, idx, rank, lcol, n_rows):
    n = he.shape[0]
    t = ROW_TILE
    tok = lambda: pl.BlockSpec((TOP_K, t), lambda i, *_: (0, i))
    return pl.pallas_call(
        _sort_kernel,
        grid_spec=pltpu.PrefetchScalarGridSpec(
            num_scalar_prefetch=5,
            grid=(n // t,),
            in_specs=[
                pl.BlockSpec((t, ROW_WIDTH), lambda i, *_: (i, 0)),
                tok(), tok(),
                pl.BlockSpec((None, N_EXPERTS, 1), lambda i, *_: (i, 0, 0)),
            ],
            out_specs=[tok(), pl.BlockSpec(memory_space=pl.ANY)],
            scratch_shapes=[pltpu.VMEM((2, LOCAL_SLOTS, ROW_WIDTH), BF16), pltpu.VMEM((MOE_TILE, ROW_WIDTH), BF16),
                            pltpu.SemaphoreType.DMA((2,)), pltpu.SemaphoreType.DMA(())],
        ),
        out_shape=[jax.ShapeDtypeStruct((TOP_K, n), I32), jax.ShapeDtypeStruct((n_rows, ROW_WIDTH), BF16)],
        compiler_params=_cparams(("arbitrary",)),
        name="sort",
    )(*tables, tail, he, idx, rank, lcol)


def _experts_kernel(blk0_ref, nblk_ref, xs_hbm, wgu_ref, bgu_ref, wd_ref, bd_ref, yb_hbm, wgu_sc, wd_sc, xbuf, ybuf,
                    xsem, ysem):
    e = pl.program_id(0)
    tm = MOE_TILE
    first, nb = blk0_ref[e], nblk_ref[e]
    last_e = pl.num_programs(0) - 1
    total = blk0_ref[last_e] + nblk_ref[last_e]

    def rows(b):
        return pl.ds(pl.multiple_of(b * tm, tm), tm)

    def x_copy(b):
        slot = b % X_SLOTS
        return pltpu.make_async_copy(xs_hbm.at[rows(b)], xbuf.at[slot], xsem.at[slot])

    def y_copy(b, slot):
        return pltpu.make_async_copy(ybuf.at[slot], yb_hbm.at[rows(b)], ysem.at[slot])

    @pl.when(e == 0)
    def _():
        for b in range(X_AHEAD):
            pl.when(b < total)(x_copy(b).start)

    def cast(r, carry):
        s = pl.multiple_of(r * LANES, LANES)
        wgu_sc[pl.ds(s, LANES), :] = wgu_ref[pl.ds(s, LANES), :].astype(BF16)
        wd_sc[pl.ds(s, LANES), :] = wd_ref[pl.ds(s, LANES), :].astype(BF16)
        return carry

    lax.fori_loop(0, D_MODEL // LANES, cast, 0)
    me = jnp.full((1, 1), e, I32).astype(F32)

    def block(i, carry):
        b = first + i
        slot = i % 2
        xslot = b % X_SLOTS

        @pl.when(b + X_AHEAD < total)
        def _():
            x_copy(b + X_AHEAD).start()

        x_copy(b).wait()

        @pl.when(i >= 2)
        def _():
            y_copy(b - 2, slot).wait()

        ext = xbuf[xslot, :, D_MODEL:].astype(F32)
        gate = jnp.zeros((tm, 1), F32)
        for k in range(TOP_K):
            c = TOP_K + 3 * k
            gk = ext[:, c:c + 1] + ext[:, c + 1:c + 2] + ext[:, c + 2:c + 3]
            gate = gate + jnp.where(ext[:, k:k + 1] == me, gk, 0.0)

        gu = _dot(xbuf[xslot, :, :D_MODEL], wgu_sc[...]) + bgu_ref[...]
        g = jnp.minimum(gu[:, :D_FF], SWIGLU_LIMIT)
        u = jnp.clip(gu[:, D_FF:], -SWIGLU_LIMIT, SWIGLU_LIMIT)
        act = (u + 1.0) * (g / (1.0 + jnp.exp(-SWIGLU_ALPHA * g)))
        ybuf[slot] = (gate * (_dot(act.astype(BF16), wd_sc[...]) + bd_ref[...])).astype(BF16)
        y_copy(b, slot).start()
        return carry

    lax.fori_loop(0, nb, block, 0)

    @pl.when(nb >= 2)
    def _():
        y_copy(first + nb - 2, nb % 2).wait()

    @pl.when(nb >= 1)
    def _():
        y_copy(first + nb - 1, (nb - 1) % 2).wait()

    @pl.when(e == last_e)
    def _():
        ybuf[0] = jnp.zeros((tm, D_MODEL), BF16)

        def fill(action):
            def per_block(b, carry):
                action(pltpu.make_async_copy(ybuf.at[0], yb_hbm.at[rows(b)], ysem.at[0]))
                return carry

            lax.fori_loop(total, yb_hbm.shape[0] // tm, per_block, 0)

        fill(_start)
        fill(_wait)


def _experts(first_block, n_block, xs, w_gate_up, b_gate_up, w_down, b_down):
    n_rows = xs.shape[0]
    exp3 = lambda e, *_: (e, 0, 0)
    return pl.pallas_call(
        _experts_kernel,
        grid_spec=pltpu.PrefetchScalarGridSpec(
            num_scalar_prefetch=2,
            grid=(N_EXPERTS,),
            in_specs=[
                pl.BlockSpec(memory_space=pl.ANY),
                pl.BlockSpec((None, D_MODEL, 2 * D_FF), exp3),
                pl.BlockSpec((None, 1, 2 * D_FF), exp3),
                pl.BlockSpec((None, D_FF, D_MODEL), exp3),
                pl.BlockSpec((None, 1, D_MODEL), exp3),
            ],
            out_specs=pl.BlockSpec(memory_space=pl.ANY),
            scratch_shapes=[pltpu.VMEM((D_MODEL, 2 * D_FF), BF16), pltpu.VMEM((D_FF, D_MODEL), BF16),
                            pltpu.VMEM((X_SLOTS, MOE_TILE, ROW_WIDTH), BF16), pltpu.VMEM((2, MOE_TILE, D_MODEL), BF16),
                            pltpu.SemaphoreType.DMA((X_SLOTS,)), pltpu.SemaphoreType.DMA((2,))],
        ),
        out_shape=jax.ShapeDtypeStruct((n_rows, D_MODEL), BF16),
        compiler_params=_cparams(("arbitrary",)),
        name="experts",
    )(first_block, n_block, xs, w_gate_up, b_gate_up.reshape(N_EXPERTS, 1, 2 * D_FF), w_down,
      b_down.reshape(N_EXPERTS, 1, D_MODEL))


def _combine_kernel(lo_ref, go_ref, cnt_ref, rows_ref, yb_hbm, lst_ref, x1_ref, mod_ref, o_ref, ybuf, sems):
    i = pl.program_id(0)
    slot = i % 2
    tables = (lo_ref, go_ref, cnt_ref, rows_ref)

    @pl.when(i == 0)
    def _():
        ybuf[...] = jnp.zeros(ybuf.shape, BF16)
        _start_pieces(tables, 0, ybuf.at[0], yb_hbm, sems.at[0], False)

    @pl.when(i + 1 < pl.num_programs(0))
    def _():
        _start_pieces(tables, i + 1, ybuf.at[1 - slot], yb_hbm, sems.at[1 - slot], False)

    _wait_rows(_tile_rows(tables, i), ybuf.at[slot], yb_hbm, sems.at[slot], False)

    lst = lst_ref[...]
    t = lst.shape[0]
    targets = [lst[:, k:k + 1] for k in range(TOP_K)]
    y = jnp.zeros((t, D_MODEL), F32)
    for jc in range(LOCAL_SLOTS // SLOT_CHUNK):
        j = lax.broadcasted_iota(I32, (t, SLOT_CHUNK), 1) + jc * SLOT_CHUNK
        pick = _one_hot_any(j, targets).astype(BF16)
        y = y + _dot(pick, ybuf[slot, jc * SLOT_CHUNK:(jc + 1) * SLOT_CHUNK, :])
    o_ref[...] = x1_ref[...] + mod_ref[5:6, :] * y


def _combine(tables, yb, ls_t, x1, mod_b, seq):
    n = x1.shape[0]
    t = ROW_TILE
    per_b = seq // t
    return pl.pallas_call(
        _combine_kernel,
        grid_spec=pltpu.PrefetchScalarGridSpec(
            num_scalar_prefetch=4,
            grid=(n // t,),
            in_specs=[
                pl.BlockSpec(memory_space=pl.ANY),
                pl.BlockSpec((t, TOP_K), lambda i, *_: (i, 0)),
                pl.BlockSpec((t, D_MODEL), lambda i, *_: (i, 0)),
                pl.BlockSpec((None, 6, D_MODEL), lambda i, *_: (i // per_b, 0, 0)),
            ],
            out_specs=pl.BlockSpec((t, D_MODEL), lambda i, *_: (i, 0)),
            scratch_shapes=[pltpu.VMEM((2, LOCAL_SLOTS, D_MODEL), BF16), pltpu.SemaphoreType.DMA((2,))],
        ),
        out_shape=jax.ShapeDtypeStruct((n, D_MODEL), F32),
        compiler_params=_cparams(("arbitrary",)),
        name="combine",
    )(*tables, yb, ls_t, x1, mod_b)


def _layer(x, c, w_ada, b_ada, norm1_g, w_in, a_q_norm_g, a_k_norm_g, b_q_norm_g, b_k_norm_g, lambda_q1,
           lambda_k1, lambda_q2, lambda_k2, diff_norm_g, w_out, norm2_g, w_router, b_router, w_gate_up,
           b_gate_up, w_down, b_down):
    bsz, seq, _ = x.shape
    n = bsz * seq
    x2 = x.reshape(n, D_MODEL)
    mod_b = _ada(c, w_ada, b_ada).transpose(1, 0, 2)

    qk_gains = jnp.stack([
        jnp.tile(a_q_norm_g, A_HEADS) * (ATTN_SCALE * LOG2E), jnp.tile(a_k_norm_g, A_HEADS),
        jnp.tile(b_k_norm_g, 2 * B_HEADS)])
    bq_gain = (jnp.tile(b_q_norm_g, 2 * B_HEADS) * (ATTN_SCALE * LOG2E)).reshape(B_WIDTH, 1)
    *a_parts, bk, bqt, bvt = _inproj(x2, mod_b, norm1_g, w_in, qk_gains, bq_gain, bsz, seq)

    dil_outs = _dilated(a_parts, bsz, seq)
    lam = (jnp.exp(jnp.sum(lambda_q1 * lambda_k1)) - jnp.exp(jnp.sum(lambda_q2 * lambda_k2)) + LAMBDA_INIT)
    slopes_b = 2.0 ** (-8.0 * np.arange(1, B_HEADS + 1) / B_HEADS)
    scalars = jnp.concatenate([lam.reshape(1), jnp.asarray(slopes_b, F32)]).astype(F32)
    bk4 = bk.reshape(bsz, seq // DIFF_BLOCK, DIFF_BLOCK, B_WIDTH)
    o_b = _diff(bqt, bk4, bvt, scalars, diff_norm_g, bsz, seq).reshape(n, B_WIDTH)

    x1, he, idx, rank, cnt = _router(dil_outs, o_b, x2, mod_b, w_out, norm2_g, w_router, b_router, seq)

    n_tiles = n // ROW_TILE
    counts = cnt[:, :, 0].astype(I32)
    seg = (counts + SEG_ALIGN - 1) // SEG_ALIGN * SEG_ALIGN
    loff = jnp.cumsum(seg, axis=1) - seg
    region = jnp.sum(seg, axis=0)
    padded = (region + MOE_TILE - 1) // MOE_TILE * MOE_TILE
    pad_end = jnp.cumsum(padded)
    pad_start = pad_end - padded
    goff = pad_start[None, :] + jnp.cumsum(seg, axis=0) - seg
    sizes = jnp.asarray(SEG_SIZES, I32)[None, :, None]
    has = (seg[:, None, :] & sizes) != 0
    within = seg[:, None, :] & ~(2 * sizes - 1)
    place = jnp.cumsum(has, axis=-1) - has
    pick = has[:, :, None, :] & (place[:, :, None, :] == jnp.arange(N_EXPERTS, dtype=I32)[None, None, :, None])
    listed = lambda rows_: jnp.sum(jnp.where(pick, rows_[:, :, None, :], 0), axis=-1).reshape(-1).astype(I32)
    tables = (listed(loff[:, None, :] + within), listed(goff[:, None, :] + within),
              jnp.sum(has, axis=-1).reshape(-1).astype(I32), jnp.sum(seg, axis=1).astype(I32))
    n_blocks = (n * TOP_K + n_tiles * N_EXPERTS * (SEG_ALIGN - 1) + N_EXPERTS * (MOE_TILE - 1)) // MOE_TILE
    n_used = (pad_end[-1] // MOE_TILE).astype(I32)
    tail = jnp.concatenate([pad_start + region, padded - region, n_used.reshape(1)])

    ls, xs = _sort(tables, tail, he, idx, rank, loff.reshape(n_tiles, N_EXPERTS, 1), n_blocks * MOE_TILE)
    yb = _experts(pad_start // MOE_TILE, padded // MOE_TILE, xs, w_gate_up, b_gate_up, w_down, b_down)
    out = _combine(tables, yb, ls.T, x1, mod_b, seq)
    return out.reshape(bsz, seq, D_MODEL)


def kernel(x, c, w_ada, b_ada, norm1_g, w_in, a_q_norm_g, a_k_norm_g, b_q_norm_g, b_k_norm_g, lambda_q1, lambda_k1,
           lambda_q2, lambda_k2, diff_norm_g, w_out, norm2_g, w_router, b_router, w_gate_up, b_gate_up, w_down,
           b_down):
    args = (w_ada, b_ada, norm1_g, w_in, a_q_norm_g, a_k_norm_g, b_q_norm_g, b_k_norm_g, lambda_q1, lambda_k1,
            lambda_q2, lambda_k2, diff_norm_g, w_out, norm2_g, w_router, b_router, w_gate_up, b_gate_up, w_down,
            b_down)
    return _layer(x, c, *[a[0] for a in args])
```

```python
import functools

import numpy as np
import jax
import jax.numpy as jnp
from jax import lax
from jax.experimental import pallas as pl
from jax.experimental.pallas import tpu as pltpu

F32 = jnp.float32
BF16 = jnp.bfloat16
I32 = jnp.int32
U32 = jnp.uint32

D_MODEL = 1024
HEAD_DIM = 64
A_WIDTH = 512
B_WIDTH = 512
A_HEADS = 8
B_HEADS = 4
IN_WIDTH = 3072
DIL_PATTERNS = ((128, 1), (512, 4), (2048, 16))
BAND = 128
N_EXPERTS = 32
TOP_K = 4
D_FF = 1024
SWIGLU_LIMIT = 7.0
SWIGLU_ALPHA = 1.702
RMS_EPS = 1e-6
ATTN_SCALE = HEAD_DIM ** -0.5
LOG2E = 1.4426950408889634
LN2 = 0.6931471805599453
LAMBDA_INIT = 0.8 - 0.6 * 1.0

LANES = 128
ROW_TILE = 512
DIFF_BLOCK = 512
DIFF_LANES = 256
DIFF_ROWS = 64
MOE_TILE = 256
X_AHEAD = 4
X_SLOTS = X_AHEAD + 1
EXTRA_LANES = 128
ROW_WIDTH = D_MODEL + EXTRA_LANES
SEG_ALIGN = 16
SEG_SIZES = (512, 256, 128, 64, 32, 16)
SLOT_CHUNK = 512
LOCAL_SLOTS = 2560
LOCAL_PIECES = LOCAL_SLOTS // SEG_ALIGN
assert ROW_TILE == DIFF_BLOCK == SEG_SIZES[0]
assert LOCAL_SLOTS >= ROW_TILE * TOP_K + N_EXPERTS * (SEG_ALIGN - 1) and LOCAL_SLOTS % SLOT_CHUNK == 0
VMEM_LIMIT = 56 * 1024 * 1024


def _cparams(sem, **flags):
    return pltpu.CompilerParams(dimension_semantics=sem, vmem_limit_bytes=VMEM_LIMIT, flags=flags or None)


def _split_bf16(a):
    hi = a.astype(BF16)
    lo = (a - hi.astype(F32)).astype(BF16)
    return hi, lo


def _dot_nt(a, b):
    return lax.dot_general(a, b, (((1,), (1,)), ((), ())), preferred_element_type=F32)


def _dot(a, b):
    return jnp.dot(a, b, preferred_element_type=F32)


def _ada_kernel(c_ref, w_ref, b_ref, o_ref):
    c = c_ref[...]
    s = c / (1.0 + jnp.exp(-c))
    sh, sl = _split_bf16(s)
    wh, wl = _split_bf16(w_ref[...])
    o_ref[0] = _dot(sh, wh) + _dot(sh, wl) + _dot(sl, wh) + b_ref[0]


def _ada(c, w_ada, b_ada):
    bsz = c.shape[0]
    return pl.pallas_call(
        _ada_kernel,
        grid=(6,),
        in_specs=[
            pl.BlockSpec((bsz, D_MODEL), lambda j: (0, 0)),
            pl.BlockSpec((D_MODEL, D_MODEL), lambda j: (0, j)),
            pl.BlockSpec((1, 1, D_MODEL), lambda j: (j, 0, 0)),
        ],
        out_specs=pl.BlockSpec((1, bsz, D_MODEL), lambda j: (j, 0, 0)),
        out_shape=jax.ShapeDtypeStruct((6, bsz, D_MODEL), F32),
        compiler_params=_cparams(("arbitrary",)),
        name="ada",
    )(c, w_ada, b_ada.reshape(6, 1, D_MODEL))


def _inproj_kernel(x_ref, mod_ref, g1_ref, wn_ref, wt_ref, gm_ref, qkg_ref, bqg_ref, a1_ref, a4_ref, a16_ref, bk_ref,
                   bqt_ref, bvt_ref, a_sc):
    a_refs = (a1_ref, a4_ref, a16_ref)
    x = x_ref[...]
    y = x * lax.rsqrt(jnp.mean(x * x, axis=-1, keepdims=True) + RMS_EPS) * g1_ref[...]
    h = (y * (1.0 + mod_ref[1:2, :]) + mod_ref[0:1, :]).astype(BF16)
    p = _dot(h, wn_ref[...])
    gm = gm_ref[...]

    def head_norm(t, g):
        ss = _dot((t * t).astype(BF16), gm)
        return t * lax.rsqrt(ss * (1.0 / HEAD_DIM) + RMS_EPS) * g

    w = A_WIDTH
    a_part = jnp.concatenate([head_norm(p[:, 0 * w:1 * w], qkg_ref[0:1, :]),
                              head_norm(p[:, 1 * w:2 * w], qkg_ref[1:2, :]), p[:, 2 * w:3 * w]], axis=1)
    n_col = a_sc.shape[0]
    for c in range(n_col):
        a_sc[c] = a_part[:, c * LANES:(c + 1) * LANES]
    for a_ref, (_, dil) in zip(a_refs, DIL_PATTERNS):
        if dil == 1:
            a_ref[0] = a_part.astype(BF16)
            continue
        rows = a_part.shape[0] // dil
        for r in range(dil):
            for c in range(n_col):
                a_ref[r, :, c * LANES:(c + 1) * LANES] = a_sc[c, pl.ds(r, rows, stride=dil), :].astype(BF16)
    bk_ref[...] = head_norm(p[:, 3 * w:4 * w], qkg_ref[2:3, :]).astype(BF16)

    pt = _dot_nt(wt_ref[...], h)
    t = pt.shape[1]
    bq = pt[:w].reshape(w // HEAD_DIM, HEAD_DIM, t)
    ss = jnp.sum(bq * bq, axis=1, keepdims=True)
    bq = (bq * lax.rsqrt(ss * (1.0 / HEAD_DIM) + RMS_EPS)).reshape(w, t) * bqg_ref[...]
    bqt_ref[...] = bq.astype(BF16)
    bvt_ref[...] = pt[w:].astype(BF16)


def _inproj(x2, mod_b, norm1_g, w_in, qk_gains, bq_gain, bsz, seq):
    n = x2.shape[0]
    t = ROW_TILE
    per_b = seq // t
    w = A_WIDTH
    head_of_lane = np.arange(w) // HEAD_DIM
    gmat = jnp.asarray(head_of_lane[:, None] == head_of_lane[None, :], BF16)
    w_bf = w_in.astype(BF16)
    w_nat = jnp.concatenate([w_bf[:, :3 * w], w_bf[:, 4 * w:5 * w]], axis=1)
    w_tr = jnp.concatenate([w_bf[:, 3 * w:4 * w], w_bf[:, 5 * w:]], axis=1).T
    tr_spec = pl.BlockSpec((None, None, w, t), lambda i: (i // per_b, i % per_b, 0, 0))
    tr_shape = jax.ShapeDtypeStruct((bsz, per_b, w, t), BF16)
    a_specs = [pl.BlockSpec((None, dil, t // dil, 3 * w), lambda i: (i // per_b, 0, i % per_b, 0))
               for _, dil in DIL_PATTERNS]
    a_shapes = [jax.ShapeDtypeStruct((bsz, dil, seq // dil, 3 * w), BF16) for _, dil in DIL_PATTERNS]
    return pl.pallas_call(
        _inproj_kernel,
        grid=(n // t,),
        in_specs=[
            pl.BlockSpec((t, D_MODEL), lambda i: (i, 0)),
            pl.BlockSpec((None, 6, D_MODEL), lambda i: (i // per_b, 0, 0)),
            pl.BlockSpec((1, D_MODEL), lambda i: (0, 0)),
            pl.BlockSpec((D_MODEL, 4 * w), lambda i: (0, 0)),
            pl.BlockSpec((2 * w, D_MODEL), lambda i: (0, 0)),
            pl.BlockSpec((w, w), lambda i: (0, 0)),
            pl.BlockSpec((3, w), lambda i: (0, 0)),
            pl.BlockSpec((w, 1), lambda i: (0, 0)),
        ],
        out_specs=a_specs + [pl.BlockSpec((t, w), lambda i: (i, 0)), tr_spec, tr_spec],
        out_shape=a_shapes + [jax.ShapeDtypeStruct((n, w), BF16), tr_shape, tr_shape],
        scratch_shapes=[pltpu.VMEM((3 * w // LANES, t, LANES), F32)],
        compiler_params=_cparams(("parallel",)),
        name="inproj",
    )(x2, mod_b, norm1_g.reshape(1, D_MODEL), w_nat, w_tr, gmat, qk_gains, bq_gain)


def _dilated_bias_table():
    ik = np.arange(2 * BAND)[:, None]
    iq = np.arange(BAND)[None, :]
    delta = iq - ik + BAND
    in_band = (delta >= 0) & (delta <= BAND)
    slopes = 2.0 ** (-8.0 * np.arange(1, A_HEADS + 1) / A_HEADS)
    tbl = np.zeros((len(DIL_PATTERNS), 2, A_HEADS // 2, 2 * BAND, 2 * BAND), np.float32)
    for p, (_, dil) in enumerate(DIL_PATTERNS):
        for first in range(2):
            valid = in_band & ((ik >= BAND) if first else True)
            for h in range(A_HEADS):
                cols = slice(BAND * (h % 2), BAND * (h % 2 + 1))
                tbl[p, first, h // 2, :, cols] = np.where(valid, -slopes[h] * LOG2E * (delta * dil), -np.inf)
    return jnp.asarray(tbl)


def _dilated_kernel(tbl_ref, *refs, n_blocks):
    lanes = 2 * HEAD_DIM
    n = pl.program_id(1)
    n_pat = len(DIL_PATTERNS)
    s_sc, p_sc = refs[len(refs) - 2:]
    outs = refs[len(refs) - 2 - 2 * n_pat:len(refs) - 2]
    lane = lax.broadcasted_iota(I32, (BAND, lanes), 1)
    low = lane < HEAD_DIM
    work = []
    pos = 0
    for p, (_, dil) in enumerate(DIL_PATTERNS):
        per_res = n_blocks // dil
        if per_res > 1:
            q_ref, kp_ref, kc_ref, vp_ref, vc_ref = refs[pos:pos + 5]
            pos += 5
            first = (n % per_res == 0).astype(I32)
        else:
            q_ref, kc_ref, vc_ref = refs[pos:pos + 3]
            kp_ref = vp_ref = first = None
            pos += 3
        for j in range(A_HEADS // 2):
            work.append((p, j, (q_ref, kp_ref, kc_ref, vp_ref, vc_ref), first))

    def keys_of(prev_ref, cur_ref, cs):
        return cur_ref[:, cs] if prev_ref is None else jnp.concatenate([prev_ref[:, cs], cur_ref[:, cs]], axis=0)

    for w, (p, j, (q_ref, kp_ref, kc_ref, _, _), first) in enumerate(work):
        cs = slice(lanes * j, lanes * (j + 1))
        q2 = q_ref[:, cs]
        zero = jnp.zeros_like(q2)
        qcat = jnp.concatenate([jnp.where(low, q2, zero), jnp.where(low, zero, q2)], axis=0)
        k2 = keys_of(kp_ref, kc_ref, cs)
        bias = tbl_ref[p, 1, j, BAND:, :] if kp_ref is None else tbl_ref[p, first, j]
        s_sc[w, :k2.shape[0], :] = _dot_nt(k2, qcat) + bias
    lses = []
    for w, (p, j, (_, kp_ref, _, _, _), _) in enumerate(work):
        nk = BAND if kp_ref is None else 2 * BAND
        s = s_sc[w, :nk, :]
        m = jnp.max(s, axis=0, keepdims=True)
        e = jnp.exp2(s - m)
        den = jnp.sum(e, axis=0, keepdims=True)
        p_sc[w, :nk, :] = (e * (1.0 / den)).astype(BF16)
        lses.append((m + jnp.log2(den)) * LN2)
    for w, (p, j, (_, kp_ref, _, vp_ref, vc_ref), _) in enumerate(work):
        nk = BAND if kp_ref is None else 2 * BAND
        cs = slice(lanes * j, lanes * (j + 1))
        v2 = keys_of(vp_ref, vc_ref, cs)
        r = lax.dot_general(p_sc[w, :nk, :], v2, (((0,), (0,)), ((), ())), preferred_element_type=F32)
        outs[2 * p][:, cs] = jnp.where(low, r[:BAND], r[BAND:]).astype(BF16)
    for p in range(n_pat):
        rows = []
        for lse in lses[p * (A_HEADS // 2):(p + 1) * (A_HEADS // 2)]:
            rows += [lse[:, :BAND], lse[:, BAND:]]
        rows.append(jnp.zeros((lanes - A_HEADS, BAND), F32))
        outs[2 * p + 1][...] = jnp.concatenate(rows, axis=0).T


def _dilated(a_parts, bsz, seq):
    n_blocks = seq // BAND
    in_specs = [pl.BlockSpec((len(DIL_PATTERNS), 2, A_HEADS // 2, 2 * BAND, 2 * BAND), lambda b, n: (0, 0, 0, 0, 0))]
    n_work = len(DIL_PATTERNS) * A_HEADS // 2
    args = [_dilated_bias_table()]
    out_specs, out_shapes = [], []
    blk = (None, None, BAND, A_WIDTH)
    for a_part, (_, dil) in zip(a_parts, DIL_PATTERNS):
        per_res = n_blocks // dil

        def cur(col, per_res=per_res):
            return lambda b, n: (b, n // per_res, n % per_res, col)

        def prev(col, per_res=per_res):
            return lambda b, n: (b, n // per_res, jnp.maximum(n % per_res - 1, 0), col)

        if per_res > 1:
            in_specs += [pl.BlockSpec(blk, cur(0)), pl.BlockSpec(blk, prev(1)), pl.BlockSpec(blk, cur(1)),
                         pl.BlockSpec(blk, prev(2)), pl.BlockSpec(blk, cur(2))]
            args += [a_part] * 5
        else:
            in_specs += [pl.BlockSpec(blk, cur(0)), pl.BlockSpec(blk, cur(1)), pl.BlockSpec(blk, cur(2))]
            args += [a_part] * 3
        out_specs += [pl.BlockSpec(blk, cur(0)), pl.BlockSpec((None, None, BAND, LANES), cur(0))]
        out_shapes += [jax.ShapeDtypeStruct((bsz, dil, seq // dil, A_WIDTH), BF16),
                       jax.ShapeDtypeStruct((bsz, dil, seq // dil, LANES), F32)]
    return pl.pallas_call(
        functools.partial(_dilated_kernel, n_blocks=n_blocks),
        grid=(bsz, n_blocks),
        in_specs=in_specs,
        out_specs=out_specs,
        out_shape=out_shapes,
        scratch_shapes=[pltpu.VMEM((n_work, 2 * BAND, 2 * BAND), F32), pltpu.VMEM((n_work, 2 * BAND, 2 * BAND), BF16)],
        compiler_params=_cparams(("parallel", "parallel")),
        name="dilated",
    )(*args)


def _diff_bias_table():
    t = DIFF_BLOCK
    rel = np.arange(t)[None, :] - np.arange(t)[:, None]
    slopes = 2.0 ** (-8.0 * np.arange(1, B_HEADS + 1) / B_HEADS)
    tbl = np.zeros((B_HEADS, 2, t, t), np.float32)
    for h in range(B_HEADS):
        tbl[h, 0] = -slopes[h] * LOG2E * rel
        tbl[h, 1] = np.where(rel >= 0, -slopes[h] * LOG2E * rel, -np.inf)
    return jnp.asarray(tbl)


def _diff_kernel(sc_ref, qt_ref, k_ref, vt_ref, g_ref, bias_sc, o_ref, qst_sc, m_sc, l_sc, a_sc, acc_sc, s_sc, p_sc):
    t = DIFF_BLOCK
    lanes = 2 * HEAD_DIM
    qi = pl.program_id(1)
    lam = sc_ref[0]
    slopes = [jnp.full((1, 1), sc_ref[1 + h], F32) for h in range(B_HEADS)]
    row = lax.broadcasted_iota(I32, (lanes, t), 0)
    for h in range(B_HEADS):
        qt = qt_ref[h * lanes:(h + 1) * lanes, :]
        zero = jnp.zeros_like(qt)
        qst_sc[h, :, :t] = jnp.where(row < HEAD_DIM, qt, zero)
        qst_sc[h, :, t:] = jnp.where(row < HEAD_DIM, zero, qt)
    m_sc[...] = jnp.full(m_sc.shape, -jnp.inf, F32)
    l_sc[...] = jnp.zeros(l_sc.shape, F32)
    acc_sc[...] = jnp.zeros(acc_sc.shape, F32)

    n_grp = 2 * t // DIFF_LANES
    items = [(h, g) for h in range(B_HEADS) for g in range(n_grp)]
    chunks = [slice(r * DIFF_ROWS, (r + 1) * DIFF_ROWS) for r in range(t // DIFF_ROWS)]

    def block(n, diag):
        cs = [-slopes[h] * LOG2E * (jnp.full((1, 1), (qi - n) * t, I32)).astype(F32) for h in range(B_HEADS)]
        for w, (h, g) in enumerate(items):
            cols = slice(g * DIFF_LANES, (g + 1) * DIFF_LANES)
            bcols = slice((g * DIFF_LANES) % t, (g * DIFF_LANES) % t + DIFF_LANES)
            s_sc[w] = _dot(k_ref[n, :, h * lanes:(h + 1) * lanes], qst_sc[h, :, cols]) + bias_sc[h, diag, :, bcols]
        for w, (h, g) in enumerate(items):
            cols = slice(g * DIFF_LANES, (g + 1) * DIFF_LANES)
            top = s_sc[w, chunks[0], :]
            for rows in chunks[1:]:
                top = jnp.maximum(top, s_sc[w, rows, :])
            m_prev = m_sc[h, :, cols]
            m_new = jnp.maximum(m_prev, jnp.max(top, axis=0, keepdims=True) + cs[h])
            alpha = jnp.exp2(m_prev - m_new)
            shift = m_new - cs[h]
            part = jnp.zeros(top.shape, F32)
            for rows in chunks:
                e = jnp.exp2(s_sc[w, rows, :] - shift)
                p_sc[w, rows, :] = e.astype(BF16)
                part = part + e
            l_sc[h, :, cols] = alpha * l_sc[h, :, cols] + jnp.sum(part, axis=0, keepdims=True)
            m_sc[h, :, cols] = m_new
            a_sc[h, :, cols] = alpha
        for w, (h, g) in enumerate(items):
            cols = slice(g * DIFF_LANES, (g + 1) * DIFF_LANES)
            pv = _dot(vt_ref[n, h * lanes:(h + 1) * lanes, :], p_sc[w])
            acc_sc[h, :, cols] = a_sc[h, :, cols] * acc_sc[h, :, cols] + pv

    def body(n, carry):
        block(n, 0)
        return carry

    lax.fori_loop(0, qi, body, 0)
    block(qi, 1)
    for h in range(B_HEADS):
        o = acc_sc[h] / l_sc[h]
        o = o[:, :t] - lam * o[:, t:]
        o = o * lax.rsqrt(jnp.mean(o * o, axis=0, keepdims=True) + RMS_EPS) * g_ref[...]
        o_ref[:, h * lanes:(h + 1) * lanes] = (o * (1.0 - LAMBDA_INIT)).T.astype(BF16)


def _diff(bqt, bk, bvt, scalars, diff_norm_g, bsz, seq):
    t = DIFF_BLOCK
    nb = seq // t
    lanes = 2 * HEAD_DIM
    n_items = B_HEADS * 2 * t // DIFF_LANES
    return pl.pallas_call(
        _diff_kernel,
        grid=(bsz, nb),
        in_specs=[
            pl.BlockSpec(memory_space=pltpu.SMEM),
            pl.BlockSpec((None, None, B_WIDTH, t), lambda b, i: (b, i, 0, 0)),
            pl.BlockSpec((None, nb, t, B_WIDTH), lambda b, i: (b, 0, 0, 0)),
            pl.BlockSpec((None, nb, B_WIDTH, t), lambda b, i: (b, 0, 0, 0)),
            pl.BlockSpec((lanes, 1), lambda b, i: (0, 0)),
            pl.BlockSpec((B_HEADS, 2, t, t), lambda b, i: (0, 0, 0, 0)),
        ],
        out_specs=pl.BlockSpec((None, t, B_WIDTH), lambda b, i: (b, i, 0)),
        out_shape=jax.ShapeDtypeStruct((bsz, seq, B_WIDTH), BF16),
        scratch_shapes=[pltpu.VMEM((B_HEADS, lanes, 2 * t), BF16)] + [pltpu.VMEM((B_HEADS, 1, 2 * t), F32)] * 3 + [
            pltpu.VMEM((B_HEADS, lanes, 2 * t), F32),
            pltpu.VMEM((n_items, t, DIFF_LANES), F32), pltpu.VMEM((n_items, t, DIFF_LANES), BF16)],
        compiler_params=_cparams(("parallel", "parallel")),
        name="diff",
    )(scalars, bqt, bk, bvt, diff_norm_g.reshape(lanes, 1), _diff_bias_table())


def _router_kernel(o0, l0, o1, l1, o2, l2, ob_ref, x_ref, mod_ref, wout_ref, g2_ref, wrh_ref, wrl_ref, br_ref,
                   tri_ref, spread_ref, x1_ref, he_ref, idx_ref, rank_ref, cnt_ref, *order_sc):
    def token_order(ref, scratch):
        dil, rows, width = ref.shape
        if dil == 1:
            return ref[0].astype(F32)
        n_col = width // LANES
        for r in range(dil):
            for c in range(n_col):
                scratch[c, pl.ds(r, rows, stride=dil), :] = ref[r, :, c * LANES:(c + 1) * LANES].astype(F32)
        return jnp.concatenate([scratch[c] for c in range(n_col)], axis=1)

    os_ = [token_order(o0, None), token_order(o1, order_sc[0]), token_order(o2, order_sc[2])]
    ls = [token_order(l0, None), token_order(l1, order_sc[1]), token_order(l2, order_sc[3])]
    mx = jnp.maximum(jnp.maximum(ls[0], ls[1]), ls[2])
    ws = [jnp.exp(l - mx) for l in ls]
    den = ws[0] + ws[1] + ws[2]
    spread = spread_ref[...]
    oa = jnp.zeros(os_[0].shape, F32)
    for w, o in zip(ws, os_):
        wh, wl = _split_bf16(w / den)
        oa = oa + (_dot(wh, spread) + _dot(wl, spread)) * o
    mixed = _dot(oa.astype(BF16), wout_ref[:A_WIDTH, :]) + _dot(ob_ref[...], wout_ref[A_WIDTH:, :])
    x1 = x_ref[...] + mod_ref[2:3, :] * mixed
    x1_ref[...] = x1
    y = x1 * lax.rsqrt(jnp.mean(x1 * x1, axis=-1, keepdims=True) + RMS_EPS) * g2_ref[...]
    h2 = y * (1.0 + mod_ref[4:5, :]) + mod_ref[3:4, :]
    he_ref[:, :D_MODEL] = h2.astype(BF16)

    hh, hl = _split_bf16(h2)
    wrh = wrh_ref[...]
    logits = _dot_nt(wrh, hh) + _dot_nt(wrh, hl) + _dot_nt(wrl_ref[...], hh) + br_ref[...]
    t = logits.shape[1]
    eid = lax.broadcasted_iota(I32, (N_EXPERTS, t), 0)
    vals, idxs, hots = [], [], []
    cur = logits
    for _ in range(TOP_K):
        v = jnp.max(cur, axis=0, keepdims=True)
        ik = jnp.min(jnp.where(cur == v, eid, N_EXPERTS), axis=0, keepdims=True)
        hot = eid == ik
        vals.append(v)
        idxs.append(ik)
        hots.append(hot)
        cur = jnp.where(hot, -jnp.inf, cur)
    es = [jnp.exp(v - vals[0]) for v in vals]
    esum = es[0] + es[1] + es[2] + es[3]
    idx_ref[...] = jnp.concatenate(idxs, axis=0)

    rows = [ik.astype(F32) for ik in idxs]
    for e in es:
        g = e / esum
        hi = g.astype(BF16).astype(F32)
        mid = (g - hi).astype(BF16).astype(F32)
        rows += [hi, mid, g - hi - mid]
    rows.append(jnp.zeros((EXTRA_LANES - len(rows), t), F32))
    he_ref[:, D_MODEL:] = jnp.concatenate(rows, axis=0).T.astype(BF16)

    sel = jnp.where(hots[0] | hots[1] | hots[2] | hots[3], 1.0, 0.0)
    before = _dot(sel.astype(BF16), tri_ref[...])
    ranks = [jnp.sum(jnp.where(hot, before, 0.0), axis=0, keepdims=True) for hot in hots]
    rank_ref[...] = jnp.concatenate(ranks, axis=0).astype(I32)
    cnt_ref[...] = jnp.broadcast_to(jnp.sum(sel, axis=1, keepdims=True), cnt_ref.shape)


def _router(dil_outs, o_b, x2, mod_b, w_out, norm2_g, w_router, b_router, seq):
    n = x2.shape[0]
    t = ROW_TILE
    per_b = seq // t
    wr_t = w_router.T
    wrh = wr_t.astype(BF16)
    wrl = (wr_t - wrh.astype(F32)).astype(BF16)
    tri = jnp.asarray(np.arange(t)[:, None] < np.arange(t)[None, :], BF16)
    row = lambda w: pl.BlockSpec((t, w), lambda i: (i, 0))
    full = lambda a, b: pl.BlockSpec((a, b), lambda i: (0, 0))
    tok = lambda: pl.BlockSpec((TOP_K, t), lambda i: (0, i))
    grouped = lambda dil, w: pl.BlockSpec((None, dil, t // dil, w), lambda i: (i // per_b, 0, i % per_b, 0))
    widths = (A_WIDTH, LANES)
    head_of_lane = np.arange(A_WIDTH) // HEAD_DIM
    spread = jnp.asarray(np.arange(LANES)[:, None] == head_of_lane[None, :], BF16)
    return pl.pallas_call(
        _router_kernel,
        grid=(n // t,),
        in_specs=[grouped(dil, w) for _, dil in DIL_PATTERNS for w in widths] + [
            row(B_WIDTH), row(D_MODEL),
            pl.BlockSpec((None, 6, D_MODEL), lambda i: (i // per_b, 0, 0)),
            full(D_MODEL, D_MODEL), full(1, D_MODEL), full(N_EXPERTS, D_MODEL), full(N_EXPERTS, D_MODEL),
            full(N_EXPERTS, 1), full(t, t), full(LANES, A_WIDTH),
        ],
        out_specs=[row(D_MODEL), row(ROW_WIDTH), tok(), tok(),
                   pl.BlockSpec((None, N_EXPERTS, 128), lambda i: (i, 0, 0))],
        out_shape=[
            jax.ShapeDtypeStruct((n, D_MODEL), F32),
            jax.ShapeDtypeStruct((n, ROW_WIDTH), BF16),
            jax.ShapeDtypeStruct((TOP_K, n), I32),
            jax.ShapeDtypeStruct((TOP_K, n), I32),
            jax.ShapeDtypeStruct((n // t, N_EXPERTS, 128), F32),
        ],
        scratch_shapes=[pltpu.VMEM((w // LANES, t, LANES), F32) for _, dil in DIL_PATTERNS if dil > 1 for w in widths],
        compiler_params=_cparams(("parallel",)),
        name="router",
    )(*dil_outs, o_b, x2, mod_b, w_out.astype(BF16), norm2_g.reshape(1, D_MODEL), wrh, wrl,
      b_router.reshape(N_EXPERTS, 1), tri, spread)


def _start_pieces(tables, tile, local_ref, hbm_ref, sem, outbound):
    lo_ref, go_ref, cnt_ref, _ = tables
    for c, size in enumerate(SEG_SIZES):
        cls = tile * len(SEG_SIZES) + c

        def per_piece(k, carry, size=size, cls=cls):
            lo, go = lo_ref[cls * N_EXPERTS + k], go_ref[cls * N_EXPERTS + k]
            loc = local_ref.at[pl.ds(pl.multiple_of(lo, SEG_ALIGN), size)]
            glob = hbm_ref.at[pl.ds(pl.multiple_of(go, SEG_ALIGN), size)]
            cp = pltpu.make_async_copy(loc, glob, sem) if outbound else pltpu.make_async_copy(glob, loc, sem)
            cp.start()
            return carry

        lax.fori_loop(0, cnt_ref[cls], per_piece, 0)


def _start(cp):
    cp.start()


def _wait(cp):
    cp.wait()


def _tile_rows(tables, tile):
    return tables[3][tile]


def _wait_rows(rows, local_ref, hbm_ref, sem, outbound):
    size = 1 << (LOCAL_SLOTS.bit_length() - 1)
    while size >= SEG_ALIGN:
        loc, glob = local_ref.at[pl.ds(0, size)], hbm_ref.at[pl.ds(0, size)]
        cp = pltpu.make_async_copy(loc, glob, sem) if outbound else pltpu.make_async_copy(glob, loc, sem)
        pl.when((rows & size) != 0)(cp.wait)
        size //= 2


def _one_hot_any(j, targets):
    out = jnp.zeros(j.shape, F32)
    for tgt in targets:
        out = jnp.where(j == tgt, 1.0, out)
    return out


def _sort_kernel(lo_ref, go_ref, cnt_ref, rows_ref, tail_ref, he_ref, idx_ref, rank_ref, lcol_ref, ls_ref, xs_hbm,
                 xl, zbuf, sems, zsem):
    tables = (lo_ref, go_ref, cnt_ref, rows_ref)
    i = pl.program_id(0)
    last = pl.num_programs(0) - 1
    slot = i % 2
    t = he_ref.shape[0]

    @pl.when(i == 0)
    def _():
        zbuf[...] = jnp.zeros(zbuf.shape, BF16)

        def tails(action):
            def per_expert(e, carry):
                off, n = tail_ref[e], tail_ref[N_EXPERTS + e]
                done = jnp.int32(0)
                for size in SEG_SIZES:
                    if size < MOE_TILE:
                        take = (n & size) != 0
                        dst = xs_hbm.at[pl.ds(pl.multiple_of(off + done, SEG_ALIGN), size)]
                        pl.when(take)(functools.partial(action, pltpu.make_async_copy(zbuf.at[pl.ds(0, size)], dst, zsem)))
                        done = done + jnp.where(take, size, 0)
                return carry

            lax.fori_loop(0, N_EXPERTS, per_expert, 0)

            def per_block(b, carry):
                dst = xs_hbm.at[pl.ds(pl.multiple_of(b * MOE_TILE, MOE_TILE), MOE_TILE)]
                action(pltpu.make_async_copy(zbuf, dst, zsem))
                return carry

            lax.fori_loop(tail_ref[2 * N_EXPERTS], xs_hbm.shape[0] // MOE_TILE, per_block, 0)

        tails(_start)
        tails(_wait)

    eid = lax.broadcasted_iota(I32, (N_EXPERTS, t), 0)
    lcol = lcol_ref[...]
    ls = []
    for k in range(TOP_K):
        off = jnp.sum(jnp.where(eid == idx_ref[k:k + 1, :], lcol, 0), axis=0, keepdims=True)
        ls.append(off + rank_ref[k:k + 1, :])
    ls_ref[...] = jnp.concatenate(ls, axis=0)

    he = he_ref[...]
    for jc in range(LOCAL_SLOTS // SLOT_CHUNK):
        j = lax.broadcasted_iota(I32, (SLOT_CHUNK, t), 0) + jc * SLOT_CHUNK
        perm = _one_hot_any(j, ls).astype(BF16)
        xl[slot, jc * SLOT_CHUNK:(jc + 1) * SLOT_CHUNK, :] = _dot(perm, he).astype(BF16)

    _start_pieces(tables, i, xl.at[slot], xs_hbm, sems.at[slot], True)

    @pl.when(i > 0)
    def _():
        _wait_rows(_tile_rows(tables, i - 1), xl.at[1 - slot], xs_hbm, sems.at[1 - slot], True)

    @pl.when(i == last)
    def _():
        _wait_rows(_tile_rows(tables, i), xl.at[slot], xs_hbm, sems.at[slot], True)


def _sort(tables, tail, he, idx, rank, lcol, n_rows):
    n = he.shape[0]
    t = ROW_TILE
    tok = lambda: pl.BlockSpec((TOP_K, t), lambda i, *_: (0, i))
    return pl.pallas_call(
        _sort_kernel,
        grid_spec=pltpu.PrefetchScalarGridSpec(
            num_scalar_prefetch=5,
            grid=(n // t,),
            in_specs=[
                pl.BlockSpec((t, ROW_WIDTH), lambda i, *_: (i, 0)),
                tok(), tok(),
                pl.BlockSpec((None, N_EXPERTS, 1), lambda i, *_: (i, 0, 0)),
            ],
            out_specs=[tok(), pl.BlockSpec(memory_space=pl.ANY)],
            scratch_shapes=[pltpu.VMEM((2, LOCAL_SLOTS, ROW_WIDTH), BF16), pltpu.VMEM((MOE_TILE, ROW_WIDTH), BF16),
                            pltpu.SemaphoreType.DMA((2,)), pltpu.SemaphoreType.DMA(())],
        ),
        out_shape=[jax.ShapeDtypeStruct((TOP_K, n), I32), jax.ShapeDtypeStruct((n_rows, ROW_WIDTH), BF16)],
        compiler_params=_cparams(("arbitrary",)),
        name="sort",
    )(*tables, tail, he, idx, rank, lcol)


def _experts_kernel(blk0_ref, nblk_ref, xs_hbm, wgu_ref, bgu_ref, wd_ref, bd_ref, yb_hbm, wgu_sc, wd_sc, xbuf, ybuf,
                    xsem, ysem):
    e = pl.program_id(0)
    tm = MOE_TILE
    first, nb = blk0_ref[e], nblk_ref[e]
    last_e = pl.num_programs(0) - 1
    total = blk0_ref[last_e] + nblk_ref[last_e]

    def rows(b):
        return pl.ds(pl.multiple_of(b * tm, tm), tm)

    def x_copy(b):
        slot = b % X_SLOTS
        return pltpu.make_async_copy(xs_hbm.at[rows(b)], xbuf.at[slot], xsem.at[slot])

    def y_copy(b, slot):
        return pltpu.make_async_copy(ybuf.at[slot], yb_hbm.at[rows(b)], ysem.at[slot])

    @pl.when(e == 0)
    def _():
        for b in range(X_AHEAD):
            pl.when(b < total)(x_copy(b).start)

    def cast(r, carry):
        s = pl.multiple_of(r * LANES, LANES)
        wgu_sc[pl.ds(s, LANES), :] = wgu_ref[pl.ds(s, LANES), :].astype(BF16)
        wd_sc[pl.ds(s, LANES), :] = wd_ref[pl.ds(s, LANES), :].astype(BF16)
        return carry

    lax.fori_loop(0, D_MODEL // LANES, cast, 0)
    me = jnp.full((1, 1), e, I32).astype(F32)

    def block(i, carry):
        b = first + i
        slot = i % 2
        xslot = b % X_SLOTS

        @pl.when(b + X_AHEAD < total)
        def _():
            x_copy(b + X_AHEAD).start()

        x_copy(b).wait()

        @pl.when(i >= 2)
        def _():
            y_copy(b - 2, slot).wait()

        ext = xbuf[xslot, :, D_MODEL:].astype(F32)
        gate = jnp.zeros((tm, 1), F32)
        for k in range(TOP_K):
            c = TOP_K + 3 * k
            gk = ext[:, c:c + 1] + ext[:, c + 1:c + 2] + ext[:, c + 2:c + 3]
            gate = gate + jnp.where(ext[:, k:k + 1] == me, gk, 0.0)

        gu = _dot(xbuf[xslot, :, :D_MODEL], wgu_sc[...]) + bgu_ref[...]
        g = jnp.minimum(gu[:, :D_FF], SWIGLU_LIMIT)
        u = jnp.clip(gu[:, D_FF:], -SWIGLU_LIMIT, SWIGLU_LIMIT)
        act = (u + 1.0) * (g / (1.0 + jnp.exp(-SWIGLU_ALPHA * g)))
        ybuf[slot] = (gate * (_dot(act.astype(BF16), wd_sc[...]) + bd_ref[...])).astype(BF16)
        y_copy(b, slot).start()
        return carry

    lax.fori_loop(0, nb, block, 0)

    @pl.when(nb >= 2)
    def _():
        y_copy(first + nb - 2, nb % 2).wait()

    @pl.when(nb >= 1)
    def _():
        y_copy(first + nb - 1, (nb - 1) % 2).wait()

    @pl.when(e == last_e)
    def _():
        ybuf[0] = jnp.zeros((tm, D_MODEL), BF16)

        def fill(action):
            def per_block(b, carry):
                action(pltpu.make_async_copy(ybuf.at[0], yb_hbm.at[rows(b)], ysem.at[0]))
                return carry

            lax.fori_loop(total, yb_hbm.shape[0] // tm, per_block, 0)

        fill(_start)
        fill(_wait)


def _experts(first_block, n_block, xs, w_gate_up, b_gate_up, w_down, b_down):
    n_rows = xs.shape[0]
    exp3 = lambda e, *_: (e, 0, 0)
    return pl.pallas_call(
        _experts_kernel,
        grid_spec=pltpu.PrefetchScalarGridSpec(
            num_scalar_prefetch=2,
            grid=(N_EXPERTS,),
            in_specs=[
                pl.BlockSpec(memory_space=pl.ANY),
                pl.BlockSpec((None, D_MODEL, 2 * D_FF), exp3),
                pl.BlockSpec((None, 1, 2 * D_FF), exp3),
                pl.BlockSpec((None, D_FF, D_MODEL), exp3),
                pl.BlockSpec((None, 1, D_MODEL), exp3),
            ],
            out_specs=pl.BlockSpec(memory_space=pl.ANY),
            scratch_shapes=[pltpu.VMEM((D_MODEL, 2 * D_FF), BF16), pltpu.VMEM((D_FF, D_MODEL), BF16),
                            pltpu.VMEM((X_SLOTS, MOE_TILE, ROW_WIDTH), BF16), pltpu.VMEM((2, MOE_TILE, D_MODEL), BF16),
                            pltpu.SemaphoreType.DMA((X_SLOTS,)), pltpu.SemaphoreType.DMA((2,))],
        ),
        out_shape=jax.ShapeDtypeStruct((n_rows, D_MODEL), BF16),
        compiler_params=_cparams(("arbitrary",)),
        name="experts",
    )(first_block, n_block, xs, w_gate_up, b_gate_up.reshape(N_EXPERTS, 1, 2 * D_FF), w_down,
      b_down.reshape(N_EXPERTS, 1, D_MODEL))


def _combine_kernel(lo_ref, go_ref, cnt_ref, rows_ref, yb_hbm, lst_ref, x1_ref, mod_ref, o_ref, ybuf, sems):
    i = pl.program_id(0)
    slot = i % 2
    tables = (lo_ref, go_ref, cnt_ref, rows_ref)

    @pl.when(i == 0)
    def _():
        ybuf[...] = jnp.zeros(ybuf.shape, BF16)
        _start_pieces(tables, 0, ybuf.at[0], yb_hbm, sems.at[0], False)

    @pl.when(i + 1 < pl.num_programs(0))
    def _():
        _start_pieces(tables, i + 1, ybuf.at[1 - slot], yb_hbm, sems.at[1 - slot], False)

    _wait_rows(_tile_rows(tables, i), ybuf.at[slot], yb_hbm, sems.at[slot], False)

    lst = lst_ref[...]
    t = lst.shape[0]
    targets = [lst[:, k:k + 1] for k in range(TOP_K)]
    y = jnp.zeros((t, D_MODEL), F32)
    for jc in range(LOCAL_SLOTS // SLOT_CHUNK):
        j = lax.broadcasted_iota(I32, (t, SLOT_CHUNK), 1) + jc * SLOT_CHUNK
        pick = _one_hot_any(j, targets).astype(BF16)
        y = y + _dot(pick, ybuf[slot, jc * SLOT_CHUNK:(jc + 1) * SLOT_CHUNK, :])
    o_ref[...] = x1_ref[...] + mod_ref[5:6, :] * y


def _combine(tables, yb, ls_t, x1, mod_b, seq):
    n = x1.shape[0]
    t = ROW_TILE
    per_b = seq // t
    return pl.pallas_call(
        _combine_kernel,
        grid_spec=pltpu.PrefetchScalarGridSpec(
            num_scalar_prefetch=4,
            grid=(n // t,),
            in_specs=[
                pl.BlockSpec(memory_space=pl.ANY),
                pl.BlockSpec((t, TOP_K), lambda i, *_: (i, 0)),
                pl.BlockSpec((t, D_MODEL), lambda i, *_: (i, 0)),
                pl.BlockSpec((None, 6, D_MODEL), lambda i, *_: (i // per_b, 0, 0)),
            ],
            out_specs=pl.BlockSpec((t, D_MODEL), lambda i, *_: (i, 0)),
            scratch_shapes=[pltpu.VMEM((2, LOCAL_SLOTS, D_MODEL), BF16), pltpu.SemaphoreType.DMA((2,))],
        ),
        out_shape=jax.ShapeDtypeStruct((n, D_MODEL), F32),
        compiler_params=_cparams(("arbitrary",)),
        name="combine",
    )(*tables, yb, ls_t, x1, mod_b)


def _layer(x, c, w_ada, b_ada, norm1_g, w_in, a_q_norm_g, a_k_norm_g, b_q_norm_g, b_k_norm_g, lambda_q1,
           lambda_k1, lambda_q2, lambda_k2, diff_norm_g, w_out, norm2_g, w_router, b_router, w_gate_up,
           b_gate_up, w_down, b_down):
    bsz, seq, _ = x.shape
    n = bsz * seq
    x2 = x.reshape(n, D_MODEL)
    mod_b = _ada(c, w_ada, b_ada).transpose(1, 0, 2)

    qk_gains = jnp.stack([
        jnp.tile(a_q_norm_g, A_HEADS) * (ATTN_SCALE * LOG2E), jnp.tile(a_k_norm_g, A_HEADS),
        jnp.tile(b_k_norm_g, 2 * B_HEADS)])
    bq_gain = (jnp.tile(b_q_norm_g, 2 * B_HEADS) * (ATTN_SCALE * LOG2E)).reshape(B_WIDTH, 1)
    *a_parts, bk, bqt, bvt = _inproj(x2, mod_b, norm1_g, w_in, qk_gains, bq_gain, bsz, seq)

    dil_outs = _dilated(a_parts, bsz, seq)
    lam = (jnp.exp(jnp.sum(lambda_q1 * lambda_k1)) - jnp.exp(jnp.sum(lambda_q2 * lambda_k2)) + LAMBDA_INIT)
    slopes_b = 2.0 ** (-8.0 * np.arange(1, B_HEADS + 1) / B_HEADS)
    scalars = jnp.concatenate([lam.reshape(1), jnp.asarray(slopes_b, F32)]).astype(F32)
    bk4 = bk.reshape(bsz, seq // DIFF_BLOCK, DIFF_BLOCK, B_WIDTH)
    o_b = _diff(bqt, bk4, bvt, scalars, diff_norm_g, bsz, seq).reshape(n, B_WIDTH)

    x1, he, idx, rank, cnt = _router(dil_outs, o_b, x2, mod_b, w_out, norm2_g, w_router, b_router, seq)

    n_tiles = n // ROW_TILE
    counts = cnt[:, :, 0].astype(I32)
    seg = (counts + SEG_ALIGN - 1) // SEG_ALIGN * SEG_ALIGN
    loff = jnp.cumsum(seg, axis=1) - seg
    region = jnp.sum(seg, axis=0)
    padded = (region + MOE_TILE - 1) // MOE_TILE * MOE_TILE
    pad_end = jnp.cumsum(padded)
    pad_start = pad_end - padded
    goff = pad_start[None, :] + jnp.cumsum(seg, axis=0) - seg
    sizes = jnp.asarray(SEG_SIZES, I32)[None, :, None]
    has = (seg[:, None, :] & sizes) != 0
    within = seg[:, None, :] & ~(2 * sizes - 1)
    place = jnp.cumsum(has, axis=-1) - has
    pick = has[:, :, None, :] & (place[:, :, None, :] == jnp.arange(N_EXPERTS, dtype=I32)[None, None, :, None])
    listed = lambda rows_: jnp.sum(jnp.where(pick, rows_[:, :, None, :], 0), axis=-1).reshape(-1).astype(I32)
    tables = (listed(loff[:, None, :] + within), listed(goff[:, None, :] + within),
              jnp.sum(has, axis=-1).reshape(-1).astype(I32), jnp.sum(seg, axis=1).astype(I32))
    n_blocks = (n * TOP_K + n_tiles * N_EXPERTS * (SEG_ALIGN - 1) + N_EXPERTS * (MOE_TILE - 1)) // MOE_TILE
    n_used = (pad_end[-1] // MOE_TILE).astype(I32)
    tail = jnp.concatenate([pad_start + region, padded - region, n_used.reshape(1)])

    ls, xs = _sort(tables, tail, he, idx, rank, loff.reshape(n_tiles, N_EXPERTS, 1), n_blocks * MOE_TILE)
    yb = _experts(pad_start // MOE_TILE, padded // MOE_TILE, xs, w_gate_up, b_gate_up, w_down, b_down)
    out = _combine(tables, yb, ls.T, x1, mod_b, seq)
    return out.reshape(bsz, seq, D_MODEL)


def kernel(x, c, w_ada, b_ada, norm1_g, w_in, a_q_norm_g, a_k_norm_g, b_q_norm_g, b_k_norm_g, lambda_q1, lambda_k1,
           lambda_q2, lambda_k2, diff_norm_g, w_out, norm2_g, w_router, b_router, w_gate_up, b_gate_up, w_down,
           b_down):
    args = (w_ada, b_ada, norm1_g, w_in, a_q_norm_g, a_k_norm_g, b_q_norm_g, b_k_norm_g, lambda_q1, lambda_k1,
            lambda_q2, lambda_k2, diff_norm_g, w_out, norm2_g, w_router, b_router, w_gate_up, b_gate_up, w_down,
            b_down)
    return _layer(x, c, *[a[0] for a in args])
```

```python
import functools

import numpy as np
import jax
import jax.numpy as jnp
from jax import lax
from jax.experimental import pallas as pl
from jax.experimental.pallas import tpu as pltpu

F32 = jnp.float32
BF16 = jnp.bfloat16
I32 = jnp.int32
U32 = jnp.uint32

D_MODEL = 1024
HEAD_DIM = 64
A_WIDTH = 512
B_WIDTH = 512
A_HEADS = 8
B_HEADS = 4
IN_WIDTH = 3072
DIL_PATTERNS = ((128, 1), (512, 4), (2048, 16))
BAND = 128
N_EXPERTS = 32
TOP_K = 4
D_FF = 1024
SWIGLU_LIMIT = 7.0
SWIGLU_ALPHA = 1.702
RMS_EPS = 1e-6
ATTN_SCALE = HEAD_DIM ** -0.5
LOG2E = 1.4426950408889634
LN2 = 0.6931471805599453
LAMBDA_INIT = 0.8 - 0.6 * 1.0

LANES = 128
ROW_TILE = 512
DIFF_BLOCK = 512
DIFF_LANES = 256
DIFF_ROWS = 64
ONES_ROWS = 16
VT_ROWS = B_HEADS * (2 * HEAD_DIM + ONES_ROWS)
MOE_TILE = 256
X_AHEAD = 4
X_SLOTS = X_AHEAD + 1
EXTRA_LANES = 128
ROW_WIDTH = D_MODEL + EXTRA_LANES
SEG_ALIGN = 16
SEG_SIZES = (512, 256, 128, 64, 32, 16)
SLOT_CHUNK = 512
LOCAL_SLOTS = 2560
LOCAL_PIECES = LOCAL_SLOTS // SEG_ALIGN
assert ROW_TILE == DIFF_BLOCK == SEG_SIZES[0]
assert LOCAL_SLOTS >= ROW_TILE * TOP_K + N_EXPERTS * (SEG_ALIGN - 1) and LOCAL_SLOTS % SLOT_CHUNK == 0
VMEM_LIMIT = 56 * 1024 * 1024


def _cparams(sem, **flags):
    return pltpu.CompilerParams(dimension_semantics=sem, vmem_limit_bytes=VMEM_LIMIT, flags=flags or None)


def _split_bf16(a):
    hi = a.astype(BF16)
    lo = (a - hi.astype(F32)).astype(BF16)
    return hi, lo


def _dot_nt(a, b):
    return lax.dot_general(a, b, (((1,), (1,)), ((), ())), preferred_element_type=F32)


def _dot(a, b):
    return jnp.dot(a, b, preferred_element_type=F32)


def _ada_kernel(c_ref, w_ref, b_ref, o_ref):
    c = c_ref[...]
    s = c / (1.0 + jnp.exp(-c))
    sh, sl = _split_bf16(s)
    wh, wl = _split_bf16(w_ref[...])
    o_ref[0] = _dot(sh, wh) + _dot(sh, wl) + _dot(sl, wh) + b_ref[0]


def _ada(c, w_ada, b_ada):
    bsz = c.shape[0]
    return pl.pallas_call(
        _ada_kernel,
        grid=(6,),
        in_specs=[
            pl.BlockSpec((bsz, D_MODEL), lambda j: (0, 0)),
            pl.BlockSpec((D_MODEL, D_MODEL), lambda j: (0, j)),
            pl.BlockSpec((1, 1, D_MODEL), lambda j: (j, 0, 0)),
        ],
        out_specs=pl.BlockSpec((1, bsz, D_MODEL), lambda j: (j, 0, 0)),
        out_shape=jax.ShapeDtypeStruct((6, bsz, D_MODEL), F32),
        compiler_params=_cparams(("arbitrary",)),
        name="ada",
    )(c, w_ada, b_ada.reshape(6, 1, D_MODEL))


def _inproj_kernel(x_ref, mod_ref, g1_ref, wn_ref, wt_ref, gm_ref, qkg_ref, bqg_ref, a1_ref, a4_ref, a16_ref, bk_ref,
                   bqt_ref, bvt_ref, a_sc):
    a_refs = (a1_ref, a4_ref, a16_ref)
    x = x_ref[...]
    y = x * lax.rsqrt(jnp.mean(x * x, axis=-1, keepdims=True) + RMS_EPS) * g1_ref[...]
    h = (y * (1.0 + mod_ref[1:2, :]) + mod_ref[0:1, :]).astype(BF16)
    p = _dot(h, wn_ref[...])
    gm = gm_ref[...]

    def head_norm(t, g):
        ss = _dot((t * t).astype(BF16), gm)
        return t * lax.rsqrt(ss * (1.0 / HEAD_DIM) + RMS_EPS) * g

    w = A_WIDTH
    a_part = jnp.concatenate([head_norm(p[:, 0 * w:1 * w], qkg_ref[0:1, :]),
                              head_norm(p[:, 1 * w:2 * w], qkg_ref[1:2, :]), p[:, 2 * w:3 * w]], axis=1)
    n_col = a_sc.shape[0]
    for c in range(n_col):
        a_sc[c] = a_part[:, c * LANES:(c + 1) * LANES]
    for a_ref, (_, dil) in zip(a_refs, DIL_PATTERNS):
        if dil == 1:
            a_ref[0] = a_part.astype(BF16)
            continue
        rows = a_part.shape[0] // dil
        for r in range(dil):
            for c in range(n_col):
                a_ref[r, :, c * LANES:(c + 1) * LANES] = a_sc[c, pl.ds(r, rows, stride=dil), :].astype(BF16)
    bk_ref[...] = head_norm(p[:, 3 * w:4 * w], qkg_ref[2:3, :]).astype(BF16)

    pt = _dot_nt(wt_ref[...], h)
    t = pt.shape[1]
    bq = pt[:w].reshape(w // HEAD_DIM, HEAD_DIM, t)
    ss = jnp.sum(bq * bq, axis=1, keepdims=True)
    bq = (bq * lax.rsqrt(ss * (1.0 / HEAD_DIM) + RMS_EPS)).reshape(w, t) * bqg_ref[...]
    bqt_ref[...] = bq.astype(BF16)
    lanes = 2 * HEAD_DIM
    for hd in range(B_HEADS):
        base = hd * (lanes + ONES_ROWS)
        bvt_ref[base:base + lanes, :] = pt[w + hd * lanes:w + (hd + 1) * lanes].astype(BF16)
        bvt_ref[base + lanes:base + lanes + ONES_ROWS, :] = jnp.ones((ONES_ROWS, t), BF16)


def _inproj(x2, mod_b, norm1_g, w_in, qk_gains, bq_gain, bsz, seq):
    n = x2.shape[0]
    t = ROW_TILE
    per_b = seq // t
    w = A_WIDTH
    head_of_lane = np.arange(w) // HEAD_DIM
    gmat = jnp.asarray(head_of_lane[:, None] == head_of_lane[None, :], BF16)
    w_bf = w_in.astype(BF16)
    w_nat = jnp.concatenate([w_bf[:, :3 * w], w_bf[:, 4 * w:5 * w]], axis=1)
    w_tr = jnp.concatenate([w_bf[:, 3 * w:4 * w], w_bf[:, 5 * w:]], axis=1).T
    tr_spec = pl.BlockSpec((None, None, w, t), lambda i: (i // per_b, i % per_b, 0, 0))
    tr_shape = jax.ShapeDtypeStruct((bsz, per_b, w, t), BF16)
    a_specs = [pl.BlockSpec((None, dil, t // dil, 3 * w), lambda i: (i // per_b, 0, i % per_b, 0))
               for _, dil in DIL_PATTERNS]
    a_shapes = [jax.ShapeDtypeStruct((bsz, dil, seq // dil, 3 * w), BF16) for _, dil in DIL_PATTERNS]
    return pl.pallas_call(
        _inproj_kernel,
        grid=(n // t,),
        in_specs=[
            pl.BlockSpec((t, D_MODEL), lambda i: (i, 0)),
            pl.BlockSpec((None, 6, D_MODEL), lambda i: (i // per_b, 0, 0)),
            pl.BlockSpec((1, D_MODEL), lambda i: (0, 0)),
            pl.BlockSpec((D_MODEL, 4 * w), lambda i: (0, 0)),
            pl.BlockSpec((2 * w, D_MODEL), lambda i: (0, 0)),
            pl.BlockSpec((w, w), lambda i: (0, 0)),
            pl.BlockSpec((3, w), lambda i: (0, 0)),
            pl.BlockSpec((w, 1), lambda i: (0, 0)),
        ],
        out_specs=a_specs + [pl.BlockSpec((t, w), lambda i: (i, 0)), tr_spec,
                             pl.BlockSpec((None, None, VT_ROWS, t), lambda i: (i // per_b, i % per_b, 0, 0))],
        out_shape=a_shapes + [jax.ShapeDtypeStruct((n, w), BF16), tr_shape,
                              jax.ShapeDtypeStruct((bsz, per_b, VT_ROWS, t), BF16)],
        scratch_shapes=[pltpu.VMEM((3 * w // LANES, t, LANES), F32)],
        compiler_params=_cparams(("parallel",)),
        name="inproj",
    )(x2, mod_b, norm1_g.reshape(1, D_MODEL), w_nat, w_tr, gmat, qk_gains, bq_gain)


def _dilated_bias_table():
    ik = np.arange(2 * BAND)[:, None]
    iq = np.arange(BAND)[None, :]
    delta = iq - ik + BAND
    in_band = (delta >= 0) & (delta <= BAND)
    slopes = 2.0 ** (-8.0 * np.arange(1, A_HEADS + 1) / A_HEADS)
    tbl = np.zeros((len(DIL_PATTERNS), 2, A_HEADS // 2, 2 * BAND, 2 * BAND), np.float32)
    for p, (_, dil) in enumerate(DIL_PATTERNS):
        for first in range(2):
            valid = in_band & ((ik >= BAND) if first else True)
            for h in range(A_HEADS):
                cols = slice(BAND * (h % 2), BAND * (h % 2 + 1))
                tbl[p, first, h // 2, :, cols] = np.where(valid, -slopes[h] * LOG2E * (delta * dil), -np.inf)
    return jnp.asarray(tbl)


def _dilated_kernel(tbl_ref, *refs, n_blocks):
    lanes = 2 * HEAD_DIM
    n = pl.program_id(1)
    n_pat = len(DIL_PATTERNS)
    s_sc, p_sc = refs[len(refs) - 2:]
    outs = refs[len(refs) - 2 - 2 * n_pat:len(refs) - 2]
    lane = lax.broadcasted_iota(I32, (BAND, lanes), 1)
    low = lane < HEAD_DIM
    work = []
    pos = 0
    for p, (_, dil) in enumerate(DIL_PATTERNS):
        per_res = n_blocks // dil
        if per_res > 1:
            q_ref, kp_ref, kc_ref, vp_ref, vc_ref = refs[pos:pos + 5]
            pos += 5
            first = (n % per_res == 0).astype(I32)
        else:
            q_ref, kc_ref, vc_ref = refs[pos:pos + 3]
            kp_ref = vp_ref = first = None
            pos += 3
        for j in range(A_HEADS // 2):
            work.append((p, j, (q_ref, kp_ref, kc_ref, vp_ref, vc_ref), first))

    def keys_of(prev_ref, cur_ref, cs):
        return cur_ref[:, cs] if prev_ref is None else jnp.concatenate([prev_ref[:, cs], cur_ref[:, cs]], axis=0)

    for w, (p, j, (q_ref, kp_ref, kc_ref, _, _), first) in enumerate(work):
        cs = slice(lanes * j, lanes * (j + 1))
        q2 = q_ref[:, cs]
        zero = jnp.zeros_like(q2)
        qcat = jnp.concatenate([jnp.where(low, q2, zero), jnp.where(low, zero, q2)], axis=0)
        k2 = keys_of(kp_ref, kc_ref, cs)
        bias = tbl_ref[p, 1, j, BAND:, :] if kp_ref is None else tbl_ref[p, first, j]
        s_sc[w, :k2.shape[0], :] = _dot_nt(k2, qcat) + bias
    lses = []
    for w, (p, j, (_, kp_ref, _, _, _), _) in enumerate(work):
        nk = BAND if kp_ref is None else 2 * BAND
        s = s_sc[w, :nk, :]
        m = jnp.max(s, axis=0, keepdims=True)
        e = jnp.exp2(s - m)
        den = jnp.sum(e, axis=0, keepdims=True)
        p_sc[w, :nk, :] = (e * (1.0 / den)).astype(BF16)
        lses.append((m + jnp.log2(den)) * LN2)
    for w, (p, j, (_, kp_ref, _, vp_ref, vc_ref), _) in enumerate(work):
        nk = BAND if kp_ref is None else 2 * BAND
        cs = slice(lanes * j, lanes * (j + 1))
        v2 = keys_of(vp_ref, vc_ref, cs)
        r = lax.dot_general(p_sc[w, :nk, :], v2, (((0,), (0,)), ((), ())), preferred_element_type=F32)
        outs[2 * p][:, cs] = jnp.where(low, r[:BAND], r[BAND:]).astype(BF16)
    for p in range(n_pat):
        rows = []
        for lse in lses[p * (A_HEADS // 2):(p + 1) * (A_HEADS // 2)]:
            rows += [lse[:, :BAND], lse[:, BAND:]]
        rows.append(jnp.zeros((lanes - A_HEADS, BAND), F32))
        outs[2 * p + 1][...] = jnp.concatenate(rows, axis=0).T


def _dilated(a_parts, bsz, seq):
    n_blocks = seq // BAND
    in_specs = [pl.BlockSpec((len(DIL_PATTERNS), 2, A_HEADS // 2, 2 * BAND, 2 * BAND), lambda b, n: (0, 0, 0, 0, 0))]
    n_work = len(DIL_PATTERNS) * A_HEADS // 2
    args = [_dilated_bias_table()]
    out_specs, out_shapes = [], []
    blk = (None, None, BAND, A_WIDTH)
    for a_part, (_, dil) in zip(a_parts, DIL_PATTERNS):
        per_res = n_blocks // dil

        def cur(col, per_res=per_res):
            return lambda b, n: (b, n // per_res, n % per_res, col)

        def prev(col, per_res=per_res):
            return lambda b, n: (b, n // per_res, jnp.maximum(n % per_res - 1, 0), col)

        if per_res > 1:
            in_specs += [pl.BlockSpec(blk, cur(0)), pl.BlockSpec(blk, prev(1)), pl.BlockSpec(blk, cur(1)),
                         pl.BlockSpec(blk, prev(2)), pl.BlockSpec(blk, cur(2))]
            args += [a_part] * 5
        else:
            in_specs += [pl.BlockSpec(blk, cur(0)), pl.BlockSpec(blk, cur(1)), pl.BlockSpec(blk, cur(2))]
            args += [a_part] * 3
        out_specs += [pl.BlockSpec(blk, cur(0)), pl.BlockSpec((None, None, BAND, LANES), cur(0))]
        out_shapes += [jax.ShapeDtypeStruct((bsz, dil, seq // dil, A_WIDTH), BF16),
                       jax.ShapeDtypeStruct((bsz, dil, seq // dil, LANES), F32)]
    return pl.pallas_call(
        functools.partial(_dilated_kernel, n_blocks=n_blocks),
        grid=(bsz, n_blocks),
        in_specs=in_specs,
        out_specs=out_specs,
        out_shape=out_shapes,
        scratch_shapes=[pltpu.VMEM((n_work, 2 * BAND, 2 * BAND), F32), pltpu.VMEM((n_work, 2 * BAND, 2 * BAND), BF16)],
        compiler_params=_cparams(("parallel", "parallel")),
        name="dilated",
    )(*args)


def _diff_bias_table():
    t = DIFF_BLOCK
    rel = np.arange(t)[None, :] - np.arange(t)[:, None]
    slopes = 2.0 ** (-8.0 * np.arange(1, B_HEADS + 1) / B_HEADS)
    tbl = np.zeros((B_HEADS, 2, t, t), np.float32)
    for h in range(B_HEADS):
        tbl[h, 0] = -slopes[h] * LOG2E * rel
        tbl[h, 1] = np.where(rel >= 0, -slopes[h] * LOG2E * rel, -np.inf)
    return jnp.asarray(tbl)


def _diff_kernel(sc_ref, qt_ref, k_ref, vt_ref, g_ref, bias_sc, o_ref, qst_sc, m_sc, l_sc, a_sc, acc_sc, s_sc, p_sc):
    t = DIFF_BLOCK
    lanes = 2 * HEAD_DIM
    qi = pl.program_id(1)
    lam = sc_ref[0]
    slopes = [jnp.full((1, 1), sc_ref[1 + h], F32) for h in range(B_HEADS)]
    row = lax.broadcasted_iota(I32, (lanes, t), 0)
    for h in range(B_HEADS):
        qt = qt_ref[h * lanes:(h + 1) * lanes, :]
        zero = jnp.zeros_like(qt)
        qst_sc[h, :, :t] = jnp.where(row < HEAD_DIM, qt, zero)
        qst_sc[h, :, t:] = jnp.where(row < HEAD_DIM, zero, qt)
    m_sc[...] = jnp.full(m_sc.shape, -jnp.inf, F32)
    l_sc[...] = jnp.zeros(l_sc.shape, F32)
    acc_sc[...] = jnp.zeros(acc_sc.shape, F32)

    n_grp = 2 * t // DIFF_LANES
    items = [(h, g) for h in range(B_HEADS) for g in range(n_grp)]
    chunks = [slice(r * DIFF_ROWS, (r + 1) * DIFF_ROWS) for r in range(t // DIFF_ROWS)]

    def keys_needed(g, diag):
        return min(t, (g * DIFF_LANES) % t + DIFF_LANES) if diag else t

    def block(n, diag):
        cs = [-slopes[h] * LOG2E * (jnp.full((1, 1), (qi - n) * t, I32)).astype(F32) for h in range(B_HEADS)]
        for w, (h, g) in enumerate(items):
            nk = keys_needed(g, diag)
            cols = slice(g * DIFF_LANES, (g + 1) * DIFF_LANES)
            bcols = slice((g * DIFF_LANES) % t, (g * DIFF_LANES) % t + DIFF_LANES)
            s_sc[w, :nk, :] = (_dot(k_ref[n, :nk, h * lanes:(h + 1) * lanes], qst_sc[h, :, cols])
                               + bias_sc[h, diag, :nk, bcols])
        for w, (h, g) in enumerate(items):
            used = chunks[:keys_needed(g, diag) // DIFF_ROWS]
            cols = slice(g * DIFF_LANES, (g + 1) * DIFF_LANES)
            top = s_sc[w, used[0], :]
            for rows in used[1:]:
                top = jnp.maximum(top, s_sc[w, rows, :])
            m_prev = m_sc[h, :, cols]
            m_new = jnp.maximum(m_prev, jnp.max(top, axis=0, keepdims=True) + cs[h])
            alpha = jnp.exp2(m_prev - m_new)
            shift = m_new - cs[h]
            for rows in used:
                p_sc[w, rows, :] = jnp.exp2(s_sc[w, rows, :] - shift).astype(BF16)
            m_sc[h, :, cols] = m_new
            a_sc[h, :, cols] = alpha
        for w, (h, g) in enumerate(items):
            nk = keys_needed(g, diag)
            cols = slice(g * DIFF_LANES, (g + 1) * DIFF_LANES)
            pv = _dot(vt_ref[n, h * (lanes + ONES_ROWS):(h + 1) * (lanes + ONES_ROWS), :nk], p_sc[w, :nk, :])
            alpha = a_sc[h, :, cols]
            acc_sc[h, :, cols] = alpha * acc_sc[h, :, cols] + pv[:lanes]
            l_sc[h, :, cols] = alpha * l_sc[h, :, cols] + pv[lanes:lanes + 1]

    def body(n, carry):
        block(n, 0)
        return carry

    lax.fori_loop(0, qi, body, 0)
    block(qi, 1)
    for h in range(B_HEADS):
        o = acc_sc[h] / l_sc[h]
        o = o[:, :t] - lam * o[:, t:]
        o = o * lax.rsqrt(jnp.mean(o * o, axis=0, keepdims=True) + RMS_EPS) * g_ref[...]
        o_ref[:, h * lanes:(h + 1) * lanes] = (o * (1.0 - LAMBDA_INIT)).T.astype(BF16)


def _diff(bqt, bk, bvt, scalars, diff_norm_g, bsz, seq):
    t = DIFF_BLOCK
    nb = seq // t
    lanes = 2 * HEAD_DIM
    n_items = B_HEADS * 2 * t // DIFF_LANES
    return pl.pallas_call(
        _diff_kernel,
        grid=(bsz, nb),
        in_specs=[
            pl.BlockSpec(memory_space=pltpu.SMEM),
            pl.BlockSpec((None, None, B_WIDTH, t), lambda b, i: (b, i, 0, 0)),
            pl.BlockSpec((None, nb, t, B_WIDTH), lambda b, i: (b, 0, 0, 0)),
            pl.BlockSpec((None, nb, VT_ROWS, t), lambda b, i: (b, 0, 0, 0)),
            pl.BlockSpec((lanes, 1), lambda b, i: (0, 0)),
            pl.BlockSpec((B_HEADS, 2, t, t), lambda b, i: (0, 0, 0, 0)),
        ],
        out_specs=pl.BlockSpec((None, t, B_WIDTH), lambda b, i: (b, i, 0)),
        out_shape=jax.ShapeDtypeStruct((bsz, seq, B_WIDTH), BF16),
        scratch_shapes=[pltpu.VMEM((B_HEADS, lanes, 2 * t), BF16)] + [pltpu.VMEM((B_HEADS, 1, 2 * t), F32)] * 3 + [
            pltpu.VMEM((B_HEADS, lanes, 2 * t), F32),
            pltpu.VMEM((n_items, t, DIFF_LANES), F32), pltpu.VMEM((n_items, t, DIFF_LANES), BF16)],
        compiler_params=_cparams(("parallel", "parallel")),
        name="diff",
    )(scalars, bqt, bk, bvt, diff_norm_g.reshape(lanes, 1), _diff_bias_table())


def _router_kernel(o0, l0, o1, l1, o2, l2, ob_ref, x_ref, mod_ref, wout_ref, g2_ref, wrh_ref, wrl_ref, br_ref,
                   tri_ref, spread_ref, x1_ref, he_ref, idx_ref, rank_ref, cnt_ref, *order_sc):
    def token_order(ref, scratch):
        dil, rows, width = ref.shape
        if dil == 1:
            return ref[0].astype(F32)
        n_col = width // LANES
        for r in range(dil):
            for c in range(n_col):
                scratch[c, pl.ds(r, rows, stride=dil), :] = ref[r, :, c * LANES:(c + 1) * LANES].astype(F32)
        return jnp.concatenate([scratch[c] for c in range(n_col)], axis=1)

    os_ = [token_order(o0, None), token_order(o1, order_sc[0]), token_order(o2, order_sc[2])]
    ls = [token_order(l0, None), token_order(l1, order_sc[1]), token_order(l2, order_sc[3])]
    mx = jnp.maximum(jnp.maximum(ls[0], ls[1]), ls[2])
    ws = [jnp.exp(l - mx) for l in ls]
    den = ws[0] + ws[1] + ws[2]
    spread = spread_ref[...]
    oa = jnp.zeros(os_[0].shape, F32)
    for w, o in zip(ws, os_):
        wh, wl = _split_bf16(w / den)
        oa = oa + (_dot(wh, spread) + _dot(wl, spread)) * o
    mixed = _dot(oa.astype(BF16), wout_ref[:A_WIDTH, :]) + _dot(ob_ref[...], wout_ref[A_WIDTH:, :])
    x1 = x_ref[...] + mod_ref[2:3, :] * mixed
    x1_ref[...] = x1
    y = x1 * lax.rsqrt(jnp.mean(x1 * x1, axis=-1, keepdims=True) + RMS_EPS) * g2_ref[...]
    h2 = y * (1.0 + mod_ref[4:5, :]) + mod_ref[3:4, :]
    he_ref[:, :D_MODEL] = h2.astype(BF16)

    hh, hl = _split_bf16(h2)
    wrh = wrh_ref[...]
    logits = _dot_nt(wrh, hh) + _dot_nt(wrh, hl) + _dot_nt(wrl_ref[...], hh) + br_ref[...]
    t = logits.shape[1]
    eid = lax.broadcasted_iota(I32, (N_EXPERTS, t), 0)
    vals, idxs, hots = [], [], []
    cur = logits
    for _ in range(TOP_K):
        v = jnp.max(cur, axis=0, keepdims=True)
        ik = jnp.min(jnp.where(cur == v, eid, N_EXPERTS), axis=0, keepdims=True)
        hot = eid == ik
        vals.append(v)
        idxs.append(ik)
        hots.append(hot)
        cur = jnp.where(hot, -jnp.inf, cur)
    es = [jnp.exp(v - vals[0]) for v in vals]
    esum = es[0] + es[1] + es[2] + es[3]
    idx_ref[...] = jnp.concatenate(idxs, axis=0)

    rows = [ik.astype(F32) for ik in idxs]
    for e in es:
        g = e / esum
        hi = g.astype(BF16).astype(F32)
        mid = (g - hi).astype(BF16).astype(F32)
        rows += [hi, mid, g - hi - mid]
    rows.append(jnp.zeros((EXTRA_LANES - len(rows), t), F32))
    he_ref[:, D_MODEL:] = jnp.concatenate(rows, axis=0).T.astype(BF16)

    sel = jnp.where(hots[0] | hots[1] | hots[2] | hots[3], 1.0, 0.0)
    before = _dot(sel.astype(BF16), tri_ref[...])
    ranks = [jnp.sum(jnp.where(hot, before, 0.0), axis=0, keepdims=True) for hot in hots]
    rank_ref[...] = jnp.concatenate(ranks, axis=0).astype(I32)
    cnt_ref[...] = jnp.broadcast_to(jnp.sum(sel, axis=1, keepdims=True), cnt_ref.shape)


def _router(dil_outs, o_b, x2, mod_b, w_out, norm2_g, w_router, b_router, seq):
    n = x2.shape[0]
    t = ROW_TILE
    per_b = seq // t
    wr_t = w_router.T
    wrh = wr_t.astype(BF16)
    wrl = (wr_t - wrh.astype(F32)).astype(BF16)
    tri = jnp.asarray(np.arange(t)[:, None] < np.arange(t)[None, :], BF16)
    row = lambda w: pl.BlockSpec((t, w), lambda i: (i, 0))
    full = lambda a, b: pl.BlockSpec((a, b), lambda i: (0, 0))
    tok = lambda: pl.BlockSpec((TOP_K, t), lambda i: (0, i))
    grouped = lambda dil, w: pl.BlockSpec((None, dil, t // dil, w), lambda i: (i // per_b, 0, i % per_b, 0))
    widths = (A_WIDTH, LANES)
    head_of_lane = np.arange(A_WIDTH) // HEAD_DIM
    spread = jnp.asarray(np.arange(LANES)[:, None] == head_of_lane[None, :], BF16)
    return pl.pallas_call(
        _router_kernel,
        grid=(n // t,),
        in_specs=[grouped(dil, w) for _, dil in DIL_PATTERNS for w in widths] + [
            row(B_WIDTH), row(D_MODEL),
            pl.BlockSpec((None, 6, D_MODEL), lambda i: (i // per_b, 0, 0)),
            full(D_MODEL, D_MODEL), full(1, D_MODEL), full(N_EXPERTS, D_MODEL), full(N_EXPERTS, D_MODEL),
            full(N_EXPERTS, 1), full(t, t), full(LANES, A_WIDTH),
        ],
        out_specs=[row(D_MODEL), row(ROW_WIDTH), tok(), tok(),
                   pl.BlockSpec((None, N_EXPERTS, 128), lambda i: (i, 0, 0))],
        out_shape=[
            jax.ShapeDtypeStruct((n, D_MODEL), F32),
            jax.ShapeDtypeStruct((n, ROW_WIDTH), BF16),
            jax.ShapeDtypeStruct((TOP_K, n), I32),
            jax.ShapeDtypeStruct((TOP_K, n), I32),
            jax.ShapeDtypeStruct((n // t, N_EXPERTS, 128), F32),
        ],
        scratch_shapes=[pltpu.VMEM((w // LANES, t, LANES), F32) for _, dil in DIL_PATTERNS if dil > 1 for w in widths],
        compiler_params=_cparams(("parallel",)),
        name="router",
    )(*dil_outs, o_b, x2, mod_b, w_out.astype(BF16), norm2_g.reshape(1, D_MODEL), wrh, wrl,
      b_router.reshape(N_EXPERTS, 1), tri, spread)


def _start_pieces(tables, tile, local_ref, hbm_ref, sem, outbound):
    lo_ref, go_ref, cnt_ref, _ = tables
    for c, size in enumerate(SEG_SIZES):
        cls = tile * len(SEG_SIZES) + c

        def per_piece(k, carry, size=size, cls=cls):
            lo, go = lo_ref[cls * N_EXPERTS + k], go_ref[cls * N_EXPERTS + k]
            loc = local_ref.at[pl.ds(pl.multiple_of(lo, SEG_ALIGN), size)]
            glob = hbm_ref.at[pl.ds(pl.multiple_of(go, SEG_ALIGN), size)]
            cp = pltpu.make_async_copy(loc, glob, sem) if outbound else pltpu.make_async_copy(glob, loc, sem)
            cp.start()
            return carry

        lax.fori_loop(0, cnt_ref[cls], per_piece, 0)


def _start(cp):
    cp.start()


def _wait(cp):
    cp.wait()


def _tile_rows(tables, tile):
    return tables[3][tile]


def _wait_rows(rows, local_ref, hbm_ref, sem, outbound):
    size = 1 << (LOCAL_SLOTS.bit_length() - 1)
    while size >= SEG_ALIGN:
        loc, glob = local_ref.at[pl.ds(0, size)], hbm_ref.at[pl.ds(0, size)]
        cp = pltpu.make_async_copy(loc, glob, sem) if outbound else pltpu.make_async_copy(glob, loc, sem)
        pl.when((rows & size) != 0)(cp.wait)
        size //= 2


def _one_hot_any(j, targets):
    out = jnp.zeros(j.shape, F32)
    for tgt in targets:
        out = jnp.where(j == tgt, 1.0, out)
    return out


def _sort_kernel(lo_ref, go_ref, cnt_ref, rows_ref, tail_ref, he_ref, idx_ref, rank_ref, lcol_ref, ls_ref, xs_hbm,
                 xl, zbuf, sems, zsem):
    tables = (lo_ref, go_ref, cnt_ref, rows_ref)
    i = pl.program_id(0)
    last = pl.num_programs(0) - 1
    slot = i % 2
    t = he_ref.shape[0]

    @pl.when(i == 0)
    def _():
        zbuf[...] = jnp.zeros(zbuf.shape, BF16)

        def tails(action):
            def per_expert(e, carry):
                off, n = tail_ref[e], tail_ref[N_EXPERTS + e]
                done = jnp.int32(0)
                for size in SEG_SIZES:
                    if size < MOE_TILE:
                        take = (n & size) != 0
                        dst = xs_hbm.at[pl.ds(pl.multiple_of(off + done, SEG_ALIGN), size)]
                        pl.when(take)(functools.partial(action, pltpu.make_async_copy(zbuf.at[pl.ds(0, size)], dst, zsem)))
                        done = done + jnp.where(take, size, 0)
                return carry

            lax.fori_loop(0, N_EXPERTS, per_expert, 0)

            def per_block(b, carry):
                dst = xs_hbm.at[pl.ds(pl.multiple_of(b * MOE_TILE, MOE_TILE), MOE_TILE)]
                action(pltpu.make_async_copy(zbuf, dst, zsem))
                return carry

            lax.fori_loop(tail_ref[2 * N_EXPERTS], xs_hbm.shape[0] // MOE_TILE, per_block, 0)

        tails(_start)
        tails(_wait)

    eid = lax.broadcasted_iota(I32, (N_EXPERTS, t), 0)
    lcol = lcol_ref[...]
    ls = []
    for k in range(TOP_K):
        off = jnp.sum(jnp.where(eid == idx_ref[k:k + 1, :], lcol, 0), axis=0, keepdims=True)
        ls.append(off + rank_ref[k:k + 1, :])
    ls_ref[...] = jnp.concatenate(ls, axis=0)

    he = he_ref[...]
    for jc in range(LOCAL_SLOTS // SLOT_CHUNK):
        j = lax.broadcasted_iota(I32, (SLOT_CHUNK, t), 0) + jc * SLOT_CHUNK
        perm = _one_hot_any(j, ls).astype(BF16)
        xl[slot, jc * SLOT_CHUNK:(jc + 1) * SLOT_CHUNK, :] = _dot(perm, he).astype(BF16)

    _start_pieces(tables, i, xl.at[slot], xs_hbm, sems.at[slot], True)

    @pl.when(i > 0)
    def _():
        _wait_rows(_tile_rows(tables, i - 1), xl.at[1 - slot], xs_hbm, sems.at[1 - slot], True)

    @pl.when(i == last)
    def _():
        _wait_rows(_tile_rows(tables, i), xl.at[slot], xs_hbm, sems.at[slot], True)


def _sort(tables, tail, he, idx, rank, lcol, n_rows):
    n = he.shape[0]
    t = ROW_TILE
    tok = lambda: pl.BlockSpec((TOP_K, t), lambda i, *_: (0, i))
    return pl.pallas_call(
        _sort_kernel,
        grid_spec=pltpu.PrefetchScalarGridSpec(
            num_scalar_prefetch=5,
            grid=(n // t,),
            in_specs=[
                pl.BlockSpec((t, ROW_WIDTH), lambda i, *_: (i, 0)),
                tok(), tok(),
                pl.BlockSpec((None, N_EXPERTS, 1), lambda i, *_: (i, 0, 0)),
            ],
            out_specs=[tok(), pl.BlockSpec(memory_space=pl.ANY)],
            scratch_shapes=[pltpu.VMEM((2, LOCAL_SLOTS, ROW_WIDTH), BF16), pltpu.VMEM((MOE_TILE, ROW_WIDTH), BF16),
                            pltpu.SemaphoreType.DMA((2,)), pltpu.SemaphoreType.DMA(())],
        ),
        out_shape=[jax.ShapeDtypeStruct((TOP_K, n), I32), jax.ShapeDtypeStruct((n_rows, ROW_WIDTH), BF16)],
        compiler_params=_cparams(("arbitrary",)),
        name="sort",
    )(*tables, tail, he, idx, rank, lcol)


def _experts_kernel(blk0_ref, nblk_ref, xs_hbm, wgu_ref, bgu_ref, wd_ref, bd_ref, yb_hbm, wgu_sc, wd_sc, xbuf, ybuf,
                    xsem, ysem):
    e = pl.program_id(0)
    tm = MOE_TILE
    first, nb = blk0_ref[e], nblk_ref[e]
    last_e = pl.num_programs(0) - 1
    total = blk0_ref[last_e] + nblk_ref[last_e]

    def rows(b):
        return pl.ds(pl.multiple_of(b * tm, tm), tm)

    def x_copy(b):
        slot = b % X_SLOTS
        return pltpu.make_async_copy(xs_hbm.at[rows(b)], xbuf.at[slot], xsem.at[slot])

    def y_copy(b, slot):
        return pltpu.make_async_copy(ybuf.at[slot], yb_hbm.at[rows(b)], ysem.at[slot])

    @pl.when(e == 0)
    def _():
        for b in range(X_AHEAD):
            pl.when(b < total)(x_copy(b).start)

    def cast(r, carry):
        s = pl.multiple_of(r * LANES, LANES)
        wgu_sc[pl.ds(s, LANES), :] = wgu_ref[pl.ds(s, LANES), :].astype(BF16)
        wd_sc[pl.ds(s, LANES), :] = wd_ref[pl.ds(s, LANES), :].astype(BF16)
        return carry

    lax.fori_loop(0, D_MODEL // LANES, cast, 0)
    me = jnp.full((1, 1), e, I32).astype(F32)

    def block(i, carry):
        b = first + i
        slot = i % 2
        xslot = b % X_SLOTS

        @pl.when(b + X_AHEAD < total)
        def _():
            x_copy(b + X_AHEAD).start()

        x_copy(b).wait()

        @pl.when(i >= 2)
        def _():
            y_copy(b - 2, slot).wait()

        ext = xbuf[xslot, :, D_MODEL:].astype(F32)
        gate = jnp.zeros((tm, 1), F32)
        for k in range(TOP_K):
            c = TOP_K + 3 * k
            gk = ext[:, c:c + 1] + ext[:, c + 1:c + 2] + ext[:, c + 2:c + 3]
            gate = gate + jnp.where(ext[:, k:k + 1] == me, gk, 0.0)

        gu = _dot(xbuf[xslot, :, :D_MODEL], wgu_sc[...]) + bgu_ref[...]
        g = jnp.minimum(gu[:, :D_FF], SWIGLU_LIMIT)
        u = jnp.clip(gu[:, D_FF:], -SWIGLU_LIMIT, SWIGLU_LIMIT)
        act = (u + 1.0) * (g / (1.0 + jnp.exp(-SWIGLU_ALPHA * g)))
        ybuf[slot] = (gate * (_dot(act.astype(BF16), wd_sc[...]) + bd_ref[...])).astype(BF16)
        y_copy(b, slot).start()
        return carry

    lax.fori_loop(0, nb, block, 0)

    @pl.when(nb >= 2)
    def _():
        y_copy(first + nb - 2, nb % 2).wait()

    @pl.when(nb >= 1)
    def _():
        y_copy(first + nb - 1, (nb - 1) % 2).wait()

    @pl.when(e == last_e)
    def _():
        ybuf[0] = jnp.zeros((tm, D_MODEL), BF16)

        def fill(action):
            def per_block(b, carry):
                action(pltpu.make_async_copy(ybuf.at[0], yb_hbm.at[rows(b)], ysem.at[0]))
                return carry

            lax.fori_loop(total, yb_hbm.shape[0] // tm, per_block, 0)

        fill(_start)
        fill(_wait)


def _experts(first_block, n_block, xs, w_gate_up, b_gate_up, w_down, b_down):
    n_rows = xs.shape[0]
    exp3 = lambda e, *_: (e, 0, 0)
    return pl.pallas_call(
        _experts_kernel,
        grid_spec=pltpu.PrefetchScalarGridSpec(
            num_scalar_prefetch=2,
            grid=(N_EXPERTS,),
            in_specs=[
                pl.BlockSpec(memory_space=pl.ANY),
                pl.BlockSpec((None, D_MODEL, 2 * D_FF), exp3),
                pl.BlockSpec((None, 1, 2 * D_FF), exp3),
                pl.BlockSpec((None, D_FF, D_MODEL), exp3),
                pl.BlockSpec((None, 1, D_MODEL), exp3),
            ],
            out_specs=pl.BlockSpec(memory_space=pl.ANY),
            scratch_shapes=[pltpu.VMEM((D_MODEL, 2 * D_FF), BF16), pltpu.VMEM((D_FF, D_MODEL), BF16),
                            pltpu.VMEM((X_SLOTS, MOE_TILE, ROW_WIDTH), BF16), pltpu.VMEM((2, MOE_TILE, D_MODEL), BF16),
                            pltpu.SemaphoreType.DMA((X_SLOTS,)), pltpu.SemaphoreType.DMA((2,))],
        ),
        out_shape=jax.ShapeDtypeStruct((n_rows, D_MODEL), BF16),
        compiler_params=_cparams(("arbitrary",)),
        name="experts",
    )(first_block, n_block, xs, w_gate_up, b_gate_up.reshape(N_EXPERTS, 1, 2 * D_FF), w_down,
      b_down.reshape(N_EXPERTS, 1, D_MODEL))


def _combine_kernel(lo_ref, go_ref, cnt_ref, rows_ref, yb_hbm, lst_ref, x1_ref, mod_ref, o_ref, ybuf, sems):
    i = pl.program_id(0)
    slot = i % 2
    tables = (lo_ref, go_ref, cnt_ref, rows_ref)

    @pl.when(i == 0)
    def _():
        ybuf[...] = jnp.zeros(ybuf.shape, BF16)
        _start_pieces(tables, 0, ybuf.at[0], yb_hbm, sems.at[0], False)

    @pl.when(i + 1 < pl.num_programs(0))
    def _():
        _start_pieces(tables, i + 1, ybuf.at[1 - slot], yb_hbm, sems.at[1 - slot], False)

    _wait_rows(_tile_rows(tables, i), ybuf.at[slot], yb_hbm, sems.at[slot], False)

    lst = lst_ref[...]
    t = lst.shape[0]
    targets = [lst[:, k:k + 1] for k in range(TOP_K)]
    y = jnp.zeros((t, D_MODEL), F32)
    for jc in range(LOCAL_SLOTS // SLOT_CHUNK):
        j = lax.broadcasted_iota(I32, (t, SLOT_CHUNK), 1) + jc * SLOT_CHUNK
        pick = _one_hot_any(j, targets).astype(BF16)
        y = y + _dot(pick, ybuf[slot, jc * SLOT_CHUNK:(jc + 1) * SLOT_CHUNK, :])
    o_ref[...] = x1_ref[...] + mod_ref[5:6, :] * y


def _combine(tables, yb, ls_t, x1, mod_b, seq):
    n = x1.shape[0]
    t = ROW_TILE
    per_b = seq // t
    return pl.pallas_call(
        _combine_kernel,
        grid_spec=pltpu.PrefetchScalarGridSpec(
            num_scalar_prefetch=4,
            grid=(n // t,),
            in_specs=[
                pl.BlockSpec(memory_space=pl.ANY),
                pl.BlockSpec((t, TOP_K), lambda i, *_: (i, 0)),
                pl.BlockSpec((t, D_MODEL), lambda i, *_: (i, 0)),
                pl.BlockSpec((None, 6, D_MODEL), lambda i, *_: (i // per_b, 0, 0)),
            ],
            out_specs=pl.BlockSpec((t, D_MODEL), lambda i, *_: (i, 0)),
            scratch_shapes=[pltpu.VMEM((2, LOCAL_SLOTS, D_MODEL), BF16), pltpu.SemaphoreType.DMA((2,))],
        ),
        out_shape=jax.ShapeDtypeStruct((n, D_MODEL), F32),
        compiler_params=_cparams(("arbitrary",)),
        name="combine",
    )(*tables, yb, ls_t, x1, mod_b)


def _layer(x, c, w_ada, b_ada, norm1_g, w_in, a_q_norm_g, a_k_norm_g, b_q_norm_g, b_k_norm_g, lambda_q1,
           lambda_k1, lambda_q2, lambda_k2, diff_norm_g, w_out, norm2_g, w_router, b_router, w_gate_up,
           b_gate_up, w_down, b_down):
    bsz, seq, _ = x.shape
    n = bsz * seq
    x2 = x.reshape(n, D_MODEL)
    mod_b = _ada(c, w_ada, b_ada).transpose(1, 0, 2)

    qk_gains = jnp.stack([
        jnp.tile(a_q_norm_g, A_HEADS) * (ATTN_SCALE * LOG2E), jnp.tile(a_k_norm_g, A_HEADS),
        jnp.tile(b_k_norm_g, 2 * B_HEADS)])
    bq_gain = (jnp.tile(b_q_norm_g, 2 * B_HEADS) * (ATTN_SCALE * LOG2E)).reshape(B_WIDTH, 1)
    *a_parts, bk, bqt, bvt = _inproj(x2, mod_b, norm1_g, w_in, qk_gains, bq_gain, bsz, seq)

    dil_outs = _dilated(a_parts, bsz, seq)
    lam = (jnp.exp(jnp.sum(lambda_q1 * lambda_k1)) - jnp.exp(jnp.sum(lambda_q2 * lambda_k2)) + LAMBDA_INIT)
    slopes_b = 2.0 ** (-8.0 * np.arange(1, B_HEADS + 1) / B_HEADS)
    scalars = jnp.concatenate([lam.reshape(1), jnp.asarray(slopes_b, F32)]).astype(F32)
    bk4 = bk.reshape(bsz, seq // DIFF_BLOCK, DIFF_BLOCK, B_WIDTH)
    o_b = _diff(bqt, bk4, bvt, scalars, diff_norm_g, bsz, seq).reshape(n, B_WIDTH)

    x1, he, idx, rank, cnt = _router(dil_outs, o_b, x2, mod_b, w_out, norm2_g, w_router, b_router, seq)

    n_tiles = n // ROW_TILE
    counts = cnt[:, :, 0].astype(I32)
    seg = (counts + SEG_ALIGN - 1) // SEG_ALIGN * SEG_ALIGN
    loff = jnp.cumsum(seg, axis=1) - seg
    region = jnp.sum(seg, axis=0)
    padded = (region + MOE_TILE - 1) // MOE_TILE * MOE_TILE
    pad_end = jnp.cumsum(padded)
    pad_start = pad_end - padded
    goff = pad_start[None, :] + jnp.cumsum(seg, axis=0) - seg
    sizes = jnp.asarray(SEG_SIZES, I32)[None, :, None]
    has = (seg[:, None, :] & sizes) != 0
    within = seg[:, None, :] & ~(2 * sizes - 1)
    place = jnp.cumsum(has, axis=-1) - has
    pick = has[:, :, None, :] & (place[:, :, None, :] == jnp.arange(N_EXPERTS, dtype=I32)[None, None, :, None])
    listed = lambda rows_: jnp.sum(jnp.where(pick, rows_[:, :, None, :], 0), axis=-1).reshape(-1).astype(I32)
    tables = (listed(loff[:, None, :] + within), listed(goff[:, None, :] + within),
              jnp.sum(has, axis=-1).reshape(-1).astype(I32), jnp.sum(seg, axis=1).astype(I32))
    n_blocks = (n * TOP_K + n_tiles * N_EXPERTS * (SEG_ALIGN - 1) + N_EXPERTS * (MOE_TILE - 1)) // MOE_TILE
    n_used = (pad_end[-1] // MOE_TILE).astype(I32)
    tail = jnp.concatenate([pad_start + region, padded - region, n_used.reshape(1)])

    ls, xs = _sort(tables, tail, he, idx, rank, loff.reshape(n_tiles, N_EXPERTS, 1), n_blocks * MOE_TILE)
    yb = _experts(pad_start // MOE_TILE, padded // MOE_TILE, xs, w_gate_up, b_gate_up, w_down, b_down)
    out = _combine(tables, yb, ls.T, x1, mod_b, seq)
    return out.reshape(bsz, seq, D_MODEL)


def kernel(x, c, w_ada, b_ada, norm1_g, w_in, a_q_norm_g, a_k_norm_g, b_q_norm_g, b_k_norm_g, lambda_q1, lambda_k1,
           lambda_q2, lambda_k2, diff_norm_g, w_out, norm2_g, w_router, b_router, w_gate_up, b_gate_up, w_down,
           b_down):
    args = (w_ada, b_ada, norm1_g, w_in, a_q_norm_g, a_k_norm_g, b_q_norm_g, b_k_norm_g, lambda_q1, lambda_k1,
            lambda_q2, lambda_k2, diff_norm_g, w_out, norm2_g, w_router, b_router, w_gate_up, b_gate_up, w_down,
            b_down)
    return _layer(x, c, *[a[0] for a in args])
```

```python
import functools

import numpy as np
import jax
import jax.numpy as jnp
from jax import lax
from jax.experimental import pallas as pl
from jax.experimental.pallas import tpu as pltpu

F32 = jnp.float32
BF16 = jnp.bfloat16
I32 = jnp.int32
U32 = jnp.uint32

D_MODEL = 1024
HEAD_DIM = 64
A_WIDTH = 512
B_WIDTH = 512
A_HEADS = 8
B_HEADS = 4
IN_WIDTH = 3072
DIL_PATTERNS = ((128, 1), (512, 4), (2048, 16))
BAND = 128
N_EXPERTS = 32
TOP_K = 4
D_FF = 1024
SWIGLU_LIMIT = 7.0
SWIGLU_ALPHA = 1.702
RMS_EPS = 1e-6
ATTN_SCALE = HEAD_DIM ** -0.5
LOG2E = 1.4426950408889634
LN2 = 0.6931471805599453
LAMBDA_INIT = 0.8 - 0.6 * 1.0

LANES = 128
ROW_TILE = 512
DIFF_BLOCK = 512
DIFF_LANES = 256
DIFF_ROWS = 64
ONES_ROWS = 16
VT_ROWS = B_HEADS * (2 * HEAD_DIM + ONES_ROWS)
MOE_TILE = 256
X_AHEAD = 4
X_SLOTS = X_AHEAD + 1
EXTRA_LANES = 128
ROW_WIDTH = D_MODEL + EXTRA_LANES
SEG_ALIGN = 16
SEG_SIZES = (512, 256, 128, 64, 32, 16)
SLOT_CHUNK = 512
LOCAL_SLOTS = 2560
LOCAL_PIECES = LOCAL_SLOTS // SEG_ALIGN
assert ROW_TILE == DIFF_BLOCK == SEG_SIZES[0]
assert LOCAL_SLOTS >= ROW_TILE * TOP_K + N_EXPERTS * (SEG_ALIGN - 1) and LOCAL_SLOTS % SLOT_CHUNK == 0
VMEM_LIMIT = 56 * 1024 * 1024


def _cparams(sem, **flags):
    return pltpu.CompilerParams(dimension_semantics=sem, vmem_limit_bytes=VMEM_LIMIT, flags=flags or None)


def _split_bf16(a):
    hi = a.astype(BF16)
    lo = (a - hi.astype(F32)).astype(BF16)
    return hi, lo


def _dot_nt(a, b):
    return lax.dot_general(a, b, (((1,), (1,)), ((), ())), preferred_element_type=F32)


def _dot(a, b):
    return jnp.dot(a, b, preferred_element_type=F32)


def _ada_kernel(c_ref, w_ref, b_ref, o_ref):
    c = c_ref[...]
    s = c / (1.0 + jnp.exp(-c))
    sh, sl = _split_bf16(s)
    wh, wl = _split_bf16(w_ref[...])
    o_ref[0] = _dot(sh, wh) + _dot(sh, wl) + _dot(sl, wh) + b_ref[0]


def _ada(c, w_ada, b_ada):
    bsz = c.shape[0]
    return pl.pallas_call(
        _ada_kernel,
        grid=(6,),
        in_specs=[
            pl.BlockSpec((bsz, D_MODEL), lambda j: (0, 0)),
            pl.BlockSpec((D_MODEL, D_MODEL), lambda j: (0, j)),
            pl.BlockSpec((1, 1, D_MODEL), lambda j: (j, 0, 0)),
        ],
        out_specs=pl.BlockSpec((1, bsz, D_MODEL), lambda j: (j, 0, 0)),
        out_shape=jax.ShapeDtypeStruct((6, bsz, D_MODEL), F32),
        compiler_params=_cparams(("arbitrary",)),
        name="ada",
    )(c, w_ada, b_ada.reshape(6, 1, D_MODEL))


def _inproj_kernel(x_ref, mod_ref, g1_ref, wn_ref, wt_ref, gm_ref, qkg_ref, bqg_ref, a1_ref, a4_ref, a16_ref, bk_ref,
                   bqt_ref, bvt_ref, a_sc):
    a_refs = (a1_ref, a4_ref, a16_ref)
    x = x_ref[...]
    y = x * lax.rsqrt(jnp.mean(x * x, axis=-1, keepdims=True) + RMS_EPS) * g1_ref[...]
    h = (y * (1.0 + mod_ref[1:2, :]) + mod_ref[0:1, :]).astype(BF16)
    p = _dot(h, wn_ref[...])
    gm = gm_ref[...]

    def head_norm(t, g):
        sq = (t * t).astype(BF16)
        half = gm.shape[0]
        ss = jnp.concatenate([_dot(sq[:, :half], gm), _dot(sq[:, half:], gm)], axis=1)
        return t * lax.rsqrt(ss * (1.0 / HEAD_DIM) + RMS_EPS) * g

    w = A_WIDTH
    a_part = jnp.concatenate([head_norm(p[:, 0 * w:1 * w], qkg_ref[0:1, :]),
                              head_norm(p[:, 1 * w:2 * w], qkg_ref[1:2, :]), p[:, 2 * w:3 * w]], axis=1)
    n_col = a_sc.shape[0]
    for c in range(n_col):
        a_sc[c] = a_part[:, c * LANES:(c + 1) * LANES]
    for a_ref, (_, dil) in zip(a_refs, DIL_PATTERNS):
        if dil == 1:
            a_ref[0] = a_part.astype(BF16)
            continue
        rows = a_part.shape[0] // dil
        for r in range(dil):
            for c in range(n_col):
                a_ref[r, :, c * LANES:(c + 1) * LANES] = a_sc[c, pl.ds(r, rows, stride=dil), :].astype(BF16)
    bk_ref[...] = head_norm(p[:, 3 * w:4 * w], qkg_ref[2:3, :]).astype(BF16)

    pt = _dot_nt(wt_ref[...], h)
    t = pt.shape[1]
    bq = pt[:w].reshape(w // HEAD_DIM, HEAD_DIM, t)
    ss = jnp.sum(bq * bq, axis=1, keepdims=True)
    bq = (bq * lax.rsqrt(ss * (1.0 / HEAD_DIM) + RMS_EPS)).reshape(w, t) * bqg_ref[...]
    bqt_ref[...] = bq.astype(BF16)
    lanes = 2 * HEAD_DIM
    for hd in range(B_HEADS):
        base = hd * (lanes + ONES_ROWS)
        bvt_ref[base:base + lanes, :] = pt[w + hd * lanes:w + (hd + 1) * lanes].astype(BF16)
        bvt_ref[base + lanes:base + lanes + ONES_ROWS, :] = jnp.ones((ONES_ROWS, t), BF16)


def _inproj(x2, mod_b, norm1_g, w_in, qk_gains, bq_gain, bsz, seq):
    n = x2.shape[0]
    t = ROW_TILE
    per_b = seq // t
    w = A_WIDTH
    head_of_lane = np.arange(w // 2) // HEAD_DIM
    gmat = jnp.asarray(head_of_lane[:, None] == head_of_lane[None, :], BF16)
    w_bf = w_in.astype(BF16)
    w_nat = jnp.concatenate([w_bf[:, :3 * w], w_bf[:, 4 * w:5 * w]], axis=1)
    w_tr = jnp.concatenate([w_bf[:, 3 * w:4 * w], w_bf[:, 5 * w:]], axis=1).T
    tr_spec = pl.BlockSpec((None, None, w, t), lambda i: (i // per_b, i % per_b, 0, 0))
    tr_shape = jax.ShapeDtypeStruct((bsz, per_b, w, t), BF16)
    a_specs = [pl.BlockSpec((None, dil, t // dil, 3 * w), lambda i: (i // per_b, 0, i % per_b, 0))
               for _, dil in DIL_PATTERNS]
    a_shapes = [jax.ShapeDtypeStruct((bsz, dil, seq // dil, 3 * w), BF16) for _, dil in DIL_PATTERNS]
    return pl.pallas_call(
        _inproj_kernel,
        grid=(n // t,),
        in_specs=[
            pl.BlockSpec((t, D_MODEL), lambda i: (i, 0)),
            pl.BlockSpec((None, 6, D_MODEL), lambda i: (i // per_b, 0, 0)),
            pl.BlockSpec((1, D_MODEL), lambda i: (0, 0)),
            pl.BlockSpec((D_MODEL, 4 * w), lambda i: (0, 0)),
            pl.BlockSpec((2 * w, D_MODEL), lambda i: (0, 0)),
            pl.BlockSpec((w // 2, w // 2), lambda i: (0, 0)),
            pl.BlockSpec((3, w), lambda i: (0, 0)),
            pl.BlockSpec((w, 1), lambda i: (0, 0)),
        ],
        out_specs=a_specs + [pl.BlockSpec((t, w), lambda i: (i, 0)), tr_spec,
                             pl.BlockSpec((None, None, VT_ROWS, t), lambda i: (i // per_b, i % per_b, 0, 0))],
        out_shape=a_shapes + [jax.ShapeDtypeStruct((n, w), BF16), tr_shape,
                              jax.ShapeDtypeStruct((bsz, per_b, VT_ROWS, t), BF16)],
        scratch_shapes=[pltpu.VMEM((3 * w // LANES, t, LANES), F32)],
        compiler_params=_cparams(("parallel",)),
        name="inproj",
    )(x2, mod_b, norm1_g.reshape(1, D_MODEL), w_nat, w_tr, gmat, qk_gains, bq_gain)


def _dilated_bias_table():
    ik = np.arange(2 * BAND)[:, None]
    iq = np.arange(BAND)[None, :]
    delta = iq - ik + BAND
    in_band = (delta >= 0) & (delta <= BAND)
    slopes = 2.0 ** (-8.0 * np.arange(1, A_HEADS + 1) / A_HEADS)
    tbl = np.zeros((len(DIL_PATTERNS), 2, A_HEADS // 2, 2 * BAND, 2 * BAND), np.float32)
    for p, (_, dil) in enumerate(DIL_PATTERNS):
        for first in range(2):
            valid = in_band & ((ik >= BAND) if first else True)
            for h in range(A_HEADS):
                cols = slice(BAND * (h % 2), BAND * (h % 2 + 1))
                tbl[p, first, h // 2, :, cols] = np.where(valid, -slopes[h] * LOG2E * (delta * dil), -np.inf)
    return jnp.asarray(tbl)


def _dilated_kernel(tbl_ref, *refs, n_blocks):
    lanes = 2 * HEAD_DIM
    n = pl.program_id(1)
    n_pat = len(DIL_PATTERNS)
    s_sc, p_sc = refs[len(refs) - 2:]
    outs = refs[len(refs) - 2 - 2 * n_pat:len(refs) - 2]
    lane = lax.broadcasted_iota(I32, (BAND, lanes), 1)
    low = lane < HEAD_DIM
    work = []
    pos = 0
    for p, (_, dil) in enumerate(DIL_PATTERNS):
        per_res = n_blocks // dil
        if per_res > 1:
            cur_ref, prev_ref = refs[pos:pos + 2]
            pos += 2
            first = (n % per_res == 0).astype(I32)
        else:
            cur_ref, prev_ref, first = refs[pos], None, None
            pos += 1
        for j in range(A_HEADS // 2):
            work.append((p, j, cur_ref, prev_ref, first))

    def keys_of(prev_ref, cur_ref, cs):
        return cur_ref[:, cs] if prev_ref is None else jnp.concatenate([prev_ref[:, cs], cur_ref[:, cs]], axis=0)

    def cols_of(part, j):
        return slice(part * A_WIDTH + lanes * j, part * A_WIDTH + lanes * (j + 1))

    for w, (p, j, cur_ref, prev_ref, first) in enumerate(work):
        q2 = cur_ref[:, cols_of(0, j)]
        zero = jnp.zeros_like(q2)
        qcat = jnp.concatenate([jnp.where(low, q2, zero), jnp.where(low, zero, q2)], axis=0)
        k2 = keys_of(prev_ref, cur_ref, cols_of(1, j))
        bias = tbl_ref[p, 1, j, BAND:, :] if prev_ref is None else tbl_ref[p, first, j]
        s_sc[w, :k2.shape[0], :] = _dot_nt(k2, qcat) + bias
    lses = []
    for w, (p, j, _, prev_ref, _) in enumerate(work):
        nk = BAND if prev_ref is None else 2 * BAND
        s = s_sc[w, :nk, :]
        m = jnp.max(s, axis=0, keepdims=True)
        e = jnp.exp2(s - m)
        den = jnp.sum(e, axis=0, keepdims=True)
        p_sc[w, :nk, :] = (e * (1.0 / den)).astype(BF16)
        lses.append((m + jnp.log2(den)) * LN2)
    for w, (p, j, cur_ref, prev_ref, _) in enumerate(work):
        nk = BAND if prev_ref is None else 2 * BAND
        cs = cols_of(0, j)
        v2 = keys_of(prev_ref, cur_ref, cols_of(2, j))
        r = lax.dot_general(p_sc[w, :nk, :], v2, (((0,), (0,)), ((), ())), preferred_element_type=F32)
        outs[2 * p][:, cs] = jnp.where(low, r[:BAND], r[BAND:]).astype(BF16)
    for p in range(n_pat):
        rows = []
        for lse in lses[p * (A_HEADS // 2):(p + 1) * (A_HEADS // 2)]:
            rows += [lse[:, :BAND], lse[:, BAND:]]
        rows.append(jnp.zeros((lanes - A_HEADS, BAND), F32))
        outs[2 * p + 1][...] = jnp.concatenate(rows, axis=0).T


def _dilated(a_parts, bsz, seq):
    n_blocks = seq // BAND
    in_specs = [pl.BlockSpec((len(DIL_PATTERNS), 2, A_HEADS // 2, 2 * BAND, 2 * BAND), lambda b, n: (0, 0, 0, 0, 0))]
    n_work = len(DIL_PATTERNS) * A_HEADS // 2
    args = [_dilated_bias_table()]
    out_specs, out_shapes = [], []
    blk = (None, None, BAND, A_WIDTH)
    qkv = (None, None, BAND, 3 * A_WIDTH)
    for a_part, (_, dil) in zip(a_parts, DIL_PATTERNS):
        per_res = n_blocks // dil
        cur = lambda b, n, per_res=per_res: (b, n // per_res, n % per_res, 0)
        prev = lambda b, n, per_res=per_res: (b, n // per_res, jnp.maximum(n % per_res - 1, 0), 0)
        in_specs += [pl.BlockSpec(qkv, cur)] + ([pl.BlockSpec(qkv, prev)] if per_res > 1 else [])
        args += [a_part] * (2 if per_res > 1 else 1)
        out_specs += [pl.BlockSpec(blk, cur), pl.BlockSpec((None, None, BAND, LANES), cur)]
        out_shapes += [jax.ShapeDtypeStruct((bsz, dil, seq // dil, A_WIDTH), BF16),
                       jax.ShapeDtypeStruct((bsz, dil, seq // dil, LANES), F32)]
    return pl.pallas_call(
        functools.partial(_dilated_kernel, n_blocks=n_blocks),
        grid=(bsz, n_blocks),
        in_specs=in_specs,
        out_specs=out_specs,
        out_shape=out_shapes,
        scratch_shapes=[pltpu.VMEM((n_work, 2 * BAND, 2 * BAND), F32), pltpu.VMEM((n_work, 2 * BAND, 2 * BAND), BF16)],
        compiler_params=_cparams(("parallel", "parallel")),
        name="dilated",
    )(*args)


def _diff_bias_table():
    t = DIFF_BLOCK
    rel = np.arange(t)[None, :] - np.arange(t)[:, None]
    slopes = 2.0 ** (-8.0 * np.arange(1, B_HEADS + 1) / B_HEADS)
    tbl = np.zeros((B_HEADS, 2, t, t), np.float32)
    for h in range(B_HEADS):
        tbl[h, 0] = -slopes[h] * LOG2E * rel
        tbl[h, 1] = np.where(rel >= 0, -slopes[h] * LOG2E * rel, -np.inf)
    return jnp.asarray(tbl)


def _diff_kernel(sc_ref, qt_ref, k_ref, vt_ref, g_ref, bias_sc, o_ref, qst_sc, m_sc, l_sc, a_sc, acc_sc, s_sc, p_sc):
    t = DIFF_BLOCK
    lanes = 2 * HEAD_DIM
    qi = pl.program_id(1)
    lam = sc_ref[0]
    slopes = [jnp.full((1, 1), sc_ref[1 + h], F32) for h in range(B_HEADS)]
    row = lax.broadcasted_iota(I32, (lanes, t), 0)
    for h in range(B_HEADS):
        qt = qt_ref[h * lanes:(h + 1) * lanes, :]
        zero = jnp.zeros_like(qt)
        qst_sc[h, :, :t] = jnp.where(row < HEAD_DIM, qt, zero)
        qst_sc[h, :, t:] = jnp.where(row < HEAD_DIM, zero, qt)
    m_sc[...] = jnp.full(m_sc.shape, -jnp.inf, F32)
    l_sc[...] = jnp.zeros(l_sc.shape, F32)
    acc_sc[...] = jnp.zeros(acc_sc.shape, F32)

    n_grp = 2 * t // DIFF_LANES
    items = [(h, g) for h in range(B_HEADS) for g in range(n_grp)]
    chunks = [slice(r * DIFF_ROWS, (r + 1) * DIFF_ROWS) for r in range(t // DIFF_ROWS)]

    def keys_needed(g, diag):
        return min(t, (g * DIFF_LANES) % t + DIFF_LANES) if diag else t

    def block(n, diag):
        cs = [-slopes[h] * LOG2E * (jnp.full((1, 1), (qi - n) * t, I32)).astype(F32) for h in range(B_HEADS)]
        for w, (h, g) in enumerate(items):
            nk = keys_needed(g, diag)
            cols = slice(g * DIFF_LANES, (g + 1) * DIFF_LANES)
            bcols = slice((g * DIFF_LANES) % t, (g * DIFF_LANES) % t + DIFF_LANES)
            s_sc[w, :nk, :] = (_dot(k_ref[n, :nk, h * lanes:(h + 1) * lanes], qst_sc[h, :, cols])
                               + bias_sc[h, diag, :nk, bcols])
        for w, (h, g) in enumerate(items):
            used = chunks[:keys_needed(g, diag) // DIFF_ROWS]
            cols = slice(g * DIFF_LANES, (g + 1) * DIFF_LANES)
            top = s_sc[w, used[0], :]
            for rows in used[1:]:
                top = jnp.maximum(top, s_sc[w, rows, :])
            m_prev = m_sc[h, :, cols]
            m_new = jnp.maximum(m_prev, jnp.max(top, axis=0, keepdims=True) + cs[h])
            alpha = jnp.exp2(m_prev - m_new)
            shift = m_new - cs[h]
            for rows in used:
                p_sc[w, rows, :] = jnp.exp2(s_sc[w, rows, :] - shift).astype(BF16)
            m_sc[h, :, cols] = m_new
            a_sc[h, :, cols] = alpha
        for w, (h, g) in enumerate(items):
            nk = keys_needed(g, diag)
            cols = slice(g * DIFF_LANES, (g + 1) * DIFF_LANES)
            pv = _dot(vt_ref[n, h * (lanes + ONES_ROWS):(h + 1) * (lanes + ONES_ROWS), :nk], p_sc[w, :nk, :])
            alpha = a_sc[h, :, cols]
            acc_sc[h, :, cols] = alpha * acc_sc[h, :, cols] + pv[:lanes]
            l_sc[h, :, cols] = alpha * l_sc[h, :, cols] + pv[lanes:lanes + 1]

    def body(n, carry):
        block(n, 0)
        return carry

    lax.fori_loop(0, qi, body, 0)
    block(qi, 1)
    for h in range(B_HEADS):
        o = acc_sc[h] / l_sc[h]
        o = o[:, :t] - lam * o[:, t:]
        o = o * lax.rsqrt(jnp.mean(o * o, axis=0, keepdims=True) + RMS_EPS) * g_ref[...]
        o_ref[:, h * lanes:(h + 1) * lanes] = (o * (1.0 - LAMBDA_INIT)).T.astype(BF16)


def _diff(bqt, bk, bvt, scalars, diff_norm_g, bsz, seq):
    t = DIFF_BLOCK
    nb = seq // t
    lanes = 2 * HEAD_DIM
    n_items = B_HEADS * 2 * t // DIFF_LANES
    return pl.pallas_call(
        _diff_kernel,
        grid=(bsz, nb),
        in_specs=[
            pl.BlockSpec(memory_space=pltpu.SMEM),
            pl.BlockSpec((None, None, B_WIDTH, t), lambda b, i: (b, i, 0, 0)),
            pl.BlockSpec((None, nb, t, B_WIDTH), lambda b, i: (b, 0, 0, 0)),
            pl.BlockSpec((None, nb, VT_ROWS, t), lambda b, i: (b, 0, 0, 0)),
            pl.BlockSpec((lanes, 1), lambda b, i: (0, 0)),
            pl.BlockSpec((B_HEADS, 2, t, t), lambda b, i: (0, 0, 0, 0)),
        ],
        out_specs=pl.BlockSpec((None, t, B_WIDTH), lambda b, i: (b, i, 0)),
        out_shape=jax.ShapeDtypeStruct((bsz, seq, B_WIDTH), BF16),
        scratch_shapes=[pltpu.VMEM((B_HEADS, lanes, 2 * t), BF16)] + [pltpu.VMEM((B_HEADS, 1, 2 * t), F32)] * 3 + [
            pltpu.VMEM((B_HEADS, lanes, 2 * t), F32),
            pltpu.VMEM((n_items, t, DIFF_LANES), F32), pltpu.VMEM((n_items, t, DIFF_LANES), BF16)],
        compiler_params=_cparams(("parallel", "parallel")),
        name="diff",
    )(scalars, bqt, bk, bvt, diff_norm_g.reshape(lanes, 1), _diff_bias_table())


def _router_kernel(o0, l0, o1, l1, o2, l2, ob_ref, x_ref, mod_ref, wout_ref, g2_ref, wrh_ref, wrl_ref, br_ref,
                   tri_ref, spread_ref, x1_ref, he_ref, idx_ref, rank_ref, cnt_ref, *order_sc):
    def token_order(ref, scratch):
        dil, rows, width = ref.shape
        if dil == 1:
            return ref[0].astype(F32)
        n_col = width // LANES
        for r in range(dil):
            for c in range(n_col):
                scratch[c, pl.ds(r, rows, stride=dil), :] = ref[r, :, c * LANES:(c + 1) * LANES].astype(F32)
        return jnp.concatenate([scratch[c] for c in range(n_col)], axis=1)

    os_ = [token_order(o0, None), token_order(o1, order_sc[0]), token_order(o2, order_sc[2])]
    ls = [token_order(l0, None), token_order(l1, order_sc[1]), token_order(l2, order_sc[3])]
    mx = jnp.maximum(jnp.maximum(ls[0], ls[1]), ls[2])
    ws = [jnp.exp(l - mx) for l in ls]
    den = ws[0] + ws[1] + ws[2]
    spread = spread_ref[...]
    oa = jnp.zeros(os_[0].shape, F32)
    for w, o in zip(ws, os_):
        wh, wl = _split_bf16(w / den)
        oa = oa + (_dot(wh, spread) + _dot(wl, spread)) * o
    mixed = _dot(oa.astype(BF16), wout_ref[:A_WIDTH, :]) + _dot(ob_ref[...], wout_ref[A_WIDTH:, :])
    x1 = x_ref[...] + mod_ref[2:3, :] * mixed
    x1_ref[...] = x1
    y = x1 * lax.rsqrt(jnp.mean(x1 * x1, axis=-1, keepdims=True) + RMS_EPS) * g2_ref[...]
    h2 = y * (1.0 + mod_ref[4:5, :]) + mod_ref[3:4, :]
    he_ref[:, :D_MODEL] = h2.astype(BF16)

    hh, hl = _split_bf16(h2)
    wrh = wrh_ref[...]
    logits = _dot_nt(wrh, hh) + _dot_nt(wrh, hl) + _dot_nt(wrl_ref[...], hh) + br_ref[...]
    t = logits.shape[1]
    eid = lax.broadcasted_iota(I32, (N_EXPERTS, t), 0)
    vals, idxs, hots = [], [], []
    cur = logits
    for _ in range(TOP_K):
        v = jnp.max(cur, axis=0, keepdims=True)
        ik = jnp.min(jnp.where(cur == v, eid, N_EXPERTS), axis=0, keepdims=True)
        hot = eid == ik
        vals.append(v)
        idxs.append(ik)
        hots.append(hot)
        cur = jnp.where(hot, -jnp.inf, cur)
    es = [jnp.exp(v - vals[0]) for v in vals]
    esum = es[0] + es[1] + es[2] + es[3]
    idx_ref[...] = jnp.concatenate(idxs, axis=0)

    rows = [ik.astype(F32) for ik in idxs]
    for e in es:
        g = e / esum
        hi = g.astype(BF16).astype(F32)
        mid = (g - hi).astype(BF16).astype(F32)
        rows += [hi, mid, g - hi - mid]
    rows.append(jnp.zeros((EXTRA_LANES - len(rows), t), F32))
    he_ref[:, D_MODEL:] = jnp.concatenate(rows, axis=0).T.astype(BF16)

    sel = jnp.where(hots[0] | hots[1] | hots[2] | hots[3], 1.0, 0.0)
    before = _dot(sel.astype(BF16), tri_ref[...])
    ranks = [jnp.sum(jnp.where(hot, before, 0.0), axis=0, keepdims=True) for hot in hots]
    rank_ref[...] = jnp.concatenate(ranks, axis=0).astype(I32)
    cnt_ref[...] = jnp.broadcast_to(jnp.sum(sel, axis=1, keepdims=True), cnt_ref.shape)


def _router(dil_outs, o_b, x2, mod_b, w_out, norm2_g, w_router, b_router, seq):
    n = x2.shape[0]
    t = ROW_TILE
    per_b = seq // t
    wr_t = w_router.T
    wrh = wr_t.astype(BF16)
    wrl = (wr_t - wrh.astype(F32)).astype(BF16)
    tri = jnp.asarray(np.arange(t)[:, None] < np.arange(t)[None, :], BF16)
    row = lambda w: pl.BlockSpec((t, w), lambda i: (i, 0))
    full = lambda a, b: pl.BlockSpec((a, b), lambda i: (0, 0))
    tok = lambda: pl.BlockSpec((TOP_K, t), lambda i: (0, i))
    grouped = lambda dil, w: pl.BlockSpec((None, dil, t // dil, w), lambda i: (i // per_b, 0, i % per_b, 0))
    widths = (A_WIDTH, LANES)
    head_of_lane = np.arange(A_WIDTH) // HEAD_DIM
    spread = jnp.asarray(np.arange(LANES)[:, None] == head_of_lane[None, :], BF16)
    return pl.pallas_call(
        _router_kernel,
        grid=(n // t,),
        in_specs=[grouped(dil, w) for _, dil in DIL_PATTERNS for w in widths] + [
            row(B_WIDTH), row(D_MODEL),
            pl.BlockSpec((None, 6, D_MODEL), lambda i: (i // per_b, 0, 0)),
            full(D_MODEL, D_MODEL), full(1, D_MODEL), full(N_EXPERTS, D_MODEL), full(N_EXPERTS, D_MODEL),
            full(N_EXPERTS, 1), full(t, t), full(LANES, A_WIDTH),
        ],
        out_specs=[row(D_MODEL), row(ROW_WIDTH), tok(), tok(),
                   pl.BlockSpec((None, N_EXPERTS, 128), lambda i: (i, 0, 0))],
        out_shape=[
            jax.ShapeDtypeStruct((n, D_MODEL), F32),
            jax.ShapeDtypeStruct((n, ROW_WIDTH), BF16),
            jax.ShapeDtypeStruct((TOP_K, n), I32),
            jax.ShapeDtypeStruct((TOP_K, n), I32),
            jax.ShapeDtypeStruct((n // t, N_EXPERTS, 128), F32),
        ],
        scratch_shapes=[pltpu.VMEM((w // LANES, t, LANES), F32) for _, dil in DIL_PATTERNS if dil > 1 for w in widths],
        compiler_params=_cparams(("parallel",)),
        name="router",
    )(*dil_outs, o_b, x2, mod_b, w_out.astype(BF16), norm2_g.reshape(1, D_MODEL), wrh, wrl,
      b_router.reshape(N_EXPERTS, 1), tri, spread)


def _start_pieces(tables, tile, local_ref, hbm_ref, sem, outbound):
    lo_ref, go_ref, cnt_ref, _ = tables
    for c, size in enumerate(SEG_SIZES):
        cls = tile * len(SEG_SIZES) + c

        def per_piece(k, carry, size=size, cls=cls):
            lo, go = lo_ref[cls * N_EXPERTS + k], go_ref[cls * N_EXPERTS + k]
            loc = local_ref.at[pl.ds(pl.multiple_of(lo, SEG_ALIGN), size)]
            glob = hbm_ref.at[pl.ds(pl.multiple_of(go, SEG_ALIGN), size)]
            cp = pltpu.make_async_copy(loc, glob, sem) if outbound else pltpu.make_async_copy(glob, loc, sem)
            cp.start()
            return carry

        lax.fori_loop(0, cnt_ref[cls], per_piece, 0)


def _start(cp):
    cp.start()


def _wait(cp):
    cp.wait()


def _tile_rows(tables, tile):
    return tables[3][tile]


def _wait_rows(rows, local_ref, hbm_ref, sem, outbound):
    size = 1 << (LOCAL_SLOTS.bit_length() - 1)
    while size >= SEG_ALIGN:
        loc, glob = local_ref.at[pl.ds(0, size)], hbm_ref.at[pl.ds(0, size)]
        cp = pltpu.make_async_copy(loc, glob, sem) if outbound else pltpu.make_async_copy(glob, loc, sem)
        pl.when((rows & size) != 0)(cp.wait)
        size //= 2


def _one_hot_any(j, targets):
    out = jnp.zeros(j.shape, F32)
    for tgt in targets:
        out = jnp.where(j == tgt, 1.0, out)
    return out


def _sort_kernel(lo_ref, go_ref, cnt_ref, rows_ref, tail_ref, he_ref, idx_ref, rank_ref, lcol_ref, ls_ref, xs_hbm,
                 xl, zbuf, sems, zsem):
    tables = (lo_ref, go_ref, cnt_ref, rows_ref)
    i = pl.program_id(0)
    last = pl.num_programs(0) - 1
    slot = i % 2
    t = he_ref.shape[0]

    @pl.when(i == 0)
    def _():
        zbuf[...] = jnp.zeros(zbuf.shape, BF16)

        def tails(action):
            def per_expert(e, carry):
                off, n = tail_ref[e], tail_ref[N_EXPERTS + e]
                done = jnp.int32(0)
                for size in SEG_SIZES:
                    if size < MOE_TILE:
                        take = (n & size) != 0
                        dst = xs_hbm.at[pl.ds(pl.multiple_of(off + done, SEG_ALIGN), size)]
                        pl.when(take)(functools.partial(action, pltpu.make_async_copy(zbuf.at[pl.ds(0, size)], dst, zsem)))
                        done = done + jnp.where(take, size, 0)
                return carry

            lax.fori_loop(0, N_EXPERTS, per_expert, 0)

            def per_block(b, carry):
                dst = xs_hbm.at[pl.ds(pl.multiple_of(b * MOE_TILE, MOE_TILE), MOE_TILE)]
                action(pltpu.make_async_copy(zbuf, dst, zsem))
                return carry

            lax.fori_loop(tail_ref[2 * N_EXPERTS], xs_hbm.shape[0] // MOE_TILE, per_block, 0)

        tails(_start)
        tails(_wait)

    eid = lax.broadcasted_iota(I32, (N_EXPERTS, t), 0)
    lcol = lcol_ref[...]
    ls = []
    for k in range(TOP_K):
        off = jnp.sum(jnp.where(eid == idx_ref[k:k + 1, :], lcol, 0), axis=0, keepdims=True)
        ls.append(off + rank_ref[k:k + 1, :])
    ls_ref[...] = jnp.concatenate(ls, axis=0)

    he = he_ref[...]
    for jc in range(LOCAL_SLOTS // SLOT_CHUNK):
        j = lax.broadcasted_iota(I32, (SLOT_CHUNK, t), 0) + jc * SLOT_CHUNK
        perm = _one_hot_any(j, ls).astype(BF16)
        xl[slot, jc * SLOT_CHUNK:(jc + 1) * SLOT_CHUNK, :] = _dot(perm, he).astype(BF16)

    _start_pieces(tables, i, xl.at[slot], xs_hbm, sems.at[slot], True)

    @pl.when(i > 0)
    def _():
        _wait_rows(_tile_rows(tables, i - 1), xl.at[1 - slot], xs_hbm, sems.at[1 - slot], True)

    @pl.when(i == last)
    def _():
        _wait_rows(_tile_rows(tables, i), xl.at[slot], xs_hbm, sems.at[slot], True)


def _sort(tables, tail, he, idx, rank, lcol, n_rows):
    n = he.shape[0]
    t = ROW_TILE
    tok = lambda: pl.BlockSpec((TOP_K, t), lambda i, *_: (0, i))
    return pl.pallas_call(
        _sort_kernel,
        grid_spec=pltpu.PrefetchScalarGridSpec(
            num_scalar_prefetch=5,
            grid=(n // t,),
            in_specs=[
                pl.BlockSpec((t, ROW_WIDTH), lambda i, *_: (i, 0)),
                tok(), tok(),
                pl.BlockSpec((None, N_EXPERTS, 1), lambda i, *_: (i, 0, 0)),
            ],
            out_specs=[tok(), pl.BlockSpec(memory_space=pl.ANY)],
            scratch_shapes=[pltpu.VMEM((2, LOCAL_SLOTS, ROW_WIDTH), BF16), pltpu.VMEM((MOE_TILE, ROW_WIDTH), BF16),
                            pltpu.SemaphoreType.DMA((2,)), pltpu.SemaphoreType.DMA(())],
        ),
        out_shape=[jax.ShapeDtypeStruct((TOP_K, n), I32), jax.ShapeDtypeStruct((n_rows, ROW_WIDTH), BF16)],
        compiler_params=_cparams(("arbitrary",)),
        name="sort",
    )(*tables, tail, he, idx, rank, lcol)


def _experts_kernel(blk0_ref, nblk_ref, xs_hbm, wgu_ref, bgu_ref, wd_ref, bd_ref, yb_hbm, wgu_sc, wd_sc, xbuf, ybuf,
                    xsem, ysem):
    e = pl.program_id(0)
    tm = MOE_TILE
    first, nb = blk0_ref[e], nblk_ref[e]
    last_e = pl.num_programs(0) - 1
    total = blk0_ref[last_e] + nblk_ref[last_e]

    def rows(b):
        return pl.ds(pl.multiple_of(b * tm, tm), tm)

    def x_copy(b):
        slot = b % X_SLOTS
        return pltpu.make_async_copy(xs_hbm.at[rows(b)], xbuf.at[slot], xsem.at[slot])

    def y_copy(b, slot):
        return pltpu.make_async_copy(ybuf.at[slot], yb_hbm.at[rows(b)], ysem.at[slot])

    @pl.when(e == 0)
    def _():
        for b in range(X_AHEAD):
            pl.when(b < total)(x_copy(b).start)

    def cast(r, carry):
        s = pl.multiple_of(r * LANES, LANES)
        wgu_sc[pl.ds(s, LANES), :] = wgu_ref[pl.ds(s, LANES), :].astype(BF16)
        wd_sc[pl.ds(s, LANES), :] = wd_ref[pl.ds(s, LANES), :].astype(BF16)
        return carry

    lax.fori_loop(0, D_MODEL // LANES, cast, 0)
    me = jnp.full((1, 1), e, I32).astype(F32)

    def block(i, carry):
        b = first + i
        slot = i % 2
        xslot = b % X_SLOTS

        @pl.when(b + X_AHEAD < total)
        def _():
            x_copy(b + X_AHEAD).start()

        x_copy(b).wait()

        @pl.when(i >= 2)
        def _():
            y_copy(b - 2, slot).wait()

        ext = xbuf[xslot, :, D_MODEL:].astype(F32)
        gate = jnp.zeros((tm, 1), F32)
        for k in range(TOP_K):
            c = TOP_K + 3 * k
            gk = ext[:, c:c + 1] + ext[:, c + 1:c + 2] + ext[:, c + 2:c + 3]
            gate = gate + jnp.where(ext[:, k:k + 1] == me, gk, 0.0)

        gu = _dot(xbuf[xslot, :, :D_MODEL], wgu_sc[...]) + bgu_ref[...]
        g = jnp.minimum(gu[:, :D_FF], SWIGLU_LIMIT)
        u = jnp.clip(gu[:, D_FF:], -SWIGLU_LIMIT, SWIGLU_LIMIT)
        act = (u + 1.0) * (g / (1.0 + jnp.exp(-SWIGLU_ALPHA * g)))
        ybuf[slot] = (gate * (_dot(act.astype(BF16), wd_sc[...]) + bd_ref[...])).astype(BF16)
        y_copy(b, slot).start()
        return carry

    lax.fori_loop(0, nb, block, 0)

    @pl.when(nb >= 2)
    def _():
        y_copy(first + nb - 2, nb % 2).wait()

    @pl.when(nb >= 1)
    def _():
        y_copy(first + nb - 1, (nb - 1) % 2).wait()

    @pl.when(e == last_e)
    def _():
        ybuf[0] = jnp.zeros((tm, D_MODEL), BF16)

        def fill(action):
            def per_block(b, carry):
                action(pltpu.make_async_copy(ybuf.at[0], yb_hbm.at[rows(b)], ysem.at[0]))
                return carry

            lax.fori_loop(total, yb_hbm.shape[0] // tm, per_block, 0)

        fill(_start)
        fill(_wait)


def _experts(first_block, n_block, xs, w_gate_up, b_gate_up, w_down, b_down):
    n_rows = xs.shape[0]
    exp3 = lambda e, *_: (e, 0, 0)
    return pl.pallas_call(
        _experts_kernel,
        grid_spec=pltpu.PrefetchScalarGridSpec(
            num_scalar_prefetch=2,
            grid=(N_EXPERTS,),
            in_specs=[
                pl.BlockSpec(memory_space=pl.ANY),
                pl.BlockSpec((None, D_MODEL, 2 * D_FF), exp3),
                pl.BlockSpec((None, 1, 2 * D_FF), exp3),
                pl.BlockSpec((None, D_FF, D_MODEL), exp3),
                pl.BlockSpec((None, 1, D_MODEL), exp3),
            ],
            out_specs=pl.BlockSpec(memory_space=pl.ANY),
            scratch_shapes=[pltpu.VMEM((D_MODEL, 2 * D_FF), BF16), pltpu.VMEM((D_FF, D_MODEL), BF16),
                            pltpu.VMEM((X_SLOTS, MOE_TILE, ROW_WIDTH), BF16), pltpu.VMEM((2, MOE_TILE, D_MODEL), BF16),
                            pltpu.SemaphoreType.DMA((X_SLOTS,)), pltpu.SemaphoreType.DMA((2,))],
        ),
        out_shape=jax.ShapeDtypeStruct((n_rows, D_MODEL), BF16),
        compiler_params=_cparams(("arbitrary",)),
        name="experts",
    )(first_block, n_block, xs, w_gate_up, b_gate_up.reshape(N_EXPERTS, 1, 2 * D_FF), w_down,
      b_down.reshape(N_EXPERTS, 1, D_MODEL))


def _combine_kernel(lo_ref, go_ref, cnt_ref, rows_ref, yb_hbm, lst_ref, x1_ref, mod_ref, o_ref, ybuf, sems):
    i = pl.program_id(0)
    slot = i % 2
    tables = (lo_ref, go_ref, cnt_ref, rows_ref)

    @pl.when(i == 0)
    def _():
        ybuf[...] = jnp.zeros(ybuf.shape, BF16)
        _start_pieces(tables, 0, ybuf.at[0], yb_hbm, sems.at[0], False)

    @pl.when(i + 1 < pl.num_programs(0))
    def _():
        _start_pieces(tables, i + 1, ybuf.at[1 - slot], yb_hbm, sems.at[1 - slot], False)

    _wait_rows(_tile_rows(tables, i), ybuf.at[slot], yb_hbm, sems.at[slot], False)

    lst = lst_ref[...]
    t = lst.shape[0]
    targets = [lst[:, k:k + 1] for k in range(TOP_K)]
    y = jnp.zeros((t, D_MODEL), F32)
    for jc in range(LOCAL_SLOTS // SLOT_CHUNK):
        j = lax.broadcasted_iota(I32, (t, SLOT_CHUNK), 1) + jc * SLOT_CHUNK
        pick = _one_hot_any(j, targets).astype(BF16)
        y = y + _dot(pick, ybuf[slot, jc * SLOT_CHUNK:(jc + 1) * SLOT_CHUNK, :])
    o_ref[...] = x1_ref[...] + mod_ref[5:6, :] * y


def _combine(tables, yb, ls_t, x1, mod_b, seq):
    n = x1.shape[0]
    t = ROW_TILE
    per_b = seq // t
    return pl.pallas_call(
        _combine_kernel,
        grid_spec=pltpu.PrefetchScalarGridSpec(
            num_scalar_prefetch=4,
            grid=(n // t,),
            in_specs=[
                pl.BlockSpec(memory_space=pl.ANY),
                pl.BlockSpec((t, TOP_K), lambda i, *_: (i, 0)),
                pl.BlockSpec((t, D_MODEL), lambda i, *_: (i, 0)),
                pl.BlockSpec((None, 6, D_MODEL), lambda i, *_: (i // per_b, 0, 0)),
            ],
            out_specs=pl.BlockSpec((t, D_MODEL), lambda i, *_: (i, 0)),
            scratch_shapes=[pltpu.VMEM((2, LOCAL_SLOTS, D_MODEL), BF16), pltpu.SemaphoreType.DMA((2,))],
        ),
        out_shape=jax.ShapeDtypeStruct((n, D_MODEL), F32),
        compiler_params=_cparams(("arbitrary",)),
        name="combine",
    )(*tables, yb, ls_t, x1, mod_b)


def _layer(x, c, w_ada, b_ada, norm1_g, w_in, a_q_norm_g, a_k_norm_g, b_q_norm_g, b_k_norm_g, lambda_q1,
           lambda_k1, lambda_q2, lambda_k2, diff_norm_g, w_out, norm2_g, w_router, b_router, w_gate_up,
           b_gate_up, w_down, b_down):
    bsz, seq, _ = x.shape
    n = bsz * seq
    x2 = x.reshape(n, D_MODEL)
    mod_b = _ada(c, w_ada, b_ada).transpose(1, 0, 2)

    qk_gains = jnp.stack([
        jnp.tile(a_q_norm_g, A_HEADS) * (ATTN_SCALE * LOG2E), jnp.tile(a_k_norm_g, A_HEADS),
        jnp.tile(b_k_norm_g, 2 * B_HEADS)])
    bq_gain = (jnp.tile(b_q_norm_g, 2 * B_HEADS) * (ATTN_SCALE * LOG2E)).reshape(B_WIDTH, 1)
    *a_parts, bk, bqt, bvt = _inproj(x2, mod_b, norm1_g, w_in, qk_gains, bq_gain, bsz, seq)

    dil_outs = _dilated(a_parts, bsz, seq)
    lam = (jnp.exp(jnp.sum(lambda_q1 * lambda_k1)) - jnp.exp(jnp.sum(lambda_q2 * lambda_k2)) + LAMBDA_INIT)
    slopes_b = 2.0 ** (-8.0 * np.arange(1, B_HEADS + 1) / B_HEADS)
    scalars = jnp.concatenate([lam.reshape(1), jnp.asarray(slopes_b, F32)]).astype(F32)
    bk4 = bk.reshape(bsz, seq // DIFF_BLOCK, DIFF_BLOCK, B_WIDTH)
    o_b = _diff(bqt, bk4, bvt, scalars, diff_norm_g, bsz, seq).reshape(n, B_WIDTH)

    x1, he, idx, rank, cnt = _router(dil_outs, o_b, x2, mod_b, w_out, norm2_g, w_router, b_router, seq)

    n_tiles = n // ROW_TILE
    counts = cnt[:, :, 0].astype(I32)
    seg = (counts + SEG_ALIGN - 1) // SEG_ALIGN * SEG_ALIGN
    loff = jnp.cumsum(seg, axis=1) - seg
    region = jnp.sum(seg, axis=0)
    padded = (region + MOE_TILE - 1) // MOE_TILE * MOE_TILE
    pad_end = jnp.cumsum(padded)
    pad_start = pad_end - padded
    goff = pad_start[None, :] + jnp.cumsum(seg, axis=0) - seg
    sizes = jnp.asarray(SEG_SIZES, I32)[None, :, None]
    has = (seg[:, None, :] & sizes) != 0
    within = seg[:, None, :] & ~(2 * sizes - 1)
    place = jnp.cumsum(has, axis=-1) - has
    pick = has[:, :, None, :] & (place[:, :, None, :] == jnp.arange(N_EXPERTS, dtype=I32)[None, None, :, None])
    listed = lambda rows_: jnp.sum(jnp.where(pick, rows_[:, :, None, :], 0), axis=-1).reshape(-1).astype(I32)
    tables = (listed(loff[:, None, :] + within), listed(goff[:, None, :] + within),
              jnp.sum(has, axis=-1).reshape(-1).astype(I32), jnp.sum(seg, axis=1).astype(I32))
    n_blocks = (n * TOP_K + n_tiles * N_EXPERTS * (SEG_ALIGN - 1) + N_EXPERTS * (MOE_TILE - 1)) // MOE_TILE
    n_used = (pad_end[-1] // MOE_TILE).astype(I32)
    tail = jnp.concatenate([pad_start + region, padded - region, n_used.reshape(1)])

    ls, xs = _sort(tables, tail, he, idx, rank, loff.reshape(n_tiles, N_EXPERTS, 1), n_blocks * MOE_TILE)
    yb = _experts(pad_start // MOE_TILE, padded // MOE_TILE, xs, w_gate_up, b_gate_up, w_down, b_down)
    out = _combine(tables, yb, ls.T, x1, mod_b, seq)
    return out.reshape(bsz, seq, D_MODEL)


def kernel(x, c, w_ada, b_ada, norm1_g, w_in, a_q_norm_g, a_k_norm_g, b_q_norm_g, b_k_norm_g, lambda_q1, lambda_k1,
           lambda_q2, lambda_k2, diff_norm_g, w_out, norm2_g, w_router, b_router, w_gate_up, b_gate_up, w_down,
           b_down):
    args = (w_ada, b_ada, norm1_g, w_in, a_q_norm_g, a_k_norm_g, b_q_norm_g, b_k_norm_g, lambda_q1, lambda_k1,
            lambda_q2, lambda_k2, diff_norm_g, w_out, norm2_g, w_router, b_router, w_gate_up, b_gate_up, w_down,
            b_down)
    return _layer(x, c, *[a[0] for a in args])
```

```python
import functools

import numpy as np
import jax
import jax.numpy as jnp
from jax import lax
from jax.experimental import pallas as pl
from jax.experimental.pallas import tpu as pltpu

F32 = jnp.float32
BF16 = jnp.bfloat16
I32 = jnp.int32
U32 = jnp.uint32

D_MODEL = 1024
HEAD_DIM = 64
A_WIDTH = 512
B_WIDTH = 512
A_HEADS = 8
B_HEADS = 4
IN_WIDTH = 3072
DIL_PATTERNS = ((128, 1), (512, 4), (2048, 16))
BAND = 128
N_EXPERTS = 32
TOP_K = 4
D_FF = 1024
SWIGLU_LIMIT = 7.0
SWIGLU_ALPHA = 1.702
RMS_EPS = 1e-6
ATTN_SCALE = HEAD_DIM ** -0.5
LOG2E = 1.4426950408889634
LN2 = 0.6931471805599453
LAMBDA_INIT = 0.8 - 0.6 * 1.0

LANES = 128
ROW_TILE = 512
DIFF_BLOCK = 512
DIFF_LANES = 256
DIFF_ROWS = 64
ONES_ROWS = 16
VT_ROWS = B_HEADS * (2 * HEAD_DIM + ONES_ROWS)
MOE_TILE = 256
X_AHEAD = 4
X_SLOTS = X_AHEAD + 2
EXTRA_LANES = 128
ROW_WIDTH = D_MODEL + EXTRA_LANES
SEG_ALIGN = 16
SEG_SIZES = (512, 256, 128, 64, 32, 16)
SLOT_CHUNK = 512
LOCAL_SLOTS = 2560
LOCAL_PIECES = LOCAL_SLOTS // SEG_ALIGN
assert ROW_TILE == DIFF_BLOCK == SEG_SIZES[0]
assert LOCAL_SLOTS >= ROW_TILE * TOP_K + N_EXPERTS * (SEG_ALIGN - 1) and LOCAL_SLOTS % SLOT_CHUNK == 0
VMEM_LIMIT = 56 * 1024 * 1024


def _cparams(sem, **flags):
    return pltpu.CompilerParams(dimension_semantics=sem, vmem_limit_bytes=VMEM_LIMIT, flags=flags or None)


def _split_bf16(a):
    hi = a.astype(BF16)
    lo = (a - hi.astype(F32)).astype(BF16)
    return hi, lo


def _dot_nt(a, b):
    return lax.dot_general(a, b, (((1,), (1,)), ((), ())), preferred_element_type=F32)


def _dot(a, b):
    return jnp.dot(a, b, preferred_element_type=F32)


def _ada_kernel(c_ref, w_ref, b_ref, o_ref):
    c = c_ref[...]
    s = c / (1.0 + jnp.exp(-c))
    sh, sl = _split_bf16(s)
    wh, wl = _split_bf16(w_ref[...])
    o_ref[0] = _dot(sh, wh) + _dot(sh, wl) + _dot(sl, wh) + b_ref[0]


def _ada(c, w_ada, b_ada):
    bsz = c.shape[0]
    return pl.pallas_call(
        _ada_kernel,
        grid=(6,),
        in_specs=[
            pl.BlockSpec((bsz, D_MODEL), lambda j: (0, 0)),
            pl.BlockSpec((D_MODEL, D_MODEL), lambda j: (0, j)),
            pl.BlockSpec((1, 1, D_MODEL), lambda j: (j, 0, 0)),
        ],
        out_specs=pl.BlockSpec((1, bsz, D_MODEL), lambda j: (j, 0, 0)),
        out_shape=jax.ShapeDtypeStruct((6, bsz, D_MODEL), F32),
        compiler_params=_cparams(("arbitrary",)),
        name="ada",
    )(c, w_ada, b_ada.reshape(6, 1, D_MODEL))


def _inproj_kernel(x_ref, mod_ref, g1_ref, wn_ref, wt_ref, gm_ref, qkg_ref, bqg_ref, a1_ref, a4_ref, a16_ref, bk_ref,
                   bqt_ref, bvt_ref, a_sc):
    a_refs = (a1_ref, a4_ref, a16_ref)
    x = x_ref[...]
    y = x * lax.rsqrt(jnp.mean(x * x, axis=-1, keepdims=True) + RMS_EPS) * g1_ref[...]
    h = (y * (1.0 + mod_ref[1:2, :]) + mod_ref[0:1, :]).astype(BF16)
    p = _dot(h, wn_ref[...])
    gm = gm_ref[...]

    def head_norm(t, g):
        sq = (t * t).astype(BF16)
        half = gm.shape[0]
        ss = jnp.concatenate([_dot(sq[:, :half], gm), _dot(sq[:, half:], gm)], axis=1)
        return t * lax.rsqrt(ss * (1.0 / HEAD_DIM) + RMS_EPS) * g

    w = A_WIDTH
    a_part = jnp.concatenate([head_norm(p[:, 0 * w:1 * w], qkg_ref[0:1, :]),
                              head_norm(p[:, 1 * w:2 * w], qkg_ref[1:2, :]), p[:, 2 * w:3 * w]], axis=1)
    n_col = a_sc.shape[0]
    for c in range(n_col):
        a_sc[c] = a_part[:, c * LANES:(c + 1) * LANES]
    for a_ref, (_, dil) in zip(a_refs, DIL_PATTERNS):
        if dil == 1:
            a_ref[0] = a_part.astype(BF16)
            continue
        rows = a_part.shape[0] // dil
        for r in range(dil):
            for c in range(n_col):
                a_ref[r, :, c * LANES:(c + 1) * LANES] = a_sc[c, pl.ds(r, rows, stride=dil), :].astype(BF16)
    bk_ref[...] = head_norm(p[:, 3 * w:4 * w], qkg_ref[2:3, :]).astype(BF16)

    pt = _dot_nt(wt_ref[...], h)
    t = pt.shape[1]
    bq = pt[:w].reshape(w // HEAD_DIM, HEAD_DIM, t)
    ss = jnp.sum(bq * bq, axis=1, keepdims=True)
    bq = (bq * lax.rsqrt(ss * (1.0 / HEAD_DIM) + RMS_EPS)).reshape(w, t) * bqg_ref[...]
    bqt_ref[...] = bq.astype(BF16)
    lanes = 2 * HEAD_DIM
    for hd in range(B_HEADS):
        base = hd * (lanes + ONES_ROWS)
        bvt_ref[base:base + lanes, :] = pt[w + hd * lanes:w + (hd + 1) * lanes].astype(BF16)
        bvt_ref[base + lanes:base + lanes + ONES_ROWS, :] = jnp.ones((ONES_ROWS, t), BF16)


def _inproj(x2, mod_b, norm1_g, w_in, qk_gains, bq_gain, bsz, seq):
    n = x2.shape[0]
    t = ROW_TILE
    per_b = seq // t
    w = A_WIDTH
    head_of_lane = np.arange(w // 2) // HEAD_DIM
    gmat = jnp.asarray(head_of_lane[:, None] == head_of_lane[None, :], BF16)
    w_bf = w_in.astype(BF16)
    w_nat = jnp.concatenate([w_bf[:, :3 * w], w_bf[:, 4 * w:5 * w]], axis=1)
    w_tr = jnp.concatenate([w_bf[:, 3 * w:4 * w], w_bf[:, 5 * w:]], axis=1).T
    tr_spec = pl.BlockSpec((None, None, w, t), lambda i: (i // per_b, i % per_b, 0, 0))
    tr_shape = jax.ShapeDtypeStruct((bsz, per_b, w, t), BF16)
    a_specs = [pl.BlockSpec((None, dil, t // dil, 3 * w), lambda i: (i // per_b, 0, i % per_b, 0))
               for _, dil in DIL_PATTERNS]
    a_shapes = [jax.ShapeDtypeStruct((bsz, dil, seq // dil, 3 * w), BF16) for _, dil in DIL_PATTERNS]
    return pl.pallas_call(
        _inproj_kernel,
        grid=(n // t,),
        in_specs=[
            pl.BlockSpec((t, D_MODEL), lambda i: (i, 0)),
            pl.BlockSpec((None, 6, D_MODEL), lambda i: (i // per_b, 0, 0)),
            pl.BlockSpec((1, D_MODEL), lambda i: (0, 0)),
            pl.BlockSpec((D_MODEL, 4 * w), lambda i: (0, 0)),
            pl.BlockSpec((2 * w, D_MODEL), lambda i: (0, 0)),
            pl.BlockSpec((w // 2, w // 2), lambda i: (0, 0)),
            pl.BlockSpec((3, w), lambda i: (0, 0)),
            pl.BlockSpec((w, 1), lambda i: (0, 0)),
        ],
        out_specs=a_specs + [pl.BlockSpec((t, w), lambda i: (i, 0)), tr_spec,
                             pl.BlockSpec((None, None, VT_ROWS, t), lambda i: (i // per_b, i % per_b, 0, 0))],
        out_shape=a_shapes + [jax.ShapeDtypeStruct((n, w), BF16), tr_shape,
                              jax.ShapeDtypeStruct((bsz, per_b, VT_ROWS, t), BF16)],
        scratch_shapes=[pltpu.VMEM((3 * w // LANES, t, LANES), F32)],
        compiler_params=_cparams(("parallel",)),
        name="inproj",
    )(x2, mod_b, norm1_g.reshape(1, D_MODEL), w_nat, w_tr, gmat, qk_gains, bq_gain)


def _dilated_bias_table():
    ik = np.arange(2 * BAND)[:, None]
    iq = np.arange(BAND)[None, :]
    delta = iq - ik + BAND
    in_band = (delta >= 0) & (delta <= BAND)
    slopes = 2.0 ** (-8.0 * np.arange(1, A_HEADS + 1) / A_HEADS)
    tbl = np.zeros((len(DIL_PATTERNS), 2, A_HEADS // 2, 2 * BAND, 2 * BAND), np.float32)
    for p, (_, dil) in enumerate(DIL_PATTERNS):
        for first in range(2):
            valid = in_band & ((ik >= BAND) if first else True)
            for h in range(A_HEADS):
                cols = slice(BAND * (h % 2), BAND * (h % 2 + 1))
                tbl[p, first, h // 2, :, cols] = np.where(valid, -slopes[h] * LOG2E * (delta * dil), -np.inf)
    return jnp.asarray(tbl)


def _dilated_kernel(tbl_ref, *refs, n_blocks):
    lanes = 2 * HEAD_DIM
    n = pl.program_id(1)
    n_pat = len(DIL_PATTERNS)
    s_sc, p_sc = refs[len(refs) - 2:]
    outs = refs[len(refs) - 2 - 2 * n_pat:len(refs) - 2]
    lane = lax.broadcasted_iota(I32, (BAND, lanes), 1)
    low = lane < HEAD_DIM
    work = []
    pos = 0
    for p, (_, dil) in enumerate(DIL_PATTERNS):
        per_res = n_blocks // dil
        if per_res > 1:
            cur_ref, prev_ref = refs[pos:pos + 2]
            pos += 2
            first = (n % per_res == 0).astype(I32)
        else:
            cur_ref, prev_ref, first = refs[pos], None, None
            pos += 1
        for j in range(A_HEADS // 2):
            work.append((p, j, cur_ref, prev_ref, first))

    def keys_of(prev_ref, cur_ref, cs):
        return cur_ref[:, cs] if prev_ref is None else jnp.concatenate([prev_ref[:, cs], cur_ref[:, cs]], axis=0)

    def cols_of(part, j):
        return slice(part * A_WIDTH + lanes * j, part * A_WIDTH + lanes * (j + 1))

    for w, (p, j, cur_ref, prev_ref, first) in enumerate(work):
        q2 = cur_ref[:, cols_of(0, j)]
        zero = jnp.zeros_like(q2)
        qcat = jnp.concatenate([jnp.where(low, q2, zero), jnp.where(low, zero, q2)], axis=0)
        k2 = keys_of(prev_ref, cur_ref, cols_of(1, j))
        bias = tbl_ref[p, 1, j, BAND:, :] if prev_ref is None else tbl_ref[p, first, j]
        s_sc[w, :k2.shape[0], :] = _dot_nt(k2, qcat) + bias
    lses = []
    for w, (p, j, _, prev_ref, _) in enumerate(work):
        nk = BAND if prev_ref is None else 2 * BAND
        s = s_sc[w, :nk, :]
        m = jnp.max(s, axis=0, keepdims=True)
        e = jnp.exp2(s - m)
        den = jnp.sum(e, axis=0, keepdims=True)
        p_sc[w, :nk, :] = (e * (1.0 / den)).astype(BF16)
        lses.append((m + jnp.log2(den)) * LN2)
    for w, (p, j, cur_ref, prev_ref, _) in enumerate(work):
        nk = BAND if prev_ref is None else 2 * BAND
        cs = cols_of(0, j)
        v2 = keys_of(prev_ref, cur_ref, cols_of(2, j))
        r = lax.dot_general(p_sc[w, :nk, :], v2, (((0,), (0,)), ((), ())), preferred_element_type=F32)
        outs[2 * p][:, cs] = jnp.where(low, r[:BAND], r[BAND:]).astype(BF16)
    for p in range(n_pat):
        rows = []
        for lse in lses[p * (A_HEADS // 2):(p + 1) * (A_HEADS // 2)]:
            rows += [lse[:, :BAND], lse[:, BAND:]]
        rows.append(jnp.zeros((lanes - A_HEADS, BAND), F32))
        outs[2 * p + 1][...] = jnp.concatenate(rows, axis=0).T


def _dilated(a_parts, bsz, seq):
    n_blocks = seq // BAND
    in_specs = [pl.BlockSpec((len(DIL_PATTERNS), 2, A_HEADS // 2, 2 * BAND, 2 * BAND), lambda b, n: (0, 0, 0, 0, 0))]
    n_work = len(DIL_PATTERNS) * A_HEADS // 2
    args = [_dilated_bias_table()]
    out_specs, out_shapes = [], []
    blk = (None, None, BAND, A_WIDTH)
    qkv = (None, None, BAND, 3 * A_WIDTH)
    for a_part, (_, dil) in zip(a_parts, DIL_PATTERNS):
        per_res = n_blocks // dil
        cur = lambda b, n, per_res=per_res: (b, n // per_res, n % per_res, 0)
        prev = lambda b, n, per_res=per_res: (b, n // per_res, jnp.maximum(n % per_res - 1, 0), 0)
        in_specs += [pl.BlockSpec(qkv, cur)] + ([pl.BlockSpec(qkv, prev)] if per_res > 1 else [])
        args += [a_part] * (2 if per_res > 1 else 1)
        out_specs += [pl.BlockSpec(blk, cur), pl.BlockSpec((None, None, BAND, LANES), cur)]
        out_shapes += [jax.ShapeDtypeStruct((bsz, dil, seq // dil, A_WIDTH), BF16),
                       jax.ShapeDtypeStruct((bsz, dil, seq // dil, LANES), F32)]
    return pl.pallas_call(
        functools.partial(_dilated_kernel, n_blocks=n_blocks),
        grid=(bsz, n_blocks),
        in_specs=in_specs,
        out_specs=out_specs,
        out_shape=out_shapes,
        scratch_shapes=[pltpu.VMEM((n_work, 2 * BAND, 2 * BAND), F32), pltpu.VMEM((n_work, 2 * BAND, 2 * BAND), BF16)],
        compiler_params=_cparams(("parallel", "parallel")),
        name="dilated",
    )(*args)


def _diff_bias_table():
    t = DIFF_BLOCK
    rel = np.arange(t)[None, :] - np.arange(t)[:, None]
    slopes = 2.0 ** (-8.0 * np.arange(1, B_HEADS + 1) / B_HEADS)
    tbl = np.zeros((B_HEADS, 2, t, t), np.float32)
    for h in range(B_HEADS):
        tbl[h, 0] = -slopes[h] * LOG2E * rel
        tbl[h, 1] = np.where(rel >= 0, -slopes[h] * LOG2E * rel, -np.inf)
    return jnp.asarray(tbl)


def _diff_kernel(sc_ref, qt_ref, k_ref, vt_ref, g_ref, bias_sc, o_ref, qst_sc, m_sc, l_sc, a_sc, acc_sc, s_sc, p_sc):
    t = DIFF_BLOCK
    lanes = 2 * HEAD_DIM
    qi = pl.program_id(1)
    lam = sc_ref[0]
    slopes = [jnp.full((1, 1), sc_ref[1 + h], F32) for h in range(B_HEADS)]
    row = lax.broadcasted_iota(I32, (lanes, t), 0)
    for h in range(B_HEADS):
        qt = qt_ref[h * lanes:(h + 1) * lanes, :]
        zero = jnp.zeros_like(qt)
        qst_sc[h, :, :t] = jnp.where(row < HEAD_DIM, qt, zero)
        qst_sc[h, :, t:] = jnp.where(row < HEAD_DIM, zero, qt)
    m_sc[...] = jnp.full(m_sc.shape, -jnp.inf, F32)
    l_sc[...] = jnp.zeros(l_sc.shape, F32)
    acc_sc[...] = jnp.zeros(acc_sc.shape, F32)

    n_grp = 2 * t // DIFF_LANES
    items = [(h, g) for h in range(B_HEADS) for g in range(n_grp)]
    chunks = [slice(r * DIFF_ROWS, (r + 1) * DIFF_ROWS) for r in range(t // DIFF_ROWS)]

    def keys_needed(g, diag):
        return min(t, (g * DIFF_LANES) % t + DIFF_LANES) if diag else t

    def block(n, diag):
        cs = [-slopes[h] * LOG2E * (jnp.full((1, 1), (qi - n) * t, I32)).astype(F32) for h in range(B_HEADS)]
        for w, (h, g) in enumerate(items):
            nk = keys_needed(g, diag)
            cols = slice(g * DIFF_LANES, (g + 1) * DIFF_LANES)
            bcols = slice((g * DIFF_LANES) % t, (g * DIFF_LANES) % t + DIFF_LANES)
            s_sc[w, :nk, :] = (_dot(k_ref[n, :nk, h * lanes:(h + 1) * lanes], qst_sc[h, :, cols])
                               + bias_sc[h, diag, :nk, bcols])
        for w, (h, g) in enumerate(items):
            used = chunks[:keys_needed(g, diag) // DIFF_ROWS]
            cols = slice(g * DIFF_LANES, (g + 1) * DIFF_LANES)
            top = s_sc[w, used[0], :]
            for rows in used[1:]:
                top = jnp.maximum(top, s_sc[w, rows, :])
            m_prev = m_sc[h, :, cols]
            m_new = jnp.maximum(m_prev, jnp.max(top, axis=0, keepdims=True) + cs[h])
            alpha = jnp.exp2(m_prev - m_new)
            shift = m_new - cs[h]
            for rows in used:
                p_sc[w, rows, :] = jnp.exp2(s_sc[w, rows, :] - shift).astype(BF16)
            m_sc[h, :, cols] = m_new
            a_sc[h, :, cols] = alpha
        for w, (h, g) in enumerate(items):
            nk = keys_needed(g, diag)
            cols = slice(g * DIFF_LANES, (g + 1) * DIFF_LANES)
            pv = _dot(vt_ref[n, h * (lanes + ONES_ROWS):(h + 1) * (lanes + ONES_ROWS), :nk], p_sc[w, :nk, :])
            alpha = a_sc[h, :, cols]
            acc_sc[h, :, cols] = alpha * acc_sc[h, :, cols] + pv[:lanes]
            l_sc[h, :, cols] = alpha * l_sc[h, :, cols] + pv[lanes:lanes + 1]

    def body(n, carry):
        block(n, 0)
        return carry

    lax.fori_loop(0, qi, body, 0)
    block(qi, 1)
    for h in range(B_HEADS):
        o = acc_sc[h] / l_sc[h]
        o = o[:, :t] - lam * o[:, t:]
        o = o * lax.rsqrt(jnp.mean(o * o, axis=0, keepdims=True) + RMS_EPS) * g_ref[...]
        o_ref[:, h * lanes:(h + 1) * lanes] = (o * (1.0 - LAMBDA_INIT)).T.astype(BF16)


def _diff(bqt, bk, bvt, scalars, diff_norm_g, bsz, seq):
    t = DIFF_BLOCK
    nb = seq // t
    lanes = 2 * HEAD_DIM
    n_items = B_HEADS * 2 * t // DIFF_LANES
    return pl.pallas_call(
        _diff_kernel,
        grid=(bsz, nb),
        in_specs=[
            pl.BlockSpec(memory_space=pltpu.SMEM),
            pl.BlockSpec((None, None, B_WIDTH, t), lambda b, i: (b, i, 0, 0)),
            pl.BlockSpec((None, nb, t, B_WIDTH), lambda b, i: (b, 0, 0, 0)),
            pl.BlockSpec((None, nb, VT_ROWS, t), lambda b, i: (b, 0, 0, 0)),
            pl.BlockSpec((lanes, 1), lambda b, i: (0, 0)),
            pl.BlockSpec((B_HEADS, 2, t, t), lambda b, i: (0, 0, 0, 0)),
        ],
        out_specs=pl.BlockSpec((None, t, B_WIDTH), lambda b, i: (b, i, 0)),
        out_shape=jax.ShapeDtypeStruct((bsz, seq, B_WIDTH), BF16),
        scratch_shapes=[pltpu.VMEM((B_HEADS, lanes, 2 * t), BF16)] + [pltpu.VMEM((B_HEADS, 1, 2 * t), F32)] * 3 + [
            pltpu.VMEM((B_HEADS, lanes, 2 * t), F32),
            pltpu.VMEM((n_items, t, DIFF_LANES), F32), pltpu.VMEM((n_items, t, DIFF_LANES), BF16)],
        compiler_params=_cparams(("parallel", "parallel")),
        name="diff",
    )(scalars, bqt, bk, bvt, diff_norm_g.reshape(lanes, 1), _diff_bias_table())


def _router_kernel(o0, l0, o1, l1, o2, l2, ob_ref, x_ref, mod_ref, wout_ref, g2_ref, wrh_ref, wrl_ref, br_ref,
                   tri_ref, spread_ref, x1_ref, he_ref, idx_ref, rank_ref, cnt_ref, *order_sc):
    def token_order(ref, scratch):
        dil, rows, width = ref.shape
        if dil == 1:
            return ref[0].astype(F32)
        n_col = width // LANES
        for r in range(dil):
            for c in range(n_col):
                scratch[c, pl.ds(r, rows, stride=dil), :] = ref[r, :, c * LANES:(c + 1) * LANES].astype(F32)
        return jnp.concatenate([scratch[c] for c in range(n_col)], axis=1)

    os_ = [token_order(o0, None), token_order(o1, order_sc[0]), token_order(o2, order_sc[2])]
    ls = [token_order(l0, None), token_order(l1, order_sc[1]), token_order(l2, order_sc[3])]
    mx = jnp.maximum(jnp.maximum(ls[0], ls[1]), ls[2])
    ws = [jnp.exp(l - mx) for l in ls]
    den = ws[0] + ws[1] + ws[2]
    spread = spread_ref[...]
    oa = jnp.zeros(os_[0].shape, F32)
    for w, o in zip(ws, os_):
        wh, wl = _split_bf16(w / den)
        oa = oa + (_dot(wh, spread) + _dot(wl, spread)) * o
    mixed = _dot(oa.astype(BF16), wout_ref[:A_WIDTH, :]) + _dot(ob_ref[...], wout_ref[A_WIDTH:, :])
    x1 = x_ref[...] + mod_ref[2:3, :] * mixed
    x1_ref[...] = x1
    y = x1 * lax.rsqrt(jnp.mean(x1 * x1, axis=-1, keepdims=True) + RMS_EPS) * g2_ref[...]
    h2 = y * (1.0 + mod_ref[4:5, :]) + mod_ref[3:4, :]
    he_ref[:, :D_MODEL] = h2.astype(BF16)

    hh, hl = _split_bf16(h2)
    wrh = wrh_ref[...]
    logits = _dot_nt(wrh, hh) + _dot_nt(wrh, hl) + _dot_nt(wrl_ref[...], hh) + br_ref[...]
    t = logits.shape[1]
    eid = lax.broadcasted_iota(I32, (N_EXPERTS, t), 0)
    vals, idxs, hots = [], [], []
    cur = logits
    for _ in range(TOP_K):
        v = jnp.max(cur, axis=0, keepdims=True)
        ik = jnp.min(jnp.where(cur == v, eid, N_EXPERTS), axis=0, keepdims=True)
        hot = eid == ik
        vals.append(v)
        idxs.append(ik)
        hots.append(hot)
        cur = jnp.where(hot, -jnp.inf, cur)
    es = [jnp.exp(v - vals[0]) for v in vals]
    esum = es[0] + es[1] + es[2] + es[3]
    idx_ref[...] = jnp.concatenate(idxs, axis=0)

    rows = [ik.astype(F32) for ik in idxs]
    for e in es:
        g = e / esum
        hi = g.astype(BF16).astype(F32)
        mid = (g - hi).astype(BF16).astype(F32)
        rows += [hi, mid, g - hi - mid]
    rows.append(jnp.zeros((EXTRA_LANES - len(rows), t), F32))
    he_ref[:, D_MODEL:] = jnp.concatenate(rows, axis=0).T.astype(BF16)

    sel = jnp.where(hots[0] | hots[1] | hots[2] | hots[3], 1.0, 0.0)
    before = _dot(sel.astype(BF16), tri_ref[...])
    ranks = [jnp.sum(jnp.where(hot, before, 0.0), axis=0, keepdims=True) for hot in hots]
    rank_ref[...] = jnp.concatenate(ranks, axis=0).astype(I32)
    cnt_ref[...] = jnp.broadcast_to(jnp.sum(sel, axis=1, keepdims=True), cnt_ref.shape)


def _router(dil_outs, o_b, x2, mod_b, w_out, norm2_g, w_router, b_router, seq):
    n = x2.shape[0]
    t = ROW_TILE
    per_b = seq // t
    wr_t = w_router.T
    wrh = wr_t.astype(BF16)
    wrl = (wr_t - wrh.astype(F32)).astype(BF16)
    tri = jnp.asarray(np.arange(t)[:, None] < np.arange(t)[None, :], BF16)
    row = lambda w: pl.BlockSpec((t, w), lambda i: (i, 0))
    full = lambda a, b: pl.BlockSpec((a, b), lambda i: (0, 0))
    tok = lambda: pl.BlockSpec((TOP_K, t), lambda i: (0, i))
    grouped = lambda dil, w: pl.BlockSpec((None, dil, t // dil, w), lambda i: (i // per_b, 0, i % per_b, 0))
    widths = (A_WIDTH, LANES)
    head_of_lane = np.arange(A_WIDTH) // HEAD_DIM
    spread = jnp.asarray(np.arange(LANES)[:, None] == head_of_lane[None, :], BF16)
    return pl.pallas_call(
        _router_kernel,
        grid=(n // t,),
        in_specs=[grouped(dil, w) for _, dil in DIL_PATTERNS for w in widths] + [
            row(B_WIDTH), row(D_MODEL),
            pl.BlockSpec((None, 6, D_MODEL), lambda i: (i // per_b, 0, 0)),
            full(D_MODEL, D_MODEL), full(1, D_MODEL), full(N_EXPERTS, D_MODEL), full(N_EXPERTS, D_MODEL),
            full(N_EXPERTS, 1), full(t, t), full(LANES, A_WIDTH),
        ],
        out_specs=[row(D_MODEL), row(ROW_WIDTH), tok(), tok(),
                   pl.BlockSpec((None, N_EXPERTS, 128), lambda i: (i, 0, 0))],
        out_shape=[
            jax.ShapeDtypeStruct((n, D_MODEL), F32),
            jax.ShapeDtypeStruct((n, ROW_WIDTH), BF16),
            jax.ShapeDtypeStruct((TOP_K, n), I32),
            jax.ShapeDtypeStruct((TOP_K, n), I32),
            jax.ShapeDtypeStruct((n // t, N_EXPERTS, 128), F32),
        ],
        scratch_shapes=[pltpu.VMEM((w // LANES, t, LANES), F32) for _, dil in DIL_PATTERNS if dil > 1 for w in widths],
        compiler_params=_cparams(("parallel",)),
        name="router",
    )(*dil_outs, o_b, x2, mod_b, w_out.astype(BF16), norm2_g.reshape(1, D_MODEL), wrh, wrl,
      b_router.reshape(N_EXPERTS, 1), tri, spread)


def _start_pieces(tables, tile, local_ref, hbm_ref, sem, outbound):
    lo_ref, go_ref, cnt_ref, _ = tables
    for c, size in enumerate(SEG_SIZES):
        cls = tile * len(SEG_SIZES) + c

        def per_piece(k, carry, size=size, cls=cls):
            lo, go = lo_ref[cls * N_EXPERTS + k], go_ref[cls * N_EXPERTS + k]
            loc = local_ref.at[pl.ds(pl.multiple_of(lo, SEG_ALIGN), size)]
            glob = hbm_ref.at[pl.ds(pl.multiple_of(go, SEG_ALIGN), size)]
            cp = pltpu.make_async_copy(loc, glob, sem) if outbound else pltpu.make_async_copy(glob, loc, sem)
            cp.start()
            return carry

        lax.fori_loop(0, cnt_ref[cls], per_piece, 0)


def _start(cp):
    cp.start()


def _wait(cp):
    cp.wait()


def _tile_rows(tables, tile):
    return tables[3][tile]


def _wait_rows(rows, local_ref, hbm_ref, sem, outbound):
    size = 1 << (LOCAL_SLOTS.bit_length() - 1)
    while size >= SEG_ALIGN:
        loc, glob = local_ref.at[pl.ds(0, size)], hbm_ref.at[pl.ds(0, size)]
        cp = pltpu.make_async_copy(loc, glob, sem) if outbound else pltpu.make_async_copy(glob, loc, sem)
        pl.when((rows & size) != 0)(cp.wait)
        size //= 2


def _one_hot_any(j, targets):
    out = jnp.zeros(j.shape, F32)
    for tgt in targets:
        out = jnp.where(j == tgt, 1.0, out)
    return out


def _sort_kernel(lo_ref, go_ref, cnt_ref, rows_ref, tail_ref, he_ref, idx_ref, rank_ref, lcol_ref, ls_ref, xs_hbm,
                 xl, zbuf, sems, zsem):
    tables = (lo_ref, go_ref, cnt_ref, rows_ref)
    i = pl.program_id(0)
    last = pl.num_programs(0) - 1
    slot = i % 2
    t = he_ref.shape[0]

    @pl.when(i == 0)
    def _():
        zbuf[...] = jnp.zeros(zbuf.shape, BF16)

        def tails(action):
            def per_expert(e, carry):
                off, n = tail_ref[e], tail_ref[N_EXPERTS + e]
                done = jnp.int32(0)
                for size in SEG_SIZES:
                    if size < MOE_TILE:
                        take = (n & size) != 0
                        dst = xs_hbm.at[pl.ds(pl.multiple_of(off + done, SEG_ALIGN), size)]
                        pl.when(take)(functools.partial(action, pltpu.make_async_copy(zbuf.at[pl.ds(0, size)], dst, zsem)))
                        done = done + jnp.where(take, size, 0)
                return carry

            lax.fori_loop(0, N_EXPERTS, per_expert, 0)

            def per_block(b, carry):
                dst = xs_hbm.at[pl.ds(pl.multiple_of(b * MOE_TILE, MOE_TILE), MOE_TILE)]
                action(pltpu.make_async_copy(zbuf, dst, zsem))
                return carry

            lax.fori_loop(tail_ref[2 * N_EXPERTS], xs_hbm.shape[0] // MOE_TILE, per_block, 0)

        tails(_start)
        tails(_wait)

    eid = lax.broadcasted_iota(I32, (N_EXPERTS, t), 0)
    lcol = lcol_ref[...]
    ls = []
    for k in range(TOP_K):
        off = jnp.sum(jnp.where(eid == idx_ref[k:k + 1, :], lcol, 0), axis=0, keepdims=True)
        ls.append(off + rank_ref[k:k + 1, :])
    ls_ref[...] = jnp.concatenate(ls, axis=0)

    he = he_ref[...]
    for jc in range(LOCAL_SLOTS // SLOT_CHUNK):
        j = lax.broadcasted_iota(I32, (SLOT_CHUNK, t), 0) + jc * SLOT_CHUNK
        perm = _one_hot_any(j, ls).astype(BF16)
        xl[slot, jc * SLOT_CHUNK:(jc + 1) * SLOT_CHUNK, :] = _dot(perm, he).astype(BF16)

    _start_pieces(tables, i, xl.at[slot], xs_hbm, sems.at[slot], True)

    @pl.when(i > 0)
    def _():
        _wait_rows(_tile_rows(tables, i - 1), xl.at[1 - slot], xs_hbm, sems.at[1 - slot], True)

    @pl.when(i == last)
    def _():
        _wait_rows(_tile_rows(tables, i), xl.at[slot], xs_hbm, sems.at[slot], True)


def _sort(tables, tail, he, idx, rank, lcol, n_rows):
    n = he.shape[0]
    t = ROW_TILE
    tok = lambda: pl.BlockSpec((TOP_K, t), lambda i, *_: (0, i))
    return pl.pallas_call(
        _sort_kernel,
        grid_spec=pltpu.PrefetchScalarGridSpec(
            num_scalar_prefetch=5,
            grid=(n // t,),
            in_specs=[
                pl.BlockSpec((t, ROW_WIDTH), lambda i, *_: (i, 0)),
                tok(), tok(),
                pl.BlockSpec((None, N_EXPERTS, 1), lambda i, *_: (i, 0, 0)),
            ],
            out_specs=[tok(), pl.BlockSpec(memory_space=pl.ANY)],
            scratch_shapes=[pltpu.VMEM((2, LOCAL_SLOTS, ROW_WIDTH), BF16), pltpu.VMEM((MOE_TILE, ROW_WIDTH), BF16),
                            pltpu.SemaphoreType.DMA((2,)), pltpu.SemaphoreType.DMA(())],
        ),
        out_shape=[jax.ShapeDtypeStruct((TOP_K, n), I32), jax.ShapeDtypeStruct((n_rows, ROW_WIDTH), BF16)],
        compiler_params=_cparams(("arbitrary",)),
        name="sort",
    )(*tables, tail, he, idx, rank, lcol)


def _experts_kernel(blk0_ref, nblk_ref, xs_hbm, wgu_ref, bgu_ref, wd_ref, bd_ref, yb_hbm, wgu_sc, wd_sc, xbuf, ybuf,
                    xsem, ysem):
    e = pl.program_id(0)
    tm = MOE_TILE
    first, nb = blk0_ref[e], nblk_ref[e]
    last_e = pl.num_programs(0) - 1
    total = blk0_ref[last_e] + nblk_ref[last_e]

    def rows(b, n=1):
        return pl.ds(pl.multiple_of(b * tm, tm), n * tm)

    def x_copy(b, slot):
        return pltpu.make_async_copy(xs_hbm.at[rows(b)], xbuf.at[rows(slot)], xsem.at[slot])

    def x_start(b):
        x_copy(b, b % X_SLOTS).start()
        pl.when(b % X_SLOTS == 0)(lambda: x_copy(b, X_SLOTS).start())

    def x_wait(b):
        x_copy(b, b % X_SLOTS).wait()
        pl.when(b % X_SLOTS == 0)(lambda: x_copy(b, X_SLOTS).wait())

    def y_copy(b, slot, n):
        return pltpu.make_async_copy(ybuf.at[slot, pl.ds(0, n * tm)], yb_hbm.at[rows(b, n)], ysem.at[slot])

    @pl.when(e == 0)
    def _():
        for b in range(X_AHEAD):
            pl.when(b < total)(functools.partial(x_start, b))

    def cast(r, carry):
        s = pl.multiple_of(r * LANES, LANES)
        wgu_sc[pl.ds(s, LANES), :] = wgu_ref[pl.ds(s, LANES), :].astype(BF16)
        wd_sc[pl.ds(s, LANES), :] = wd_ref[pl.ds(s, LANES), :].astype(BF16)
        return carry

    lax.fori_loop(0, D_MODEL // LANES, cast, 0)
    me = jnp.full((1, 1), e, I32).astype(F32)

    def unit(it, b, n):
        slot = it % 2
        for k in range(n):
            pl.when(b + X_AHEAD + k < total)(functools.partial(x_start, b + X_AHEAD + k))
        for k in range(n):
            x_wait(b + k)

        @pl.when(it >= 2)
        def _():
            y_copy(b, slot, 2).wait()

        x = xbuf.at[rows(b % X_SLOTS, n)]
        ext = x[:, D_MODEL:].astype(F32)
        gate = jnp.zeros((n * tm, 1), F32)
        for k in range(TOP_K):
            c = TOP_K + 3 * k
            gk = ext[:, c:c + 1] + ext[:, c + 1:c + 2] + ext[:, c + 2:c + 3]
            gate = gate + jnp.where(ext[:, k:k + 1] == me, gk, 0.0)

        gu = _dot(x[:, :D_MODEL], wgu_sc[...]) + bgu_ref[...]
        g = jnp.minimum(gu[:, :D_FF], SWIGLU_LIMIT)
        u = jnp.clip(gu[:, D_FF:], -SWIGLU_LIMIT, SWIGLU_LIMIT)
        act = (u + 1.0) * (g / (1.0 + jnp.exp(-SWIGLU_ALPHA * g)))
        ybuf[slot, :n * tm, :] = (gate * (_dot(act.astype(BF16), wd_sc[...]) + bd_ref[...])).astype(BF16)
        y_copy(b, slot, n).start()

    n_pair = nb // 2
    odd = nb % 2 == 1

    def pair(it, carry):
        unit(it, first + 2 * it, 2)
        return carry

    lax.fori_loop(0, n_pair, pair, 0)
    pl.when(odd)(functools.partial(unit, n_pair, first + 2 * n_pair, 1))

    last_slot = (n_pair - 1) % 2
    pl.when(odd)(y_copy(first, n_pair % 2, 1).wait)
    pl.when(odd & (n_pair >= 1))(y_copy(first, last_slot, 2).wait)
    pl.when(jnp.logical_not(odd) & (n_pair >= 2))(y_copy(first, n_pair % 2, 2).wait)
    pl.when(jnp.logical_not(odd) & (n_pair >= 1))(y_copy(first, last_slot, 2).wait)

    @pl.when(e == last_e)
    def _():
        ybuf[0, :tm, :] = jnp.zeros((tm, D_MODEL), BF16)

        def fill(action):
            def per_block(b, carry):
                action(pltpu.make_async_copy(ybuf.at[0, pl.ds(0, tm)], yb_hbm.at[rows(b)], ysem.at[0]))
                return carry

            lax.fori_loop(total, yb_hbm.shape[0] // tm, per_block, 0)

        fill(_start)
        fill(_wait)


def _experts(first_block, n_block, xs, w_gate_up, b_gate_up, w_down, b_down):
    n_rows = xs.shape[0]
    exp3 = lambda e, *_: (e, 0, 0)
    return pl.pallas_call(
        _experts_kernel,
        grid_spec=pltpu.PrefetchScalarGridSpec(
            num_scalar_prefetch=2,
            grid=(N_EXPERTS,),
            in_specs=[
                pl.BlockSpec(memory_space=pl.ANY),
                pl.BlockSpec((None, D_MODEL, 2 * D_FF), exp3),
                pl.BlockSpec((None, 1, 2 * D_FF), exp3),
                pl.BlockSpec((None, D_FF, D_MODEL), exp3),
                pl.BlockSpec((None, 1, D_MODEL), exp3),
            ],
            out_specs=pl.BlockSpec(memory_space=pl.ANY),
            scratch_shapes=[pltpu.VMEM((D_MODEL, 2 * D_FF), BF16), pltpu.VMEM((D_FF, D_MODEL), BF16),
                            pltpu.VMEM(((X_SLOTS + 1) * MOE_TILE, ROW_WIDTH), BF16),
                            pltpu.VMEM((2, 2 * MOE_TILE, D_MODEL), BF16),
                            pltpu.SemaphoreType.DMA((X_SLOTS + 1,)), pltpu.SemaphoreType.DMA((2,))],
        ),
        out_shape=jax.ShapeDtypeStruct((n_rows, D_MODEL), BF16),
        compiler_params=_cparams(("arbitrary",)),
        name="experts",
    )(first_block, n_block, xs, w_gate_up, b_gate_up.reshape(N_EXPERTS, 1, 2 * D_FF), w_down,
      b_down.reshape(N_EXPERTS, 1, D_MODEL))


def _combine_kernel(lo_ref, go_ref, cnt_ref, rows_ref, yb_hbm, lst_ref, x1_ref, mod_ref, o_ref, ybuf, sems):
    i = pl.program_id(0)
    slot = i % 2
    tables = (lo_ref, go_ref, cnt_ref, rows_ref)

    @pl.when(i == 0)
    def _():
        ybuf[...] = jnp.zeros(ybuf.shape, BF16)
        _start_pieces(tables, 0, ybuf.at[0], yb_hbm, sems.at[0], False)

    @pl.when(i + 1 < pl.num_programs(0))
    def _():
        _start_pieces(tables, i + 1, ybuf.at[1 - slot], yb_hbm, sems.at[1 - slot], False)

    _wait_rows(_tile_rows(tables, i), ybuf.at[slot], yb_hbm, sems.at[slot], False)

    lst = lst_ref[...]
    t = lst.shape[0]
    targets = [lst[:, k:k + 1] for k in range(TOP_K)]
    y = jnp.zeros((t, D_MODEL), F32)
    for jc in range(LOCAL_SLOTS // SLOT_CHUNK):
        j = lax.broadcasted_iota(I32, (t, SLOT_CHUNK), 1) + jc * SLOT_CHUNK
        pick = _one_hot_any(j, targets).astype(BF16)
        y = y + _dot(pick, ybuf[slot, jc * SLOT_CHUNK:(jc + 1) * SLOT_CHUNK, :])
    o_ref[...] = x1_ref[...] + mod_ref[5:6, :] * y


def _combine(tables, yb, ls_t, x1, mod_b, seq):
    n = x1.shape[0]
    t = ROW_TILE
    per_b = seq // t
    return pl.pallas_call(
        _combine_kernel,
        grid_spec=pltpu.PrefetchScalarGridSpec(
            num_scalar_prefetch=4,
            grid=(n // t,),
            in_specs=[
                pl.BlockSpec(memory_space=pl.ANY),
                pl.BlockSpec((t, TOP_K), lambda i, *_: (i, 0)),
                pl.BlockSpec((t, D_MODEL), lambda i, *_: (i, 0)),
                pl.BlockSpec((None, 6, D_MODEL), lambda i, *_: (i // per_b, 0, 0)),
            ],
            out_specs=pl.BlockSpec((t, D_MODEL), lambda i, *_: (i, 0)),
            scratch_shapes=[pltpu.VMEM((2, LOCAL_SLOTS, D_MODEL), BF16), pltpu.SemaphoreType.DMA((2,))],
        ),
        out_shape=jax.ShapeDtypeStruct((n, D_MODEL), F32),
        compiler_params=_cparams(("arbitrary",)),
        name="combine",
    )(*tables, yb, ls_t, x1, mod_b)


def _layer(x, c, w_ada, b_ada, norm1_g, w_in, a_q_norm_g, a_k_norm_g, b_q_norm_g, b_k_norm_g, lambda_q1,
           lambda_k1, lambda_q2, lambda_k2, diff_norm_g, w_out, norm2_g, w_router, b_router, w_gate_up,
           b_gate_up, w_down, b_down):
    bsz, seq, _ = x.shape
    n = bsz * seq
    x2 = x.reshape(n, D_MODEL)
    mod_b = _ada(c, w_ada, b_ada).transpose(1, 0, 2)

    qk_gains = jnp.stack([
        jnp.tile(a_q_norm_g, A_HEADS) * (ATTN_SCALE * LOG2E), jnp.tile(a_k_norm_g, A_HEADS),
        jnp.tile(b_k_norm_g, 2 * B_HEADS)])
    bq_gain = (jnp.tile(b_q_norm_g, 2 * B_HEADS) * (ATTN_SCALE * LOG2E)).reshape(B_WIDTH, 1)
    *a_parts, bk, bqt, bvt = _inproj(x2, mod_b, norm1_g, w_in, qk_gains, bq_gain, bsz, seq)

    dil_outs = _dilated(a_parts, bsz, seq)
    lam = (jnp.exp(jnp.sum(lambda_q1 * lambda_k1)) - jnp.exp(jnp.sum(lambda_q2 * lambda_k2)) + LAMBDA_INIT)
    slopes_b = 2.0 ** (-8.0 * np.arange(1, B_HEADS + 1) / B_HEADS)
    scalars = jnp.concatenate([lam.reshape(1), jnp.asarray(slopes_b, F32)]).astype(F32)
    bk4 = bk.reshape(bsz, seq // DIFF_BLOCK, DIFF_BLOCK, B_WIDTH)
    o_b = _diff(bqt, bk4, bvt, scalars, diff_norm_g, bsz, seq).reshape(n, B_WIDTH)

    x1, he, idx, rank, cnt = _router(dil_outs, o_b, x2, mod_b, w_out, norm2_g, w_router, b_router, seq)

    n_tiles = n // ROW_TILE
    counts = cnt[:, :, 0].astype(I32)
    seg = (counts + SEG_ALIGN - 1) // SEG_ALIGN * SEG_ALIGN
    loff = jnp.cumsum(seg, axis=1) - seg
    region = jnp.sum(seg, axis=0)
    padded = (region + MOE_TILE - 1) // MOE_TILE * MOE_TILE
    pad_end = jnp.cumsum(padded)
    pad_start = pad_end - padded
    goff = pad_start[None, :] + jnp.cumsum(seg, axis=0) - seg
    sizes = jnp.asarray(SEG_SIZES, I32)[None, :, None]
    has = (seg[:, None, :] & sizes) != 0
    within = seg[:, None, :] & ~(2 * sizes - 1)
    place = jnp.cumsum(has, axis=-1) - has
    pick = has[:, :, None, :] & (place[:, :, None, :] == jnp.arange(N_EXPERTS, dtype=I32)[None, None, :, None])
    listed = lambda rows_: jnp.sum(jnp.where(pick, rows_[:, :, None, :], 0), axis=-1).reshape(-1).astype(I32)
    tables = (listed(loff[:, None, :] + within), listed(goff[:, None, :] + within),
              jnp.sum(has, axis=-1).reshape(-1).astype(I32), jnp.sum(seg, axis=1).astype(I32))
    n_blocks = (n * TOP_K + n_tiles * N_EXPERTS * (SEG_ALIGN - 1) + N_EXPERTS * (MOE_TILE - 1)) // MOE_TILE
    n_used = (pad_end[-1] // MOE_TILE).astype(I32)
    tail = jnp.concatenate([pad_start + region, padded - region, n_used.reshape(1)])

    ls, xs = _sort(tables, tail, he, idx, rank, loff.reshape(n_tiles, N_EXPERTS, 1), n_blocks * MOE_TILE)
    yb = _experts(pad_start // MOE_TILE, padded // MOE_TILE, xs, w_gate_up, b_gate_up, w_down, b_down)
    out = _combine(tables, yb, ls.T, x1, mod_b, seq)
    return out.reshape(bsz, seq, D_MODEL)


def kernel(x, c, w_ada, b_ada, norm1_g, w_in, a_q_norm_g, a_k_norm_g, b_q_norm_g, b_k_norm_g, lambda_q1, lambda_k1,
           lambda_q2, lambda_k2, diff_norm_g, w_out, norm2_g, w_router, b_router, w_gate_up, b_gate_up, w_down,
           b_down):
    args = (w_ada, b_ada, norm1_g, w_in, a_q_norm_g, a_k_norm_g, b_q_norm_g, b_k_norm_g, lambda_q1, lambda_k1,
            lambda_q2, lambda_k2, diff_norm_g, w_out, norm2_g, w_router, b_router, w_gate_up, b_gate_up, w_down,
            b_down)
    return _layer(x, c, *[a[0] for a in args])
```

```python
import functools

import numpy as np
import jax
import jax.numpy as jnp
from jax import lax
from jax.experimental import pallas as pl
from jax.experimental.pallas import tpu as pltpu

F32 = jnp.float32
BF16 = jnp.bfloat16
I32 = jnp.int32

D_MODEL = 1024
HEAD_DIM = 64
A_WIDTH = 512
B_WIDTH = 512
A_HEADS = 8
B_HEADS = 4
DIL_PATTERNS = ((128, 1), (512, 4), (2048, 16))
BAND = 128
N_EXPERTS = 32
TOP_K = 4
D_FF = 1024
SWIGLU_LIMIT = 7.0
SWIGLU_ALPHA = 1.702
RMS_EPS = 1e-6
ATTN_SCALE = HEAD_DIM ** -0.5
LOG2E = 1.4426950408889634
LN2 = 0.6931471805599453
LAMBDA_INIT = 0.8 - 0.6 * 1.0

LANES = 128
ROW_TILE = 512
DIFF_BLOCK = 512
DIFF_LANES = 256
DIFF_ROWS = 64
ONES_ROWS = 16
VT_ROWS = B_HEADS * (2 * HEAD_DIM + ONES_ROWS)
MOE_TILE = 256
X_AHEAD = 4
X_SLOTS = X_AHEAD + 2
EXTRA_LANES = 128
ROW_WIDTH = D_MODEL + EXTRA_LANES
SEG_ALIGN = 16
SEG_SIZES = (512, 256, 128, 64, 32, 16)
SLOT_CHUNK = 512
LOCAL_SLOTS = 2560
assert ROW_TILE == DIFF_BLOCK == SEG_SIZES[0]
assert LOCAL_SLOTS >= ROW_TILE * TOP_K + N_EXPERTS * (SEG_ALIGN - 1) and LOCAL_SLOTS % SLOT_CHUNK == 0
VMEM_LIMIT = 56 * 1024 * 1024


def _cparams(sem, **flags):
    return pltpu.CompilerParams(dimension_semantics=sem, vmem_limit_bytes=VMEM_LIMIT, flags=flags or None)


def _split_bf16(a):
    hi = a.astype(BF16)
    lo = (a - hi.astype(F32)).astype(BF16)
    return hi, lo


def _dot_nt(a, b):
    return lax.dot_general(a, b, (((1,), (1,)), ((), ())), preferred_element_type=F32)


def _dot(a, b):
    return jnp.dot(a, b, preferred_element_type=F32)


def _ada_kernel(c_ref, w_ref, b_ref, o_ref):
    c = c_ref[...]
    s = c / (1.0 + jnp.exp(-c))
    sh, sl = _split_bf16(s)
    wh, wl = _split_bf16(w_ref[...])
    o_ref[0] = _dot(sh, wh) + _dot(sh, wl) + _dot(sl, wh) + b_ref[0]


def _ada(c, w_ada, b_ada):
    bsz = c.shape[0]
    return pl.pallas_call(
        _ada_kernel,
        grid=(6,),
        in_specs=[
            pl.BlockSpec((bsz, D_MODEL), lambda j: (0, 0)),
            pl.BlockSpec((D_MODEL, D_MODEL), lambda j: (0, j)),
            pl.BlockSpec((1, 1, D_MODEL), lambda j: (j, 0, 0)),
        ],
        out_specs=pl.BlockSpec((1, bsz, D_MODEL), lambda j: (j, 0, 0)),
        out_shape=jax.ShapeDtypeStruct((6, bsz, D_MODEL), F32),
        compiler_params=_cparams(("arbitrary",)),
        name="ada",
    )(c, w_ada, b_ada.reshape(6, 1, D_MODEL))


def _inproj_kernel(x_ref, mod_ref, g1_ref, wn_ref, wt_ref, gm_ref, qkg_ref, bqg_ref, a1_ref, a4_ref, a16_ref, bk_ref,
                   bqt_ref, bvt_ref, a_sc):
    a_refs = (a1_ref, a4_ref, a16_ref)
    x = x_ref[...]
    y = x * lax.rsqrt(jnp.mean(x * x, axis=-1, keepdims=True) + RMS_EPS) * g1_ref[...]
    h = (y * (1.0 + mod_ref[1:2, :]) + mod_ref[0:1, :]).astype(BF16)
    p = _dot(h, wn_ref[...])
    gm = gm_ref[...]

    def head_norm(t, g):
        sq = (t * t).astype(BF16)
        half = gm.shape[0]
        ss = jnp.concatenate([_dot(sq[:, :half], gm), _dot(sq[:, half:], gm)], axis=1)
        return t * lax.rsqrt(ss * (1.0 / HEAD_DIM) + RMS_EPS) * g

    w = A_WIDTH
    a_part = jnp.concatenate([head_norm(p[:, 0 * w:1 * w], qkg_ref[0:1, :]),
                              head_norm(p[:, 1 * w:2 * w], qkg_ref[1:2, :]), p[:, 2 * w:3 * w]], axis=1)
    n_col = a_sc.shape[0]
    for c in range(n_col):
        a_sc[c] = a_part[:, c * LANES:(c + 1) * LANES]
    for a_ref, (_, dil) in zip(a_refs, DIL_PATTERNS):
        if dil == 1:
            a_ref[0] = a_part.astype(BF16)
            continue
        rows = a_part.shape[0] // dil
        for r in range(dil):
            for c in range(n_col):
                a_ref[r, :, c * LANES:(c + 1) * LANES] = a_sc[c, pl.ds(r, rows, stride=dil), :].astype(BF16)
    bk_ref[...] = head_norm(p[:, 3 * w:4 * w], qkg_ref[2:3, :]).astype(BF16)

    pt = _dot_nt(wt_ref[...], h)
    t = pt.shape[1]
    bq = pt[:w].reshape(w // HEAD_DIM, HEAD_DIM, t)
    ss = jnp.sum(bq * bq, axis=1, keepdims=True)
    bq = (bq * lax.rsqrt(ss * (1.0 / HEAD_DIM) + RMS_EPS)).reshape(w, t) * bqg_ref[...]
    bqt_ref[...] = bq.astype(BF16)
    lanes = 2 * HEAD_DIM
    for hd in range(B_HEADS):
        base = hd * (lanes + ONES_ROWS)
        bvt_ref[base:base + lanes, :] = pt[w + hd * lanes:w + (hd + 1) * lanes].astype(BF16)
        bvt_ref[base + lanes:base + lanes + ONES_ROWS, :] = jnp.ones((ONES_ROWS, t), BF16)


def _inproj(x2, mod_b, norm1_g, w_in, qk_gains, bq_gain, bsz, seq):
    n = x2.shape[0]
    t = ROW_TILE
    per_b = seq // t
    w = A_WIDTH
    head_of_lane = np.arange(w // 2) // HEAD_DIM
    gmat = jnp.asarray(head_of_lane[:, None] == head_of_lane[None, :], BF16)
    w_bf = w_in.astype(BF16)
    w_nat = jnp.concatenate([w_bf[:, :3 * w], w_bf[:, 4 * w:5 * w]], axis=1)
    w_tr = jnp.concatenate([w_bf[:, 3 * w:4 * w], w_bf[:, 5 * w:]], axis=1).T
    tr_spec = pl.BlockSpec((None, None, w, t), lambda i: (i // per_b, i % per_b, 0, 0))
    tr_shape = jax.ShapeDtypeStruct((bsz, per_b, w, t), BF16)
    a_specs = [pl.BlockSpec((None, dil, t // dil, 3 * w), lambda i: (i // per_b, 0, i % per_b, 0))
               for _, dil in DIL_PATTERNS]
    a_shapes = [jax.ShapeDtypeStruct((bsz, dil, seq // dil, 3 * w), BF16) for _, dil in DIL_PATTERNS]
    return pl.pallas_call(
        _inproj_kernel,
        grid=(n // t,),
        in_specs=[
            pl.BlockSpec((t, D_MODEL), lambda i: (i, 0)),
            pl.BlockSpec((None, 6, D_MODEL), lambda i: (i // per_b, 0, 0)),
            pl.BlockSpec((1, D_MODEL), lambda i: (0, 0)),
            pl.BlockSpec((D_MODEL, 4 * w), lambda i: (0, 0)),
            pl.BlockSpec((2 * w, D_MODEL), lambda i: (0, 0)),
            pl.BlockSpec((w // 2, w // 2), lambda i: (0, 0)),
            pl.BlockSpec((3, w), lambda i: (0, 0)),
            pl.BlockSpec((w, 1), lambda i: (0, 0)),
        ],
        out_specs=a_specs + [pl.BlockSpec((t, w), lambda i: (i, 0)), tr_spec,
                             pl.BlockSpec((None, None, VT_ROWS, t), lambda i: (i // per_b, i % per_b, 0, 0))],
        out_shape=a_shapes + [jax.ShapeDtypeStruct((n, w), BF16), tr_shape,
                              jax.ShapeDtypeStruct((bsz, per_b, VT_ROWS, t), BF16)],
        scratch_shapes=[pltpu.VMEM((3 * w // LANES, t, LANES), F32)],
        compiler_params=_cparams(("parallel",)),
        name="inproj",
    )(x2, mod_b, norm1_g.reshape(1, D_MODEL), w_nat, w_tr, gmat, qk_gains, bq_gain)


def _dilated_bias_table():
    ik = np.arange(2 * BAND)[:, None]
    iq = np.arange(BAND)[None, :]
    delta = iq - ik + BAND
    in_band = (delta >= 0) & (delta <= BAND)
    slopes = 2.0 ** (-8.0 * np.arange(1, A_HEADS + 1) / A_HEADS)
    tbl = np.zeros((len(DIL_PATTERNS), 2, A_HEADS // 2, 2 * BAND, 2 * BAND), np.float32)
    for p, (_, dil) in enumerate(DIL_PATTERNS):
        for first in range(2):
            valid = in_band & ((ik >= BAND) if first else True)
            for h in range(A_HEADS):
                cols = slice(BAND * (h % 2), BAND * (h % 2 + 1))
                tbl[p, first, h // 2, :, cols] = np.where(valid, -slopes[h] * LOG2E * (delta * dil), -np.inf)
    return jnp.asarray(tbl)


def _dilated_kernel(tbl_ref, *refs, n_blocks):
    lanes = 2 * HEAD_DIM
    n = pl.program_id(1)
    n_pat = len(DIL_PATTERNS)
    s_sc, p_sc = refs[len(refs) - 2:]
    outs = refs[len(refs) - 2 - 2 * n_pat:len(refs) - 2]
    lane = lax.broadcasted_iota(I32, (BAND, lanes), 1)
    low = lane < HEAD_DIM
    work = []
    pos = 0
    for p, (_, dil) in enumerate(DIL_PATTERNS):
        per_res = n_blocks // dil
        if per_res > 1:
            cur_ref, prev_ref = refs[pos:pos + 2]
            pos += 2
            first = (n % per_res == 0).astype(I32)
        else:
            cur_ref, prev_ref, first = refs[pos], None, None
            pos += 1
        for j in range(A_HEADS // 2):
            work.append((p, j, cur_ref, prev_ref, first))

    def keys_of(prev_ref, cur_ref, cs):
        return cur_ref[:, cs] if prev_ref is None else jnp.concatenate([prev_ref[:, cs], cur_ref[:, cs]], axis=0)

    def cols_of(part, j):
        return slice(part * A_WIDTH + lanes * j, part * A_WIDTH + lanes * (j + 1))

    for w, (p, j, cur_ref, prev_ref, first) in enumerate(work):
        q2 = cur_ref[:, cols_of(0, j)]
        zero = jnp.zeros_like(q2)
        qcat = jnp.concatenate([jnp.where(low, q2, zero), jnp.where(low, zero, q2)], axis=0)
        k2 = keys_of(prev_ref, cur_ref, cols_of(1, j))
        bias = tbl_ref[p, 1, j, BAND:, :] if prev_ref is None else tbl_ref[p, first, j]
        s_sc[w, :k2.shape[0], :] = _dot_nt(k2, qcat) + bias
    lses = []
    for w, (p, j, _, prev_ref, _) in enumerate(work):
        nk = BAND if prev_ref is None else 2 * BAND
        s = s_sc[w, :nk, :]
        m = jnp.max(s, axis=0, keepdims=True)
        e = jnp.exp2(s - m)
        den = jnp.sum(e, axis=0, keepdims=True)
        p_sc[w, :nk, :] = (e * (1.0 / den)).astype(BF16)
        lses.append((m + jnp.log2(den)) * LN2)
    for w, (p, j, cur_ref, prev_ref, _) in enumerate(work):
        nk = BAND if prev_ref is None else 2 * BAND
        cs = cols_of(0, j)
        v2 = keys_of(prev_ref, cur_ref, cols_of(2, j))
        r = lax.dot_general(p_sc[w, :nk, :], v2, (((0,), (0,)), ((), ())), preferred_element_type=F32)
        outs[2 * p][:, cs] = jnp.where(low, r[:BAND], r[BAND:]).astype(BF16)
    for p in range(n_pat):
        rows = []
        for lse in lses[p * (A_HEADS // 2):(p + 1) * (A_HEADS // 2)]:
            rows += [lse[:, :BAND], lse[:, BAND:]]
        rows.append(jnp.zeros((lanes - A_HEADS, BAND), F32))
        outs[2 * p + 1][...] = jnp.concatenate(rows, axis=0).T


def _dilated(a_parts, bsz, seq):
    n_blocks = seq // BAND
    in_specs = [pl.BlockSpec((len(DIL_PATTERNS), 2, A_HEADS // 2, 2 * BAND, 2 * BAND), lambda b, n: (0, 0, 0, 0, 0))]
    n_work = len(DIL_PATTERNS) * A_HEADS // 2
    args = [_dilated_bias_table()]
    out_specs, out_shapes = [], []
    blk = (None, None, BAND, A_WIDTH)
    qkv = (None, None, BAND, 3 * A_WIDTH)
    for a_part, (_, dil) in zip(a_parts, DIL_PATTERNS):
        per_res = n_blocks // dil
        cur = lambda b, n, per_res=per_res: (b, n // per_res, n % per_res, 0)
        prev = lambda b, n, per_res=per_res: (b, n // per_res, jnp.maximum(n % per_res - 1, 0), 0)
        in_specs += [pl.BlockSpec(qkv, cur)] + ([pl.BlockSpec(qkv, prev)] if per_res > 1 else [])
        args += [a_part] * (2 if per_res > 1 else 1)
        out_specs += [pl.BlockSpec(blk, cur), pl.BlockSpec((None, None, BAND, LANES), cur)]
        out_shapes += [jax.ShapeDtypeStruct((bsz, dil, seq // dil, A_WIDTH), BF16),
                       jax.ShapeDtypeStruct((bsz, dil, seq // dil, LANES), F32)]
    return pl.pallas_call(
        functools.partial(_dilated_kernel, n_blocks=n_blocks),
        grid=(bsz, n_blocks),
        in_specs=in_specs,
        out_specs=out_specs,
        out_shape=out_shapes,
        scratch_shapes=[pltpu.VMEM((n_work, 2 * BAND, 2 * BAND), F32), pltpu.VMEM((n_work, 2 * BAND, 2 * BAND), BF16)],
        compiler_params=_cparams(("parallel", "parallel")),
        name="dilated",
    )(*args)


def _diff_bias_table():
    t = DIFF_BLOCK
    rel = np.arange(t)[None, :] - np.arange(t)[:, None]
    slopes = 2.0 ** (-8.0 * np.arange(1, B_HEADS + 1) / B_HEADS)
    tbl = np.zeros((B_HEADS, 2, t, t), np.float32)
    for h in range(B_HEADS):
        tbl[h, 0] = -slopes[h] * LOG2E * rel
        tbl[h, 1] = np.where(rel >= 0, -slopes[h] * LOG2E * rel, -np.inf)
    return jnp.asarray(tbl)


def _diff_kernel(sc_ref, qt_ref, k_ref, vt_ref, g_ref, bias_sc, o_ref, qst_sc, m_sc, l_sc, a_sc, acc_sc, s_sc, p_sc):
    t = DIFF_BLOCK
    lanes = 2 * HEAD_DIM
    qi = pl.program_id(1)
    lam = sc_ref[0]
    slopes = [jnp.full((1, 1), sc_ref[1 + h], F32) for h in range(B_HEADS)]
    row = lax.broadcasted_iota(I32, (lanes, t), 0)
    for h in range(B_HEADS):
        qt = qt_ref[h * lanes:(h + 1) * lanes, :]
        zero = jnp.zeros_like(qt)
        qst_sc[h, :, :t] = jnp.where(row < HEAD_DIM, qt, zero)
        qst_sc[h, :, t:] = jnp.where(row < HEAD_DIM, zero, qt)
    m_sc[...] = jnp.full(m_sc.shape, -jnp.inf, F32)
    l_sc[...] = jnp.zeros(l_sc.shape, F32)
    acc_sc[...] = jnp.zeros(acc_sc.shape, F32)

    n_grp = 2 * t // DIFF_LANES
    items = [(h, g) for h in range(B_HEADS) for g in range(n_grp)]
    chunks = [slice(r * DIFF_ROWS, (r + 1) * DIFF_ROWS) for r in range(t // DIFF_ROWS)]

    def keys_needed(g, diag):
        return min(t, (g * DIFF_LANES) % t + DIFF_LANES) if diag else t

    def block(n, diag):
        cs = [-slopes[h] * LOG2E * (jnp.full((1, 1), (qi - n) * t, I32)).astype(F32) for h in range(B_HEADS)]
        for w, (h, g) in enumerate(items):
            nk = keys_needed(g, diag)
            cols = slice(g * DIFF_LANES, (g + 1) * DIFF_LANES)
            bcols = slice((g * DIFF_LANES) % t, (g * DIFF_LANES) % t + DIFF_LANES)
            s_sc[w, :nk, :] = (_dot(k_ref[n, :nk, h * lanes:(h + 1) * lanes], qst_sc[h, :, cols])
                               + bias_sc[h, diag, :nk, bcols])
        for w, (h, g) in enumerate(items):
            used = chunks[:keys_needed(g, diag) // DIFF_ROWS]
            cols = slice(g * DIFF_LANES, (g + 1) * DIFF_LANES)
            top = s_sc[w, used[0], :]
            for rows in used[1:]:
                top = jnp.maximum(top, s_sc[w, rows, :])
            m_prev = m_sc[h, :, cols]
            m_new = jnp.maximum(m_prev, jnp.max(top, axis=0, keepdims=True) + cs[h])
            alpha = jnp.exp2(m_prev - m_new)
            shift = m_new - cs[h]
            for rows in used:
                p_sc[w, rows, :] = jnp.exp2(s_sc[w, rows, :] - shift).astype(BF16)
            m_sc[h, :, cols] = m_new
            a_sc[h, :, cols] = alpha
        for w, (h, g) in enumerate(items):
            nk = keys_needed(g, diag)
            cols = slice(g * DIFF_LANES, (g + 1) * DIFF_LANES)
            pv = _dot(vt_ref[n, h * (lanes + ONES_ROWS):(h + 1) * (lanes + ONES_ROWS), :nk], p_sc[w, :nk, :])
            alpha = a_sc[h, :, cols]
            acc_sc[h, :, cols] = alpha * acc_sc[h, :, cols] + pv[:lanes]
            l_sc[h, :, cols] = alpha * l_sc[h, :, cols] + pv[lanes:lanes + 1]

    def body(n, carry):
        block(n, 0)
        return carry

    lax.fori_loop(0, qi, body, 0)
    block(qi, 1)
    for h in range(B_HEADS):
        o = acc_sc[h] / l_sc[h]
        o = o[:, :t] - lam * o[:, t:]
        o = o * lax.rsqrt(jnp.mean(o * o, axis=0, keepdims=True) + RMS_EPS) * g_ref[...]
        o_ref[:, h * lanes:(h + 1) * lanes] = (o * (1.0 - LAMBDA_INIT)).T.astype(BF16)


def _diff(bqt, bk, bvt, scalars, diff_norm_g, bsz, seq):
    t = DIFF_BLOCK
    nb = seq // t
    lanes = 2 * HEAD_DIM
    n_items = B_HEADS * 2 * t // DIFF_LANES
    return pl.pallas_call(
        _diff_kernel,
        grid=(bsz, nb),
        in_specs=[
            pl.BlockSpec(memory_space=pltpu.SMEM),
            pl.BlockSpec((None, None, B_WIDTH, t), lambda b, i: (b, i, 0, 0)),
            pl.BlockSpec((None, nb, t, B_WIDTH), lambda b, i: (b, 0, 0, 0)),
            pl.BlockSpec((None, nb, VT_ROWS, t), lambda b, i: (b, 0, 0, 0)),
            pl.BlockSpec((lanes, 1), lambda b, i: (0, 0)),
            pl.BlockSpec((B_HEADS, 2, t, t), lambda b, i: (0, 0, 0, 0)),
        ],
        out_specs=pl.BlockSpec((None, t, B_WIDTH), lambda b, i: (b, i, 0)),
        out_shape=jax.ShapeDtypeStruct((bsz, seq, B_WIDTH), BF16),
        scratch_shapes=[pltpu.VMEM((B_HEADS, lanes, 2 * t), BF16)] + [pltpu.VMEM((B_HEADS, 1, 2 * t), F32)] * 3 + [
            pltpu.VMEM((B_HEADS, lanes, 2 * t), F32),
            pltpu.VMEM((n_items, t, DIFF_LANES), F32), pltpu.VMEM((n_items, t, DIFF_LANES), BF16)],
        compiler_params=_cparams(("parallel", "parallel")),
        name="diff",
    )(scalars, bqt, bk, bvt, diff_norm_g.reshape(lanes, 1), _diff_bias_table())


def _router_kernel(o0, l0, o1, l1, o2, l2, ob_ref, x_ref, mod_ref, wout_ref, g2_ref, wrh_ref, wrl_ref, br_ref,
                   tri_ref, spread_ref, x1_ref, he_ref, idx_ref, rank_ref, cnt_ref, *order_sc):
    def token_order(ref, scratch):
        dil, rows, width = ref.shape
        if dil == 1:
            return ref[0].astype(F32)
        n_col = width // LANES
        for r in range(dil):
            for c in range(n_col):
                scratch[c, pl.ds(r, rows, stride=dil), :] = ref[r, :, c * LANES:(c + 1) * LANES].astype(F32)
        return jnp.concatenate([scratch[c] for c in range(n_col)], axis=1)

    os_ = [token_order(o0, None), token_order(o1, order_sc[0]), token_order(o2, order_sc[2])]
    ls = [token_order(l0, None), token_order(l1, order_sc[1]), token_order(l2, order_sc[3])]
    mx = jnp.maximum(jnp.maximum(ls[0], ls[1]), ls[2])
    ws = [jnp.exp(l - mx) for l in ls]
    den = ws[0] + ws[1] + ws[2]
    spread = spread_ref[...]
    oa = jnp.zeros(os_[0].shape, F32)
    for w, o in zip(ws, os_):
        wh, wl = _split_bf16(w / den)
        oa = oa + (_dot(wh, spread) + _dot(wl, spread)) * o
    mixed = _dot(oa.astype(BF16), wout_ref[:A_WIDTH, :]) + _dot(ob_ref[...], wout_ref[A_WIDTH:, :])
    x1 = x_ref[...] + mod_ref[2:3, :] * mixed
    x1_ref[...] = x1
    y = x1 * lax.rsqrt(jnp.mean(x1 * x1, axis=-1, keepdims=True) + RMS_EPS) * g2_ref[...]
    h2 = y * (1.0 + mod_ref[4:5, :]) + mod_ref[3:4, :]
    he_ref[:, :D_MODEL] = h2.astype(BF16)

    hh, hl = _split_bf16(h2)
    wrh = wrh_ref[...]
    logits = _dot_nt(wrh, hh) + _dot_nt(wrh, hl) + _dot_nt(wrl_ref[...], hh) + br_ref[...]
    t = logits.shape[1]
    eid = lax.broadcasted_iota(I32, (N_EXPERTS, t), 0)
    vals, idxs, hots = [], [], []
    cur = logits
    for _ in range(TOP_K):
        v = jnp.max(cur, axis=0, keepdims=True)
        ik = jnp.min(jnp.where(cur == v, eid, N_EXPERTS), axis=0, keepdims=True)
        hot = eid == ik
        vals.append(v)
        idxs.append(ik)
        hots.append(hot)
        cur = jnp.where(hot, -jnp.inf, cur)
    es = [jnp.exp(v - vals[0]) for v in vals]
    esum = es[0] + es[1] + es[2] + es[3]
    idx_ref[...] = jnp.concatenate(idxs, axis=0)

    rows = [ik.astype(F32) for ik in idxs]
    for e in es:
        g = e / esum
        hi = g.astype(BF16).astype(F32)
        mid = (g - hi).astype(BF16).astype(F32)
        rows += [hi, mid, g - hi - mid]
    rows.append(jnp.zeros((EXTRA_LANES - len(rows), t), F32))
    he_ref[:, D_MODEL:] = jnp.concatenate(rows, axis=0).T.astype(BF16)

    sel = jnp.where(hots[0] | hots[1] | hots[2] | hots[3], 1.0, 0.0)
    before = _dot(sel.astype(BF16), tri_ref[...])
    ranks = [jnp.sum(jnp.where(hot, before, 0.0), axis=0, keepdims=True) for hot in hots]
    rank_ref[...] = jnp.concatenate(ranks, axis=0).astype(I32)
    cnt_ref[...] = jnp.broadcast_to(jnp.sum(sel, axis=1, keepdims=True), cnt_ref.shape)


def _router(dil_outs, o_b, x2, mod_b, w_out, norm2_g, w_router, b_router, seq):
    n = x2.shape[0]
    t = ROW_TILE
    per_b = seq // t
    wr_t = w_router.T
    wrh = wr_t.astype(BF16)
    wrl = (wr_t - wrh.astype(F32)).astype(BF16)
    tri = jnp.asarray(np.arange(t)[:, None] < np.arange(t)[None, :], BF16)
    row = lambda w: pl.BlockSpec((t, w), lambda i: (i, 0))
    full = lambda a, b: pl.BlockSpec((a, b), lambda i: (0, 0))
    tok = lambda: pl.BlockSpec((TOP_K, t), lambda i: (0, i))
    grouped = lambda dil, w: pl.BlockSpec((None, dil, t // dil, w), lambda i: (i // per_b, 0, i % per_b, 0))
    widths = (A_WIDTH, LANES)
    head_of_lane = np.arange(A_WIDTH) // HEAD_DIM
    spread = jnp.asarray(np.arange(LANES)[:, None] == head_of_lane[None, :], BF16)
    return pl.pallas_call(
        _router_kernel,
        grid=(n // t,),
        in_specs=[grouped(dil, w) for _, dil in DIL_PATTERNS for w in widths] + [
            row(B_WIDTH), row(D_MODEL),
            pl.BlockSpec((None, 6, D_MODEL), lambda i: (i // per_b, 0, 0)),
            full(D_MODEL, D_MODEL), full(1, D_MODEL), full(N_EXPERTS, D_MODEL), full(N_EXPERTS, D_MODEL),
            full(N_EXPERTS, 1), full(t, t), full(LANES, A_WIDTH),
        ],
        out_specs=[row(D_MODEL), row(ROW_WIDTH), tok(), tok(),
                   pl.BlockSpec((None, N_EXPERTS, 128), lambda i: (i, 0, 0))],
        out_shape=[
            jax.ShapeDtypeStruct((n, D_MODEL), F32),
            jax.ShapeDtypeStruct((n, ROW_WIDTH), BF16),
            jax.ShapeDtypeStruct((TOP_K, n), I32),
            jax.ShapeDtypeStruct((TOP_K, n), I32),
            jax.ShapeDtypeStruct((n // t, N_EXPERTS, 128), F32),
        ],
        scratch_shapes=[pltpu.VMEM((w // LANES, t, LANES), F32) for _, dil in DIL_PATTERNS if dil > 1 for w in widths],
        compiler_params=_cparams(("parallel",)),
        name="router",
    )(*dil_outs, o_b, x2, mod_b, w_out.astype(BF16), norm2_g.reshape(1, D_MODEL), wrh, wrl,
      b_router.reshape(N_EXPERTS, 1), tri, spread)


def _start_pieces(tables, tile, local_ref, hbm_ref, sem, outbound):
    lo_ref, go_ref, cnt_ref, _ = tables
    for c, size in enumerate(SEG_SIZES):
        cls = tile * len(SEG_SIZES) + c

        def per_piece(k, carry, size=size, cls=cls):
            lo, go = lo_ref[cls * N_EXPERTS + k], go_ref[cls * N_EXPERTS + k]
            loc = local_ref.at[pl.ds(pl.multiple_of(lo, SEG_ALIGN), size)]
            glob = hbm_ref.at[pl.ds(pl.multiple_of(go, SEG_ALIGN), size)]
            cp = pltpu.make_async_copy(loc, glob, sem) if outbound else pltpu.make_async_copy(glob, loc, sem)
            cp.start()
            return carry

        lax.fori_loop(0, cnt_ref[cls], per_piece, 0)


def _start(cp):
    cp.start()


def _wait(cp):
    cp.wait()


def _tile_rows(tables, tile):
    return tables[3][tile]


def _wait_rows(rows, local_ref, hbm_ref, sem, outbound):
    size = 1 << (LOCAL_SLOTS.bit_length() - 1)
    while size >= SEG_ALIGN:
        loc, glob = local_ref.at[pl.ds(0, size)], hbm_ref.at[pl.ds(0, size)]
        cp = pltpu.make_async_copy(loc, glob, sem) if outbound else pltpu.make_async_copy(glob, loc, sem)
        pl.when((rows & size) != 0)(cp.wait)
        size //= 2


def _one_hot_any(j, targets):
    out = jnp.zeros(j.shape, F32)
    for tgt in targets:
        out = jnp.where(j == tgt, 1.0, out)
    return out


def _sort_kernel(lo_ref, go_ref, cnt_ref, rows_ref, tail_ref, he_ref, idx_ref, rank_ref, lcol_ref, ls_ref, xs_hbm,
                 xl, zbuf, sems, zsem):
    tables = (lo_ref, go_ref, cnt_ref, rows_ref)
    i = pl.program_id(0)
    last = pl.num_programs(0) - 1
    slot = i % 2
    t = he_ref.shape[0]

    @pl.when(i == 0)
    def _():
        zbuf[...] = jnp.zeros(zbuf.shape, BF16)

        def tails(action):
            def per_expert(e, carry):
                off, n = tail_ref[e], tail_ref[N_EXPERTS + e]
                done = jnp.int32(0)
                for size in SEG_SIZES:
                    if size < MOE_TILE:
                        take = (n & size) != 0
                        dst = xs_hbm.at[pl.ds(pl.multiple_of(off + done, SEG_ALIGN), size)]
                        pl.when(take)(functools.partial(action, pltpu.make_async_copy(zbuf.at[pl.ds(0, size)], dst, zsem)))
                        done = done + jnp.where(take, size, 0)
                return carry

            lax.fori_loop(0, N_EXPERTS, per_expert, 0)

            def per_block(b, carry):
                dst = xs_hbm.at[pl.ds(pl.multiple_of(b * MOE_TILE, MOE_TILE), MOE_TILE)]
                action(pltpu.make_async_copy(zbuf, dst, zsem))
                return carry

            lax.fori_loop(tail_ref[2 * N_EXPERTS], xs_hbm.shape[0] // MOE_TILE, per_block, 0)

        tails(_start)
        tails(_wait)

    eid = lax.broadcasted_iota(I32, (N_EXPERTS, t), 0)
    lcol = lcol_ref[...]
    ls = []
    for k in range(TOP_K):
        off = jnp.sum(jnp.where(eid == idx_ref[k:k + 1, :], lcol, 0), axis=0, keepdims=True)
        ls.append(off + rank_ref[k:k + 1, :])
    ls_ref[...] = jnp.concatenate(ls, axis=0)

    he = he_ref[...]
    for jc in range(LOCAL_SLOTS // SLOT_CHUNK):
        j = lax.broadcasted_iota(I32, (SLOT_CHUNK, t), 0) + jc * SLOT_CHUNK
        perm = _one_hot_any(j, ls).astype(BF16)
        xl[slot, jc * SLOT_CHUNK:(jc + 1) * SLOT_CHUNK, :] = _dot(perm, he).astype(BF16)

    _start_pieces(tables, i, xl.at[slot], xs_hbm, sems.at[slot], True)

    @pl.when(i > 0)
    def _():
        _wait_rows(_tile_rows(tables, i - 1), xl.at[1 - slot], xs_hbm, sems.at[1 - slot], True)

    @pl.when(i == last)
    def _():
        _wait_rows(_tile_rows(tables, i), xl.at[slot], xs_hbm, sems.at[slot], True)


def _sort(tables, tail, he, idx, rank, lcol, n_rows):
    n = he.shape[0]
    t = ROW_TILE
    tok = lambda: pl.BlockSpec((TOP_K, t), lambda i, *_: (0, i))
    return pl.pallas_call(
        _sort_kernel,
        grid_spec=pltpu.PrefetchScalarGridSpec(
            num_scalar_prefetch=5,
            grid=(n // t,),
            in_specs=[
                pl.BlockSpec((t, ROW_WIDTH), lambda i, *_: (i, 0)),
                tok(), tok(),
                pl.BlockSpec((None, N_EXPERTS, 1), lambda i, *_: (i, 0, 0)),
            ],
            out_specs=[tok(), pl.BlockSpec(memory_space=pl.ANY)],
            scratch_shapes=[pltpu.VMEM((2, LOCAL_SLOTS, ROW_WIDTH), BF16), pltpu.VMEM((MOE_TILE, ROW_WIDTH), BF16),
                            pltpu.SemaphoreType.DMA((2,)), pltpu.SemaphoreType.DMA(())],
        ),
        out_shape=[jax.ShapeDtypeStruct((TOP_K, n), I32), jax.ShapeDtypeStruct((n_rows, ROW_WIDTH), BF16)],
        compiler_params=_cparams(("arbitrary",)),
        name="sort",
    )(*tables, tail, he, idx, rank, lcol)


def _experts_kernel(blk0_ref, nblk_ref, xs_hbm, wgu_ref, bgu_ref, wd_ref, bd_ref, yb_hbm, wgu_sc, wd_sc, xbuf, ybuf,
                    xsem, ysem):
    e = pl.program_id(0)
    tm = MOE_TILE
    first, nb = blk0_ref[e], nblk_ref[e]
    last_e = pl.num_programs(0) - 1
    total = blk0_ref[last_e] + nblk_ref[last_e]

    def rows(b, n=1):
        return pl.ds(pl.multiple_of(b * tm, tm), n * tm)

    def x_copy(b, slot):
        return pltpu.make_async_copy(xs_hbm.at[rows(b)], xbuf.at[rows(slot)], xsem.at[slot])

    def x_start(b):
        x_copy(b, b % X_SLOTS).start()
        pl.when(b % X_SLOTS == 0)(lambda: x_copy(b, X_SLOTS).start())

    def x_wait(b):
        x_copy(b, b % X_SLOTS).wait()
        pl.when(b % X_SLOTS == 0)(lambda: x_copy(b, X_SLOTS).wait())

    def y_copy(b, slot, n):
        return pltpu.make_async_copy(ybuf.at[slot, pl.ds(0, n * tm)], yb_hbm.at[rows(b, n)], ysem.at[slot])

    @pl.when(e == 0)
    def _():
        for b in range(X_AHEAD):
            pl.when(b < total)(functools.partial(x_start, b))

    def cast(r, carry):
        s = pl.multiple_of(r * LANES, LANES)
        wgu_sc[pl.ds(s, LANES), :] = wgu_ref[pl.ds(s, LANES), :].astype(BF16)
        wd_sc[pl.ds(s, LANES), :] = wd_ref[pl.ds(s, LANES), :].astype(BF16)
        return carry

    lax.fori_loop(0, D_MODEL // LANES, cast, 0)
    me = jnp.full((1, 1), e, I32).astype(F32)

    def unit(it, b, n):
        slot = it % 2
        for k in range(n):
            pl.when(b + X_AHEAD + k < total)(functools.partial(x_start, b + X_AHEAD + k))
        for k in range(n):
            x_wait(b + k)

        @pl.when(it >= 2)
        def _():
            y_copy(b, slot, 2).wait()

        x = xbuf.at[rows(b % X_SLOTS, n)]
        ext = x[:, D_MODEL:].astype(F32)
        gate = jnp.zeros((n * tm, 1), F32)
        for k in range(TOP_K):
            c = TOP_K + 3 * k
            gk = ext[:, c:c + 1] + ext[:, c + 1:c + 2] + ext[:, c + 2:c + 3]
            gate = gate + jnp.where(ext[:, k:k + 1] == me, gk, 0.0)

        gu = _dot(x[:, :D_MODEL], wgu_sc[...]) + bgu_ref[...]
        g = jnp.minimum(gu[:, :D_FF], SWIGLU_LIMIT)
        u = jnp.clip(gu[:, D_FF:], -SWIGLU_LIMIT, SWIGLU_LIMIT)
        act = (u + 1.0) * (g / (1.0 + jnp.exp(-SWIGLU_ALPHA * g)))
        ybuf[slot, :n * tm, :] = (gate * (_dot(act.astype(BF16), wd_sc[...]) + bd_ref[...])).astype(BF16)
        y_copy(b, slot, n).start()

    n_pair = nb // 2
    odd = nb % 2 == 1

    def pair(it, carry):
        unit(it, first + 2 * it, 2)
        return carry

    lax.fori_loop(0, n_pair, pair, 0)
    pl.when(odd)(functools.partial(unit, n_pair, first + 2 * n_pair, 1))

    last_slot = (n_pair - 1) % 2
    pl.when(odd)(y_copy(first, n_pair % 2, 1).wait)
    pl.when(odd & (n_pair >= 1))(y_copy(first, last_slot, 2).wait)
    pl.when(jnp.logical_not(odd) & (n_pair >= 2))(y_copy(first, n_pair % 2, 2).wait)
    pl.when(jnp.logical_not(odd) & (n_pair >= 1))(y_copy(first, last_slot, 2).wait)

    @pl.when(e == last_e)
    def _():
        ybuf[0, :tm, :] = jnp.zeros((tm, D_MODEL), BF16)

        def fill(action):
            def per_block(b, carry):
                action(pltpu.make_async_copy(ybuf.at[0, pl.ds(0, tm)], yb_hbm.at[rows(b)], ysem.at[0]))
                return carry

            lax.fori_loop(total, yb_hbm.shape[0] // tm, per_block, 0)

        fill(_start)
        fill(_wait)


def _experts(first_block, n_block, xs, w_gate_up, b_gate_up, w_down, b_down):
    n_rows = xs.shape[0]
    exp3 = lambda e, *_: (e, 0, 0)
    return pl.pallas_call(
        _experts_kernel,
        grid_spec=pltpu.PrefetchScalarGridSpec(
            num_scalar_prefetch=2,
            grid=(N_EXPERTS,),
            in_specs=[
                pl.BlockSpec(memory_space=pl.ANY),
                pl.BlockSpec((None, D_MODEL, 2 * D_FF), exp3),
                pl.BlockSpec((None, 1, 2 * D_FF), exp3),
                pl.BlockSpec((None, D_FF, D_MODEL), exp3),
                pl.BlockSpec((None, 1, D_MODEL), exp3),
            ],
            out_specs=pl.BlockSpec(memory_space=pl.ANY),
            scratch_shapes=[pltpu.VMEM((D_MODEL, 2 * D_FF), BF16), pltpu.VMEM((D_FF, D_MODEL), BF16),
                            pltpu.VMEM(((X_SLOTS + 1) * MOE_TILE, ROW_WIDTH), BF16),
                            pltpu.VMEM((2, 2 * MOE_TILE, D_MODEL), BF16),
                            pltpu.SemaphoreType.DMA((X_SLOTS + 1,)), pltpu.SemaphoreType.DMA((2,))],
        ),
        out_shape=jax.ShapeDtypeStruct((n_rows, D_MODEL), BF16),
        compiler_params=_cparams(("arbitrary",)),
        name="experts",
    )(first_block, n_block, xs, w_gate_up, b_gate_up.reshape(N_EXPERTS, 1, 2 * D_FF), w_down,
      b_down.reshape(N_EXPERTS, 1, D_MODEL))


def _combine_kernel(lo_ref, go_ref, cnt_ref, rows_ref, yb_hbm, lst_ref, x1_ref, mod_ref, o_ref, ybuf, sems):
    i = pl.program_id(0)
    slot = i % 2
    tables = (lo_ref, go_ref, cnt_ref, rows_ref)

    @pl.when(i == 0)
    def _():
        ybuf[...] = jnp.zeros(ybuf.shape, BF16)
        _start_pieces(tables, 0, ybuf.at[0], yb_hbm, sems.at[0], False)

    @pl.when(i + 1 < pl.num_programs(0))
    def _():
        _start_pieces(tables, i + 1, ybuf.at[1 - slot], yb_hbm, sems.at[1 - slot], False)

    _wait_rows(_tile_rows(tables, i), ybuf.at[slot], yb_hbm, sems.at[slot], False)

    lst = lst_ref[...]
    t = lst.shape[0]
    targets = [lst[:, k:k + 1] for k in range(TOP_K)]
    y = jnp.zeros((t, D_MODEL), F32)
    for jc in range(LOCAL_SLOTS // SLOT_CHUNK):
        j = lax.broadcasted_iota(I32, (t, SLOT_CHUNK), 1) + jc * SLOT_CHUNK
        pick = _one_hot_any(j, targets).astype(BF16)
        y = y + _dot(pick, ybuf[slot, jc * SLOT_CHUNK:(jc + 1) * SLOT_CHUNK, :])
    o_ref[...] = x1_ref[...] + mod_ref[5:6, :] * y


def _combine(tables, yb, ls_t, x1, mod_b, seq):
    n = x1.shape[0]
    t = ROW_TILE
    per_b = seq // t
    return pl.pallas_call(
        _combine_kernel,
        grid_spec=pltpu.PrefetchScalarGridSpec(
            num_scalar_prefetch=4,
            grid=(n // t,),
            in_specs=[
                pl.BlockSpec(memory_space=pl.ANY),
                pl.BlockSpec((t, TOP_K), lambda i, *_: (i, 0)),
                pl.BlockSpec((t, D_MODEL), lambda i, *_: (i, 0)),
                pl.BlockSpec((None, 6, D_MODEL), lambda i, *_: (i // per_b, 0, 0)),
            ],
            out_specs=pl.BlockSpec((t, D_MODEL), lambda i, *_: (i, 0)),
            scratch_shapes=[pltpu.VMEM((2, LOCAL_SLOTS, D_MODEL), BF16), pltpu.SemaphoreType.DMA((2,))],
        ),
        out_shape=jax.ShapeDtypeStruct((n, D_MODEL), F32),
        compiler_params=_cparams(("arbitrary",)),
        name="combine",
    )(*tables, yb, ls_t, x1, mod_b)


def _layer(x, c, w_ada, b_ada, norm1_g, w_in, a_q_norm_g, a_k_norm_g, b_q_norm_g, b_k_norm_g, lambda_q1,
           lambda_k1, lambda_q2, lambda_k2, diff_norm_g, w_out, norm2_g, w_router, b_router, w_gate_up,
           b_gate_up, w_down, b_down):
    bsz, seq, _ = x.shape
    n = bsz * seq
    x2 = x.reshape(n, D_MODEL)
    mod_b = _ada(c, w_ada, b_ada).transpose(1, 0, 2)

    qk_gains = jnp.stack([
        jnp.tile(a_q_norm_g, A_HEADS) * (ATTN_SCALE * LOG2E), jnp.tile(a_k_norm_g, A_HEADS),
        jnp.tile(b_k_norm_g, 2 * B_HEADS)])
    bq_gain = (jnp.tile(b_q_norm_g, 2 * B_HEADS) * (ATTN_SCALE * LOG2E)).reshape(B_WIDTH, 1)
    *a_parts, bk, bqt, bvt = _inproj(x2, mod_b, norm1_g, w_in, qk_gains, bq_gain, bsz, seq)

    dil_outs = _dilated(a_parts, bsz, seq)
    lam = (jnp.exp(jnp.sum(lambda_q1 * lambda_k1)) - jnp.exp(jnp.sum(lambda_q2 * lambda_k2)) + LAMBDA_INIT)
    slopes_b = 2.0 ** (-8.0 * np.arange(1, B_HEADS + 1) / B_HEADS)
    scalars = jnp.concatenate([lam.reshape(1), jnp.asarray(slopes_b, F32)]).astype(F32)
    bk4 = bk.reshape(bsz, seq // DIFF_BLOCK, DIFF_BLOCK, B_WIDTH)
    o_b = _diff(bqt, bk4, bvt, scalars, diff_norm_g, bsz, seq).reshape(n, B_WIDTH)

    x1, he, idx, rank, cnt = _router(dil_outs, o_b, x2, mod_b, w_out, norm2_g, w_router, b_router, seq)

    n_tiles = n // ROW_TILE
    counts = cnt[:, :, 0].astype(I32)
    seg = (counts + SEG_ALIGN - 1) // SEG_ALIGN * SEG_ALIGN
    loff = jnp.cumsum(seg, axis=1) - seg
    region = jnp.sum(seg, axis=0)
    padded = (region + MOE_TILE - 1) // MOE_TILE * MOE_TILE
    pad_end = jnp.cumsum(padded)
    pad_start = pad_end - padded
    goff = pad_start[None, :] + jnp.cumsum(seg, axis=0) - seg
    sizes = jnp.asarray(SEG_SIZES, I32)[None, :, None]
    has = (seg[:, None, :] & sizes) != 0
    within = seg[:, None, :] & ~(2 * sizes - 1)
    place = jnp.cumsum(has, axis=-1) - has
    pick = has[:, :, None, :] & (place[:, :, None, :] == jnp.arange(N_EXPERTS, dtype=I32)[None, None, :, None])
    listed = lambda rows_: jnp.sum(jnp.where(pick, rows_[:, :, None, :], 0), axis=-1).reshape(-1).astype(I32)
    tables = (listed(loff[:, None, :] + within), listed(goff[:, None, :] + within),
              jnp.sum(has, axis=-1).reshape(-1).astype(I32), jnp.sum(seg, axis=1).astype(I32))
    n_blocks = (n * TOP_K + n_tiles * N_EXPERTS * (SEG_ALIGN - 1) + N_EXPERTS * (MOE_TILE - 1)) // MOE_TILE
    n_used = (pad_end[-1] // MOE_TILE).astype(I32)
    tail = jnp.concatenate([pad_start + region, padded - region, n_used.reshape(1)])

    ls, xs = _sort(tables, tail, he, idx, rank, loff.reshape(n_tiles, N_EXPERTS, 1), n_blocks * MOE_TILE)
    yb = _experts(pad_start // MOE_TILE, padded // MOE_TILE, xs, w_gate_up, b_gate_up, w_down, b_down)
    out = _combine(tables, yb, ls.T, x1, mod_b, seq)
    return out.reshape(bsz, seq, D_MODEL)


def kernel(x, c, w_ada, b_ada, norm1_g, w_in, a_q_norm_g, a_k_norm_g, b_q_norm_g, b_k_norm_g, lambda_q1, lambda_k1,
           lambda_q2, lambda_k2, diff_norm_g, w_out, norm2_g, w_router, b_router, w_gate_up, b_gate_up, w_down,
           b_down):
    args = (w_ada, b_ada, norm1_g, w_in, a_q_norm_g, a_k_norm_g, b_q_norm_g, b_k_norm_g, lambda_q1, lambda_k1,
            lambda_q2, lambda_k2, diff_norm_g, w_out, norm2_g, w_router, b_router, w_gate_up, b_gate_up, w_down,
            b_down)
    return _layer(x, c, *[a[0] for a in args])
```

```python
import functools

import numpy as np
import jax
import jax.numpy as jnp
from jax import lax
from jax.experimental import pallas as pl
from jax.experimental.pallas import tpu as pltpu

F32 = jnp.float32
BF16 = jnp.bfloat16
I32 = jnp.int32

D_MODEL = 1024
HEAD_DIM = 64
A_WIDTH = 512
B_WIDTH = 512
A_HEADS = 8
B_HEADS = 4
DIL_PATTERNS = ((128, 1), (512, 4), (2048, 16))
BAND = 128
N_EXPERTS = 32
TOP_K = 4
D_FF = 1024
SWIGLU_LIMIT = 7.0
SWIGLU_ALPHA = 1.702
RMS_EPS = 1e-6
ATTN_SCALE = HEAD_DIM ** -0.5
LOG2E = 1.4426950408889634
LN2 = 0.6931471805599453
LAMBDA_INIT = 0.8 - 0.6 * 1.0

LANES = 128
ROW_TILE = 512
DIFF_BLOCK = 512
DIFF_LANES = 256
DIFF_ROWS = 64
ONES_ROWS = 16
VT_ROWS = B_HEADS * (2 * HEAD_DIM + ONES_ROWS)
MOE_TILE = 256
X_AHEAD = 4
X_SLOTS = X_AHEAD + 2
EXTRA_LANES = 128
ROW_WIDTH = D_MODEL + EXTRA_LANES
SEG_ALIGN = 16
SEG_SIZES = (512, 256, 128, 64, 32, 16)
SLOT_CHUNK = 512
LOCAL_SLOTS = 2560
assert ROW_TILE == DIFF_BLOCK == SEG_SIZES[0]
assert LOCAL_SLOTS >= ROW_TILE * TOP_K + N_EXPERTS * (SEG_ALIGN - 1) and LOCAL_SLOTS % SLOT_CHUNK == 0
VMEM_LIMIT = 56 * 1024 * 1024


def _cparams(sem, **flags):
    return pltpu.CompilerParams(dimension_semantics=sem, vmem_limit_bytes=VMEM_LIMIT, flags=flags or None)


def _split_bf16(a):
    hi = a.astype(BF16)
    lo = (a - hi.astype(F32)).astype(BF16)
    return hi, lo


def _dot_nt(a, b):
    return lax.dot_general(a, b, (((1,), (1,)), ((), ())), preferred_element_type=F32)


def _dot(a, b):
    return jnp.dot(a, b, preferred_element_type=F32)


def _ada_kernel(c_ref, w_ref, b_ref, o_ref):
    c = c_ref[...]
    s = c / (1.0 + jnp.exp(-c))
    sh, sl = _split_bf16(s)
    wh, wl = _split_bf16(w_ref[...])
    o_ref[0] = _dot(sh, wh) + _dot(sh, wl) + _dot(sl, wh) + b_ref[0]


def _ada(c, w_ada, b_ada):
    bsz = c.shape[0]
    return pl.pallas_call(
        _ada_kernel,
        grid=(6,),
        in_specs=[
            pl.BlockSpec((bsz, D_MODEL), lambda j: (0, 0)),
            pl.BlockSpec((D_MODEL, D_MODEL), lambda j: (0, j)),
            pl.BlockSpec((1, 1, D_MODEL), lambda j: (j, 0, 0)),
        ],
        out_specs=pl.BlockSpec((1, bsz, D_MODEL), lambda j: (j, 0, 0)),
        out_shape=jax.ShapeDtypeStruct((6, bsz, D_MODEL), F32),
        compiler_params=_cparams(("arbitrary",)),
        name="ada",
    )(c, w_ada, b_ada.reshape(6, 1, D_MODEL))


def _inproj_kernel(x_ref, mod_ref, g1_ref, w_ref, gm_ref, qkg_ref, bqg_ref, a1_ref, a4_ref, a16_ref, bk_ref,
                   bqt_ref, bvt_ref, a_sc, wt_sc):
    a_refs = (a1_ref, a4_ref, a16_ref)
    w = A_WIDTH

    @pl.when(pl.program_id(0) == 0)
    def _():
        wt_sc[:w, :] = w_ref[:, 3 * w:4 * w].T
        wt_sc[w:, :] = w_ref[:, 5 * w:6 * w].T

    x = x_ref[...]
    y = x * lax.rsqrt(jnp.mean(x * x, axis=-1, keepdims=True) + RMS_EPS) * g1_ref[...]
    h = (y * (1.0 + mod_ref[1:2, :]) + mod_ref[0:1, :]).astype(BF16)
    p = _dot(h, w_ref[:, :3 * w])
    gm = gm_ref[...]

    def head_norm(t, g):
        sq = (t * t).astype(BF16)
        half = gm.shape[0]
        ss = jnp.concatenate([_dot(sq[:, :half], gm), _dot(sq[:, half:], gm)], axis=1)
        return t * lax.rsqrt(ss * (1.0 / HEAD_DIM) + RMS_EPS) * g

    a_part = jnp.concatenate([head_norm(p[:, 0 * w:1 * w], qkg_ref[0:1, :]),
                              head_norm(p[:, 1 * w:2 * w], qkg_ref[1:2, :]), p[:, 2 * w:3 * w]], axis=1)
    n_col = a_sc.shape[0]
    for c in range(n_col):
        a_sc[c] = a_part[:, c * LANES:(c + 1) * LANES]
    for a_ref, (_, dil) in zip(a_refs, DIL_PATTERNS):
        if dil == 1:
            a_ref[0] = a_part.astype(BF16)
            continue
        rows = a_part.shape[0] // dil
        for r in range(dil):
            for c in range(n_col):
                a_ref[r, :, c * LANES:(c + 1) * LANES] = a_sc[c, pl.ds(r, rows, stride=dil), :].astype(BF16)
    bk_ref[...] = head_norm(_dot(h, w_ref[:, 4 * w:5 * w]), qkg_ref[2:3, :]).astype(BF16)

    pt = _dot_nt(wt_sc[...], h)
    t = pt.shape[1]
    bq = pt[:w].reshape(w // HEAD_DIM, HEAD_DIM, t)
    ss = jnp.sum(bq * bq, axis=1, keepdims=True)
    bq = (bq * lax.rsqrt(ss * (1.0 / HEAD_DIM) + RMS_EPS)).reshape(w, t) * bqg_ref[...]
    bqt_ref[...] = bq.astype(BF16)
    lanes = 2 * HEAD_DIM
    for hd in range(B_HEADS):
        base = hd * (lanes + ONES_ROWS)
        bvt_ref[base:base + lanes, :] = pt[w + hd * lanes:w + (hd + 1) * lanes].astype(BF16)
        bvt_ref[base + lanes:base + lanes + ONES_ROWS, :] = jnp.ones((ONES_ROWS, t), BF16)


def _inproj(x2, mod_b, norm1_g, w_in, qk_gains, bq_gain, bsz, seq):
    n = x2.shape[0]
    t = ROW_TILE
    per_b = seq // t
    w = A_WIDTH
    head_of_lane = np.arange(w // 2) // HEAD_DIM
    gmat = jnp.asarray(head_of_lane[:, None] == head_of_lane[None, :], BF16)
    tr_spec = pl.BlockSpec((None, None, w, t), lambda i: (i // per_b, i % per_b, 0, 0))
    tr_shape = jax.ShapeDtypeStruct((bsz, per_b, w, t), BF16)
    a_specs = [pl.BlockSpec((None, dil, t // dil, 3 * w), lambda i: (i // per_b, 0, i % per_b, 0))
               for _, dil in DIL_PATTERNS]
    a_shapes = [jax.ShapeDtypeStruct((bsz, dil, seq // dil, 3 * w), BF16) for _, dil in DIL_PATTERNS]
    return pl.pallas_call(
        _inproj_kernel,
        grid=(n // t,),
        in_specs=[
            pl.BlockSpec((t, D_MODEL), lambda i: (i, 0)),
            pl.BlockSpec((None, 6, D_MODEL), lambda i: (i // per_b, 0, 0)),
            pl.BlockSpec((1, D_MODEL), lambda i: (0, 0)),
            pl.BlockSpec((D_MODEL, 6 * w), lambda i: (0, 0)),
            pl.BlockSpec((w // 2, w // 2), lambda i: (0, 0)),
            pl.BlockSpec((3, w), lambda i: (0, 0)),
            pl.BlockSpec((w, 1), lambda i: (0, 0)),
        ],
        out_specs=a_specs + [pl.BlockSpec((t, w), lambda i: (i, 0)), tr_spec,
                             pl.BlockSpec((None, None, VT_ROWS, t), lambda i: (i // per_b, i % per_b, 0, 0))],
        out_shape=a_shapes + [jax.ShapeDtypeStruct((n, w), BF16), tr_shape,
                              jax.ShapeDtypeStruct((bsz, per_b, VT_ROWS, t), BF16)],
        scratch_shapes=[pltpu.VMEM((3 * w // LANES, t, LANES), F32), pltpu.VMEM((2 * w, D_MODEL), BF16)],
        compiler_params=_cparams(("arbitrary",)),
        name="inproj",
    )(x2, mod_b, norm1_g.reshape(1, D_MODEL), w_in.astype(BF16), gmat, qk_gains, bq_gain)


def _dilated_bias_table():
    ik = np.arange(2 * BAND)[:, None]
    iq = np.arange(BAND)[None, :]
    delta = iq - ik + BAND
    in_band = (delta >= 0) & (delta <= BAND)
    slopes = 2.0 ** (-8.0 * np.arange(1, A_HEADS + 1) / A_HEADS)
    tbl = np.zeros((len(DIL_PATTERNS), 2, A_HEADS // 2, 2 * BAND, 2 * BAND), np.float32)
    for p, (_, dil) in enumerate(DIL_PATTERNS):
        for first in range(2):
            valid = in_band & ((ik >= BAND) if first else True)
            for h in range(A_HEADS):
                cols = slice(BAND * (h % 2), BAND * (h % 2 + 1))
                tbl[p, first, h // 2, :, cols] = np.where(valid, -slopes[h] * LOG2E * (delta * dil), -np.inf)
    return jnp.asarray(tbl)


def _dilated_kernel(tbl_ref, *refs, n_blocks):
    lanes = 2 * HEAD_DIM
    n = pl.program_id(1)
    n_pat = len(DIL_PATTERNS)
    s_sc, p_sc = refs[len(refs) - 2:]
    outs = refs[len(refs) - 2 - 2 * n_pat:len(refs) - 2]
    lane = lax.broadcasted_iota(I32, (BAND, lanes), 1)
    low = lane < HEAD_DIM
    work = []
    pos = 0
    for p, (_, dil) in enumerate(DIL_PATTERNS):
        per_res = n_blocks // dil
        if per_res > 1:
            cur_ref, prev_ref = refs[pos:pos + 2]
            pos += 2
            first = (n % per_res == 0).astype(I32)
        else:
            cur_ref, prev_ref, first = refs[pos], None, None
            pos += 1
        for j in range(A_HEADS // 2):
            work.append((p, j, cur_ref, prev_ref, first))

    def keys_of(prev_ref, cur_ref, cs):
        return cur_ref[:, cs] if prev_ref is None else jnp.concatenate([prev_ref[:, cs], cur_ref[:, cs]], axis=0)

    def cols_of(part, j):
        return slice(part * A_WIDTH + lanes * j, part * A_WIDTH + lanes * (j + 1))

    for w, (p, j, cur_ref, prev_ref, first) in enumerate(work):
        q2 = cur_ref[:, cols_of(0, j)]
        zero = jnp.zeros_like(q2)
        qcat = jnp.concatenate([jnp.where(low, q2, zero), jnp.where(low, zero, q2)], axis=0)
        k2 = keys_of(prev_ref, cur_ref, cols_of(1, j))
        bias = tbl_ref[p, 1, j, BAND:, :] if prev_ref is None else tbl_ref[p, first, j]
        s_sc[w, :k2.shape[0], :] = _dot_nt(k2, qcat) + bias
    lses = []
    for w, (p, j, _, prev_ref, _) in enumerate(work):
        nk = BAND if prev_ref is None else 2 * BAND
        s = s_sc[w, :nk, :]
        m = jnp.max(s, axis=0, keepdims=True)
        e = jnp.exp2(s - m)
        den = jnp.sum(e, axis=0, keepdims=True)
        p_sc[w, :nk, :] = (e * (1.0 / den)).astype(BF16)
        lses.append((m + jnp.log2(den)) * LN2)
    for w, (p, j, cur_ref, prev_ref, _) in enumerate(work):
        nk = BAND if prev_ref is None else 2 * BAND
        cs = cols_of(0, j)
        v2 = keys_of(prev_ref, cur_ref, cols_of(2, j))
        r = lax.dot_general(p_sc[w, :nk, :], v2, (((0,), (0,)), ((), ())), preferred_element_type=F32)
        outs[2 * p][:, cs] = jnp.where(low, r[:BAND], r[BAND:]).astype(BF16)
    for p in range(n_pat):
        rows = []
        for lse in lses[p * (A_HEADS // 2):(p + 1) * (A_HEADS // 2)]:
            rows += [lse[:, :BAND], lse[:, BAND:]]
        rows.append(jnp.zeros((lanes - A_HEADS, BAND), F32))
        outs[2 * p + 1][...] = jnp.concatenate(rows, axis=0).T


def _dilated(a_parts, bsz, seq):
    n_blocks = seq // BAND
    in_specs = [pl.BlockSpec((len(DIL_PATTERNS), 2, A_HEADS // 2, 2 * BAND, 2 * BAND), lambda b, n: (0, 0, 0, 0, 0))]
    n_work = len(DIL_PATTERNS) * A_HEADS // 2
    args = [_dilated_bias_table()]
    out_specs, out_shapes = [], []
    blk = (None, None, BAND, A_WIDTH)
    qkv = (None, None, BAND, 3 * A_WIDTH)
    for a_part, (_, dil) in zip(a_parts, DIL_PATTERNS):
        per_res = n_blocks // dil
        cur = lambda b, n, per_res=per_res: (b, n // per_res, n % per_res, 0)
        prev = lambda b, n, per_res=per_res: (b, n // per_res, jnp.maximum(n % per_res - 1, 0), 0)
        in_specs += [pl.BlockSpec(qkv, cur)] + ([pl.BlockSpec(qkv, prev)] if per_res > 1 else [])
        args += [a_part] * (2 if per_res > 1 else 1)
        out_specs += [pl.BlockSpec(blk, cur), pl.BlockSpec((None, None, BAND, LANES), cur)]
        out_shapes += [jax.ShapeDtypeStruct((bsz, dil, seq // dil, A_WIDTH), BF16),
                       jax.ShapeDtypeStruct((bsz, dil, seq // dil, LANES), F32)]
    return pl.pallas_call(
        functools.partial(_dilated_kernel, n_blocks=n_blocks),
        grid=(bsz, n_blocks),
        in_specs=in_specs,
        out_specs=out_specs,
        out_shape=out_shapes,
        scratch_shapes=[pltpu.VMEM((n_work, 2 * BAND, 2 * BAND), F32), pltpu.VMEM((n_work, 2 * BAND, 2 * BAND), BF16)],
        compiler_params=_cparams(("parallel", "parallel")),
        name="dilated",
    )(*args)


def _diff_bias_table():
    t = DIFF_BLOCK
    rel = np.arange(t)[None, :] - np.arange(t)[:, None]
    slopes = 2.0 ** (-8.0 * np.arange(1, B_HEADS + 1) / B_HEADS)
    tbl = np.zeros((B_HEADS, 2, t, t), np.float32)
    for h in range(B_HEADS):
        tbl[h, 0] = -slopes[h] * LOG2E * rel
        tbl[h, 1] = np.where(rel >= 0, -slopes[h] * LOG2E * rel, -np.inf)
    return jnp.asarray(tbl)


def _diff_kernel(sc_ref, qt_ref, k_ref, vt_ref, g_ref, bias_sc, o_ref, qst_sc, m_sc, l_sc, a_sc, acc_sc, s_sc, p_sc):
    t = DIFF_BLOCK
    lanes = 2 * HEAD_DIM
    qi = pl.program_id(1)
    lam = sc_ref[0]
    slopes = [jnp.full((1, 1), sc_ref[1 + h], F32) for h in range(B_HEADS)]
    row = lax.broadcasted_iota(I32, (lanes, t), 0)
    for h in range(B_HEADS):
        qt = qt_ref[h * lanes:(h + 1) * lanes, :]
        zero = jnp.zeros_like(qt)
        qst_sc[h, :, :t] = jnp.where(row < HEAD_DIM, qt, zero)
        qst_sc[h, :, t:] = jnp.where(row < HEAD_DIM, zero, qt)
    m_sc[...] = jnp.full(m_sc.shape, -jnp.inf, F32)
    l_sc[...] = jnp.zeros(l_sc.shape, F32)
    acc_sc[...] = jnp.zeros(acc_sc.shape, F32)

    n_grp = 2 * t // DIFF_LANES
    items = [(h, g) for h in range(B_HEADS) for g in range(n_grp)]
    chunks = [slice(r * DIFF_ROWS, (r + 1) * DIFF_ROWS) for r in range(t // DIFF_ROWS)]

    def keys_needed(g, diag):
        return min(t, (g * DIFF_LANES) % t + DIFF_LANES) if diag else t

    def block(n, diag):
        cs = [-slopes[h] * LOG2E * (jnp.full((1, 1), (qi - n) * t, I32)).astype(F32) for h in range(B_HEADS)]
        for w, (h, g) in enumerate(items):
            nk = keys_needed(g, diag)
            cols = slice(g * DIFF_LANES, (g + 1) * DIFF_LANES)
            bcols = slice((g * DIFF_LANES) % t, (g * DIFF_LANES) % t + DIFF_LANES)
            s_sc[w, :nk, :] = (_dot(k_ref[n, :nk, h * lanes:(h + 1) * lanes], qst_sc[h, :, cols])
                               + bias_sc[h, diag, :nk, bcols])
        for w, (h, g) in enumerate(items):
            used = chunks[:keys_needed(g, diag) // DIFF_ROWS]
            cols = slice(g * DIFF_LANES, (g + 1) * DIFF_LANES)
            top = s_sc[w, used[0], :]
            for rows in used[1:]:
                top = jnp.maximum(top, s_sc[w, rows, :])
            m_prev = m_sc[h, :, cols]
            m_new = jnp.maximum(m_prev, jnp.max(top, axis=0, keepdims=True) + cs[h])
            alpha = jnp.exp2(m_prev - m_new)
            shift = m_new - cs[h]
            for rows in used:
                p_sc[w, rows, :] = jnp.exp2(s_sc[w, rows, :] - shift).astype(BF16)
            m_sc[h, :, cols] = m_new
            a_sc[h, :, cols] = alpha
        for w, (h, g) in enumerate(items):
            nk = keys_needed(g, diag)
            cols = slice(g * DIFF_LANES, (g + 1) * DIFF_LANES)
            pv = _dot(vt_ref[n, h * (lanes + ONES_ROWS):(h + 1) * (lanes + ONES_ROWS), :nk], p_sc[w, :nk, :])
            alpha = a_sc[h, :, cols]
            acc_sc[h, :, cols] = alpha * acc_sc[h, :, cols] + pv[:lanes]
            l_sc[h, :, cols] = alpha * l_sc[h, :, cols] + pv[lanes:lanes + 1]

    def body(n, carry):
        block(n, 0)
        return carry

    lax.fori_loop(0, qi, body, 0)
    block(qi, 1)
    for h in range(B_HEADS):
        o = acc_sc[h] / l_sc[h]
        o = o[:, :t] - lam * o[:, t:]
        o = o * lax.rsqrt(jnp.mean(o * o, axis=0, keepdims=True) + RMS_EPS) * g_ref[...]
        o_ref[:, h * lanes:(h + 1) * lanes] = (o * (1.0 - LAMBDA_INIT)).T.astype(BF16)


def _diff(bqt, bk, bvt, scalars, diff_norm_g, bsz, seq):
    t = DIFF_BLOCK
    nb = seq // t
    lanes = 2 * HEAD_DIM
    n_items = B_HEADS * 2 * t // DIFF_LANES
    return pl.pallas_call(
        _diff_kernel,
        grid=(bsz, nb),
        in_specs=[
            pl.BlockSpec(memory_space=pltpu.SMEM),
            pl.BlockSpec((None, None, B_WIDTH, t), lambda b, i: (b, i, 0, 0)),
            pl.BlockSpec((None, nb, t, B_WIDTH), lambda b, i: (b, 0, 0, 0)),
            pl.BlockSpec((None, nb, VT_ROWS, t), lambda b, i: (b, 0, 0, 0)),
            pl.BlockSpec((lanes, 1), lambda b, i: (0, 0)),
            pl.BlockSpec((B_HEADS, 2, t, t), lambda b, i: (0, 0, 0, 0)),
        ],
        out_specs=pl.BlockSpec((None, t, B_WIDTH), lambda b, i: (b, i, 0)),
        out_shape=jax.ShapeDtypeStruct((bsz, seq, B_WIDTH), BF16),
        scratch_shapes=[pltpu.VMEM((B_HEADS, lanes, 2 * t), BF16)] + [pltpu.VMEM((B_HEADS, 1, 2 * t), F32)] * 3 + [
            pltpu.VMEM((B_HEADS, lanes, 2 * t), F32),
            pltpu.VMEM((n_items, t, DIFF_LANES), F32), pltpu.VMEM((n_items, t, DIFF_LANES), BF16)],
        compiler_params=_cparams(("parallel", "parallel")),
        name="diff",
    )(scalars, bqt, bk, bvt, diff_norm_g.reshape(lanes, 1), _diff_bias_table())


def _router_kernel(o0, l0, o1, l1, o2, l2, ob_ref, x_ref, mod_ref, wout_ref, g2_ref, wrh_ref, wrl_ref, br_ref,
                   tri_ref, spread_ref, x1_ref, he_ref, idx_ref, rank_ref, cnt_ref, *order_sc):
    def token_order(ref, scratch):
        dil, rows, width = ref.shape
        if dil == 1:
            return ref[0].astype(F32)
        n_col = width // LANES
        for r in range(dil):
            for c in range(n_col):
                scratch[c, pl.ds(r, rows, stride=dil), :] = ref[r, :, c * LANES:(c + 1) * LANES].astype(F32)
        return jnp.concatenate([scratch[c] for c in range(n_col)], axis=1)

    os_ = [token_order(o0, None), token_order(o1, order_sc[0]), token_order(o2, order_sc[2])]
    ls = [token_order(l0, None), token_order(l1, order_sc[1]), token_order(l2, order_sc[3])]
    mx = jnp.maximum(jnp.maximum(ls[0], ls[1]), ls[2])
    ws = [jnp.exp(l - mx) for l in ls]
    den = ws[0] + ws[1] + ws[2]
    spread = spread_ref[...]
    oa = jnp.zeros(os_[0].shape, F32)
    for w, o in zip(ws, os_):
        wh, wl = _split_bf16(w / den)
        oa = oa + (_dot(wh, spread) + _dot(wl, spread)) * o
    mixed = _dot(oa.astype(BF16), wout_ref[:A_WIDTH, :]) + _dot(ob_ref[...], wout_ref[A_WIDTH:, :])
    x1 = x_ref[...] + mod_ref[2:3, :] * mixed
    x1_ref[...] = x1
    y = x1 * lax.rsqrt(jnp.mean(x1 * x1, axis=-1, keepdims=True) + RMS_EPS) * g2_ref[...]
    h2 = y * (1.0 + mod_ref[4:5, :]) + mod_ref[3:4, :]
    he_ref[:, :D_MODEL] = h2.astype(BF16)

    hh, hl = _split_bf16(h2)
    wrh = wrh_ref[...]
    logits = _dot_nt(wrh, hh) + _dot_nt(wrh, hl) + _dot_nt(wrl_ref[...], hh) + br_ref[...]
    t = logits.shape[1]
    eid = lax.broadcasted_iota(I32, (N_EXPERTS, t), 0)
    vals, idxs, hots = [], [], []
    cur = logits
    for _ in range(TOP_K):
        v = jnp.max(cur, axis=0, keepdims=True)
        ik = jnp.min(jnp.where(cur == v, eid, N_EXPERTS), axis=0, keepdims=True)
        hot = eid == ik
        vals.append(v)
        idxs.append(ik)
        hots.append(hot)
        cur = jnp.where(hot, -jnp.inf, cur)
    es = [jnp.exp(v - vals[0]) for v in vals]
    esum = es[0] + es[1] + es[2] + es[3]
    idx_ref[...] = jnp.concatenate(idxs, axis=0)

    rows = [ik.astype(F32) for ik in idxs]
    for e in es:
        g = e / esum
        hi = g.astype(BF16).astype(F32)
        mid = (g - hi).astype(BF16).astype(F32)
        rows += [hi, mid, g - hi - mid]
    rows.append(jnp.zeros((EXTRA_LANES - len(rows), t), F32))
    he_ref[:, D_MODEL:] = jnp.concatenate(rows, axis=0).T.astype(BF16)

    sel = jnp.where(hots[0] | hots[1] | hots[2] | hots[3], 1.0, 0.0)
    before = _dot(sel.astype(BF16), tri_ref[...])
    ranks = [jnp.sum(jnp.where(hot, before, 0.0), axis=0, keepdims=True) for hot in hots]
    rank_ref[...] = jnp.concatenate(ranks, axis=0).astype(I32)
    cnt_ref[...] = jnp.broadcast_to(jnp.sum(sel, axis=1, keepdims=True), cnt_ref.shape)


def _router(dil_outs, o_b, x2, mod_b, w_out, norm2_g, w_router, b_router, seq):
    n = x2.shape[0]
    t = ROW_TILE
    per_b = seq // t
    wr_t = w_router.T
    wrh = wr_t.astype(BF16)
    wrl = (wr_t - wrh.astype(F32)).astype(BF16)
    tri = jnp.asarray(np.arange(t)[:, None] < np.arange(t)[None, :], BF16)
    row = lambda w: pl.BlockSpec((t, w), lambda i: (i, 0))
    full = lambda a, b: pl.BlockSpec((a, b), lambda i: (0, 0))
    tok = lambda: pl.BlockSpec((TOP_K, t), lambda i: (0, i))
    grouped = lambda dil, w: pl.BlockSpec((None, dil, t // dil, w), lambda i: (i // per_b, 0, i % per_b, 0))
    widths = (A_WIDTH, LANES)
    head_of_lane = np.arange(A_WIDTH) // HEAD_DIM
    spread = jnp.asarray(np.arange(LANES)[:, None] == head_of_lane[None, :], BF16)
    return pl.pallas_call(
        _router_kernel,
        grid=(n // t,),
        in_specs=[grouped(dil, w) for _, dil in DIL_PATTERNS for w in widths] + [
            row(B_WIDTH), row(D_MODEL),
            pl.BlockSpec((None, 6, D_MODEL), lambda i: (i // per_b, 0, 0)),
            full(D_MODEL, D_MODEL), full(1, D_MODEL), full(N_EXPERTS, D_MODEL), full(N_EXPERTS, D_MODEL),
            full(N_EXPERTS, 1), full(t, t), full(LANES, A_WIDTH),
        ],
        out_specs=[row(D_MODEL), row(ROW_WIDTH), tok(), tok(),
                   pl.BlockSpec((None, N_EXPERTS, 128), lambda i: (i, 0, 0))],
        out_shape=[
            jax.ShapeDtypeStruct((n, D_MODEL), F32),
            jax.ShapeDtypeStruct((n, ROW_WIDTH), BF16),
            jax.ShapeDtypeStruct((TOP_K, n), I32),
            jax.ShapeDtypeStruct((TOP_K, n), I32),
            jax.ShapeDtypeStruct((n // t, N_EXPERTS, 128), F32),
        ],
        scratch_shapes=[pltpu.VMEM((w // LANES, t, LANES), F32) for _, dil in DIL_PATTERNS if dil > 1 for w in widths],
        compiler_params=_cparams(("parallel",)),
        name="router",
    )(*dil_outs, o_b, x2, mod_b, w_out.astype(BF16), norm2_g.reshape(1, D_MODEL), wrh, wrl,
      b_router.reshape(N_EXPERTS, 1), tri, spread)


def _start_pieces(tables, tile, local_ref, hbm_ref, sem, outbound):
    lo_ref, go_ref, cnt_ref, _ = tables
    for c, size in enumerate(SEG_SIZES):
        cls = tile * len(SEG_SIZES) + c

        def per_piece(k, carry, size=size, cls=cls):
            lo, go = lo_ref[cls * N_EXPERTS + k], go_ref[cls * N_EXPERTS + k]
            loc = local_ref.at[pl.ds(pl.multiple_of(lo, SEG_ALIGN), size)]
            glob = hbm_ref.at[pl.ds(pl.multiple_of(go, SEG_ALIGN), size)]
            cp = pltpu.make_async_copy(loc, glob, sem) if outbound else pltpu.make_async_copy(glob, loc, sem)
            cp.start()
            return carry

        lax.fori_loop(0, cnt_ref[cls], per_piece, 0)


def _start(cp):
    cp.start()


def _wait(cp):
    cp.wait()


def _tile_rows(tables, tile):
    return tables[3][tile]


def _wait_rows(rows, local_ref, hbm_ref, sem, outbound):
    size = 1 << (LOCAL_SLOTS.bit_length() - 1)
    while size >= SEG_ALIGN:
        loc, glob = local_ref.at[pl.ds(0, size)], hbm_ref.at[pl.ds(0, size)]
        cp = pltpu.make_async_copy(loc, glob, sem) if outbound else pltpu.make_async_copy(glob, loc, sem)
        pl.when((rows & size) != 0)(cp.wait)
        size //= 2


def _one_hot_any(j, targets):
    out = jnp.zeros(j.shape, F32)
    for tgt in targets:
        out = jnp.where(j == tgt, 1.0, out)
    return out


def _sort_kernel(lo_ref, go_ref, cnt_ref, rows_ref, tail_ref, he_ref, idx_ref, rank_ref, lcol_ref, ls_ref, xs_hbm,
                 xl, zbuf, sems, zsem):
    tables = (lo_ref, go_ref, cnt_ref, rows_ref)
    i = pl.program_id(0)
    last = pl.num_programs(0) - 1
    slot = i % 2
    t = he_ref.shape[0]

    @pl.when(i == 0)
    def _():
        zbuf[...] = jnp.zeros(zbuf.shape, BF16)

        def tails(action):
            def per_expert(e, carry):
                off, n = tail_ref[e], tail_ref[N_EXPERTS + e]
                done = jnp.int32(0)
                for size in SEG_SIZES:
                    if size < MOE_TILE:
                        take = (n & size) != 0
                        dst = xs_hbm.at[pl.ds(pl.multiple_of(off + done, SEG_ALIGN), size)]
                        pl.when(take)(functools.partial(action, pltpu.make_async_copy(zbuf.at[pl.ds(0, size)], dst, zsem)))
                        done = done + jnp.where(take, size, 0)
                return carry

            lax.fori_loop(0, N_EXPERTS, per_expert, 0)

            def per_block(b, carry):
                dst = xs_hbm.at[pl.ds(pl.multiple_of(b * MOE_TILE, MOE_TILE), MOE_TILE)]
                action(pltpu.make_async_copy(zbuf, dst, zsem))
                return carry

            lax.fori_loop(tail_ref[2 * N_EXPERTS], xs_hbm.shape[0] // MOE_TILE, per_block, 0)

        tails(_start)
        tails(_wait)

    eid = lax.broadcasted_iota(I32, (N_EXPERTS, t), 0)
    lcol = lcol_ref[...]
    ls = []
    for k in range(TOP_K):
        off = jnp.sum(jnp.where(eid == idx_ref[k:k + 1, :], lcol, 0), axis=0, keepdims=True)
        ls.append(off + rank_ref[k:k + 1, :])
    ls_ref[...] = jnp.concatenate(ls, axis=0)

    he = he_ref[...]
    for jc in range(LOCAL_SLOTS // SLOT_CHUNK):
        j = lax.broadcasted_iota(I32, (SLOT_CHUNK, t), 0) + jc * SLOT_CHUNK
        perm = _one_hot_any(j, ls).astype(BF16)
        xl[slot, jc * SLOT_CHUNK:(jc + 1) * SLOT_CHUNK, :] = _dot(perm, he).astype(BF16)

    _start_pieces(tables, i, xl.at[slot], xs_hbm, sems.at[slot], True)

    @pl.when(i > 0)
    def _():
        _wait_rows(_tile_rows(tables, i - 1), xl.at[1 - slot], xs_hbm, sems.at[1 - slot], True)

    @pl.when(i == last)
    def _():
        _wait_rows(_tile_rows(tables, i), xl.at[slot], xs_hbm, sems.at[slot], True)


def _sort(tables, tail, he, idx, rank, lcol, n_rows):
    n = he.shape[0]
    t = ROW_TILE
    tok = lambda: pl.BlockSpec((TOP_K, t), lambda i, *_: (0, i))
    return pl.pallas_call(
        _sort_kernel,
        grid_spec=pltpu.PrefetchScalarGridSpec(
            num_scalar_prefetch=5,
            grid=(n // t,),
            in_specs=[
                pl.BlockSpec((t, ROW_WIDTH), lambda i, *_: (i, 0)),
                tok(), tok(),
                pl.BlockSpec((None, N_EXPERTS, 1), lambda i, *_: (i, 0, 0)),
            ],
            out_specs=[tok(), pl.BlockSpec(memory_space=pl.ANY)],
            scratch_shapes=[pltpu.VMEM((2, LOCAL_SLOTS, ROW_WIDTH), BF16), pltpu.VMEM((MOE_TILE, ROW_WIDTH), BF16),
                            pltpu.SemaphoreType.DMA((2,)), pltpu.SemaphoreType.DMA(())],
        ),
        out_shape=[jax.ShapeDtypeStruct((TOP_K, n), I32), jax.ShapeDtypeStruct((n_rows, ROW_WIDTH), BF16)],
        compiler_params=_cparams(("arbitrary",)),
        name="sort",
    )(*tables, tail, he, idx, rank, lcol)


def _experts_kernel(blk0_ref, nblk_ref, xs_hbm, wgu_ref, bgu_ref, wd_ref, bd_ref, yb_hbm, wgu_sc, wd_sc, xbuf, ybuf,
                    xsem, ysem):
    e = pl.program_id(0)
    tm = MOE_TILE
    first, nb = blk0_ref[e], nblk_ref[e]
    last_e = pl.num_programs(0) - 1
    total = blk0_ref[last_e] + nblk_ref[last_e]

    def rows(b, n=1):
        return pl.ds(pl.multiple_of(b * tm, tm), n * tm)

    def x_copy(b, slot):
        return pltpu.make_async_copy(xs_hbm.at[rows(b)], xbuf.at[rows(slot)], xsem.at[slot])

    def x_start(b):
        x_copy(b, b % X_SLOTS).start()
        pl.when(b % X_SLOTS == 0)(lambda: x_copy(b, X_SLOTS).start())

    def x_wait(b):
        x_copy(b, b % X_SLOTS).wait()
        pl.when(b % X_SLOTS == 0)(lambda: x_copy(b, X_SLOTS).wait())

    def y_copy(b, slot, n):
        return pltpu.make_async_copy(ybuf.at[slot, pl.ds(0, n * tm)], yb_hbm.at[rows(b, n)], ysem.at[slot])

    @pl.when(e == 0)
    def _():
        for b in range(X_AHEAD):
            pl.when(b < total)(functools.partial(x_start, b))

    def cast(r, carry):
        s = pl.multiple_of(r * LANES, LANES)
        wgu_sc[pl.ds(s, LANES), :] = wgu_ref[pl.ds(s, LANES), :].astype(BF16)
        wd_sc[pl.ds(s, LANES), :] = wd_ref[pl.ds(s, LANES), :].astype(BF16)
        return carry

    lax.fori_loop(0, D_MODEL // LANES, cast, 0)
    me = jnp.full((1, 1), e, I32).astype(F32)

    def unit(it, b, n):
        slot = it % 2
        for k in range(n):
            pl.when(b + X_AHEAD + k < total)(functools.partial(x_start, b + X_AHEAD + k))
        for k in range(n):
            x_wait(b + k)

        @pl.when(it >= 2)
        def _():
            y_copy(b, slot, 2).wait()

        x = xbuf.at[rows(b % X_SLOTS, n)]
        ext = x[:, D_MODEL:].astype(F32)
        gate = jnp.zeros((n * tm, 1), F32)
        for k in range(TOP_K):
            c = TOP_K + 3 * k
            gk = ext[:, c:c + 1] + ext[:, c + 1:c + 2] + ext[:, c + 2:c + 3]
            gate = gate + jnp.where(ext[:, k:k + 1] == me, gk, 0.0)

        gu = _dot(x[:, :D_MODEL], wgu_sc[...]) + bgu_ref[...]
        g = jnp.minimum(gu[:, :D_FF], SWIGLU_LIMIT)
        u = jnp.clip(gu[:, D_FF:], -SWIGLU_LIMIT, SWIGLU_LIMIT)
        act = (u + 1.0) * (g / (1.0 + jnp.exp(-SWIGLU_ALPHA * g)))
        ybuf[slot, :n * tm, :] = (gate * (_dot(act.astype(BF16), wd_sc[...]) + bd_ref[...])).astype(BF16)
        y_copy(b, slot, n).start()

    n_pair = nb // 2
    odd = nb % 2 == 1

    def pair(it, carry):
        unit(it, first + 2 * it, 2)
        return carry

    lax.fori_loop(0, n_pair, pair, 0)
    pl.when(odd)(functools.partial(unit, n_pair, first + 2 * n_pair, 1))

    last_slot = (n_pair - 1) % 2
    pl.when(odd)(y_copy(first, n_pair % 2, 1).wait)
    pl.when(odd & (n_pair >= 1))(y_copy(first, last_slot, 2).wait)
    pl.when(jnp.logical_not(odd) & (n_pair >= 2))(y_copy(first, n_pair % 2, 2).wait)
    pl.when(jnp.logical_not(odd) & (n_pair >= 1))(y_copy(first, last_slot, 2).wait)

    @pl.when(e == last_e)
    def _():
        ybuf[0, :tm, :] = jnp.zeros((tm, D_MODEL), BF16)

        def fill(action):
            def per_block(b, carry):
                action(pltpu.make_async_copy(ybuf.at[0, pl.ds(0, tm)], yb_hbm.at[rows(b)], ysem.at[0]))
                return carry

            lax.fori_loop(total, yb_hbm.shape[0] // tm, per_block, 0)

        fill(_start)
        fill(_wait)


def _experts(first_block, n_block, xs, w_gate_up, b_gate_up, w_down, b_down):
    n_rows = xs.shape[0]
    exp3 = lambda e, *_: (e, 0, 0)
    return pl.pallas_call(
        _experts_kernel,
        grid_spec=pltpu.PrefetchScalarGridSpec(
            num_scalar_prefetch=2,
            grid=(N_EXPERTS,),
            in_specs=[
                pl.BlockSpec(memory_space=pl.ANY),
                pl.BlockSpec((None, D_MODEL, 2 * D_FF), exp3),
                pl.BlockSpec((None, 1, 2 * D_FF), exp3),
                pl.BlockSpec((None, D_FF, D_MODEL), exp3),
                pl.BlockSpec((None, 1, D_MODEL), exp3),
            ],
            out_specs=pl.BlockSpec(memory_space=pl.ANY),
            scratch_shapes=[pltpu.VMEM((D_MODEL, 2 * D_FF), BF16), pltpu.VMEM((D_FF, D_MODEL), BF16),
                            pltpu.VMEM(((X_SLOTS + 1) * MOE_TILE, ROW_WIDTH), BF16),
                            pltpu.VMEM((2, 2 * MOE_TILE, D_MODEL), BF16),
                            pltpu.SemaphoreType.DMA((X_SLOTS + 1,)), pltpu.SemaphoreType.DMA((2,))],
        ),
        out_shape=jax.ShapeDtypeStruct((n_rows, D_MODEL), BF16),
        compiler_params=_cparams(("arbitrary",)),
        name="experts",
    )(first_block, n_block, xs, w_gate_up, b_gate_up.reshape(N_EXPERTS, 1, 2 * D_FF), w_down,
      b_down.reshape(N_EXPERTS, 1, D_MODEL))


def _combine_kernel(lo_ref, go_ref, cnt_ref, rows_ref, yb_hbm, lst_ref, x1_ref, mod_ref, o_ref, ybuf, sems):
    i = pl.program_id(0)
    slot = i % 2
    tables = (lo_ref, go_ref, cnt_ref, rows_ref)

    @pl.when(i == 0)
    def _():
        ybuf[...] = jnp.zeros(ybuf.shape, BF16)
        _start_pieces(tables, 0, ybuf.at[0], yb_hbm, sems.at[0], False)

    @pl.when(i + 1 < pl.num_programs(0))
    def _():
        _start_pieces(tables, i + 1, ybuf.at[1 - slot], yb_hbm, sems.at[1 - slot], False)

    _wait_rows(_tile_rows(tables, i), ybuf.at[slot], yb_hbm, sems.at[slot], False)

    lst = lst_ref[...]
    t = lst.shape[0]
    targets = [lst[:, k:k + 1] for k in range(TOP_K)]
    y = jnp.zeros((t, D_MODEL), F32)
    for jc in range(LOCAL_SLOTS // SLOT_CHUNK):
        j = lax.broadcasted_iota(I32, (t, SLOT_CHUNK), 1) + jc * SLOT_CHUNK
        pick = _one_hot_any(j, targets).astype(BF16)
        y = y + _dot(pick, ybuf[slot, jc * SLOT_CHUNK:(jc + 1) * SLOT_CHUNK, :])
    o_ref[...] = x1_ref[...] + mod_ref[5:6, :] * y


def _combine(tables, yb, ls_t, x1, mod_b, seq):
    n = x1.shape[0]
    t = ROW_TILE
    per_b = seq // t
    return pl.pallas_call(
        _combine_kernel,
        grid_spec=pltpu.PrefetchScalarGridSpec(
            num_scalar_prefetch=4,
            grid=(n // t,),
            in_specs=[
                pl.BlockSpec(memory_space=pl.ANY),
                pl.BlockSpec((t, TOP_K), lambda i, *_: (i, 0)),
                pl.BlockSpec((t, D_MODEL), lambda i, *_: (i, 0)),
                pl.BlockSpec((None, 6, D_MODEL), lambda i, *_: (i // per_b, 0, 0)),
            ],
            out_specs=pl.BlockSpec((t, D_MODEL), lambda i, *_: (i, 0)),
            scratch_shapes=[pltpu.VMEM((2, LOCAL_SLOTS, D_MODEL), BF16), pltpu.SemaphoreType.DMA((2,))],
        ),
        out_shape=jax.ShapeDtypeStruct((n, D_MODEL), F32),
        compiler_params=_cparams(("arbitrary",)),
        name="combine",
    )(*tables, yb, ls_t, x1, mod_b)


def _layer(x, c, w_ada, b_ada, norm1_g, w_in, a_q_norm_g, a_k_norm_g, b_q_norm_g, b_k_norm_g, lambda_q1,
           lambda_k1, lambda_q2, lambda_k2, diff_norm_g, w_out, norm2_g, w_router, b_router, w_gate_up,
           b_gate_up, w_down, b_down):
    bsz, seq, _ = x.shape
    n = bsz * seq
    x2 = x.reshape(n, D_MODEL)
    mod_b = _ada(c, w_ada, b_ada).transpose(1, 0, 2)

    qk_gains = jnp.stack([
        jnp.tile(a_q_norm_g, A_HEADS) * (ATTN_SCALE * LOG2E), jnp.tile(a_k_norm_g, A_HEADS),
        jnp.tile(b_k_norm_g, 2 * B_HEADS)])
    bq_gain = (jnp.tile(b_q_norm_g, 2 * B_HEADS) * (ATTN_SCALE * LOG2E)).reshape(B_WIDTH, 1)
    *a_parts, bk, bqt, bvt = _inproj(x2, mod_b, norm1_g, w_in, qk_gains, bq_gain, bsz, seq)

    dil_outs = _dilated(a_parts, bsz, seq)
    lam = (jnp.exp(jnp.sum(lambda_q1 * lambda_k1)) - jnp.exp(jnp.sum(lambda_q2 * lambda_k2)) + LAMBDA_INIT)
    slopes_b = 2.0 ** (-8.0 * np.arange(1, B_HEADS + 1) / B_HEADS)
    scalars = jnp.concatenate([lam.reshape(1), jnp.asarray(slopes_b, F32)]).astype(F32)
    bk4 = bk.reshape(bsz, seq // DIFF_BLOCK, DIFF_BLOCK, B_WIDTH)
    o_b = _diff(bqt, bk4, bvt, scalars, diff_norm_g, bsz, seq).reshape(n, B_WIDTH)

    x1, he, idx, rank, cnt = _router(dil_outs, o_b, x2, mod_b, w_out, norm2_g, w_router, b_router, seq)

    n_tiles = n // ROW_TILE
    counts = cnt[:, :, 0].astype(I32)
    seg = (counts + SEG_ALIGN - 1) // SEG_ALIGN * SEG_ALIGN
    loff = jnp.cumsum(seg, axis=1) - seg
    region = jnp.sum(seg, axis=0)
    padded = (region + MOE_TILE - 1) // MOE_TILE * MOE_TILE
    pad_end = jnp.cumsum(padded)
    pad_start = pad_end - padded
    goff = pad_start[None, :] + jnp.cumsum(seg, axis=0) - seg
    sizes = jnp.asarray(SEG_SIZES, I32)[None, :, None]
    has = (seg[:, None, :] & sizes) != 0
    within = seg[:, None, :] & ~(2 * sizes - 1)
    place = jnp.cumsum(has, axis=-1) - has
    pick = has[:, :, None, :] & (place[:, :, None, :] == jnp.arange(N_EXPERTS, dtype=I32)[None, None, :, None])
    listed = lambda rows_: jnp.sum(jnp.where(pick, rows_[:, :, None, :], 0), axis=-1).reshape(-1).astype(I32)
    tables = (listed(loff[:, None, :] + within), listed(goff[:, None, :] + within),
              jnp.sum(has, axis=-1).reshape(-1).astype(I32), jnp.sum(seg, axis=1).astype(I32))
    n_blocks = (n * TOP_K + n_tiles * N_EXPERTS * (SEG_ALIGN - 1) + N_EXPERTS * (MOE_TILE - 1)) // MOE_TILE
    n_used = (pad_end[-1] // MOE_TILE).astype(I32)
    tail = jnp.concatenate([pad_start + region, padded - region, n_used.reshape(1)])

    ls, xs = _sort(tables, tail, he, idx, rank, loff.reshape(n_tiles, N_EXPERTS, 1), n_blocks * MOE_TILE)
    yb = _experts(pad_start // MOE_TILE, padded // MOE_TILE, xs, w_gate_up, b_gate_up, w_down, b_down)
    out = _combine(tables, yb, ls.T, x1, mod_b, seq)
    return out.reshape(bsz, seq, D_MODEL)


def kernel(x, c, w_ada, b_ada, norm1_g, w_in, a_q_norm_g, a_k_norm_g, b_q_norm_g, b_k_norm_g, lambda_q1, lambda_k1,
           lambda_q2, lambda_k2, diff_norm_g, w_out, norm2_g, w_router, b_router, w_gate_up, b_gate_up, w_down,
           b_down):
    args = (w_ada, b_ada, norm1_g, w_in, a_q_norm_g, a_k_norm_g, b_q_norm_g, b_k_norm_g, lambda_q1, lambda_k1,
            lambda_q2, lambda_k2, diff_norm_g, w_out, norm2_g, w_router, b_router, w_gate_up, b_gate_up, w_down,
            b_down)
    return _layer(x, c, *[a[0] for a in args])
```

```python
import functools

import numpy as np
import jax
import jax.numpy as jnp
from jax import lax
from jax.experimental import pallas as pl
from jax.experimental.pallas import tpu as pltpu

F32 = jnp.float32
BF16 = jnp.bfloat16
I32 = jnp.int32

D_MODEL = 1024
HEAD_DIM = 64
A_WIDTH = 512
B_WIDTH = 512
A_HEADS = 8
B_HEADS = 4
DIL_PATTERNS = ((128, 1), (512, 4), (2048, 16))
BAND = 128
N_EXPERTS = 32
TOP_K = 4
D_FF = 1024
SWIGLU_LIMIT = 7.0
SWIGLU_ALPHA = 1.702
RMS_EPS = 1e-6
ATTN_SCALE = HEAD_DIM ** -0.5
LOG2E = 1.4426950408889634
LN2 = 0.6931471805599453
LAMBDA_INIT = 0.8 - 0.6 * 1.0

LANES = 128
ROW_TILE = 512
DIFF_BLOCK = 512
DIFF_LANES = 256
DIFF_ROWS = 64
ONES_ROWS = 16
VT_ROWS = B_HEADS * (2 * HEAD_DIM + ONES_ROWS)
MOE_TILE = 256
FF_HALF = D_FF // 2
X_AHEAD = 4
X_SLOTS = X_AHEAD + 2
EXTRA_LANES = 128
ROW_WIDTH = D_MODEL + EXTRA_LANES
SEG_ALIGN = 16
SEG_SIZES = (512, 256, 128, 64, 32, 16)
SLOT_CHUNK = 512
LOCAL_SLOTS = 2560
assert ROW_TILE == DIFF_BLOCK == SEG_SIZES[0]
assert LOCAL_SLOTS >= ROW_TILE * TOP_K + N_EXPERTS * (SEG_ALIGN - 1) and LOCAL_SLOTS % SLOT_CHUNK == 0
VMEM_LIMIT = 56 * 1024 * 1024


def _cparams(sem, **flags):
    return pltpu.CompilerParams(dimension_semantics=sem, vmem_limit_bytes=VMEM_LIMIT, flags=flags or None)


def _split_bf16(a):
    hi = a.astype(BF16)
    lo = (a - hi.astype(F32)).astype(BF16)
    return hi, lo


def _dot_nt(a, b):
    return lax.dot_general(a, b, (((1,), (1,)), ((), ())), preferred_element_type=F32)


def _dot(a, b):
    return jnp.dot(a, b, preferred_element_type=F32)


def _ada_kernel(c_ref, w_ref, b_ref, o_ref):
    c = c_ref[...]
    s = c / (1.0 + jnp.exp(-c))
    sh, sl = _split_bf16(s)
    wh, wl = _split_bf16(w_ref[...])
    o_ref[0] = _dot(sh, wh) + _dot(sh, wl) + _dot(sl, wh) + b_ref[0]


def _ada(c, w_ada, b_ada):
    bsz = c.shape[0]
    return pl.pallas_call(
        _ada_kernel,
        grid=(6,),
        in_specs=[
            pl.BlockSpec((bsz, D_MODEL), lambda j: (0, 0)),
            pl.BlockSpec((D_MODEL, D_MODEL), lambda j: (0, j)),
            pl.BlockSpec((1, 1, D_MODEL), lambda j: (j, 0, 0)),
        ],
        out_specs=pl.BlockSpec((1, bsz, D_MODEL), lambda j: (j, 0, 0)),
        out_shape=jax.ShapeDtypeStruct((6, bsz, D_MODEL), F32),
        compiler_params=_cparams(("arbitrary",)),
        name="ada",
    )(c, w_ada, b_ada.reshape(6, 1, D_MODEL))


def _inproj_kernel(x_ref, mod_ref, g1_ref, w_ref, gm_ref, qkg_ref, bqg_ref, a1_ref, a4_ref, a16_ref, bk_ref,
                   bqt_ref, bvt_ref, a_sc, wt_sc):
    a_refs = (a1_ref, a4_ref, a16_ref)
    w = A_WIDTH

    @pl.when(pl.program_id(0) == 0)
    def _():
        wt_sc[:w, :] = w_ref[:, 3 * w:4 * w].T
        wt_sc[w:, :] = w_ref[:, 5 * w:6 * w].T

    x = x_ref[...]
    y = x * lax.rsqrt(jnp.mean(x * x, axis=-1, keepdims=True) + RMS_EPS) * g1_ref[...]
    h = (y * (1.0 + mod_ref[1:2, :]) + mod_ref[0:1, :]).astype(BF16)
    p = _dot(h, w_ref[:, :3 * w])
    gm = gm_ref[...]

    def head_norm(t, g):
        sq = (t * t).astype(BF16)
        half = gm.shape[0]
        ss = jnp.concatenate([_dot(sq[:, :half], gm), _dot(sq[:, half:], gm)], axis=1)
        return t * lax.rsqrt(ss * (1.0 / HEAD_DIM) + RMS_EPS) * g

    a_part = jnp.concatenate([head_norm(p[:, 0 * w:1 * w], qkg_ref[0:1, :]),
                              head_norm(p[:, 1 * w:2 * w], qkg_ref[1:2, :]), p[:, 2 * w:3 * w]], axis=1)
    n_col = a_sc.shape[0]
    for c in range(n_col):
        a_sc[c] = a_part[:, c * LANES:(c + 1) * LANES]
    for a_ref, (_, dil) in zip(a_refs, DIL_PATTERNS):
        if dil == 1:
            a_ref[0] = a_part.astype(BF16)
            continue
        rows = a_part.shape[0] // dil
        for r in range(dil):
            for c in range(n_col):
                a_ref[r, :, c * LANES:(c + 1) * LANES] = a_sc[c, pl.ds(r, rows, stride=dil), :].astype(BF16)
    bk_ref[...] = head_norm(_dot(h, w_ref[:, 4 * w:5 * w]), qkg_ref[2:3, :]).astype(BF16)

    pt = _dot_nt(wt_sc[...], h)
    t = pt.shape[1]
    bq = pt[:w].reshape(w // HEAD_DIM, HEAD_DIM, t)
    ss = jnp.sum(bq * bq, axis=1, keepdims=True)
    bq = (bq * lax.rsqrt(ss * (1.0 / HEAD_DIM) + RMS_EPS)).reshape(w, t) * bqg_ref[...]
    bqt_ref[...] = bq.astype(BF16)
    lanes = 2 * HEAD_DIM
    for hd in range(B_HEADS):
        base = hd * (lanes + ONES_ROWS)
        bvt_ref[base:base + lanes, :] = pt[w + hd * lanes:w + (hd + 1) * lanes].astype(BF16)
        bvt_ref[base + lanes:base + lanes + ONES_ROWS, :] = jnp.ones((ONES_ROWS, t), BF16)


def _inproj(x2, mod_b, norm1_g, w_in, qk_gains, bq_gain, bsz, seq):
    n = x2.shape[0]
    t = ROW_TILE
    per_b = seq // t
    w = A_WIDTH
    head_of_lane = np.arange(w // 2) // HEAD_DIM
    gmat = jnp.asarray(head_of_lane[:, None] == head_of_lane[None, :], BF16)
    tr_spec = pl.BlockSpec((None, None, w, t), lambda i: (i // per_b, i % per_b, 0, 0))
    tr_shape = jax.ShapeDtypeStruct((bsz, per_b, w, t), BF16)
    a_specs = [pl.BlockSpec((None, dil, t // dil, 3 * w), lambda i: (i // per_b, 0, i % per_b, 0))
               for _, dil in DIL_PATTERNS]
    a_shapes = [jax.ShapeDtypeStruct((bsz, dil, seq // dil, 3 * w), BF16) for _, dil in DIL_PATTERNS]
    return pl.pallas_call(
        _inproj_kernel,
        grid=(n // t,),
        in_specs=[
            pl.BlockSpec((t, D_MODEL), lambda i: (i, 0)),
            pl.BlockSpec((None, 6, D_MODEL), lambda i: (i // per_b, 0, 0)),
            pl.BlockSpec((1, D_MODEL), lambda i: (0, 0)),
            pl.BlockSpec((D_MODEL, 6 * w), lambda i: (0, 0)),
            pl.BlockSpec((w // 2, w // 2), lambda i: (0, 0)),
            pl.BlockSpec((3, w), lambda i: (0, 0)),
            pl.BlockSpec((w, 1), lambda i: (0, 0)),
        ],
        out_specs=a_specs + [pl.BlockSpec((t, w), lambda i: (i, 0)), tr_spec,
                             pl.BlockSpec((None, None, VT_ROWS, t), lambda i: (i // per_b, i % per_b, 0, 0))],
        out_shape=a_shapes + [jax.ShapeDtypeStruct((n, w), BF16), tr_shape,
                              jax.ShapeDtypeStruct((bsz, per_b, VT_ROWS, t), BF16)],
        scratch_shapes=[pltpu.VMEM((3 * w // LANES, t, LANES), F32), pltpu.VMEM((2 * w, D_MODEL), BF16)],
        compiler_params=_cparams(("arbitrary",)),
        name="inproj",
    )(x2, mod_b, norm1_g.reshape(1, D_MODEL), w_in.astype(BF16), gmat, qk_gains, bq_gain)


def _dilated_bias_table():
    ik = np.arange(2 * BAND)[:, None]
    iq = np.arange(BAND)[None, :]
    delta = iq - ik + BAND
    in_band = (delta >= 0) & (delta <= BAND)
    slopes = 2.0 ** (-8.0 * np.arange(1, A_HEADS + 1) / A_HEADS)
    tbl = np.zeros((len(DIL_PATTERNS), 2, A_HEADS // 2, 2 * BAND, 2 * BAND), np.float32)
    for p, (_, dil) in enumerate(DIL_PATTERNS):
        for first in range(2):
            valid = in_band & ((ik >= BAND) if first else True)
            for h in range(A_HEADS):
                cols = slice(BAND * (h % 2), BAND * (h % 2 + 1))
                tbl[p, first, h // 2, :, cols] = np.where(valid, -slopes[h] * LOG2E * (delta * dil), -np.inf)
    return jnp.asarray(tbl)


def _dilated_kernel(tbl_ref, *refs, n_blocks):
    lanes = 2 * HEAD_DIM
    n = pl.program_id(1)
    n_pat = len(DIL_PATTERNS)
    s_sc, p_sc = refs[len(refs) - 2:]
    outs = refs[len(refs) - 2 - 2 * n_pat:len(refs) - 2]
    lane = lax.broadcasted_iota(I32, (BAND, lanes), 1)
    low = lane < HEAD_DIM
    work = []
    pos = 0
    for p, (_, dil) in enumerate(DIL_PATTERNS):
        per_res = n_blocks // dil
        if per_res > 1:
            cur_ref, prev_ref = refs[pos:pos + 2]
            pos += 2
            first = (n % per_res == 0).astype(I32)
        else:
            cur_ref, prev_ref, first = refs[pos], None, None
            pos += 1
        for j in range(A_HEADS // 2):
            work.append((p, j, cur_ref, prev_ref, first))

    def keys_of(prev_ref, cur_ref, cs):
        return cur_ref[:, cs] if prev_ref is None else jnp.concatenate([prev_ref[:, cs], cur_ref[:, cs]], axis=0)

    def cols_of(part, j):
        return slice(part * A_WIDTH + lanes * j, part * A_WIDTH + lanes * (j + 1))

    for w, (p, j, cur_ref, prev_ref, first) in enumerate(work):
        q2 = cur_ref[:, cols_of(0, j)]
        zero = jnp.zeros_like(q2)
        qcat = jnp.concatenate([jnp.where(low, q2, zero), jnp.where(low, zero, q2)], axis=0)
        k2 = keys_of(prev_ref, cur_ref, cols_of(1, j))
        bias = tbl_ref[p, 1, j, BAND:, :] if prev_ref is None else tbl_ref[p, first, j]
        s_sc[w, :k2.shape[0], :] = _dot_nt(k2, qcat) + bias
    lses = []
    for w, (p, j, _, prev_ref, _) in enumerate(work):
        nk = BAND if prev_ref is None else 2 * BAND
        s = s_sc[w, :nk, :]
        m = jnp.max(s, axis=0, keepdims=True)
        e = jnp.exp2(s - m)
        den = jnp.sum(e, axis=0, keepdims=True)
        p_sc[w, :nk, :] = (e * (1.0 / den)).astype(BF16)
        lses.append((m + jnp.log2(den)) * LN2)
    for w, (p, j, cur_ref, prev_ref, _) in enumerate(work):
        nk = BAND if prev_ref is None else 2 * BAND
        cs = cols_of(0, j)
        v2 = keys_of(prev_ref, cur_ref, cols_of(2, j))
        r = lax.dot_general(p_sc[w, :nk, :], v2, (((0,), (0,)), ((), ())), preferred_element_type=F32)
        outs[2 * p][:, cs] = jnp.where(low, r[:BAND], r[BAND:]).astype(BF16)
    for p in range(n_pat):
        rows = []
        for lse in lses[p * (A_HEADS // 2):(p + 1) * (A_HEADS // 2)]:
            rows += [lse[:, :BAND], lse[:, BAND:]]
        rows.append(jnp.zeros((lanes - A_HEADS, BAND), F32))
        outs[2 * p + 1][...] = jnp.concatenate(rows, axis=0).T


def _dilated(a_parts, bsz, seq):
    n_blocks = seq // BAND
    in_specs = [pl.BlockSpec((len(DIL_PATTERNS), 2, A_HEADS // 2, 2 * BAND, 2 * BAND), lambda b, n: (0, 0, 0, 0, 0))]
    n_work = len(DIL_PATTERNS) * A_HEADS // 2
    args = [_dilated_bias_table()]
    out_specs, out_shapes = [], []
    blk = (None, None, BAND, A_WIDTH)
    qkv = (None, None, BAND, 3 * A_WIDTH)
    for a_part, (_, dil) in zip(a_parts, DIL_PATTERNS):
        per_res = n_blocks // dil
        cur = lambda b, n, per_res=per_res: (b, n // per_res, n % per_res, 0)
        prev = lambda b, n, per_res=per_res: (b, n // per_res, jnp.maximum(n % per_res - 1, 0), 0)
        in_specs += [pl.BlockSpec(qkv, cur)] + ([pl.BlockSpec(qkv, prev)] if per_res > 1 else [])
        args += [a_part] * (2 if per_res > 1 else 1)
        out_specs += [pl.BlockSpec(blk, cur), pl.BlockSpec((None, None, BAND, LANES), cur)]
        out_shapes += [jax.ShapeDtypeStruct((bsz, dil, seq // dil, A_WIDTH), BF16),
                       jax.ShapeDtypeStruct((bsz, dil, seq // dil, LANES), F32)]
    return pl.pallas_call(
        functools.partial(_dilated_kernel, n_blocks=n_blocks),
        grid=(bsz, n_blocks),
        in_specs=in_specs,
        out_specs=out_specs,
        out_shape=out_shapes,
        scratch_shapes=[pltpu.VMEM((n_work, 2 * BAND, 2 * BAND), F32), pltpu.VMEM((n_work, 2 * BAND, 2 * BAND), BF16)],
        compiler_params=_cparams(("parallel", "parallel")),
        name="dilated",
    )(*args)


def _diff_bias_table():
    t = DIFF_BLOCK
    rel = np.arange(t)[None, :] - np.arange(t)[:, None]
    slopes = 2.0 ** (-8.0 * np.arange(1, B_HEADS + 1) / B_HEADS)
    tbl = np.zeros((B_HEADS, 2, t, t), np.float32)
    for h in range(B_HEADS):
        tbl[h, 0] = -slopes[h] * LOG2E * rel
        tbl[h, 1] = np.where(rel >= 0, -slopes[h] * LOG2E * rel, -np.inf)
    return jnp.asarray(tbl)


def _diff_kernel(sc_ref, qt_ref, k_ref, vt_ref, g_ref, bias_sc, o_ref, qst_sc, m_sc, l_sc, a_sc, acc_sc, s_sc, p_sc):
    t = DIFF_BLOCK
    lanes = 2 * HEAD_DIM
    qi = pl.program_id(1)
    lam = sc_ref[0]
    slopes = [jnp.full((1, 1), sc_ref[1 + h], F32) for h in range(B_HEADS)]
    row = lax.broadcasted_iota(I32, (lanes, t), 0)
    for h in range(B_HEADS):
        qt = qt_ref[h * lanes:(h + 1) * lanes, :]
        zero = jnp.zeros_like(qt)
        qst_sc[h, :, :t] = jnp.where(row < HEAD_DIM, qt, zero)
        qst_sc[h, :, t:] = jnp.where(row < HEAD_DIM, zero, qt)
    m_sc[...] = jnp.full(m_sc.shape, -jnp.inf, F32)
    l_sc[...] = jnp.zeros(l_sc.shape, F32)
    acc_sc[...] = jnp.zeros(acc_sc.shape, F32)

    n_grp = 2 * t // DIFF_LANES
    items = [(h, g) for h in range(B_HEADS) for g in range(n_grp)]
    chunks = [slice(r * DIFF_ROWS, (r + 1) * DIFF_ROWS) for r in range(t // DIFF_ROWS)]

    def keys_needed(g, diag):
        return min(t, (g * DIFF_LANES) % t + DIFF_LANES) if diag else t

    def block(n, diag):
        cs = [-slopes[h] * LOG2E * (jnp.full((1, 1), (qi - n) * t, I32)).astype(F32) for h in range(B_HEADS)]
        for w, (h, g) in enumerate(items):
            nk = keys_needed(g, diag)
            cols = slice(g * DIFF_LANES, (g + 1) * DIFF_LANES)
            bcols = slice((g * DIFF_LANES) % t, (g * DIFF_LANES) % t + DIFF_LANES)
            s_sc[w, :nk, :] = (_dot(k_ref[n, :nk, h * lanes:(h + 1) * lanes], qst_sc[h, :, cols])
                               + bias_sc[h, diag, :nk, bcols])
        for w, (h, g) in enumerate(items):
            used = chunks[:keys_needed(g, diag) // DIFF_ROWS]
            cols = slice(g * DIFF_LANES, (g + 1) * DIFF_LANES)
            top = s_sc[w, used[0], :]
            for rows in used[1:]:
                top = jnp.maximum(top, s_sc[w, rows, :])
            m_prev = m_sc[h, :, cols]
            m_new = jnp.maximum(m_prev, jnp.max(top, axis=0, keepdims=True) + cs[h])
            alpha = jnp.exp2(m_prev - m_new)
            shift = m_new - cs[h]
            for rows in used:
                p_sc[w, rows, :] = jnp.exp2(s_sc[w, rows, :] - shift).astype(BF16)
            m_sc[h, :, cols] = m_new
            a_sc[h, :, cols] = alpha
        for w, (h, g) in enumerate(items):
            nk = keys_needed(g, diag)
            cols = slice(g * DIFF_LANES, (g + 1) * DIFF_LANES)
            pv = _dot(vt_ref[n, h * (lanes + ONES_ROWS):(h + 1) * (lanes + ONES_ROWS), :nk], p_sc[w, :nk, :])
            alpha = a_sc[h, :, cols]
            acc_sc[h, :, cols] = alpha * acc_sc[h, :, cols] + pv[:lanes]
            l_sc[h, :, cols] = alpha * l_sc[h, :, cols] + pv[lanes:lanes + 1]

    def body(n, carry):
        block(n, 0)
        return carry

    lax.fori_loop(0, qi, body, 0)
    block(qi, 1)
    for h in range(B_HEADS):
        o = acc_sc[h] / l_sc[h]
        o = o[:, :t] - lam * o[:, t:]
        o = o * lax.rsqrt(jnp.mean(o * o, axis=0, keepdims=True) + RMS_EPS) * g_ref[...]
        o_ref[:, h * lanes:(h + 1) * lanes] = (o * (1.0 - LAMBDA_INIT)).T.astype(BF16)


def _diff(bqt, bk, bvt, scalars, diff_norm_g, bsz, seq):
    t = DIFF_BLOCK
    nb = seq // t
    lanes = 2 * HEAD_DIM
    n_items = B_HEADS * 2 * t // DIFF_LANES
    return pl.pallas_call(
        _diff_kernel,
        grid=(bsz, nb),
        in_specs=[
            pl.BlockSpec(memory_space=pltpu.SMEM),
            pl.BlockSpec((None, None, B_WIDTH, t), lambda b, i: (b, i, 0, 0)),
            pl.BlockSpec((None, nb, t, B_WIDTH), lambda b, i: (b, 0, 0, 0)),
            pl.BlockSpec((None, nb, VT_ROWS, t), lambda b, i: (b, 0, 0, 0)),
            pl.BlockSpec((lanes, 1), lambda b, i: (0, 0)),
            pl.BlockSpec((B_HEADS, 2, t, t), lambda b, i: (0, 0, 0, 0)),
        ],
        out_specs=pl.BlockSpec((None, t, B_WIDTH), lambda b, i: (b, i, 0)),
        out_shape=jax.ShapeDtypeStruct((bsz, seq, B_WIDTH), BF16),
        scratch_shapes=[pltpu.VMEM((B_HEADS, lanes, 2 * t), BF16)] + [pltpu.VMEM((B_HEADS, 1, 2 * t), F32)] * 3 + [
            pltpu.VMEM((B_HEADS, lanes, 2 * t), F32),
            pltpu.VMEM((n_items, t, DIFF_LANES), F32), pltpu.VMEM((n_items, t, DIFF_LANES), BF16)],
        compiler_params=_cparams(("parallel", "parallel")),
        name="diff",
    )(scalars, bqt, bk, bvt, diff_norm_g.reshape(lanes, 1), _diff_bias_table())


def _router_kernel(o0, l0, o1, l1, o2, l2, ob_ref, x_ref, mod_ref, wout_ref, g2_ref, wrh_ref, wrl_ref, br_ref,
                   tri_ref, spread_ref, x1_ref, he_ref, idx_ref, rank_ref, cnt_ref, *order_sc):
    def token_order(ref, scratch):
        dil, rows, width = ref.shape
        if dil == 1:
            return ref[0].astype(F32)
        n_col = width // LANES
        for r in range(dil):
            for c in range(n_col):
                scratch[c, pl.ds(r, rows, stride=dil), :] = ref[r, :, c * LANES:(c + 1) * LANES].astype(F32)
        return jnp.concatenate([scratch[c] for c in range(n_col)], axis=1)

    os_ = [token_order(o0, None), token_order(o1, order_sc[0]), token_order(o2, order_sc[2])]
    ls = [token_order(l0, None), token_order(l1, order_sc[1]), token_order(l2, order_sc[3])]
    mx = jnp.maximum(jnp.maximum(ls[0], ls[1]), ls[2])
    ws = [jnp.exp(l - mx) for l in ls]
    den = ws[0] + ws[1] + ws[2]
    spread = spread_ref[...]
    oa = jnp.zeros(os_[0].shape, F32)
    for w, o in zip(ws, os_):
        wh, wl = _split_bf16(w / den)
        oa = oa + (_dot(wh, spread) + _dot(wl, spread)) * o
    mixed = _dot(oa.astype(BF16), wout_ref[:A_WIDTH, :]) + _dot(ob_ref[...], wout_ref[A_WIDTH:, :])
    x1 = x_ref[...] + mod_ref[2:3, :] * mixed
    x1_ref[...] = x1
    y = x1 * lax.rsqrt(jnp.mean(x1 * x1, axis=-1, keepdims=True) + RMS_EPS) * g2_ref[...]
    h2 = y * (1.0 + mod_ref[4:5, :]) + mod_ref[3:4, :]
    he_ref[:, :D_MODEL] = h2.astype(BF16)

    hh, hl = _split_bf16(h2)
    wrh = wrh_ref[...]
    logits = _dot_nt(wrh, hh) + _dot_nt(wrh, hl) + _dot_nt(wrl_ref[...], hh) + br_ref[...]
    t = logits.shape[1]
    eid = lax.broadcasted_iota(I32, (N_EXPERTS, t), 0)
    vals, idxs, hots = [], [], []
    cur = logits
    for _ in range(TOP_K):
        v = jnp.max(cur, axis=0, keepdims=True)
        ik = jnp.min(jnp.where(cur == v, eid, N_EXPERTS), axis=0, keepdims=True)
        hot = eid == ik
        vals.append(v)
        idxs.append(ik)
        hots.append(hot)
        cur = jnp.where(hot, -jnp.inf, cur)
    es = [jnp.exp(v - vals[0]) for v in vals]
    esum = es[0] + es[1] + es[2] + es[3]
    idx_ref[...] = jnp.concatenate(idxs, axis=0)

    rows = [ik.astype(F32) for ik in idxs]
    for e in es:
        g = e / esum
        hi = g.astype(BF16).astype(F32)
        mid = (g - hi).astype(BF16).astype(F32)
        rows += [hi, mid, g - hi - mid]
    rows.append(jnp.zeros((EXTRA_LANES - len(rows), t), F32))
    he_ref[:, D_MODEL:] = jnp.concatenate(rows, axis=0).T.astype(BF16)

    sel = jnp.where(hots[0] | hots[1] | hots[2] | hots[3], 1.0, 0.0)
    before = _dot(sel.astype(BF16), tri_ref[...])
    ranks = [jnp.sum(jnp.where(hot, before, 0.0), axis=0, keepdims=True) for hot in hots]
    rank_ref[...] = jnp.concatenate(ranks, axis=0).astype(I32)
    cnt_ref[...] = jnp.broadcast_to(jnp.sum(sel, axis=1, keepdims=True), cnt_ref.shape)


def _router(dil_outs, o_b, x2, mod_b, w_out, norm2_g, w_router, b_router, seq):
    n = x2.shape[0]
    t = ROW_TILE
    per_b = seq // t
    wr_t = w_router.T
    wrh = wr_t.astype(BF16)
    wrl = (wr_t - wrh.astype(F32)).astype(BF16)
    tri = jnp.asarray(np.arange(t)[:, None] < np.arange(t)[None, :], BF16)
    row = lambda w: pl.BlockSpec((t, w), lambda i: (i, 0))
    full = lambda a, b: pl.BlockSpec((a, b), lambda i: (0, 0))
    tok = lambda: pl.BlockSpec((TOP_K, t), lambda i: (0, i))
    grouped = lambda dil, w: pl.BlockSpec((None, dil, t // dil, w), lambda i: (i // per_b, 0, i % per_b, 0))
    widths = (A_WIDTH, LANES)
    head_of_lane = np.arange(A_WIDTH) // HEAD_DIM
    spread = jnp.asarray(np.arange(LANES)[:, None] == head_of_lane[None, :], BF16)
    return pl.pallas_call(
        _router_kernel,
        grid=(n // t,),
        in_specs=[grouped(dil, w) for _, dil in DIL_PATTERNS for w in widths] + [
            row(B_WIDTH), row(D_MODEL),
            pl.BlockSpec((None, 6, D_MODEL), lambda i: (i // per_b, 0, 0)),
            full(D_MODEL, D_MODEL), full(1, D_MODEL), full(N_EXPERTS, D_MODEL), full(N_EXPERTS, D_MODEL),
            full(N_EXPERTS, 1), full(t, t), full(LANES, A_WIDTH),
        ],
        out_specs=[row(D_MODEL), row(ROW_WIDTH), tok(), tok(),
                   pl.BlockSpec((None, N_EXPERTS, 128), lambda i: (i, 0, 0))],
        out_shape=[
            jax.ShapeDtypeStruct((n, D_MODEL), F32),
            jax.ShapeDtypeStruct((n, ROW_WIDTH), BF16),
            jax.ShapeDtypeStruct((TOP_K, n), I32),
            jax.ShapeDtypeStruct((TOP_K, n), I32),
            jax.ShapeDtypeStruct((n // t, N_EXPERTS, 128), F32),
        ],
        scratch_shapes=[pltpu.VMEM((w // LANES, t, LANES), F32) for _, dil in DIL_PATTERNS if dil > 1 for w in widths],
        compiler_params=_cparams(("parallel",)),
        name="router",
    )(*dil_outs, o_b, x2, mod_b, w_out.astype(BF16), norm2_g.reshape(1, D_MODEL), wrh, wrl,
      b_router.reshape(N_EXPERTS, 1), tri, spread)


def _start_pieces(tables, tile, local_ref, hbm_ref, sem, outbound):
    lo_ref, go_ref, cnt_ref, _ = tables
    for c, size in enumerate(SEG_SIZES):
        cls = tile * len(SEG_SIZES) + c

        def per_piece(k, carry, size=size, cls=cls):
            lo, go = lo_ref[cls * N_EXPERTS + k], go_ref[cls * N_EXPERTS + k]
            loc = local_ref.at[pl.ds(pl.multiple_of(lo, SEG_ALIGN), size)]
            glob = hbm_ref.at[pl.ds(pl.multiple_of(go, SEG_ALIGN), size)]
            cp = pltpu.make_async_copy(loc, glob, sem) if outbound else pltpu.make_async_copy(glob, loc, sem)
            cp.start()
            return carry

        lax.fori_loop(0, cnt_ref[cls], per_piece, 0)


def _start(cp):
    cp.start()


def _wait(cp):
    cp.wait()


def _tile_rows(tables, tile):
    return tables[3][tile]


def _wait_rows(rows, local_ref, hbm_ref, sem, outbound):
    size = 1 << (LOCAL_SLOTS.bit_length() - 1)
    while size >= SEG_ALIGN:
        loc, glob = local_ref.at[pl.ds(0, size)], hbm_ref.at[pl.ds(0, size)]
        cp = pltpu.make_async_copy(loc, glob, sem) if outbound else pltpu.make_async_copy(glob, loc, sem)
        pl.when((rows & size) != 0)(cp.wait)
        size //= 2


def _one_hot_any(j, targets):
    out = jnp.zeros(j.shape, F32)
    for tgt in targets:
        out = jnp.where(j == tgt, 1.0, out)
    return out


def _sort_kernel(lo_ref, go_ref, cnt_ref, rows_ref, tail_ref, he_ref, idx_ref, rank_ref, lcol_ref, ls_ref, xs_hbm,
                 xl, zbuf, sems, zsem):
    tables = (lo_ref, go_ref, cnt_ref, rows_ref)
    i = pl.program_id(0)
    last = pl.num_programs(0) - 1
    slot = i % 2
    t = he_ref.shape[0]

    @pl.when(i == 0)
    def _():
        zbuf[...] = jnp.zeros(zbuf.shape, BF16)

        def tails(action):
            def per_expert(e, carry):
                off, n = tail_ref[e], tail_ref[N_EXPERTS + e]
                done = jnp.int32(0)
                for size in SEG_SIZES:
                    if size < MOE_TILE:
                        take = (n & size) != 0
                        dst = xs_hbm.at[pl.ds(pl.multiple_of(off + done, SEG_ALIGN), size)]
                        pl.when(take)(functools.partial(action, pltpu.make_async_copy(zbuf.at[pl.ds(0, size)], dst, zsem)))
                        done = done + jnp.where(take, size, 0)
                return carry

            lax.fori_loop(0, N_EXPERTS, per_expert, 0)

            def per_block(b, carry):
                dst = xs_hbm.at[pl.ds(pl.multiple_of(b * MOE_TILE, MOE_TILE), MOE_TILE)]
                action(pltpu.make_async_copy(zbuf, dst, zsem))
                return carry

            lax.fori_loop(tail_ref[2 * N_EXPERTS], xs_hbm.shape[0] // MOE_TILE, per_block, 0)

        tails(_start)
        tails(_wait)

    eid = lax.broadcasted_iota(I32, (N_EXPERTS, t), 0)
    lcol = lcol_ref[...]
    ls = []
    for k in range(TOP_K):
        off = jnp.sum(jnp.where(eid == idx_ref[k:k + 1, :], lcol, 0), axis=0, keepdims=True)
        ls.append(off + rank_ref[k:k + 1, :])
    ls_ref[...] = jnp.concatenate(ls, axis=0)

    he = he_ref[...]
    for jc in range(LOCAL_SLOTS // SLOT_CHUNK):
        j = lax.broadcasted_iota(I32, (SLOT_CHUNK, t), 0) + jc * SLOT_CHUNK
        perm = _one_hot_any(j, ls).astype(BF16)
        xl[slot, jc * SLOT_CHUNK:(jc + 1) * SLOT_CHUNK, :] = _dot(perm, he).astype(BF16)

    _start_pieces(tables, i, xl.at[slot], xs_hbm, sems.at[slot], True)

    @pl.when(i > 0)
    def _():
        _wait_rows(_tile_rows(tables, i - 1), xl.at[1 - slot], xs_hbm, sems.at[1 - slot], True)

    @pl.when(i == last)
    def _():
        _wait_rows(_tile_rows(tables, i), xl.at[slot], xs_hbm, sems.at[slot], True)


def _sort(tables, tail, he, idx, rank, lcol, n_rows):
    n = he.shape[0]
    t = ROW_TILE
    tok = lambda: pl.BlockSpec((TOP_K, t), lambda i, *_: (0, i))
    return pl.pallas_call(
        _sort_kernel,
        grid_spec=pltpu.PrefetchScalarGridSpec(
            num_scalar_prefetch=5,
            grid=(n // t,),
            in_specs=[
                pl.BlockSpec((t, ROW_WIDTH), lambda i, *_: (i, 0)),
                tok(), tok(),
                pl.BlockSpec((None, N_EXPERTS, 1), lambda i, *_: (i, 0, 0)),
            ],
            out_specs=[tok(), pl.BlockSpec(memory_space=pl.ANY)],
            scratch_shapes=[pltpu.VMEM((2, LOCAL_SLOTS, ROW_WIDTH), BF16), pltpu.VMEM((MOE_TILE, ROW_WIDTH), BF16),
                            pltpu.SemaphoreType.DMA((2,)), pltpu.SemaphoreType.DMA(())],
        ),
        out_shape=[jax.ShapeDtypeStruct((TOP_K, n), I32), jax.ShapeDtypeStruct((n_rows, ROW_WIDTH), BF16)],
        compiler_params=_cparams(("arbitrary",)),
        name="sort",
    )(*tables, tail, he, idx, rank, lcol)


def _experts_kernel(blk0_ref, nblk_ref, xs_hbm, wgu_ref, bgu_ref, wd_ref, bd_ref, yb_hbm, wgu_sc, wd_sc, xbuf, ybuf,
                    xsem, ysem):
    e = pl.program_id(0)
    tm = MOE_TILE
    first, nb = blk0_ref[e], nblk_ref[e]
    last_e = pl.num_programs(0) - 1
    total = blk0_ref[last_e] + nblk_ref[last_e]

    def rows(b, n=1):
        return pl.ds(pl.multiple_of(b * tm, tm), n * tm)

    def x_copy(b, slot):
        return pltpu.make_async_copy(xs_hbm.at[rows(b)], xbuf.at[rows(slot)], xsem.at[slot])

    def x_start(b):
        x_copy(b, b % X_SLOTS).start()
        pl.when(b % X_SLOTS == 0)(lambda: x_copy(b, X_SLOTS).start())

    def x_wait(b):
        x_copy(b, b % X_SLOTS).wait()
        pl.when(b % X_SLOTS == 0)(lambda: x_copy(b, X_SLOTS).wait())

    def y_copy(b, slot, n):
        return pltpu.make_async_copy(ybuf.at[slot, pl.ds(0, n * tm)], yb_hbm.at[rows(b, n)], ysem.at[slot])

    @pl.when(e == 0)
    def _():
        for b in range(X_AHEAD):
            pl.when(b < total)(functools.partial(x_start, b))

    def cast(r, carry):
        s = pl.multiple_of(r * LANES, LANES)
        wgu_sc[pl.ds(s, LANES), :] = wgu_ref[pl.ds(s, LANES), :].astype(BF16)
        wd_sc[pl.ds(s, LANES), :] = wd_ref[pl.ds(s, LANES), :].astype(BF16)
        return carry

    lax.fori_loop(0, D_MODEL // LANES, cast, 0)
    me = jnp.full((1, 1), e, I32).astype(F32)

    def unit(it, b, n):
        slot = it % 2
        for k in range(n):
            pl.when(b + X_AHEAD + k < total)(functools.partial(x_start, b + X_AHEAD + k))
        for k in range(n):
            x_wait(b + k)

        @pl.when(it >= 2)
        def _():
            y_copy(b, slot, 2).wait()

        x = xbuf.at[rows(b % X_SLOTS, n)]
        ext = x[:, D_MODEL:].astype(F32)
        gate = jnp.zeros((n * tm, 1), F32)
        for k in range(TOP_K):
            c = TOP_K + 3 * k
            gk = ext[:, c:c + 1] + ext[:, c + 1:c + 2] + ext[:, c + 2:c + 3]
            gate = gate + jnp.where(ext[:, k:k + 1] == me, gk, 0.0)

        xd = x[:, :D_MODEL]
        y = jnp.zeros((n * tm, D_MODEL), F32)
        for c0 in range(0, D_FF, FF_HALF):
            gcols, ucols = slice(c0, c0 + FF_HALF), slice(D_FF + c0, D_FF + c0 + FF_HALF)
            g = jnp.minimum(_dot(xd, wgu_sc[:, gcols]) + bgu_ref[:, gcols], SWIGLU_LIMIT)
            u = jnp.clip(_dot(xd, wgu_sc[:, ucols]) + bgu_ref[:, ucols], -SWIGLU_LIMIT, SWIGLU_LIMIT)
            act = (u + 1.0) * (g / (1.0 + jnp.exp(-SWIGLU_ALPHA * g)))
            y = y + _dot(act.astype(BF16), wd_sc[c0:c0 + FF_HALF, :])
        ybuf[slot, :n * tm, :] = (gate * (y + bd_ref[...])).astype(BF16)
        y_copy(b, slot, n).start()

    n_pair = nb // 2
    odd = nb % 2 == 1

    def pair(it, carry):
        unit(it, first + 2 * it, 2)
        return carry

    lax.fori_loop(0, n_pair, pair, 0)
    pl.when(odd)(functools.partial(unit, n_pair, first + 2 * n_pair, 1))

    last_slot = (n_pair - 1) % 2
    pl.when(odd)(y_copy(first, n_pair % 2, 1).wait)
    pl.when(odd & (n_pair >= 1))(y_copy(first, last_slot, 2).wait)
    pl.when(jnp.logical_not(odd) & (n_pair >= 2))(y_copy(first, n_pair % 2, 2).wait)
    pl.when(jnp.logical_not(odd) & (n_pair >= 1))(y_copy(first, last_slot, 2).wait)

    @pl.when(e == last_e)
    def _():
        ybuf[0, :tm, :] = jnp.zeros((tm, D_MODEL), BF16)

        def fill(action):
            def per_block(b, carry):
                action(pltpu.make_async_copy(ybuf.at[0, pl.ds(0, tm)], yb_hbm.at[rows(b)], ysem.at[0]))
                return carry

            lax.fori_loop(total, yb_hbm.shape[0] // tm, per_block, 0)

        fill(_start)
        fill(_wait)


def _experts(first_block, n_block, xs, w_gate_up, b_gate_up, w_down, b_down):
    n_rows = xs.shape[0]
    exp3 = lambda e, *_: (e, 0, 0)
    return pl.pallas_call(
        _experts_kernel,
        grid_spec=pltpu.PrefetchScalarGridSpec(
            num_scalar_prefetch=2,
            grid=(N_EXPERTS,),
            in_specs=[
                pl.BlockSpec(memory_space=pl.ANY),
                pl.BlockSpec((None, D_MODEL, 2 * D_FF), exp3),
                pl.BlockSpec((None, 1, 2 * D_FF), exp3),
                pl.BlockSpec((None, D_FF, D_MODEL), exp3),
                pl.BlockSpec((None, 1, D_MODEL), exp3),
            ],
            out_specs=pl.BlockSpec(memory_space=pl.ANY),
            scratch_shapes=[pltpu.VMEM((D_MODEL, 2 * D_FF), BF16), pltpu.VMEM((D_FF, D_MODEL), BF16),
                            pltpu.VMEM(((X_SLOTS + 1) * MOE_TILE, ROW_WIDTH), BF16),
                            pltpu.VMEM((2, 2 * MOE_TILE, D_MODEL), BF16),
                            pltpu.SemaphoreType.DMA((X_SLOTS + 1,)), pltpu.SemaphoreType.DMA((2,))],
        ),
        out_shape=jax.ShapeDtypeStruct((n_rows, D_MODEL), BF16),
        compiler_params=_cparams(("arbitrary",)),
        name="experts",
    )(first_block, n_block, xs, w_gate_up, b_gate_up.reshape(N_EXPERTS, 1, 2 * D_FF), w_down,
      b_down.reshape(N_EXPERTS, 1, D_MODEL))


def _combine_kernel(lo_ref, go_ref, cnt_ref, rows_ref, yb_hbm, lst_ref, x1_ref, mod_ref, o_ref, ybuf, sems):
    i = pl.program_id(0)
    slot = i % 2
    tables = (lo_ref, go_ref, cnt_ref, rows_ref)

    @pl.when(i == 0)
    def _():
        ybuf[...] = jnp.zeros(ybuf.shape, BF16)
        _start_pieces(tables, 0, ybuf.at[0], yb_hbm, sems.at[0], False)

    @pl.when(i + 1 < pl.num_programs(0))
    def _():
        _start_pieces(tables, i + 1, ybuf.at[1 - slot], yb_hbm, sems.at[1 - slot], False)

    _wait_rows(_tile_rows(tables, i), ybuf.at[slot], yb_hbm, sems.at[slot], False)

    lst = lst_ref[...]
    t = lst.shape[0]
    targets = [lst[:, k:k + 1] for k in range(TOP_K)]
    y = jnp.zeros((t, D_MODEL), F32)
    for jc in range(LOCAL_SLOTS // SLOT_CHUNK):
        j = lax.broadcasted_iota(I32, (t, SLOT_CHUNK), 1) + jc * SLOT_CHUNK
        pick = _one_hot_any(j, targets).astype(BF16)
        y = y + _dot(pick, ybuf[slot, jc * SLOT_CHUNK:(jc + 1) * SLOT_CHUNK, :])
    o_ref[...] = x1_ref[...] + mod_ref[5:6, :] * y


def _combine(tables, yb, ls_t, x1, mod_b, seq):
    n = x1.shape[0]
    t = ROW_TILE
    per_b = seq // t
    return pl.pallas_call(
        _combine_kernel,
        grid_spec=pltpu.PrefetchScalarGridSpec(
            num_scalar_prefetch=4,
            grid=(n // t,),
            in_specs=[
                pl.BlockSpec(memory_space=pl.ANY),
                pl.BlockSpec((t, TOP_K), lambda i, *_: (i, 0)),
                pl.BlockSpec((t, D_MODEL), lambda i, *_: (i, 0)),
                pl.BlockSpec((None, 6, D_MODEL), lambda i, *_: (i // per_b, 0, 0)),
            ],
            out_specs=pl.BlockSpec((t, D_MODEL), lambda i, *_: (i, 0)),
            scratch_shapes=[pltpu.VMEM((2, LOCAL_SLOTS, D_MODEL), BF16), pltpu.SemaphoreType.DMA((2,))],
        ),
        out_shape=jax.ShapeDtypeStruct((n, D_MODEL), F32),
        compiler_params=_cparams(("arbitrary",)),
        name="combine",
    )(*tables, yb, ls_t, x1, mod_b)


def _layer(x, c, w_ada, b_ada, norm1_g, w_in, a_q_norm_g, a_k_norm_g, b_q_norm_g, b_k_norm_g, lambda_q1,
           lambda_k1, lambda_q2, lambda_k2, diff_norm_g, w_out, norm2_g, w_router, b_router, w_gate_up,
           b_gate_up, w_down, b_down):
    bsz, seq, _ = x.shape
    n = bsz * seq
    x2 = x.reshape(n, D_MODEL)
    mod_b = _ada(c, w_ada, b_ada).transpose(1, 0, 2)

    qk_gains = jnp.stack([
        jnp.tile(a_q_norm_g, A_HEADS) * (ATTN_SCALE * LOG2E), jnp.tile(a_k_norm_g, A_HEADS),
        jnp.tile(b_k_norm_g, 2 * B_HEADS)])
    bq_gain = (jnp.tile(b_q_norm_g, 2 * B_HEADS) * (ATTN_SCALE * LOG2E)).reshape(B_WIDTH, 1)
    *a_parts, bk, bqt, bvt = _inproj(x2, mod_b, norm1_g, w_in, qk_gains, bq_gain, bsz, seq)

    dil_outs = _dilated(a_parts, bsz, seq)
    lam = (jnp.exp(jnp.sum(lambda_q1 * lambda_k1)) - jnp.exp(jnp.sum(lambda_q2 * lambda_k2)) + LAMBDA_INIT)
    slopes_b = 2.0 ** (-8.0 * np.arange(1, B_HEADS + 1) / B_HEADS)
    scalars = jnp.concatenate([lam.reshape(1), jnp.asarray(slopes_b, F32)]).astype(F32)
    bk4 = bk.reshape(bsz, seq // DIFF_BLOCK, DIFF_BLOCK, B_WIDTH)
    o_b = _diff(bqt, bk4, bvt, scalars, diff_norm_g, bsz, seq).reshape(n, B_WIDTH)

    x1, he, idx, rank, cnt = _router(dil_outs, o_b, x2, mod_b, w_out, norm2_g, w_router, b_router, seq)

    n_tiles = n // ROW_TILE
    counts = cnt[:, :, 0].astype(I32)
    seg = (counts + SEG_ALIGN - 1) // SEG_ALIGN * SEG_ALIGN
    loff = jnp.cumsum(seg, axis=1) - seg
    region = jnp.sum(seg, axis=0)
    padded = (region + MOE_TILE - 1) // MOE_TILE * MOE_TILE
    pad_end = jnp.cumsum(padded)
    pad_start = pad_end - padded
    goff = pad_start[None, :] + jnp.cumsum(seg, axis=0) - seg
    sizes = jnp.asarray(SEG_SIZES, I32)[None, :, None]
    has = (seg[:, None, :] & sizes) != 0
    within = seg[:, None, :] & ~(2 * sizes - 1)
    place = jnp.cumsum(has, axis=-1) - has
    pick = has[:, :, None, :] & (place[:, :, None, :] == jnp.arange(N_EXPERTS, dtype=I32)[None, None, :, None])
    listed = lambda rows_: jnp.sum(jnp.where(pick, rows_[:, :, None, :], 0), axis=-1).reshape(-1).astype(I32)
    tables = (listed(loff[:, None, :] + within), listed(goff[:, None, :] + within),
              jnp.sum(has, axis=-1).reshape(-1).astype(I32), jnp.sum(seg, axis=1).astype(I32))
    n_blocks = (n * TOP_K + n_tiles * N_EXPERTS * (SEG_ALIGN - 1) + N_EXPERTS * (MOE_TILE - 1)) // MOE_TILE
    n_used = (pad_end[-1] // MOE_TILE).astype(I32)
    tail = jnp.concatenate([pad_start + region, padded - region, n_used.reshape(1)])

    ls, xs = _sort(tables, tail, he, idx, rank, loff.reshape(n_tiles, N_EXPERTS, 1), n_blocks * MOE_TILE)
    yb = _experts(pad_start // MOE_TILE, padded // MOE_TILE, xs, w_gate_up, b_gate_up, w_down, b_down)
    out = _combine(tables, yb, ls.T, x1, mod_b, seq)
    return out.reshape(bsz, seq, D_MODEL)


def kernel(x, c, w_ada, b_ada, norm1_g, w_in, a_q_norm_g, a_k_norm_g, b_q_norm_g, b_k_norm_g, lambda_q1, lambda_k1,
           lambda_q2, lambda_k2, diff_norm_g, w_out, norm2_g, w_router, b_router, w_gate_up, b_gate_up, w_down,
           b_down):
    args = (w_ada, b_ada, norm1_g, w_in, a_q_norm_g, a_k_norm_g, b_q_norm_g, b_k_norm_g, lambda_q1, lambda_k1,
            lambda_q2, lambda_k2, diff_norm_g, w_out, norm2_g, w_router, b_router, w_gate_up, b_gate_up, w_down,
            b_down)
    return _layer(x, c, *[a[0] for a in args])
```
